```python
import jax, jax.numpy as jnp
from jax import lax
import numpy as np

D_MODEL = 1024
BATCH = 32
SEQ = 256
DEPTH = 1
DEC_BATCH = 8
DEC_SEQ = 2048
PAST_LEN = 256

GRID_W = 64
D_CONV = 1024
CONV_K = 3
N_HEADS = 8
D_QK = 64
D_V = 128
D_M = N_HEADS * D_V
CHUNK = 64
N_DIR = 2
N_EXPERTS = 32
TOP_K = 4
D_FF = 1024
SWIGLU_LIMIT = 7.0
SWIGLU_ALPHA = 1.702
FORGET_BIAS = 3.0
EPS = 1e-6
N_MOD = 6
SPLIT_SIZES = (D_CONV, D_CONV, D_CONV, N_HEADS * D_QK, N_HEADS * D_QK, D_M, D_M, 4 * N_HEADS, D_MODEL, D_MODEL)
D_IN = 3 * D_CONV + 2 * N_HEADS * D_QK + 2 * D_M + 4 * N_HEADS + 2 * D_MODEL
GATE_OFF = 3 * D_CONV + 2 * N_HEADS * D_QK + 2 * D_M
F_FWD_OFF = GATE_OFF + N_HEADS
F_BWD_OFF = GATE_OFF + 3 * N_HEADS

kernel_name = 'hybrid_conv_bimlstm_moe_diffusion_step'


def rmsnorm(x, g):
    xf = x.astype(jnp.float32)
    y = xf * lax.rsqrt(jnp.mean(xf * xf, axis=-1, keepdims=True) + EPS) * g.astype(jnp.float32)
    return y.astype(x.dtype)


def adaln(cvec, w, b, dtype):
    m = jax.nn.silu(cvec.astype(jnp.float32)) @ w.astype(jnp.float32) + b.astype(jnp.float32)
    return [t[:, None, :].astype(dtype) for t in jnp.split(m, N_MOD, axis=-1)]


def conv3_centred(u, w, b):
    L = u.shape[-2]
    up = jnp.pad(u, [(0, 0)] * (u.ndim - 2) + [(1, 1), (0, 0)])
    return up[..., 0:L, :] * w[0] + up[..., 1:L + 1, :] * w[1] + up[..., 2:L + 2, :] * w[2] + b


def mlstm_chunkwise(q, k, v, log_i, log_f, C0, n0, m0):
    Bn, H, S, _ = q.shape
    nc = S // CHUNK

    def chunks(a):
        return jnp.moveaxis(a.reshape(Bn, H, nc, CHUNK, *a.shape[3:]), 2, 0)

    causal = jnp.tril(jnp.ones((CHUNK, CHUNK), dtype=bool))

    def step(carry, xs):
        C, n, m = carry
        qc, kc, vc, ic, fc = xs
        b = jnp.cumsum(fc, axis=-1)
        d = jnp.where(causal, b[..., :, None] - b[..., None, :] + ic[..., None, :], -jnp.inf)
        m_inter = b + m[..., None]
        m_t = jnp.maximum(m_inter, jnp.max(d, axis=-1))
        w_inter = jnp.exp(m_inter - m_t)
        s = jnp.einsum('bhtd,bhsd->bhts', qc, kc) * jnp.exp(d - m_t[..., None])
        num = jnp.einsum('bhts,bhsv->bhtv', s, vc) + w_inter[..., None] * jnp.einsum('bhtd,bhdv->bhtv', qc, C)
        den = jnp.sum(s, axis=-1) + w_inter * jnp.einsum('bhtd,bhd->bht', qc, n)
        h = num / jnp.maximum(jnp.abs(den), jnp.exp(-m_t))[..., None]
        b_last = b[..., -1]
        g = b_last[..., None] - b + ic
        m_new = jnp.maximum(b_last + m, jnp.max(g, axis=-1))
        wk = jnp.exp(g - m_new[..., None])
        decay = jnp.exp(b_last + m - m_new)
        C_new = decay[..., None, None] * C + jnp.einsum('bhs,bhsd,bhsv->bhdv', wk, kc, vc)
        n_new = decay[..., None] * n + jnp.einsum('bhs,bhsd->bhd', wk, kc)
        return (C_new, n_new, m_new), h

    xs = (chunks(q), chunks(k), chunks(v), chunks(log_i), chunks(log_f))
    (C, n, m), h = lax.scan(step, (C0, n0, m0), xs)
    h = jnp.moveaxis(h, 0, 2).reshape(Bn, H, S, v.shape[-1])
    return h, C, n, m


def mixer(h, C0, n0, m0, is_grid, w_in, b_in, conv_w, conv_b, w_conv_out, mh_norm_g, w_m_out, w_o):
    Bn, S, _ = h.shape
    z = h @ w_in + b_in
    split_points = [int(p) for p in np.cumsum(SPLIT_SIZES)[:-1]]
    xc, bc, cc, q, k, v, o, gif, ga, gb = jnp.split(z, split_points, axis=-1)
    u = cc * xc
    if is_grid:
        rows = S // GRID_W
        u = conv3_centred(u.reshape(Bn, rows, GRID_W, D_CONV), conv_w, conv_b).reshape(Bn, S, D_CONV)
    else:
        u = conv3_centred(u, conv_w, conv_b)
    y_conv = (bc * u) @ w_conv_out
    q = q.reshape(Bn, S, N_HEADS, D_QK).transpose(0, 2, 1, 3).astype(jnp.float32) * (D_QK ** -0.5)
    k = k.reshape(Bn, S, N_HEADS, D_QK).transpose(0, 2, 1, 3).astype(jnp.float32)
    v = v.reshape(Bn, S, N_HEADS, D_V).transpose(0, 2, 1, 3).astype(jnp.float32)
    g = gif.reshape(Bn, S, 4, N_HEADS).transpose(2, 0, 3, 1).astype(jnp.float32)
    h_f, C_f, n_f, m_f = mlstm_chunkwise(q, k, v, g[0], jax.nn.log_sigmoid(g[1]), C0[:, 0], n0[:, 0], m0[:, 0])
    flip = lambda a: jnp.flip(a, axis=2)
    h_b, C_b, n_b, m_b = mlstm_chunkwise(flip(q), flip(k), flip(v), flip(g[2]), flip(jax.nn.log_sigmoid(g[3])),
                                         C0[:, 1], n0[:, 1], m0[:, 1])
    hm = h_f + flip(h_b)
    hm = hm * lax.rsqrt(jnp.mean(hm * hm, axis=-1, keepdims=True) + EPS) * mh_norm_g.reshape(N_HEADS, 1, D_V).astype(jnp.float32)
    hm = hm.transpose(0, 2, 1, 3).reshape(Bn, S, D_M).astype(h.dtype) * jax.nn.sigmoid(o)
    y_m = hm @ w_m_out
    out = (jax.nn.sigmoid(ga) * y_conv + jax.nn.sigmoid(gb) * y_m) @ w_o
    C = jnp.stack([C_f, C_b], axis=1)
    n = jnp.stack([n_f, n_b], axis=1)
    m = jnp.stack([m_f, m_b], axis=1)
    return out, C, n, m


def moe(h, router_w, router_b, w_gate, b_gate, w_lin, b_lin, w_down, b_down):
    Bn, S, D = h.shape
    hf = h.reshape(Bn * S, D)
    logits = hf.astype(jnp.float32) @ router_w.astype(jnp.float32) + router_b.astype(jnp.float32)
    top_val, top_idx = lax.top_k(logits, TOP_K)
    probs = jax.nn.softmax(top_val, axis=-1)
    combine = jnp.sum(jax.nn.one_hot(top_idx, N_EXPERTS, dtype=jnp.float32) * probs[..., None], axis=1)

    def expert_step(acc, xs):
        wg, bg, wl, bl, wd, bd, cw = xs
        gt = jnp.minimum(hf @ wg + bg, SWIGLU_LIMIT)
        lin = jnp.clip(hf @ wl + bl, -SWIGLU_LIMIT, SWIGLU_LIMIT)
        a = gt * jax.nn.sigmoid(SWIGLU_ALPHA * gt) * (lin + 1.0)
        y = (a @ wd + bd).astype(jnp.float32)
        return acc + cw[:, None] * y, None

    acc, _ = lax.scan(expert_step, jnp.zeros((Bn * S, D), jnp.float32),
                      (w_gate, b_gate, w_lin, b_lin, w_down, b_down, combine.T))
    return acc.reshape(Bn, S, D).astype(h.dtype)


def block(x, mod, C0, n0, m0, is_grid, norm1_g, norm2_g, w_in, b_in, conv_w, conv_b, w_conv_out,
          mh_norm_g, w_m_out, w_o, router_w, router_b, w_gate, b_gate, w_lin, b_lin, w_down, b_down):
    shift1, scale1, gate1, shift2, scale2, gate2 = mod
    h = rmsnorm(x, norm1_g) * (1 + scale1) + shift1
    y, C, n, m = mixer(h, C0, n0, m0, is_grid, w_in, b_in, conv_w, conv_b, w_conv_out, mh_norm_g, w_m_out, w_o)
    x = x + gate1 * y
    h = rmsnorm(x, norm2_g) * (1 + scale2) + shift2
    x = x + gate2 * moe(h, router_w, router_b, w_gate, b_gate, w_lin, b_lin, w_down, b_down)
    return x, C, n, m


def setup_inputs(seed: int = 0) -> dict:
    key = jax.random.key(seed)
    ks = jax.random.split(key, 32)
    nrm = lambda k, shape, s: s * jax.random.normal(k, shape, jnp.float32)
    b_in = nrm(ks[12], (DEPTH, D_IN), 0.02)
    b_in = b_in.at[:, F_FWD_OFF:F_FWD_OFF + N_HEADS].add(FORGET_BIAS)
    b_in = b_in.at[:, F_BWD_OFF:F_BWD_OFF + N_HEADS].add(FORGET_BIAS)
    return {
        'x_prompt': nrm(ks[0], (BATCH, SEQ, D_MODEL), 1.0),
        'x_sample': nrm(ks[1], (DEC_BATCH, DEC_SEQ, D_MODEL), 1.0),
        'c': nrm(ks[2], (DEC_BATCH, D_MODEL), 1.0),
        'state_C': nrm(ks[3], (DEC_BATCH, DEPTH, N_DIR, N_HEADS, D_QK, D_V), 0.1),
        'state_n': nrm(ks[4], (DEC_BATCH, DEPTH, N_DIR, N_HEADS, D_QK), 0.1),
        'state_m': nrm(ks[5], (DEC_BATCH, DEPTH, N_DIR, N_HEADS), 0.5),
        'c_ctx': nrm(ks[6], (D_MODEL,), 1.0),
        'ada_w': nrm(ks[7], (DEPTH, D_MODEL, N_MOD * D_MODEL), 0.5 * D_MODEL ** -0.5),
        'ada_b': nrm(ks[8], (DEPTH, N_MOD * D_MODEL), 0.02),
        'norm1_g': 1.0 + nrm(ks[9], (DEPTH, D_MODEL), 0.02),
        'norm2_g': 1.0 + nrm(ks[10], (DEPTH, D_MODEL), 0.02),
        'w_in': nrm(ks[11], (DEPTH, D_MODEL, D_IN), D_MODEL ** -0.5),
        'b_in': b_in,
        'conv_w': nrm(ks[13], (DEPTH, CONV_K, D_CONV), 0.5),
        'conv_b': nrm(ks[14], (DEPTH, D_CONV), 0.02),
        'w_conv_out': nrm(ks[15], (DEPTH, D_CONV, D_MODEL), D_CONV ** -0.5),
        'mh_norm_g': 1.0 + nrm(ks[16], (DEPTH, D_M), 0.02),
        'w_m_out': nrm(ks[17], (DEPTH, D_M, D_MODEL), D_M ** -0.5),
        'w_o': nrm(ks[18], (DEPTH, D_MODEL, D_MODEL), D_MODEL ** -0.5),
        'router_w': nrm(ks[19], (DEPTH, D_MODEL, N_EXPERTS), D_MODEL ** -0.5),
        'router_b': nrm(ks[20], (DEPTH, N_EXPERTS), 0.01),
        'w_gate': nrm(ks[21], (DEPTH, N_EXPERTS, D_MODEL, D_FF), D_MODEL ** -0.5),
        'b_gate': nrm(ks[22], (DEPTH, N_EXPERTS, D_FF), 0.02),
        'w_lin': nrm(ks[23], (DEPTH, N_EXPERTS, D_MODEL, D_FF), D_MODEL ** -0.5),
        'b_lin': nrm(ks[24], (DEPTH, N_EXPERTS, D_FF), 0.02),
        'w_down': nrm(ks[25], (DEPTH, N_EXPERTS, D_FF, D_MODEL), D_FF ** -0.5),
        'b_down': nrm(ks[26], (DEPTH, N_EXPERTS, D_MODEL), 0.02),
        'final_g': 1.0 + nrm(ks[27], (D_MODEL,), 0.02),
    }


def reference(x_prompt, x_sample, c, state_C, state_n, state_m, c_ctx, ada_w, ada_b, norm1_g, norm2_g,
              w_in, b_in, conv_w, conv_b, w_conv_out, mh_norm_g, w_m_out, w_o, router_w, router_b,
              w_gate, b_gate, w_lin, b_lin, w_down, b_down, final_g):
    nb = x_prompt.shape[0]
    yp, ys = x_prompt, x_sample
    Cs, ns, ms = [], [], []
    for l in range(DEPTH):
        lw = (norm1_g[l], norm2_g[l], w_in[l], b_in[l], conv_w[l], conv_b[l], w_conv_out[l], mh_norm_g[l],
              w_m_out[l], w_o[l], router_w[l], router_b[l], w_gate[l], b_gate[l], w_lin[l], b_lin[l],
              w_down[l], b_down[l])
        mod_ctx = adaln(c_ctx[None, :], ada_w[l], ada_b[l], yp.dtype)
        mod_lat = adaln(c, ada_w[l], ada_b[l], ys.dtype)
        zC = jnp.zeros((nb, N_DIR, N_HEADS, D_QK, D_V), jnp.float32)
        zn = jnp.zeros((nb, N_DIR, N_HEADS, D_QK), jnp.float32)
        zm = jnp.zeros((nb, N_DIR, N_HEADS), jnp.float32)
        yp, C_l, n_l, m_l = block(yp, mod_ctx, zC, zn, zm, False, *lw)
        Cs.append(C_l)
        ns.append(n_l)
        ms.append(m_l)
        ys, _, _, _ = block(ys, mod_lat, state_C[:, l].astype(jnp.float32), state_n[:, l].astype(jnp.float32),
                            state_m[:, l].astype(jnp.float32), True, *lw)
    y_prompt = rmsnorm(yp, final_g)
    y_sample = rmsnorm(ys, final_g)
    new_C = jnp.stack(Cs, axis=1)
    new_n = jnp.stack(ns, axis=1)
    new_m = jnp.stack(ms, axis=1)
    return (y_prompt, y_sample, new_C, new_n, new_m)
```

```python
import functools

import jax
import jax.numpy as jnp
from jax import lax
from jax.experimental import pallas as pl
from jax.experimental.pallas import tpu as pltpu

F32 = jnp.float32
BF16 = jnp.bfloat16
HIGHEST = lax.Precision.HIGHEST

N_HEADS = 8
D_QK = 64
D_V = 128
GRID_W = 64
TOP_K = 4
SWIGLU_LIMIT = 7.0
SWIGLU_ALPHA = 1.702
EPS = 1e-6
N_MOD = 6

LANES = 128
TM = 256
N_HD = 2 * N_HEADS
VMEM_LIMIT = 56 * 1024 * 1024

_D = 1024
C_CONV = (0, 3 * _D)
C_Q = (3 * _D, 4 * _D)
C_K = (4 * _D, 5 * _D)
C_V = (5 * _D, 6 * _D)
C_O = (6 * _D, 7 * _D)
C_GA = (7 * _D, 8 * _D)
C_GB = (8 * _D, 9 * _D)
C_G = (9 * _D, 9 * _D + 2 * LANES)
W_COLS = C_G[1]


def _dot(a, b, precision=None):
    return jnp.dot(a, b, preferred_element_type=F32, precision=precision)


def _rms(x):
    return x * lax.rsqrt(jnp.mean(x * x, axis=-1, keepdims=True) + EPS)


def _const_spec(shape):
    return pl.BlockSpec(shape, lambda *_: (0,) * len(shape), pipeline_mode=pl.Buffered(1))


def _mod_kernel(c_ref, w_ref, b_ref, o_ref):
    c = c_ref[...]
    o_ref[...] = _dot(c * jax.nn.sigmoid(c), w_ref[...], HIGHEST) + b_ref[...]


def _modulation(cvec, ada_w, ada_b):
    n, d = cvec.shape
    nout = ada_w.shape[1]
    return pl.pallas_call(
        _mod_kernel,
        grid=(nout // d,),
        in_specs=[pl.BlockSpec((n, d), lambda j: (0, 0)),
                  pl.BlockSpec((d, d), lambda j: (0, j)),
                  pl.BlockSpec((1, d), lambda j: (0, j))],
        out_specs=pl.BlockSpec((n, d), lambda j: (0, j)),
        out_shape=jax.ShapeDtypeStruct((n, nout), F32),
        name="adaln_mod",
    )(cvec, ada_w, ada_b.reshape(1, nout))


def _proj_kernel(x_ref, mod_ref, g1_ref, w_ref, b_ref, cw_ref, cb_ref, wco_ref,
                 ycg_ref, sgb_ref, q_ref, k_ref, v_ref, so_ref, gt_ref, *, n_ctx_tiles, ctx_row):
    i = pl.program_id(0)
    x = x_ref[...]
    h = (_rms(x) * g1_ref[...] * (1.0 + mod_ref[0, 1:2, :]) + mod_ref[0, 0:1, :]).astype(BF16)

    def proj(cols):
        return _dot(h, w_ref[:, cols[0]:cols[1]]) + b_ref[:, cols[0]:cols[1]]

    zc = proj(C_CONV)
    d = x.shape[1]
    u = zc[:, 2 * d:3 * d] * zc[:, 0:d]
    rowlen = jnp.where(i < n_ctx_tiles, ctx_row, GRID_W)
    pos = lax.broadcasted_iota(jnp.int32, (TM, 1), 0) & (rowlen - 1)
    u_prev = jnp.where(pos == 0, 0.0, pltpu.roll(u, 1, 0))
    u_next = jnp.where(pos == rowlen - 1, 0.0, pltpu.roll(u, TM - 1, 0))
    uc = u_prev * cw_ref[0:1, :] + u * cw_ref[1:2, :] + u_next * cw_ref[2:3, :] + cb_ref[...]
    yconv = _dot((zc[:, d:2 * d] * uc).astype(BF16), wco_ref[...])

    ycg_ref[...] = (jax.nn.sigmoid(proj(C_GA)) * yconv).astype(BF16)
    sgb_ref[...] = jax.nn.sigmoid(proj(C_GB)).astype(BF16)
    q_ref[...] = (proj(C_Q) * (D_QK ** -0.5)).astype(BF16)
    k_ref[...] = proj(C_K).astype(BF16)
    v_ref[...] = proj(C_V).astype(BF16)
    so_ref[...] = jax.nn.sigmoid(proj(C_O)).astype(BF16)
    gt_ref[...] = proj(C_G)


def _tri_masks():
    row = lax.broadcasted_iota(jnp.int32, (TM, TM), 0)
    col = lax.broadcasted_iota(jnp.int32, (TM, TM), 1)
    return row >= col, row <= col


def _gate_cumsums(gi, gf):
    lower, upper = _tri_masks()
    lf = jax.nn.log_sigmoid(gf)
    fwd_lane = lax.broadcasted_iota(jnp.int32, (1, LANES), 1) < N_HEADS
    bsum = jnp.where(fwd_lane, _dot(lower.astype(F32), lf, HIGHEST), _dot(upper.astype(F32), lf, HIGHEST))
    return lf, bsum, fwd_lane


def _state_kernel(*refs, zero_init, emit_before, emit_after):
    it = iter(refs)
    kf_ref, vf_ref, gf_ref, kb_ref, vb_ref, gb_ref = (next(it) for _ in range(6))
    if not zero_init:
        c0_ref, n0_ref, m0_ref = (next(it) for _ in range(3))
    if emit_before:
        cbf_ref, cbb_ref, nbf_ref, nbb_ref, mbf_ref, mbb_ref = (next(it) for _ in range(6))
    if emit_after:
        ca_ref, na_ref, ma_ref = (next(it) for _ in range(3))
    c_scr, n_scr, m_scr = (next(it) for _ in range(3))
    c = pl.program_id(1)

    @pl.when(c == 0)
    def _():
        if zero_init:
            c_scr[...] = jnp.zeros_like(c_scr)
            n_scr[...] = jnp.zeros_like(n_scr)
            m_scr[...] = jnp.zeros_like(m_scr)
        else:
            c_scr[...] = c0_ref[0]
            n_scr[...] = n0_ref[0]
            m_scr[...] = m0_ref[0]

    if emit_before:
        cbf_ref[0] = c_scr[0:N_HEADS]
        cbb_ref[0] = c_scr[N_HEADS:N_HD]
        nbf_ref[0] = n_scr[0:N_HEADS]
        nbb_ref[0] = n_scr[N_HEADS:N_HD]
        mbf_ref[0] = m_scr[...]
        mbb_ref[0] = m_scr[...]

    fwd_lane = lax.broadcasted_iota(jnp.int32, (1, LANES), 1) < N_HEADS
    gi = jnp.where(fwd_lane, gf_ref[:, 0:LANES], gb_ref[:, 0:LANES])
    gfg = jnp.where(fwd_lane, gf_ref[:, LANES:2 * LANES], gb_ref[:, LANES:2 * LANES])
    lf, bsum, _ = _gate_cumsums(gi, gfg)
    total = jnp.sum(lf, axis=0, keepdims=True)
    g = total - bsum + gi
    m_prev = m_scr[...]
    m_new = jnp.maximum(total + m_prev, jnp.max(g, axis=0, keepdims=True))
    wk = jnp.exp(g - m_new)
    decay = jnp.exp(total + m_prev - m_new)
    for hd in range(N_HD):
        h = hd % N_HEADS
        k_ref, v_ref = (kf_ref, vf_ref) if hd < N_HEADS else (kb_ref, vb_ref)
        wkk = wk[:, hd:hd + 1] * k_ref[:, h * LANES:(h + 1) * LANES].astype(F32)
        upd = lax.dot_general(wkk.astype(BF16), v_ref[:, h * D_V:(h + 1) * D_V],
                              (((0,), (0,)), ((), ())), preferred_element_type=F32)
        dec = decay[:, hd:hd + 1]
        c_scr[hd] = dec * c_scr[hd] + upd[0:D_QK, :]
        n_scr[hd:hd + 1, :] = dec * n_scr[hd:hd + 1, :] + jnp.sum(wkk, axis=0, keepdims=True)
    m_scr[...] = m_new

    if emit_after:
        @pl.when(c == pl.num_programs(1) - 1)
        def _():
            ca_ref[0] = c_scr[...]
            na_ref[0] = n_scr[...]
            ma_ref[0] = m_scr[...]


def _state_scan(k, v, gt, tile0, n_seq, n_chunk, init, emit_before, emit_after):
    d = k.shape[1]
    fwd = lambda s, c: (tile0 + s * n_chunk + c, 0)
    bwd = lambda s, c: (tile0 + s * n_chunk + n_chunk - 1 - c, 0)
    in_specs = [pl.BlockSpec((TM, d), fwd), pl.BlockSpec((TM, d), fwd), pl.BlockSpec((TM, 2 * LANES), fwd),
                pl.BlockSpec((TM, d), bwd), pl.BlockSpec((TM, d), bwd), pl.BlockSpec((TM, 2 * LANES), bwd)]
    args = [k, v, gt, k, v, gt]
    if init is not None:
        in_specs += [pl.BlockSpec((1, N_HD, D_QK, D_V), lambda s, c: (s, 0, 0, 0)),
                     pl.BlockSpec((1, N_HD, LANES), lambda s, c: (s, 0, 0)),
                     pl.BlockSpec((1, 1, LANES), lambda s, c: (s, 0, 0))]
        args += list(init)
    out_specs, out_shape = [], []
    n_tot = n_seq * n_chunk
    if emit_before:
        cf = lambda s, c: (s * n_chunk + c, 0, 0, 0)
        cb = lambda s, c: (s * n_chunk + n_chunk - 1 - c, 0, 0, 0)
        nf = lambda s, c: (s * n_chunk + c, 0, 0)
        nb = lambda s, c: (s * n_chunk + n_chunk - 1 - c, 0, 0)
        out_specs += [pl.BlockSpec((1, N_HEADS, D_QK, D_V), cf), pl.BlockSpec((1, N_HEADS, D_QK, D_V), cb),
                      pl.BlockSpec((1, N_HEADS, LANES), nf), pl.BlockSpec((1, N_HEADS, LANES), nb),
                      pl.BlockSpec((1, 1, LANES), nf), pl.BlockSpec((1, 1, LANES), nb)]
        out_shape += [jax.ShapeDtypeStruct((n_tot, N_HEADS, D_QK, D_V), F32)] * 2
        out_shape += [jax.ShapeDtypeStruct((n_tot, N_HEADS, LANES), F32)] * 2
        out_shape += [jax.ShapeDtypeStruct((n_tot, 1, LANES), F32)] * 2
    if emit_after:
        out_specs += [pl.BlockSpec((1, N_HD, D_QK, D_V), lambda s, c: (s, 0, 0, 0)),
                      pl.BlockSpec((1, N_HD, LANES), lambda s, c: (s, 0, 0)),
                      pl.BlockSpec((1, 1, LANES), lambda s, c: (s, 0, 0))]
        out_shape += [jax.ShapeDtypeStruct((n_seq, N_HD, D_QK, D_V), F32),
                      jax.ShapeDtypeStruct((n_seq, N_HD, LANES), F32),
                      jax.ShapeDtypeStruct((n_seq, 1, LANES), F32)]
    return pl.pallas_call(
        functools.partial(_state_kernel, zero_init=init is None, emit_before=emit_before, emit_after=emit_after),
        grid=(n_seq, n_chunk),
        in_specs=in_specs,
        out_specs=out_specs,
        out_shape=out_shape,
        scratch_shapes=[pltpu.VMEM((N_HD, D_QK, D_V), F32), pltpu.VMEM((N_HD, LANES), F32),
                        pltpu.VMEM((1, LANES), F32)],
        compiler_params=pltpu.CompilerParams(dimension_semantics=("arbitrary", "arbitrary"),
                                             vmem_limit_bytes=VMEM_LIMIT),
        name="mlstm_state_scan",
    )(*args)


def _mix_kernel(q_ref, k_ref, v_ref, so_ref, sgb_ref, ycg_ref, gt_ref, x_ref, mod_ref,
                cbf_ref, cbb_ref, nbf_ref, nbb_ref, mbf_ref, mbb_ref,
                gmh_ref, wmo_ref, wo_ref, g2_ref, rw_ref, rb_ref,
                x1_ref, h2_ref, cw_ref, hm_scr, *, n_ctx_tiles, n_experts):
    i = pl.program_id(0)
    is_lat = i >= n_ctx_tiles
    lat_f = is_lat.astype(F32)
    lower, upper = _tri_masks()
    gi = gt_ref[:, 0:LANES]
    lf, bsum, fwd_lane = _gate_cumsums(gi, gt_ref[:, LANES:2 * LANES])
    a = gi - bsum
    t_idx = lax.broadcasted_iota(jnp.int32, (TM, LANES), 0)
    pmax, smax = a, a
    step = 1
    while step < TM:
        pmax = jnp.maximum(pmax, jnp.where(t_idx >= step, pltpu.roll(pmax, step, 0), -jnp.inf))
        smax = jnp.maximum(smax, jnp.where(t_idx < TM - step, pltpu.roll(smax, TM - step, 0), -jnp.inf))
        step *= 2
    m_prev = jnp.where(fwd_lane, mbf_ref[0], mbb_ref[0]) * lat_f
    mrow = jnp.maximum(m_prev, jnp.where(fwd_lane, pmax, smax))
    w_inter = jnp.exp(m_prev - mrow)
    e_floor = jnp.exp(-(bsum + mrow))
    a_t = a.T
    n_all = jnp.concatenate([nbf_ref[0], nbb_ref[0], jnp.zeros((LANES - N_HD, LANES), F32)], axis=0) * lat_f
    n_t = n_all.T
    lane0 = lax.broadcasted_iota(jnp.int32, (1, LANES), 1) == 0
    ones_col = jnp.where(lane0, 1.0, 0.0).astype(BF16) * jnp.ones((TM, 1), BF16)
    zpad = jnp.zeros((LANES - D_QK, D_V), F32)

    for h in range(N_HEADS):
        qh = q_ref[:, h * LANES:(h + 1) * LANES]
        kh = k_ref[:, h * LANES:(h + 1) * LANES]
        vext = jnp.concatenate([v_ref[:, h * D_V:(h + 1) * D_V], ones_col], axis=1)
        qk = lax.dot_general(qh, kh, (((1,), (1,)), ((), ())), preferred_element_type=F32)
        hsum = None
        for d in range(2):
            hd = d * N_HEADS + h
            mask = lower if d == 0 else upper
            c_prev = (cbf_ref if d == 0 else cbb_ref)[0, h]
            e = jnp.exp(jnp.where(mask, a_t[hd:hd + 1, :] - mrow[:, hd:hd + 1], -jnp.inf))
            nd = _dot((qk * e).astype(BF16), vext)
            cext = jnp.concatenate(
                [jnp.concatenate([c_prev * lat_f, zpad], axis=0),
                 jnp.where(lane0, n_t[:, hd:hd + 1], 0.0)], axis=1).astype(BF16)
            qc = _dot(qh, cext)
            wi = w_inter[:, hd:hd + 1]
            num = nd[:, 0:D_V] + wi * qc[:, 0:D_V]
            den = nd[:, D_V:D_V + 1] + wi * qc[:, D_V:D_V + 1]
            r = 1.0 / jnp.maximum(jnp.abs(den), e_floor[:, hd:hd + 1])
            hsum = num * r if hsum is None else hsum + num * r
        hs = slice(h * D_V, (h + 1) * D_V)
        hm_scr[:, hs] = (_rms(hsum) * gmh_ref[:, hs] * so_ref[:, hs].astype(F32)).astype(BF16)

    ym = _dot(hm_scr[...], wmo_ref[...])
    mix = (ycg_ref[...].astype(F32) + sgb_ref[...].astype(F32) * ym).astype(BF16)
    x1 = x_ref[...] + mod_ref[0, 2:3, :] * _dot(mix, wo_ref[...])
    x1_ref[...] = x1
    h2 = _rms(x1) * g2_ref[...] * (1.0 + mod_ref[0, 4:5, :]) + mod_ref[0, 3:4, :]
    h2_ref[...] = h2.astype(BF16)

    lane = lax.broadcasted_iota(jnp.int32, (TM, LANES), 1)
    work = jnp.where(lane < n_experts, _dot(h2, rw_ref[...], HIGHEST) + rb_ref[...], -jnp.inf)
    sels, exps = [], []
    top = None
    for _ in range(TOP_K):
        mx = jnp.max(work, axis=-1, keepdims=True)
        ix = jnp.min(jnp.where(work == mx, lane, LANES), axis=-1, keepdims=True)
        sel = lane == ix
        work = jnp.where(sel, -jnp.inf, work)
        top = mx if top is None else top
        sels.append(sel)
        exps.append(jnp.exp(mx - top))
    inv = 1.0 / functools.reduce(lambda p, q: p + q, exps)
    cw = jnp.zeros((TM, LANES), F32)
    for sel, ex in zip(sels, exps):
        cw = cw + jnp.where(sel, ex * inv, 0.0)
    cw_ref[...] = cw


def _moe_kernel(h2_ref, cw_ref, x1_ref, mod_ref, wg_ref, bg_ref, wl_ref, bl_ref, wd_ref, bd_ref, fg_ref,
                out_ref, acc_ref):
    e = pl.program_id(1)

    @pl.when(e == 0)
    def _():
        acc_ref[...] = jnp.zeros_like(acc_ref)

    h = h2_ref[...]
    gt = jnp.minimum(_dot(h, wg_ref[0]) + bg_ref[0], SWIGLU_LIMIT)
    lin = jnp.clip(_dot(h, wl_ref[0]) + bl_ref[0], -SWIGLU_LIMIT, SWIGLU_LIMIT)
    act = gt * jax.nn.sigmoid(SWIGLU_ALPHA * gt) * (lin + 1.0)
    y = _dot(act.astype(BF16), wd_ref[0]) + bd_ref[0]
    lane = lax.broadcasted_iota(jnp.int32, (1, LANES), 1)
    cwe = jnp.sum(jnp.where(lane == e, cw_ref[...], 0.0), axis=-1, keepdims=True)
    acc_ref[...] += cwe * y

    @pl.when(e == pl.num_programs(1) - 1)
    def _():
        out_ref[...] = _rms(x1_ref[...] + mod_ref[0, 5:6, :] * acc_ref[...]) * fg_ref[...]


def _pack_in_proj(w, b):
    d = w.shape[0]
    o_q, o_k, o_v, o_o, o_g, o_ga, o_gb = 3 * d, 3 * d + 512, 3 * d + 1024, 4 * d + 1024, 5 * d + 1024, \
        5 * d + 1024 + 4 * N_HEADS, 6 * d + 1024 + 4 * N_HEADS

    def pad_heads(m):
        m = m.reshape(m.shape[0], N_HEADS, D_QK)
        return jnp.pad(m, ((0, 0), (0, 0), (0, LANES - D_QK))).reshape(m.shape[0], N_HEADS * LANES)

    def gates(m):
        gz = jnp.zeros((m.shape[0], LANES - N_HD), m.dtype)
        i_f, f_f, i_b, f_b = (m[:, o_g + j * N_HEADS:o_g + (j + 1) * N_HEADS] for j in range(4))
        return jnp.concatenate([i_f, i_b, gz, f_f, f_b, gz], axis=1)

    def pack(m):
        return jnp.concatenate([m[:, 0:o_q], pad_heads(m[:, o_q:o_k]), pad_heads(m[:, o_k:o_v]), m[:, o_v:o_o],
                                m[:, o_o:o_g], m[:, o_ga:o_gb], m[:, o_gb:o_gb + d], gates(m)], axis=1)

    return pack(w).astype(BF16), pack(b.reshape(1, -1))


def kernel(x_prompt, x_sample, c, state_C, state_n, state_m, c_ctx, ada_w, ada_b, norm1_g, norm2_g, w_in, b_in,
           conv_w, conv_b, w_conv_out, mh_norm_g, w_m_out, w_o, router_w, router_b, w_gate, b_gate, w_lin, b_lin,
           w_down, b_down, final_g):
    nb, seq, d = x_prompt.shape
    nd, dseq, _ = x_sample.shape
    n_experts = w_gate.shape[1]
    assert d == _D and w_in.shape[0] == 1 and seq == TM and dseq % TM == 0 and TM % GRID_W == 0
    t_ctx, t_lat = nb * seq, nd * dseq
    n_tok = t_ctx + t_lat
    n_ctx_tiles, n_tiles = t_ctx // TM, n_tok // TM
    lat_chunks = dseq // TM
    n_lat_tiles = n_tiles - n_ctx_tiles

    n_c = 1 + nd
    n_cp = -(-n_c // 8) * 8
    cvec = jnp.concatenate([c_ctx[None, :], c, jnp.zeros((n_cp - n_c, d), F32)], axis=0)
    mod = _modulation(cvec, ada_w[0], ada_b[0]).reshape(n_cp, N_MOD, d)
    mod = jnp.pad(mod, ((0, 0), (0, 8 - N_MOD), (0, 0)))

    def mod_row(tile_tokens):
        ctx_t, per_seq = t_ctx // tile_tokens, dseq // tile_tokens
        return lambda i, *_: (jnp.where(i < ctx_t, 0, 1 + (i - ctx_t) // per_seq), 0, 0)

    x_all = jnp.concatenate([x_prompt.reshape(t_ctx, d), x_sample.reshape(t_lat, d)], axis=0)
    w_all, b_all = _pack_in_proj(w_in[0], b_in[0])
    tile = lambda cols: pl.BlockSpec((TM, cols), lambda i: (i, 0))
    params = pltpu.CompilerParams(dimension_semantics=("arbitrary",), vmem_limit_bytes=VMEM_LIMIT)
    bf = lambda cols: jax.ShapeDtypeStruct((n_tok, cols), BF16)

    ycg, sgb, q, k, v, so, gt = pl.pallas_call(
        functools.partial(_proj_kernel, n_ctx_tiles=n_ctx_tiles, ctx_row=seq),
        grid=(n_tiles,),
        in_specs=[tile(d), pl.BlockSpec((1, 8, d), mod_row(TM)), _const_spec((1, d)),
                  _const_spec((d, W_COLS)), _const_spec((1, W_COLS)), _const_spec((3, d)), _const_spec((1, d)),
                  _const_spec((d, d))],
        out_specs=[tile(d), tile(d), tile(d), tile(d), tile(d), tile(d), tile(2 * LANES)],
        out_shape=[bf(d), bf(d), bf(d), bf(d), bf(d), bf(d), jax.ShapeDtypeStruct((n_tok, 2 * LANES), F32)],
        compiler_params=params,
        name="in_proj_conv",
    )(x_all, mod, norm1_g, w_all, b_all, conv_w[0], conv_b, w_conv_out[0].astype(BF16))

    c_new, n_new, m_new = _state_scan(k, v, gt, 0, nb, 1, None, False, True)
    init = (state_C[:, 0].astype(F32).reshape(nd, N_HD, D_QK, D_V),
            jnp.pad(state_n[:, 0].astype(F32).reshape(nd, N_HD, D_QK), ((0, 0), (0, 0), (0, LANES - D_QK))),
            jnp.pad(state_m[:, 0].astype(F32).reshape(nd, 1, N_HD), ((0, 0), (0, 0), (0, LANES - N_HD))))
    cbf, cbb, nbf, nbb, mbf, mbb = _state_scan(k, v, gt, n_ctx_tiles, nd, lat_chunks, init, True, False)

    lat_idx = lambda i: jnp.maximum(i - n_ctx_tiles, 0)
    st4 = pl.BlockSpec((1, N_HEADS, D_QK, D_V), lambda i: (lat_idx(i), 0, 0, 0))
    st3 = pl.BlockSpec((1, N_HEADS, LANES), lambda i: (lat_idx(i), 0, 0))
    st1 = pl.BlockSpec((1, 1, LANES), lambda i: (lat_idx(i), 0, 0))
    rw = jnp.pad(router_w[0], ((0, 0), (0, LANES - n_experts)))
    rb = jnp.pad(router_b[0], (0, LANES - n_experts)).reshape(1, LANES)
    x1, h2, cw = pl.pallas_call(
        functools.partial(_mix_kernel, n_ctx_tiles=n_ctx_tiles, n_experts=n_experts),
        grid=(n_tiles,),
        in_specs=[tile(d), tile(d), tile(d), tile(d), tile(d), tile(d), tile(2 * LANES), tile(d),
                  pl.BlockSpec((1, 8, d), mod_row(TM)), st4, st4, st3, st3, st1, st1,
                  _const_spec((1, d)), _const_spec((d, d)), _const_spec((d, d)), _const_spec((1, d)),
                  _const_spec((d, LANES)), _const_spec((1, LANES))],
        out_specs=[tile(d), tile(d), tile(LANES)],
        out_shape=[jax.ShapeDtypeStruct((n_tok, d), F32), bf(d), jax.ShapeDtypeStruct((n_tok, LANES), F32)],
        scratch_shapes=[pltpu.VMEM((TM, d), BF16)],
        compiler_params=params,
        name="mlstm_mix_router",
    )(q, k, v, so, sgb, ycg, gt, x_all, mod, cbf, cbb, nbf, nbb, mbf, mbb,
      mh_norm_g, w_m_out[0].astype(BF16), w_o[0].astype(BF16), norm2_g, rw, rb)

    tmd = 1024 if (t_ctx % 1024 == 0 and dseq % 1024 == 0) else TM
    dff = w_gate.shape[-1]
    wspec = lambda a, b_: pl.BlockSpec((1, a, b_), lambda i, e: (e, 0, 0))
    y = pl.pallas_call(
        _moe_kernel,
        grid=(n_tok // tmd, n_experts),
        in_specs=[pl.BlockSpec((tmd, d), lambda i, e: (i, 0)), pl.BlockSpec((tmd, LANES), lambda i, e: (i, 0)),
                  pl.BlockSpec((tmd, d), lambda i, e: (i, 0)), pl.BlockSpec((1, 8, d), mod_row(tmd)),
                  wspec(d, dff), wspec(1, dff), wspec(d, dff), wspec(1, dff), wspec(dff, d), wspec(1, d),
                  pl.BlockSpec((1, d), lambda i, e: (0, 0))],
        out_specs=pl.BlockSpec((tmd, d), lambda i, e: (i, 0)),
        out_shape=jax.ShapeDtypeStruct((n_tok, d), F32),
        scratch_shapes=[pltpu.VMEM((tmd, d), F32)],
        compiler_params=pltpu.CompilerParams(dimension_semantics=("arbitrary", "arbitrary"),
                                             vmem_limit_bytes=VMEM_LIMIT),
        name="moe_dense",
    )(h2, cw, x1, mod, w_gate[0].astype(BF16), b_gate[0].reshape(n_experts, 1, dff), w_lin[0].astype(BF16),
      b_lin[0].reshape(n_experts, 1, dff), w_down[0].astype(BF16), b_down[0].reshape(n_experts, 1, d),
      final_g.reshape(1, d))

    y_prompt = y[:t_ctx].reshape(nb, seq, d)
    y_sample = y[t_ctx:].reshape(nd, dseq, d)
    new_c = c_new.reshape(nb, 1, 2, N_HEADS, D_QK, D_V)
    new_n = n_new[:, :, :D_QK].reshape(nb, 1, 2, N_HEADS, D_QK)
    new_m = m_new[:, 0, :N_HD].reshape(nb, 1, 2, N_HEADS)
    return (y_prompt, y_sample, new_c, new_n, new_m)
```

```python
import functools

import jax
import jax.numpy as jnp
from jax import lax
from jax.experimental import pallas as pl
from jax.experimental.pallas import tpu as pltpu

F32 = jnp.float32
BF16 = jnp.bfloat16
HIGHEST = lax.Precision.HIGHEST

N_HEADS = 8
D_QK = 64
D_V = 128
GRID_W = 64
TOP_K = 4
SWIGLU_LIMIT = 7.0
SWIGLU_ALPHA = 1.702
EPS = 1e-6
N_MOD = 6

LANES = 128
TM = 256
TMX = 256
N_HD = 2 * N_HEADS
VMEM_LIMIT = 56 * 1024 * 1024

_D = 1024
C_CONV = (0, 3 * _D)
C_Q = (3 * _D, 4 * _D)
C_K = (4 * _D, 5 * _D)
C_V = (5 * _D, 6 * _D)
C_O = (6 * _D, 7 * _D)
C_GA = (7 * _D, 8 * _D)
C_GB = (8 * _D, 9 * _D)
C_G = (9 * _D, 9 * _D + 2 * LANES)
W_COLS = C_G[1]


def _dot(a, b, precision=None):
    return jnp.dot(a, b, preferred_element_type=F32, precision=precision)


def _rms(x):
    return x * lax.rsqrt(jnp.mean(x * x, axis=-1, keepdims=True) + EPS)


def _const_spec(shape):
    return pl.BlockSpec(shape, lambda *_: (0,) * len(shape), pipeline_mode=pl.Buffered(1))


def _mod_kernel(c_ref, w_ref, b_ref, o_ref):
    c = c_ref[...]
    o_ref[...] = _dot(c * jax.nn.sigmoid(c), w_ref[...], HIGHEST) + b_ref[...]


def _modulation(cvec, ada_w, ada_b):
    n, d = cvec.shape
    nout = ada_w.shape[1]
    return pl.pallas_call(
        _mod_kernel,
        grid=(nout // d,),
        in_specs=[pl.BlockSpec((n, d), lambda j: (0, 0)),
                  pl.BlockSpec((d, d), lambda j: (0, j)),
                  pl.BlockSpec((1, d), lambda j: (0, j))],
        out_specs=pl.BlockSpec((n, d), lambda j: (0, j)),
        out_shape=jax.ShapeDtypeStruct((n, nout), F32),
        name="adaln_mod",
    )(cvec, ada_w, ada_b.reshape(1, nout))


def _proj_kernel(x_ref, mod_ref, g1_ref, w_ref, b_ref, cw_ref, cb_ref, wco_ref,
                 ycg_ref, sgb_ref, q_ref, k_ref, v_ref, so_ref, gt_ref, *, n_ctx_tiles, ctx_row):
    i = pl.program_id(0)
    x = x_ref[...]
    h = (_rms(x) * g1_ref[...] * (1.0 + mod_ref[0, 1:2, :]) + mod_ref[0, 0:1, :]).astype(BF16)

    def proj(cols):
        return _dot(h, w_ref[:, cols[0]:cols[1]]) + b_ref[:, cols[0]:cols[1]]

    zc = proj(C_CONV)
    d = x.shape[1]
    u = zc[:, 2 * d:3 * d] * zc[:, 0:d]
    rowlen = jnp.where(i < n_ctx_tiles, ctx_row, GRID_W)
    pos = lax.broadcasted_iota(jnp.int32, (TM, 1), 0) & (rowlen - 1)
    u_prev = jnp.where(pos == 0, 0.0, pltpu.roll(u, 1, 0))
    u_next = jnp.where(pos == rowlen - 1, 0.0, pltpu.roll(u, TM - 1, 0))
    uc = u_prev * cw_ref[0:1, :] + u * cw_ref[1:2, :] + u_next * cw_ref[2:3, :] + cb_ref[...]
    yconv = _dot((zc[:, d:2 * d] * uc).astype(BF16), wco_ref[...])

    ycg_ref[...] = (jax.nn.sigmoid(proj(C_GA)) * yconv).astype(BF16)
    sgb_ref[...] = jax.nn.sigmoid(proj(C_GB)).astype(BF16)
    q_ref[...] = (proj(C_Q) * (D_QK ** -0.5)).astype(BF16)
    k_ref[...] = proj(C_K).astype(BF16)
    v_ref[...] = proj(C_V).astype(BF16)
    so_ref[...] = jax.nn.sigmoid(proj(C_O)).astype(BF16)
    gt_ref[...] = proj(C_G)


def _tri_masks():
    row = lax.broadcasted_iota(jnp.int32, (TM, TM), 0)
    col = lax.broadcasted_iota(jnp.int32, (TM, TM), 1)
    return row >= col, row <= col


def _gate_cumsums(gi, gf):
    lower, upper = _tri_masks()
    lf = jax.nn.log_sigmoid(gf)
    fwd_lane = lax.broadcasted_iota(jnp.int32, (1, LANES), 1) < N_HEADS
    bsum = jnp.where(fwd_lane, _dot(lower.astype(F32), lf, HIGHEST), _dot(upper.astype(F32), lf, HIGHEST))
    return lf, bsum, fwd_lane


def _state_kernel(*refs, zero_init, emit_before, emit_after):
    it = iter(refs)
    kf_ref, vf_ref, gf_ref, kb_ref, vb_ref, gb_ref = (next(it) for _ in range(6))
    if not zero_init:
        c0_ref, n0_ref, m0_ref = (next(it) for _ in range(3))
    if emit_before:
        cbf_ref, cbb_ref, nbf_ref, nbb_ref, mbf_ref, mbb_ref = (next(it) for _ in range(6))
    if emit_after:
        ca_ref, na_ref, ma_ref = (next(it) for _ in range(3))
    c_scr, n_scr, m_scr = (next(it) for _ in range(3))
    c = pl.program_id(1)

    @pl.when(c == 0)
    def _():
        if zero_init:
            c_scr[...] = jnp.zeros_like(c_scr)
            n_scr[...] = jnp.zeros_like(n_scr)
            m_scr[...] = jnp.zeros_like(m_scr)
        else:
            c_scr[...] = c0_ref[0]
            n_scr[...] = n0_ref[0]
            m_scr[...] = m0_ref[0]

    if emit_before:
        cbf_ref[0] = c_scr[0:N_HEADS]
        cbb_ref[0] = c_scr[N_HEADS:N_HD]
        nbf_ref[0] = n_scr[0:N_HEADS]
        nbb_ref[0] = n_scr[N_HEADS:N_HD]
        mbf_ref[0] = m_scr[...]
        mbb_ref[0] = m_scr[...]

    fwd_lane = lax.broadcasted_iota(jnp.int32, (1, LANES), 1) < N_HEADS
    gi = jnp.where(fwd_lane, gf_ref[:, 0:LANES], gb_ref[:, 0:LANES])
    gfg = jnp.where(fwd_lane, gf_ref[:, LANES:2 * LANES], gb_ref[:, LANES:2 * LANES])
    lf, bsum, _ = _gate_cumsums(gi, gfg)
    total = jnp.sum(lf, axis=0, keepdims=True)
    g = total - bsum + gi
    m_prev = m_scr[...]
    m_new = jnp.maximum(total + m_prev, jnp.max(g, axis=0, keepdims=True))
    wk = jnp.exp(g - m_new)
    decay = jnp.exp(total + m_prev - m_new)
    for hd in range(N_HD):
        h = hd % N_HEADS
        k_ref, v_ref = (kf_ref, vf_ref) if hd < N_HEADS else (kb_ref, vb_ref)
        wkk = wk[:, hd:hd + 1] * k_ref[:, h * LANES:(h + 1) * LANES].astype(F32)
        upd = lax.dot_general(wkk.astype(BF16), v_ref[:, h * D_V:(h + 1) * D_V],
                              (((0,), (0,)), ((), ())), preferred_element_type=F32)
        dec = decay[:, hd:hd + 1]
        c_scr[hd] = dec * c_scr[hd] + upd[0:D_QK, :]
        n_scr[hd:hd + 1, :] = dec * n_scr[hd:hd + 1, :] + jnp.sum(wkk, axis=0, keepdims=True)
    m_scr[...] = m_new

    if emit_after:
        @pl.when(c == pl.num_programs(1) - 1)
        def _():
            ca_ref[0] = c_scr[...]
            na_ref[0] = n_scr[...]
            ma_ref[0] = m_scr[...]


def _state_scan(k, v, gt, tile0, n_seq, n_chunk, init, emit_before, emit_after):
    d = k.shape[1]
    fwd = lambda s, c: (tile0 + s * n_chunk + c, 0)
    bwd = lambda s, c: (tile0 + s * n_chunk + n_chunk - 1 - c, 0)
    in_specs = [pl.BlockSpec((TM, d), fwd), pl.BlockSpec((TM, d), fwd), pl.BlockSpec((TM, 2 * LANES), fwd),
                pl.BlockSpec((TM, d), bwd), pl.BlockSpec((TM, d), bwd), pl.BlockSpec((TM, 2 * LANES), bwd)]
    args = [k, v, gt, k, v, gt]
    if init is not None:
        in_specs += [pl.BlockSpec((1, N_HD, D_QK, D_V), lambda s, c: (s, 0, 0, 0)),
                     pl.BlockSpec((1, N_HD, LANES), lambda s, c: (s, 0, 0)),
                     pl.BlockSpec((1, 1, LANES), lambda s, c: (s, 0, 0))]
        args += list(init)
    out_specs, out_shape = [], []
    n_tot = n_seq * n_chunk
    if emit_before:
        cf = lambda s, c: (s * n_chunk + c, 0, 0, 0)
        cb = lambda s, c: (s * n_chunk + n_chunk - 1 - c, 0, 0, 0)
        nf = lambda s, c: (s * n_chunk + c, 0, 0)
        nb = lambda s, c: (s * n_chunk + n_chunk - 1 - c, 0, 0)
        out_specs += [pl.BlockSpec((1, N_HEADS, D_QK, D_V), cf), pl.BlockSpec((1, N_HEADS, D_QK, D_V), cb),
                      pl.BlockSpec((1, N_HEADS, LANES), nf), pl.BlockSpec((1, N_HEADS, LANES), nb),
                      pl.BlockSpec((1, 1, LANES), nf), pl.BlockSpec((1, 1, LANES), nb)]
        out_shape += [jax.ShapeDtypeStruct((n_tot, N_HEADS, D_QK, D_V), F32)] * 2
        out_shape += [jax.ShapeDtypeStruct((n_tot, N_HEADS, LANES), F32)] * 2
        out_shape += [jax.ShapeDtypeStruct((n_tot, 1, LANES), F32)] * 2
    if emit_after:
        out_specs += [pl.BlockSpec((1, N_HD, D_QK, D_V), lambda s, c: (s, 0, 0, 0)),
                      pl.BlockSpec((1, N_HD, LANES), lambda s, c: (s, 0, 0)),
                      pl.BlockSpec((1, 1, LANES), lambda s, c: (s, 0, 0))]
        out_shape += [jax.ShapeDtypeStruct((n_seq, N_HD, D_QK, D_V), F32),
                      jax.ShapeDtypeStruct((n_seq, N_HD, LANES), F32),
                      jax.ShapeDtypeStruct((n_seq, 1, LANES), F32)]
    return pl.pallas_call(
        functools.partial(_state_kernel, zero_init=init is None, emit_before=emit_before, emit_after=emit_after),
        grid=(n_seq, n_chunk),
        in_specs=in_specs,
        out_specs=out_specs,
        out_shape=out_shape,
        scratch_shapes=[pltpu.VMEM((N_HD, D_QK, D_V), F32), pltpu.VMEM((N_HD, LANES), F32),
                        pltpu.VMEM((1, LANES), F32)],
        compiler_params=pltpu.CompilerParams(dimension_semantics=("arbitrary", "arbitrary"),
                                             vmem_limit_bytes=VMEM_LIMIT),
        name="mlstm_state_scan",
    )(*args)


def _mix_kernel(q_ref, k_ref, v_ref, so_ref, sgb_ref, ycg_ref, gt_ref, x_ref, mod_ref,
                cbf_ref, cbb_ref, nbf_ref, nbb_ref, mbf_ref, mbb_ref,
                gmh_ref, wmo_ref, wo_ref, g2_ref, rw_ref, rb_ref,
                x1_ref, h2_ref, route_ref, routet_ref, cnt_ref, hm_scr, carry_scr, *, n_ctx_tiles, n_experts):
    i = pl.program_id(0)
    is_lat = i >= n_ctx_tiles
    lat_f = is_lat.astype(F32)
    lower, upper = _tri_masks()
    gi = gt_ref[:, 0:LANES]
    lf, bsum, fwd_lane = _gate_cumsums(gi, gt_ref[:, LANES:2 * LANES])
    a = gi - bsum
    t_idx = lax.broadcasted_iota(jnp.int32, (TM, LANES), 0)
    pmax, smax = a, a
    step = 1
    while step < TM:
        pmax = jnp.maximum(pmax, jnp.where(t_idx >= step, pltpu.roll(pmax, step, 0), -jnp.inf))
        smax = jnp.maximum(smax, jnp.where(t_idx < TM - step, pltpu.roll(smax, TM - step, 0), -jnp.inf))
        step *= 2
    m_prev = jnp.where(fwd_lane, mbf_ref[0], mbb_ref[0]) * lat_f
    mrow = jnp.maximum(m_prev, jnp.where(fwd_lane, pmax, smax))
    w_inter = jnp.exp(m_prev - mrow)
    e_floor = jnp.exp(-(bsum + mrow))
    a_t = a.T
    n_all = jnp.concatenate([nbf_ref[0], nbb_ref[0], jnp.zeros((LANES - N_HD, LANES), F32)], axis=0) * lat_f
    n_t = n_all.T
    lane0 = lax.broadcasted_iota(jnp.int32, (1, LANES), 1) == 0
    ones_col = jnp.where(lane0, 1.0, 0.0).astype(BF16) * jnp.ones((TM, 1), BF16)
    zpad = jnp.zeros((LANES - D_QK, D_V), F32)

    for h in range(N_HEADS):
        qh = q_ref[:, h * LANES:(h + 1) * LANES]
        kh = k_ref[:, h * LANES:(h + 1) * LANES]
        vext = jnp.concatenate([v_ref[:, h * D_V:(h + 1) * D_V], ones_col], axis=1)
        qk = lax.dot_general(qh, kh, (((1,), (1,)), ((), ())), preferred_element_type=F32)
        hsum = None
        for d in range(2):
            hd = d * N_HEADS + h
            mask = lower if d == 0 else upper
            c_prev = (cbf_ref if d == 0 else cbb_ref)[0, h]
            e = jnp.exp(jnp.where(mask, a_t[hd:hd + 1, :] - mrow[:, hd:hd + 1], -jnp.inf))
            nd = _dot((qk * e).astype(BF16), vext)
            cext = jnp.concatenate(
                [jnp.concatenate([c_prev * lat_f, zpad], axis=0),
                 jnp.where(lane0, n_t[:, hd:hd + 1], 0.0)], axis=1).astype(BF16)
            qc = _dot(qh, cext)
            wi = w_inter[:, hd:hd + 1]
            num = nd[:, 0:D_V] + wi * qc[:, 0:D_V]
            den = nd[:, D_V:D_V + 1] + wi * qc[:, D_V:D_V + 1]
            r = 1.0 / jnp.maximum(jnp.abs(den), e_floor[:, hd:hd + 1])
            hsum = num * r if hsum is None else hsum + num * r
        hs = slice(h * D_V, (h + 1) * D_V)
        hm_scr[:, hs] = (_rms(hsum) * gmh_ref[:, hs] * so_ref[:, hs].astype(F32)).astype(BF16)

    ym = _dot(hm_scr[...], wmo_ref[...])
    mix = (ycg_ref[...].astype(F32) + sgb_ref[...].astype(F32) * ym).astype(BF16)
    x1 = x_ref[...] + mod_ref[0, 2:3, :] * _dot(mix, wo_ref[...])
    x1_ref[...] = x1
    h2 = _rms(x1) * g2_ref[...] * (1.0 + mod_ref[0, 4:5, :]) + mod_ref[0, 3:4, :]
    h2_ref[...] = h2

    lane = lax.broadcasted_iota(jnp.int32, (TM, LANES), 1)
    work = jnp.where(lane < n_experts, _dot(h2, rw_ref[...], HIGHEST) + rb_ref[...], -jnp.inf)
    sels, exps, idxs = [], [], []
    top = None
    for _ in range(TOP_K):
        mx = jnp.max(work, axis=-1, keepdims=True)
        ix = jnp.min(jnp.where(work == mx, lane, LANES), axis=-1, keepdims=True)
        sel = lane == ix
        work = jnp.where(sel, -jnp.inf, work)
        top = mx if top is None else top
        sels.append(sel)
        idxs.append(ix.astype(F32))
        exps.append(jnp.exp(mx - top))
    inv = 1.0 / functools.reduce(lambda p, q: p + q, exps)

    @pl.when(i == 0)
    def _():
        carry_scr[...] = jnp.zeros_like(carry_scr)

    onehot = functools.reduce(lambda p, q: p + q, [jnp.where(s, 1.0, 0.0) for s in sels])
    row = lax.broadcasted_iota(jnp.int32, (TM, TM), 0)
    col = lax.broadcasted_iota(jnp.int32, (TM, TM), 1)
    before = _dot((row > col).astype(BF16), onehot.astype(BF16)) + carry_scr[...]
    carry_scr[...] += jnp.sum(onehot, axis=0, keepdims=True)
    cnt_ref[...] = carry_scr[...]
    route = jnp.zeros((TM, LANES), F32)
    for j in range(TOP_K):
        slot = jnp.sum(jnp.where(sels[j], before, 0.0), axis=-1, keepdims=True)
        route = jnp.where(lane == j, idxs[j], route)
        route = jnp.where(lane == TOP_K + j, slot, route)
        route = jnp.where(lane == 2 * TOP_K + j, exps[j] * inv, route)
    route_ref[...] = route
    routet_ref[...] = route.T[0:8, :].astype(jnp.int32)


def _dispatch_kernel(pos_ref, h2_ref, xs_ref, sem):
    def copy(t, j):
        return pltpu.make_async_copy(h2_ref.at[pl.ds(t, 1)], xs_ref.at[pl.ds(pos_ref[j, t], 1)], sem)

    def start(t, carry):
        for j in range(TOP_K):
            copy(t, j).start()
        return carry

    def wait(t, carry):
        for j in range(TOP_K):
            copy(t, j).wait()
        return carry

    lax.fori_loop(0, TM, start, 0)
    lax.fori_loop(0, TM, wait, 0)


def _expert_kernel(tile_ref, exp_ref, flag_ref, lo_ref, hi_ref,
                   xs_ref, wg_ref, bg_ref, wl_ref, bl_ref, wd_ref, bd_ref, ys_ref, w_scr):
    w = pl.program_id(0)
    flags = flag_ref[w]

    @pl.when((flags & 4) != 0)
    def _():
        w_scr[0] = wg_ref[0].astype(BF16)
        w_scr[1] = wl_ref[0].astype(BF16)
        w_scr[2] = wd_ref[0].astype(BF16)

    @pl.when((flags & 1) != 0)
    def _():
        x = xs_ref[...].astype(BF16)
        gt = jnp.minimum(_dot(x, w_scr[0]) + bg_ref[0], SWIGLU_LIMIT)
        lin = jnp.clip(_dot(x, w_scr[1]) + bl_ref[0], -SWIGLU_LIMIT, SWIGLU_LIMIT)
        act = gt * jax.nn.sigmoid(SWIGLU_ALPHA * gt) * (lin + 1.0)
        y = _dot(act.astype(BF16), w_scr[2]) + bd_ref[0]
        rows = lax.broadcasted_iota(jnp.int32, (xs_ref.shape[0], 1), 0)
        mine = (rows >= lo_ref[w]) & (rows < hi_ref[w])

        @pl.when((flags & 2) != 0)
        def _():
            ys_ref[...] = jnp.where(mine, y, 0.0)

        @pl.when((flags & 2) == 0)
        def _():
            ys_ref[...] = jnp.where(mine, y, ys_ref[...])


def _combine_kernel(pos_ref, posn_ref, route_ref, x1_ref, mod_ref, fg_ref, ys_ref, outc_ref, outl_ref,
                    buf, sem, *, n_ctx_tiles):
    i = pl.program_id(0)
    n = pl.num_programs(0)

    def copy(p_ref, slot, t, j):
        return pltpu.make_async_copy(ys_ref.at[pl.ds(p_ref[j, t], 1)], buf.at[slot, j, pl.ds(t, 1)], sem.at[slot])

    def start_all(p_ref, slot):
        def body(t, carry):
            for j in range(TOP_K):
                copy(p_ref, slot, t, j).start()
            return carry
        lax.fori_loop(0, TM, body, 0)

    @pl.when(i == 0)
    def _():
        start_all(pos_ref, 0)

    @pl.when(i + 1 < n)
    def _():
        start_all(posn_ref, (i + 1) % 2)

    slot = i % 2

    def wait(t, carry):
        for j in range(TOP_K):
            copy(pos_ref, slot, t, j).wait()
        return carry

    lax.fori_loop(0, TM, wait, 0)
    acc = None
    for j in range(TOP_K):
        term = route_ref[:, 2 * TOP_K + j:2 * TOP_K + j + 1] * buf[slot, j]
        acc = term if acc is None else acc + term
    out = _rms(x1_ref[...] + mod_ref[0, 5:6, :] * acc) * fg_ref[...]

    @pl.when(i < n_ctx_tiles)
    def _():
        outc_ref[...] = out

    @pl.when(i >= n_ctx_tiles)
    def _():
        outl_ref[...] = out


def _work_items(counts, n_experts, n_rows, tmx):
    n_items_max = n_rows // tmx + n_experts - 1
    cnt = counts.astype(jnp.int32)
    offs = jnp.concatenate([jnp.zeros((1,), jnp.int32), jnp.cumsum(cnt)])
    first_tile = offs[:-1] // tmx
    n_it = jnp.where(cnt > 0, (offs[1:] - 1) // tmx - first_tile + 1, 0)
    it_start = jnp.concatenate([jnp.zeros((1,), jnp.int32), jnp.cumsum(n_it)])
    total = it_start[-1]
    w = jnp.arange(n_items_max, dtype=jnp.int32)
    wc = jnp.minimum(w, total - 1)
    e = jnp.sum((it_start[None, 1:] <= wc[:, None]).astype(jnp.int32), axis=1)
    e = jnp.minimum(e, n_experts - 1)
    tile = first_tile[e] + wc - it_start[e]
    valid = w < total
    prev = lambda a: jnp.concatenate([jnp.full((1,), -1, jnp.int32), a[:-1]])
    flags = (valid.astype(jnp.int32) + 2 * (valid & (tile != prev(tile))).astype(jnp.int32)
             + 4 * (valid & (e != prev(e))).astype(jnp.int32))
    lo = jnp.clip(offs[e] - tile * tmx, 0, tmx)
    hi = jnp.clip(offs[e + 1] - tile * tmx, 0, tmx)
    return offs, tile, e, flags, lo, hi


def _pack_in_proj(w, b):
    d = w.shape[0]
    o_q, o_k, o_v, o_o, o_g, o_ga, o_gb = 3 * d, 3 * d + 512, 3 * d + 1024, 4 * d + 1024, 5 * d + 1024, \
        5 * d + 1024 + 4 * N_HEADS, 6 * d + 1024 + 4 * N_HEADS

    def pad_heads(m):
        m = m.reshape(m.shape[0], N_HEADS, D_QK)
        return jnp.pad(m, ((0, 0), (0, 0), (0, LANES - D_QK))).reshape(m.shape[0], N_HEADS * LANES)

    def gates(m):
        gz = jnp.zeros((m.shape[0], LANES - N_HD), m.dtype)
        i_f, f_f, i_b, f_b = (m[:, o_g + j * N_HEADS:o_g + (j + 1) * N_HEADS] for j in range(4))
        return jnp.concatenate([i_f, i_b, gz, f_f, f_b, gz], axis=1)

    def pack(m):
        return jnp.concatenate([m[:, 0:o_q], pad_heads(m[:, o_q:o_k]), pad_heads(m[:, o_k:o_v]), m[:, o_v:o_o],
                                m[:, o_o:o_g], m[:, o_ga:o_gb], m[:, o_gb:o_gb + d], gates(m)], axis=1)

    return pack(w).astype(BF16), pack(b.reshape(1, -1))


def kernel(x_prompt, x_sample, c, state_C, state_n, state_m, c_ctx, ada_w, ada_b, norm1_g, norm2_g, w_in, b_in,
           conv_w, conv_b, w_conv_out, mh_norm_g, w_m_out, w_o, router_w, router_b, w_gate, b_gate, w_lin, b_lin,
           w_down, b_down, final_g):
    nb, seq, d = x_prompt.shape
    nd, dseq, _ = x_sample.shape
    n_experts = w_gate.shape[1]
    assert d == _D and w_in.shape[0] == 1 and seq == TM and dseq % TM == 0 and TM % GRID_W == 0
    t_ctx, t_lat = nb * seq, nd * dseq
    n_tok = t_ctx + t_lat
    n_ctx_tiles, n_tiles = t_ctx // TM, n_tok // TM
    lat_chunks = dseq // TM
    n_lat_tiles = n_tiles - n_ctx_tiles

    n_c = 1 + nd
    n_cp = -(-n_c // 8) * 8
    cvec = jnp.concatenate([c_ctx[None, :], c, jnp.zeros((n_cp - n_c, d), F32)], axis=0)
    mod = _modulation(cvec, ada_w[0], ada_b[0]).reshape(n_cp, N_MOD, d)
    mod = jnp.pad(mod, ((0, 0), (0, 8 - N_MOD), (0, 0)))

    def mod_row(tile_tokens):
        ctx_t, per_seq = t_ctx // tile_tokens, dseq // tile_tokens
        return lambda i, *_: (jnp.where(i < ctx_t, 0, 1 + (i - ctx_t) // per_seq), 0, 0)

    x_all = jnp.concatenate([x_prompt.reshape(t_ctx, d), x_sample.reshape(t_lat, d)], axis=0)
    w_all, b_all = _pack_in_proj(w_in[0], b_in[0])
    tile = lambda cols: pl.BlockSpec((TM, cols), lambda i: (i, 0))
    params = pltpu.CompilerParams(dimension_semantics=("arbitrary",), vmem_limit_bytes=VMEM_LIMIT)
    bf = lambda cols: jax.ShapeDtypeStruct((n_tok, cols), BF16)

    ycg, sgb, q, k, v, so, gt = pl.pallas_call(
        functools.partial(_proj_kernel, n_ctx_tiles=n_ctx_tiles, ctx_row=seq),
        grid=(n_tiles,),
        in_specs=[tile(d), pl.BlockSpec((1, 8, d), mod_row(TM)), _const_spec((1, d)),
                  _const_spec((d, W_COLS)), _const_spec((1, W_COLS)), _const_spec((3, d)), _const_spec((1, d)),
                  _const_spec((d, d))],
        out_specs=[tile(d), tile(d), tile(d), tile(d), tile(d), tile(d), tile(2 * LANES)],
        out_shape=[bf(d), bf(d), bf(d), bf(d), bf(d), bf(d), jax.ShapeDtypeStruct((n_tok, 2 * LANES), F32)],
        compiler_params=params,
        name="in_proj_conv",
    )(x_all, mod, norm1_g, w_all, b_all, conv_w[0], conv_b, w_conv_out[0].astype(BF16))

    c_new, n_new, m_new = _state_scan(k, v, gt, 0, nb, 1, None, False, True)
    init = (state_C[:, 0].astype(F32).reshape(nd, N_HD, D_QK, D_V),
            jnp.pad(state_n[:, 0].astype(F32).reshape(nd, N_HD, D_QK), ((0, 0), (0, 0), (0, LANES - D_QK))),
            jnp.pad(state_m[:, 0].astype(F32).reshape(nd, 1, N_HD), ((0, 0), (0, 0), (0, LANES - N_HD))))
    cbf, cbb, nbf, nbb, mbf, mbb = _state_scan(k, v, gt, n_ctx_tiles, nd, lat_chunks, init, True, False)

    lat_idx = lambda i: jnp.maximum(i - n_ctx_tiles, 0)
    st4 = pl.BlockSpec((1, N_HEADS, D_QK, D_V), lambda i: (lat_idx(i), 0, 0, 0))
    st3 = pl.BlockSpec((1, N_HEADS, LANES), lambda i: (lat_idx(i), 0, 0))
    st1 = pl.BlockSpec((1, 1, LANES), lambda i: (lat_idx(i), 0, 0))
    rw = jnp.pad(router_w[0], ((0, 0), (0, LANES - n_experts)))
    rb = jnp.pad(router_b[0], (0, LANES - n_experts)).reshape(1, LANES)
    x1, h2, route, route_t, counts = pl.pallas_call(
        functools.partial(_mix_kernel, n_ctx_tiles=n_ctx_tiles, n_experts=n_experts),
        grid=(n_tiles,),
        in_specs=[tile(d), tile(d), tile(d), tile(d), tile(d), tile(d), tile(2 * LANES), tile(d),
                  pl.BlockSpec((1, 8, d), mod_row(TM)), st4, st4, st3, st3, st1, st1,
                  _const_spec((1, d)), _const_spec((d, d)), _const_spec((d, d)), _const_spec((1, d)),
                  _const_spec((d, LANES)), _const_spec((1, LANES))],
        out_specs=[tile(d), tile(d), tile(LANES), pl.BlockSpec((8, TM), lambda i: (0, i)),
                   pl.BlockSpec((1, LANES), lambda i: (0, 0))],
        out_shape=[jax.ShapeDtypeStruct((n_tok, d), F32), jax.ShapeDtypeStruct((n_tok, d), F32),
                   jax.ShapeDtypeStruct((n_tok, LANES), F32), jax.ShapeDtypeStruct((8, n_tok), jnp.int32),
                   jax.ShapeDtypeStruct((1, LANES), F32)],
        scratch_shapes=[pltpu.VMEM((TM, d), BF16), pltpu.VMEM((1, LANES), F32)],
        compiler_params=params,
        name="mlstm_mix_router",
    )(q, k, v, so, sgb, ycg, gt, x_all, mod, cbf, cbb, nbf, nbb, mbf, mbb,
      mh_norm_g, w_m_out[0].astype(BF16), w_o[0].astype(BF16), norm2_g, rw, rb)

    n_rows = TOP_K * n_tok
    offs, it_tile, it_exp, it_flags, it_lo, it_hi = _work_items(counts[0, :n_experts], n_experts, n_rows, TMX)
    pos_t = offs[route_t[0:TOP_K]] + route_t[TOP_K:2 * TOP_K]
    pos_spec = lambda f: pl.BlockSpec((TOP_K, TM), f, memory_space=pltpu.SMEM)
    any_spec = pl.BlockSpec(memory_space=pl.ANY)

    xs = pl.pallas_call(
        _dispatch_kernel,
        grid=(n_tiles,),
        in_specs=[pos_spec(lambda i: (0, i)), tile(d)],
        out_specs=any_spec,
        out_shape=jax.ShapeDtypeStruct((n_rows, d), F32),
        scratch_shapes=[pltpu.SemaphoreType.DMA],
        compiler_params=params,
        name="moe_dispatch",
    )(pos_t, h2)

    dff = w_gate.shape[-1]
    wspec = lambda a, b_: pl.BlockSpec((1, a, b_), lambda w, tl, ex, *_: (ex[w], 0, 0))
    ys = pl.pallas_call(
        _expert_kernel,
        grid_spec=pltpu.PrefetchScalarGridSpec(
            num_scalar_prefetch=5,
            grid=(it_tile.shape[0],),
            in_specs=[pl.BlockSpec((TMX, d), lambda w, tl, *_: (tl[w], 0)),
                      wspec(d, dff), wspec(1, dff), wspec(d, dff), wspec(1, dff), wspec(dff, d), wspec(1, d)],
            out_specs=pl.BlockSpec((TMX, d), lambda w, tl, *_: (tl[w], 0)),
            scratch_shapes=[pltpu.VMEM((3, d, dff), BF16)]),
        out_shape=jax.ShapeDtypeStruct((n_rows, d), F32),
        compiler_params=params,
        name="moe_experts",
    )(it_tile, it_exp, it_flags, it_lo, it_hi, xs, w_gate[0], b_gate[0].reshape(n_experts, 1, dff), w_lin[0],
      b_lin[0].reshape(n_experts, 1, dff), w_down[0], b_down[0].reshape(n_experts, 1, d))

    ctx_i = lambda i: (jnp.minimum(i, n_ctx_tiles - 1), 0)
    lat_i = lambda i: (jnp.maximum(i - n_ctx_tiles, 0), 0)
    y_prompt, y_sample = pl.pallas_call(
        functools.partial(_combine_kernel, n_ctx_tiles=n_ctx_tiles),
        grid=(n_tiles,),
        in_specs=[pos_spec(lambda i: (0, i)), pos_spec(lambda i: (0, jnp.minimum(i + 1, n_tiles - 1))),
                  tile(LANES), tile(d), pl.BlockSpec((1, 8, d), mod_row(TM)), _const_spec((1, d)), any_spec],
        out_specs=[pl.BlockSpec((TM, d), ctx_i), pl.BlockSpec((TM, d), lat_i)],
        out_shape=[jax.ShapeDtypeStruct((t_ctx, d), F32), jax.ShapeDtypeStruct((t_lat, d), F32)],
        scratch_shapes=[pltpu.VMEM((2, TOP_K, TM, d), F32), pltpu.SemaphoreType.DMA((2,))],
        compiler_params=params,
        name="moe_combine",
    )(pos_t, pos_t, route, x1, mod, final_g.reshape(1, d), ys)

    y_prompt = y_prompt.reshape(nb, seq, d)
    y_sample = y_sample.reshape(nd, dseq, d)
    new_c = c_new.reshape(nb, 1, 2, N_HEADS, D_QK, D_V)
    new_n = n_new[:, :, :D_QK].reshape(nb, 1, 2, N_HEADS, D_QK)
    new_m = m_new[:, 0, :N_HD].reshape(nb, 1, 2, N_HEADS)
    return (y_prompt, y_sample, new_c, new_n, new_m)
```

```python
import functools

import jax
import jax.numpy as jnp
from jax import lax
from jax.experimental import pallas as pl
from jax.experimental.pallas import tpu as pltpu

F32 = jnp.float32
BF16 = jnp.bfloat16
HIGHEST = lax.Precision.HIGHEST

N_HEADS = 8
D_QK = 64
D_V = 128
GRID_W = 64
TOP_K = 4
SWIGLU_LIMIT = 7.0
SWIGLU_ALPHA = 1.702
EPS = 1e-6
N_MOD = 6

LANES = 128
TM = 256
TMX = 512
N_HD = 2 * N_HEADS
VMEM_LIMIT = 56 * 1024 * 1024

_D = 1024
C_CONV = (0, 3 * _D)
C_Q = (3 * _D, 4 * _D)
C_K = (4 * _D, 5 * _D)
C_V = (5 * _D, 6 * _D)
C_O = (6 * _D, 7 * _D)
C_GA = (7 * _D, 8 * _D)
C_GB = (8 * _D, 9 * _D)
C_G = (9 * _D, 9 * _D + 2 * LANES)
W_COLS = C_G[1]


def _dot(a, b, precision=None):
    return jnp.dot(a, b, preferred_element_type=F32, precision=precision)


def _rms(x):
    return x * lax.rsqrt(jnp.mean(x * x, axis=-1, keepdims=True) + EPS)


def _const_spec(shape):
    return pl.BlockSpec(shape, lambda *_: (0,) * len(shape), pipeline_mode=pl.Buffered(1))


def _mod_kernel(c_ref, w_ref, b_ref, o_ref):
    c = c_ref[...]
    o_ref[...] = _dot(c * jax.nn.sigmoid(c), w_ref[...], HIGHEST) + b_ref[...]


def _modulation(cvec, ada_w, ada_b):
    n, d = cvec.shape
    nout = ada_w.shape[1]
    return pl.pallas_call(
        _mod_kernel,
        grid=(nout // d,),
        in_specs=[pl.BlockSpec((n, d), lambda j: (0, 0)),
                  pl.BlockSpec((d, d), lambda j: (0, j)),
                  pl.BlockSpec((1, d), lambda j: (0, j))],
        out_specs=pl.BlockSpec((n, d), lambda j: (0, j)),
        out_shape=jax.ShapeDtypeStruct((n, nout), F32),
        name="adaln_mod",
    )(cvec, ada_w, ada_b.reshape(1, nout))


def _proj_kernel(xc_ref, xl_ref, mod_ref, g1_ref, w_ref, b_ref, cw_ref, cb_ref, wco_ref,
                 ycg_ref, sgb_ref, q_ref, k_ref, v_ref, so_ref, gt_ref, *, n_ctx_tiles, ctx_row):
    i = pl.program_id(0)
    x = jnp.where(i < n_ctx_tiles, xc_ref[...], xl_ref[...])
    h = (_rms(x) * g1_ref[...] * (1.0 + mod_ref[0, 1:2, :]) + mod_ref[0, 0:1, :]).astype(BF16)

    def proj(cols):
        return _dot(h, w_ref[:, cols[0]:cols[1]]) + b_ref[:, cols[0]:cols[1]]

    zc = proj(C_CONV)
    d = x.shape[1]
    u = zc[:, 2 * d:3 * d] * zc[:, 0:d]
    rowlen = jnp.where(i < n_ctx_tiles, ctx_row, GRID_W)
    pos = lax.broadcasted_iota(jnp.int32, (TM, 1), 0) & (rowlen - 1)
    u_prev = jnp.where(pos == 0, 0.0, pltpu.roll(u, 1, 0))
    u_next = jnp.where(pos == rowlen - 1, 0.0, pltpu.roll(u, TM - 1, 0))
    uc = u_prev * cw_ref[0:1, :] + u * cw_ref[1:2, :] + u_next * cw_ref[2:3, :] + cb_ref[...]
    yconv = _dot((zc[:, d:2 * d] * uc).astype(BF16), wco_ref[...])

    ycg_ref[...] = (jax.nn.sigmoid(proj(C_GA)) * yconv).astype(BF16)
    sgb_ref[...] = jax.nn.sigmoid(proj(C_GB)).astype(BF16)
    q_ref[...] = (proj(C_Q) * (D_QK ** -0.5)).astype(BF16)
    k_ref[...] = proj(C_K).astype(BF16)
    v_ref[...] = proj(C_V).astype(BF16)
    so_ref[...] = jax.nn.sigmoid(proj(C_O)).astype(BF16)
    gt_ref[...] = proj(C_G)


def _tri_masks():
    row = lax.broadcasted_iota(jnp.int32, (TM, TM), 0)
    col = lax.broadcasted_iota(jnp.int32, (TM, TM), 1)
    return row >= col, row <= col


def _gate_cumsums(gi, gf):
    lower, upper = _tri_masks()
    lf = jax.nn.log_sigmoid(gf)
    fwd_lane = lax.broadcasted_iota(jnp.int32, (1, LANES), 1) < N_HEADS
    bsum = jnp.where(fwd_lane, _dot(lower.astype(F32), lf, HIGHEST), _dot(upper.astype(F32), lf, HIGHEST))
    return lf, bsum, fwd_lane


def _state_kernel(*refs, zero_init, emit_before, emit_after):
    it = iter(refs)
    kf_ref, vf_ref, gf_ref, kb_ref, vb_ref, gb_ref = (next(it) for _ in range(6))
    if not zero_init:
        c0_ref, n0_ref, m0_ref = (next(it) for _ in range(3))
    if emit_before:
        cbf_ref, cbb_ref, nbf_ref, nbb_ref, mbf_ref, mbb_ref = (next(it) for _ in range(6))
    if emit_after:
        ca_ref, na_ref, ma_ref = (next(it) for _ in range(3))
    c_scr, n_scr, m_scr = (next(it) for _ in range(3))
    c = pl.program_id(1)

    @pl.when(c == 0)
    def _():
        if zero_init:
            c_scr[...] = jnp.zeros_like(c_scr)
            n_scr[...] = jnp.zeros_like(n_scr)
            m_scr[...] = jnp.zeros_like(m_scr)
        else:
            c_scr[...] = c0_ref[0]
            n_scr[...] = n0_ref[0]
            m_scr[...] = m0_ref[0]

    if emit_before:
        cbf_ref[0] = c_scr[0:N_HEADS]
        cbb_ref[0] = c_scr[N_HEADS:N_HD]
        nbf_ref[0] = n_scr[0:N_HEADS]
        nbb_ref[0] = n_scr[N_HEADS:N_HD]
        mbf_ref[0] = m_scr[...]
        mbb_ref[0] = m_scr[...]

    fwd_lane = lax.broadcasted_iota(jnp.int32, (1, LANES), 1) < N_HEADS
    gi = jnp.where(fwd_lane, gf_ref[:, 0:LANES], gb_ref[:, 0:LANES])
    gfg = jnp.where(fwd_lane, gf_ref[:, LANES:2 * LANES], gb_ref[:, LANES:2 * LANES])
    lf, bsum, _ = _gate_cumsums(gi, gfg)
    total = jnp.sum(lf, axis=0, keepdims=True)
    g = total - bsum + gi
    m_prev = m_scr[...]
    m_new = jnp.maximum(total + m_prev, jnp.max(g, axis=0, keepdims=True))
    wk = jnp.exp(g - m_new)
    decay = jnp.exp(total + m_prev - m_new)
    for hd in range(N_HD):
        h = hd % N_HEADS
        k_ref, v_ref = (kf_ref, vf_ref) if hd < N_HEADS else (kb_ref, vb_ref)
        wkk = wk[:, hd:hd + 1] * k_ref[:, h * LANES:(h + 1) * LANES].astype(F32)
        upd = lax.dot_general(wkk.astype(BF16), v_ref[:, h * D_V:(h + 1) * D_V],
                              (((0,), (0,)), ((), ())), preferred_element_type=F32)
        dec = decay[:, hd:hd + 1]
        c_scr[hd] = dec * c_scr[hd] + upd[0:D_QK, :]
        n_scr[hd:hd + 1, :] = dec * n_scr[hd:hd + 1, :] + jnp.sum(wkk, axis=0, keepdims=True)
    m_scr[...] = m_new

    if emit_after:
        @pl.when(c == pl.num_programs(1) - 1)
        def _():
            ca_ref[0] = c_scr[...]
            na_ref[0] = n_scr[...]
            ma_ref[0] = m_scr[...]


def _state_scan(k, v, gt, tile0, n_seq, n_chunk, init, emit_before, emit_after):
    d = k.shape[1]
    fwd = lambda s, c: (tile0 + s * n_chunk + c, 0)
    bwd = lambda s, c: (tile0 + s * n_chunk + n_chunk - 1 - c, 0)
    in_specs = [pl.BlockSpec((TM, d), fwd), pl.BlockSpec((TM, d), fwd), pl.BlockSpec((TM, 2 * LANES), fwd),
                pl.BlockSpec((TM, d), bwd), pl.BlockSpec((TM, d), bwd), pl.BlockSpec((TM, 2 * LANES), bwd)]
    args = [k, v, gt, k, v, gt]
    if init is not None:
        in_specs += [pl.BlockSpec((1, N_HD, D_QK, D_V), lambda s, c: (s, 0, 0, 0)),
                     pl.BlockSpec((1, N_HD, LANES), lambda s, c: (s, 0, 0)),
                     pl.BlockSpec((1, 1, LANES), lambda s, c: (s, 0, 0))]
        args += list(init)
    out_specs, out_shape = [], []
    n_tot = n_seq * n_chunk
    if emit_before:
        cf = lambda s, c: (s * n_chunk + c, 0, 0, 0)
        cb = lambda s, c: (s * n_chunk + n_chunk - 1 - c, 0, 0, 0)
        nf = lambda s, c: (s * n_chunk + c, 0, 0)
        nb = lambda s, c: (s * n_chunk + n_chunk - 1 - c, 0, 0)
        out_specs += [pl.BlockSpec((1, N_HEADS, D_QK, D_V), cf), pl.BlockSpec((1, N_HEADS, D_QK, D_V), cb),
                      pl.BlockSpec((1, N_HEADS, LANES), nf), pl.BlockSpec((1, N_HEADS, LANES), nb),
                      pl.BlockSpec((1, 1, LANES), nf), pl.BlockSpec((1, 1, LANES), nb)]
        out_shape += [jax.ShapeDtypeStruct((n_tot, N_HEADS, D_QK, D_V), F32)] * 2
        out_shape += [jax.ShapeDtypeStruct((n_tot, N_HEADS, LANES), F32)] * 2
        out_shape += [jax.ShapeDtypeStruct((n_tot, 1, LANES), F32)] * 2
    if emit_after:
        out_specs += [pl.BlockSpec((1, N_HD, D_QK, D_V), lambda s, c: (s, 0, 0, 0)),
                      pl.BlockSpec((1, N_HD, LANES), lambda s, c: (s, 0, 0)),
                      pl.BlockSpec((1, 1, LANES), lambda s, c: (s, 0, 0))]
        out_shape += [jax.ShapeDtypeStruct((n_seq, N_HD, D_QK, D_V), F32),
                      jax.ShapeDtypeStruct((n_seq, N_HD, LANES), F32),
                      jax.ShapeDtypeStruct((n_seq, 1, LANES), F32)]
    return pl.pallas_call(
        functools.partial(_state_kernel, zero_init=init is None, emit_before=emit_before, emit_after=emit_after),
        grid=(n_seq, n_chunk),
        in_specs=in_specs,
        out_specs=out_specs,
        out_shape=out_shape,
        scratch_shapes=[pltpu.VMEM((N_HD, D_QK, D_V), F32), pltpu.VMEM((N_HD, LANES), F32),
                        pltpu.VMEM((1, LANES), F32)],
        compiler_params=pltpu.CompilerParams(dimension_semantics=("arbitrary", "arbitrary"),
                                             vmem_limit_bytes=VMEM_LIMIT),
        name="mlstm_state_scan",
    )(*args)


def _mix_kernel(q_ref, k_ref, v_ref, so_ref, sgb_ref, ycg_ref, gt_ref, xc_ref, xl_ref, mod_ref,
                cbf_ref, cbb_ref, nbf_ref, nbb_ref, mbf_ref, mbb_ref,
                gmh_ref, wmo_ref, wo_ref, g2_ref, rw_ref, rb_ref,
                x1_ref, h2_ref, route_ref, routet_ref, cnt_ref, hm_scr, carry_scr, *, n_ctx_tiles, n_experts):
    i = pl.program_id(0)
    is_lat = i >= n_ctx_tiles
    lat_f = is_lat.astype(F32)
    lower, upper = _tri_masks()
    gi = gt_ref[:, 0:LANES]
    lf, bsum, fwd_lane = _gate_cumsums(gi, gt_ref[:, LANES:2 * LANES])
    a = gi - bsum
    t_idx = lax.broadcasted_iota(jnp.int32, (TM, LANES), 0)
    pmax, smax = a, a
    step = 1
    while step < TM:
        pmax = jnp.maximum(pmax, jnp.where(t_idx >= step, pltpu.roll(pmax, step, 0), -jnp.inf))
        smax = jnp.maximum(smax, jnp.where(t_idx < TM - step, pltpu.roll(smax, TM - step, 0), -jnp.inf))
        step *= 2
    m_prev = jnp.where(fwd_lane, mbf_ref[0], mbb_ref[0]) * lat_f
    mrow = jnp.maximum(m_prev, jnp.where(fwd_lane, pmax, smax))
    w_inter = jnp.exp(m_prev - mrow)
    e_floor = jnp.exp(-(bsum + mrow))
    a_t = a.T
    n_all = jnp.concatenate([nbf_ref[0], nbb_ref[0], jnp.zeros((LANES - N_HD, LANES), F32)], axis=0) * lat_f
    n_t = n_all.T
    lane0 = lax.broadcasted_iota(jnp.int32, (1, LANES), 1) == 0
    ones_col = jnp.where(lane0, 1.0, 0.0).astype(BF16) * jnp.ones((TM, 1), BF16)
    zpad = jnp.zeros((LANES - D_QK, D_V), F32)

    for h in range(N_HEADS):
        qh = q_ref[:, h * LANES:(h + 1) * LANES]
        kh = k_ref[:, h * LANES:(h + 1) * LANES]
        vext = jnp.concatenate([v_ref[:, h * D_V:(h + 1) * D_V], ones_col], axis=1)
        qk = lax.dot_general(qh, kh, (((1,), (1,)), ((), ())), preferred_element_type=F32)
        hsum = None
        for d in range(2):
            hd = d * N_HEADS + h
            mask = lower if d == 0 else upper
            c_prev = (cbf_ref if d == 0 else cbb_ref)[0, h]
            e = jnp.exp(jnp.where(mask, a_t[hd:hd + 1, :] - mrow[:, hd:hd + 1], -jnp.inf))
            nd = _dot((qk * e).astype(BF16), vext)
            cext = jnp.concatenate(
                [jnp.concatenate([c_prev * lat_f, zpad], axis=0),
                 jnp.where(lane0, n_t[:, hd:hd + 1], 0.0)], axis=1).astype(BF16)
            qc = _dot(qh, cext)
            wi = w_inter[:, hd:hd + 1]
            num = nd[:, 0:D_V] + wi * qc[:, 0:D_V]
            den = nd[:, D_V:D_V + 1] + wi * qc[:, D_V:D_V + 1]
            r = 1.0 / jnp.maximum(jnp.abs(den), e_floor[:, hd:hd + 1])
            hsum = num * r if hsum is None else hsum + num * r
        hs = slice(h * D_V, (h + 1) * D_V)
        hm_scr[:, hs] = (_rms(hsum) * gmh_ref[:, hs] * so_ref[:, hs].astype(F32)).astype(BF16)

    ym = _dot(hm_scr[...], wmo_ref[...])
    mix = (ycg_ref[...].astype(F32) + sgb_ref[...].astype(F32) * ym).astype(BF16)
    x1 = jnp.where(is_lat, xl_ref[...], xc_ref[...]) + mod_ref[0, 2:3, :] * _dot(mix, wo_ref[...])
    x1_ref[...] = x1
    h2 = _rms(x1) * g2_ref[...] * (1.0 + mod_ref[0, 4:5, :]) + mod_ref[0, 3:4, :]
    h2_ref[...] = h2

    lane = lax.broadcasted_iota(jnp.int32, (TM, LANES), 1)
    work = jnp.where(lane < n_experts, _dot(h2, rw_ref[...], HIGHEST) + rb_ref[...], -jnp.inf)
    sels, exps, idxs = [], [], []
    top = None
    for _ in range(TOP_K):
        mx = jnp.max(work, axis=-1, keepdims=True)
        ix = jnp.min(jnp.where(work == mx, lane, LANES), axis=-1, keepdims=True)
        sel = lane == ix
        work = jnp.where(sel, -jnp.inf, work)
        top = mx if top is None else top
        sels.append(sel)
        idxs.append(ix.astype(F32))
        exps.append(jnp.exp(mx - top))
    inv = 1.0 / functools.reduce(lambda p, q: p + q, exps)

    @pl.when(i == 0)
    def _():
        carry_scr[...] = jnp.zeros_like(carry_scr)

    onehot = functools.reduce(lambda p, q: p + q, [jnp.where(s, 1.0, 0.0) for s in sels])
    row = lax.broadcasted_iota(jnp.int32, (TM, TM), 0)
    col = lax.broadcasted_iota(jnp.int32, (TM, TM), 1)
    before = _dot((row > col).astype(BF16), onehot.astype(BF16)) + carry_scr[...]
    carry_scr[...] += jnp.sum(onehot, axis=0, keepdims=True)
    cnt_ref[...] = carry_scr[...]
    route = jnp.zeros((TM, LANES), F32)
    for j in range(TOP_K):
        slot = jnp.sum(jnp.where(sels[j], before, 0.0), axis=-1, keepdims=True)
        route = jnp.where(lane == j, idxs[j], route)
        route = jnp.where(lane == TOP_K + j, slot, route)
        route = jnp.where(lane == 2 * TOP_K + j, exps[j] * inv, route)
    route_ref[...] = route
    routet_ref[...] = route.T[0:8, :].astype(jnp.int32)


def _dispatch_kernel(pos_ref, h2_ref, xs_ref, sem):
    def copy(t, j):
        return pltpu.make_async_copy(h2_ref.at[pl.ds(t, 1)], xs_ref.at[pl.ds(pos_ref[j, t], 1)], sem)

    def start(t, carry):
        for j in range(TOP_K):
            copy(t, j).start(priority=j % 2)
        return carry

    def wait(t, carry):
        for j in range(TOP_K):
            copy(t, j).wait()
        return carry

    lax.fori_loop(0, TM, start, 0)
    lax.fori_loop(0, TM, wait, 0)


def _expert_kernel(tile_ref, exp_ref, flag_ref, lo_ref, hi_ref,
                   xs_ref, wg_ref, bg_ref, wl_ref, bl_ref, wd_ref, bd_ref, ys_ref, w_scr):
    w = pl.program_id(0)
    flags = flag_ref[w]

    @pl.when((flags & 4) != 0)
    def _():
        w_scr[0] = wg_ref[0].astype(BF16)
        w_scr[1] = wl_ref[0].astype(BF16)
        w_scr[2] = wd_ref[0].astype(BF16)

    @pl.when((flags & 1) != 0)
    def _():
        x = xs_ref[...].astype(BF16)
        gt = jnp.minimum(_dot(x, w_scr[0]) + bg_ref[0], SWIGLU_LIMIT)
        lin = jnp.clip(_dot(x, w_scr[1]) + bl_ref[0], -SWIGLU_LIMIT, SWIGLU_LIMIT)
        act = gt * jax.nn.sigmoid(SWIGLU_ALPHA * gt) * (lin + 1.0)
        y = _dot(act.astype(BF16), w_scr[2]) + bd_ref[0]
        rows = lax.broadcasted_iota(jnp.int32, (xs_ref.shape[0], 1), 0)
        mine = (rows >= lo_ref[w]) & (rows < hi_ref[w])

        @pl.when((flags & 2) != 0)
        def _():
            ys_ref[...] = jnp.where(mine, y, 0.0)

        @pl.when((flags & 2) == 0)
        def _():
            ys_ref[...] = jnp.where(mine, y, ys_ref[...])


def _combine_kernel(pos_ref, posn_ref, route_ref, x1_ref, mod_ref, fg_ref, ys_ref, outc_ref, outl_ref,
                    buf, sem, *, n_ctx_tiles):
    i = pl.program_id(0)
    n = pl.num_programs(0)

    def copy(p_ref, slot, t, j):
        return pltpu.make_async_copy(ys_ref.at[pl.ds(p_ref[j, t], 1)], buf.at[slot, j, pl.ds(t, 1)], sem.at[slot])

    def start_all(p_ref, slot):
        def body(t, carry):
            for j in range(TOP_K):
                copy(p_ref, slot, t, j).start(priority=j % 2)
            return carry
        lax.fori_loop(0, TM, body, 0)

    @pl.when(i == 0)
    def _():
        start_all(pos_ref, 0)

    @pl.when(i + 1 < n)
    def _():
        start_all(posn_ref, (i + 1) % 2)

    slot = i % 2

    def wait(t, carry):
        for j in range(TOP_K):
            copy(pos_ref, slot, t, j).wait()
        return carry

    lax.fori_loop(0, TM, wait, 0)
    acc = None
    for j in range(TOP_K):
        term = route_ref[:, 2 * TOP_K + j:2 * TOP_K + j + 1] * buf[slot, j]
        acc = term if acc is None else acc + term
    out = _rms(x1_ref[...] + mod_ref[0, 5:6, :] * acc) * fg_ref[...]

    @pl.when(i < n_ctx_tiles)
    def _():
        outc_ref[...] = out

    @pl.when(i >= n_ctx_tiles)
    def _():
        outl_ref[...] = out


def _work_items(counts, n_experts, n_rows, tmx):
    n_items_max = n_rows // tmx + n_experts - 1
    cnt = counts.astype(jnp.int32)
    offs = jnp.concatenate([jnp.zeros((1,), jnp.int32), jnp.cumsum(cnt)])
    first_tile = offs[:-1] // tmx
    n_it = jnp.where(cnt > 0, (offs[1:] - 1) // tmx - first_tile + 1, 0)
    it_start = jnp.concatenate([jnp.zeros((1,), jnp.int32), jnp.cumsum(n_it)])
    total = it_start[-1]
    w = jnp.arange(n_items_max, dtype=jnp.int32)
    wc = jnp.minimum(w, total - 1)
    e = jnp.sum((it_start[None, 1:] <= wc[:, None]).astype(jnp.int32), axis=1)
    e = jnp.minimum(e, n_experts - 1)
    tile = first_tile[e] + wc - it_start[e]
    valid = w < total
    prev = lambda a: jnp.concatenate([jnp.full((1,), -1, jnp.int32), a[:-1]])
    flags = (valid.astype(jnp.int32) + 2 * (valid & (tile != prev(tile))).astype(jnp.int32)
             + 4 * (valid & (e != prev(e))).astype(jnp.int32))
    lo = jnp.clip(offs[e] - tile * tmx, 0, tmx)
    hi = jnp.clip(offs[e + 1] - tile * tmx, 0, tmx)
    return offs, tile, e, flags, lo, hi


def _pack_in_proj(w, b):
    d = w.shape[0]
    o_q, o_k, o_v, o_o, o_g, o_ga, o_gb = 3 * d, 3 * d + 512, 3 * d + 1024, 4 * d + 1024, 5 * d + 1024, \
        5 * d + 1024 + 4 * N_HEADS, 6 * d + 1024 + 4 * N_HEADS

    def pad_heads(m):
        m = m.reshape(m.shape[0], N_HEADS, D_QK)
        return jnp.pad(m, ((0, 0), (0, 0), (0, LANES - D_QK))).reshape(m.shape[0], N_HEADS * LANES)

    def gates(m):
        gz = jnp.zeros((m.shape[0], LANES - N_HD), m.dtype)
        i_f, f_f, i_b, f_b = (m[:, o_g + j * N_HEADS:o_g + (j + 1) * N_HEADS] for j in range(4))
        return jnp.concatenate([i_f, i_b, gz, f_f, f_b, gz], axis=1)

    def pack(m):
        return jnp.concatenate([m[:, 0:o_q], pad_heads(m[:, o_q:o_k]), pad_heads(m[:, o_k:o_v]), m[:, o_v:o_o],
                                m[:, o_o:o_g], m[:, o_ga:o_gb], m[:, o_gb:o_gb + d], gates(m)], axis=1)

    return pack(w).astype(BF16), pack(b.reshape(1, -1))


def kernel(x_prompt, x_sample, c, state_C, state_n, state_m, c_ctx, ada_w, ada_b, norm1_g, norm2_g, w_in, b_in,
           conv_w, conv_b, w_conv_out, mh_norm_g, w_m_out, w_o, router_w, router_b, w_gate, b_gate, w_lin, b_lin,
           w_down, b_down, final_g):
    nb, seq, d = x_prompt.shape
    nd, dseq, _ = x_sample.shape
    n_experts = w_gate.shape[1]
    assert d == _D and w_in.shape[0] == 1 and seq == TM and dseq % TM == 0 and TM % GRID_W == 0
    t_ctx, t_lat = nb * seq, nd * dseq
    n_tok = t_ctx + t_lat
    n_ctx_tiles, n_tiles = t_ctx // TM, n_tok // TM
    lat_chunks = dseq // TM
    n_lat_tiles = n_tiles - n_ctx_tiles

    n_c = 1 + nd
    n_cp = -(-n_c // 8) * 8
    cvec = jnp.concatenate([c_ctx[None, :], c, jnp.zeros((n_cp - n_c, d), F32)], axis=0)
    mod = _modulation(cvec, ada_w[0], ada_b[0]).reshape(n_cp, N_MOD, d)
    mod = jnp.pad(mod, ((0, 0), (0, 8 - N_MOD), (0, 0)))

    def mod_row(tile_tokens):
        ctx_t, per_seq = t_ctx // tile_tokens, dseq // tile_tokens
        return lambda i, *_: (jnp.where(i < ctx_t, 0, 1 + (i - ctx_t) // per_seq), 0, 0)

    x_ctx, x_lat = x_prompt.reshape(t_ctx, d), x_sample.reshape(t_lat, d)
    ctx_spec = pl.BlockSpec((TM, d), lambda i: (jnp.minimum(i, n_ctx_tiles - 1), 0))
    lat_spec = pl.BlockSpec((TM, d), lambda i: (jnp.maximum(i - n_ctx_tiles, 0), 0))
    w_all, b_all = _pack_in_proj(w_in[0], b_in[0])
    tile = lambda cols: pl.BlockSpec((TM, cols), lambda i: (i, 0))
    params = pltpu.CompilerParams(dimension_semantics=("arbitrary",), vmem_limit_bytes=VMEM_LIMIT)
    bf = lambda cols: jax.ShapeDtypeStruct((n_tok, cols), BF16)

    ycg, sgb, q, k, v, so, gt = pl.pallas_call(
        functools.partial(_proj_kernel, n_ctx_tiles=n_ctx_tiles, ctx_row=seq),
        grid=(n_tiles,),
        in_specs=[ctx_spec, lat_spec, pl.BlockSpec((1, 8, d), mod_row(TM)), _const_spec((1, d)),
                  _const_spec((d, W_COLS)), _const_spec((1, W_COLS)), _const_spec((3, d)), _const_spec((1, d)),
                  _const_spec((d, d))],
        out_specs=[tile(d), tile(d), tile(d), tile(d), tile(d), tile(d), tile(2 * LANES)],
        out_shape=[bf(d), bf(d), bf(d), bf(d), bf(d), bf(d), jax.ShapeDtypeStruct((n_tok, 2 * LANES), F32)],
        compiler_params=params,
        name="in_proj_conv",
    )(x_ctx, x_lat, mod, norm1_g, w_all, b_all, conv_w[0], conv_b, w_conv_out[0].astype(BF16))

    c_new, n_new, m_new = _state_scan(k, v, gt, 0, nb, 1, None, False, True)
    init = (state_C[:, 0].astype(F32).reshape(nd, N_HD, D_QK, D_V),
            jnp.pad(state_n[:, 0].astype(F32).reshape(nd, N_HD, D_QK), ((0, 0), (0, 0), (0, LANES - D_QK))),
            jnp.pad(state_m[:, 0].astype(F32).reshape(nd, 1, N_HD), ((0, 0), (0, 0), (0, LANES - N_HD))))
    cbf, cbb, nbf, nbb, mbf, mbb = _state_scan(k, v, gt, n_ctx_tiles, nd, lat_chunks, init, True, False)

    lat_idx = lambda i: jnp.maximum(i - n_ctx_tiles, 0)
    st4 = pl.BlockSpec((1, N_HEADS, D_QK, D_V), lambda i: (lat_idx(i), 0, 0, 0))
    st3 = pl.BlockSpec((1, N_HEADS, LANES), lambda i: (lat_idx(i), 0, 0))
    st1 = pl.BlockSpec((1, 1, LANES), lambda i: (lat_idx(i), 0, 0))
    rw = jnp.pad(router_w[0], ((0, 0), (0, LANES - n_experts)))
    rb = jnp.pad(router_b[0], (0, LANES - n_experts)).reshape(1, LANES)
    x1, h2, route, route_t, counts = pl.pallas_call(
        functools.partial(_mix_kernel, n_ctx_tiles=n_ctx_tiles, n_experts=n_experts),
        grid=(n_tiles,),
        in_specs=[tile(d), tile(d), tile(d), tile(d), tile(d), tile(d), tile(2 * LANES), ctx_spec, lat_spec,
                  pl.BlockSpec((1, 8, d), mod_row(TM)), st4, st4, st3, st3, st1, st1,
                  _const_spec((1, d)), _const_spec((d, d)), _const_spec((d, d)), _const_spec((1, d)),
                  _const_spec((d, LANES)), _const_spec((1, LANES))],
        out_specs=[tile(d), tile(d), tile(LANES), pl.BlockSpec((8, TM), lambda i: (0, i)),
                   pl.BlockSpec((1, LANES), lambda i: (0, 0))],
        out_shape=[jax.ShapeDtypeStruct((n_tok, d), F32), jax.ShapeDtypeStruct((n_tok, d), F32),
                   jax.ShapeDtypeStruct((n_tok, LANES), F32), jax.ShapeDtypeStruct((8, n_tok), jnp.int32),
                   jax.ShapeDtypeStruct((1, LANES), F32)],
        scratch_shapes=[pltpu.VMEM((TM, d), BF16), pltpu.VMEM((1, LANES), F32)],
        compiler_params=params,
        name="mlstm_mix_router",
    )(q, k, v, so, sgb, ycg, gt, x_ctx, x_lat, mod, cbf, cbb, nbf, nbb, mbf, mbb,
      mh_norm_g, w_m_out[0].astype(BF16), w_o[0].astype(BF16), norm2_g, rw, rb)

    n_rows = TOP_K * n_tok
    offs, it_tile, it_exp, it_flags, it_lo, it_hi = _work_items(counts[0, :n_experts], n_experts, n_rows, TMX)
    is_exp = route_t[0:TOP_K, :, None] == jnp.arange(n_experts, dtype=jnp.int32)
    pos_t = jnp.sum(jnp.where(is_exp, offs[:n_experts], 0), axis=-1) + route_t[TOP_K:2 * TOP_K]
    pos_spec = lambda f: pl.BlockSpec((TOP_K, TM), f, memory_space=pltpu.SMEM)
    any_spec = pl.BlockSpec(memory_space=pl.ANY)

    xs = pl.pallas_call(
        _dispatch_kernel,
        grid=(n_tiles,),
        in_specs=[pos_spec(lambda i: (0, i)), tile(d)],
        out_specs=any_spec,
        out_shape=jax.ShapeDtypeStruct((n_rows, d), F32),
        scratch_shapes=[pltpu.SemaphoreType.DMA],
        compiler_params=params,
        name="moe_dispatch",
    )(pos_t, h2)

    dff = w_gate.shape[-1]
    wspec = lambda a, b_: pl.BlockSpec((1, a, b_), lambda w, tl, ex, *_: (ex[w], 0, 0))
    ys = pl.pallas_call(
        _expert_kernel,
        grid_spec=pltpu.PrefetchScalarGridSpec(
            num_scalar_prefetch=5,
            grid=(it_tile.shape[0],),
            in_specs=[pl.BlockSpec((TMX, d), lambda w, tl, *_: (tl[w], 0)),
                      wspec(d, dff), wspec(1, dff), wspec(d, dff), wspec(1, dff), wspec(dff, d), wspec(1, d)],
            out_specs=pl.BlockSpec((TMX, d), lambda w, tl, *_: (tl[w], 0)),
            scratch_shapes=[pltpu.VMEM((3, d, dff), BF16)]),
        out_shape=jax.ShapeDtypeStruct((n_rows, d), F32),
        compiler_params=params,
        name="moe_experts",
    )(it_tile, it_exp, it_flags, it_lo, it_hi, xs, w_gate[0], b_gate[0].reshape(n_experts, 1, dff), w_lin[0],
      b_lin[0].reshape(n_experts, 1, dff), w_down[0], b_down[0].reshape(n_experts, 1, d))

    ctx_i = lambda i: (jnp.minimum(i, n_ctx_tiles - 1), 0)
    lat_i = lambda i: (jnp.maximum(i - n_ctx_tiles, 0), 0)
    y_prompt, y_sample = pl.pallas_call(
        functools.partial(_combine_kernel, n_ctx_tiles=n_ctx_tiles),
        grid=(n_tiles,),
        in_specs=[pos_spec(lambda i: (0, i)), pos_spec(lambda i: (0, jnp.minimum(i + 1, n_tiles - 1))),
                  tile(LANES), tile(d), pl.BlockSpec((1, 8, d), mod_row(TM)), _const_spec((1, d)), any_spec],
        out_specs=[pl.BlockSpec((TM, d), ctx_i), pl.BlockSpec((TM, d), lat_i)],
        out_shape=[jax.ShapeDtypeStruct((t_ctx, d), F32), jax.ShapeDtypeStruct((t_lat, d), F32)],
        scratch_shapes=[pltpu.VMEM((2, TOP_K, TM, d), F32), pltpu.SemaphoreType.DMA((2,))],
        compiler_params=params,
        name="moe_combine",
    )(pos_t, pos_t, route, x1, mod, final_g.reshape(1, d), ys)

    y_prompt = y_prompt.reshape(nb, seq, d)
    y_sample = y_sample.reshape(nd, dseq, d)
    new_c = c_new.reshape(nb, 1, 2, N_HEADS, D_QK, D_V)
    new_n = n_new[:, :, :D_QK].reshape(nb, 1, 2, N_HEADS, D_QK)
    new_m = m_new[:, 0, :N_HD].reshape(nb, 1, 2, N_HEADS)
    return (y_prompt, y_sample, new_c, new_n, new_m)
```

```python
import functools

import jax
import jax.numpy as jnp
from jax import lax
from jax.experimental import pallas as pl
from jax.experimental.pallas import tpu as pltpu

F32 = jnp.float32
BF16 = jnp.bfloat16
HIGHEST = lax.Precision.HIGHEST

N_HEADS = 8
D_QK = 64
D_V = 128
GRID_W = 64
TOP_K = 4
SWIGLU_LIMIT = 7.0
SWIGLU_ALPHA = 1.702
EPS = 1e-6
N_MOD = 6

LANES = 128
TM = 256
TMX = 512
N_HD = 2 * N_HEADS
VMEM_LIMIT = 56 * 1024 * 1024

_D = 1024
C_CONV = (0, 3 * _D)
C_Q = (3 * _D, 4 * _D)
C_K = (4 * _D, 5 * _D)
C_V = (5 * _D, 6 * _D)
C_O = (6 * _D, 7 * _D)
C_GA = (7 * _D, 8 * _D)
C_GB = (8 * _D, 9 * _D)
C_G = (9 * _D, 9 * _D + 2 * LANES)
W_COLS = C_G[1]


def _dot(a, b, precision=None):
    return jnp.dot(a, b, preferred_element_type=F32, precision=precision)


def _rms(x):
    return x * lax.rsqrt(jnp.mean(x * x, axis=-1, keepdims=True) + EPS)


ROW_SUB = 8


def _store_rows(ref, val):
    n = val.shape[0]
    for s in range(ROW_SUB):
        ref[pl.ds(s, n, stride=ROW_SUB), :] = val[:, s * LANES:(s + 1) * LANES]


def _load_rows(ref):
    n = ref.shape[0] // ROW_SUB
    return jnp.concatenate([ref[pl.ds(s, n, stride=ROW_SUB), :] for s in range(ROW_SUB)], axis=1)


def _row_tile(ref, r):
    return ref.at[pl.ds(pl.multiple_of(r * ROW_SUB, ROW_SUB), ROW_SUB)]


def _const_spec(shape):
    return pl.BlockSpec(shape, lambda *_: (0,) * len(shape), pipeline_mode=pl.Buffered(1))


def _mod_kernel(c_ref, w_ref, b_ref, o_ref):
    c = c_ref[...]
    o_ref[...] = _dot(c * jax.nn.sigmoid(c), w_ref[...], HIGHEST) + b_ref[...]


def _modulation(cvec, ada_w, ada_b):
    n, d = cvec.shape
    nout = ada_w.shape[1]
    return pl.pallas_call(
        _mod_kernel,
        grid=(nout // d,),
        in_specs=[pl.BlockSpec((n, d), lambda j: (0, 0)),
                  pl.BlockSpec((d, d), lambda j: (0, j)),
                  pl.BlockSpec((1, d), lambda j: (0, j))],
        out_specs=pl.BlockSpec((n, d), lambda j: (0, j)),
        out_shape=jax.ShapeDtypeStruct((n, nout), F32),
        name="adaln_mod",
    )(cvec, ada_w, ada_b.reshape(1, nout))


def _proj_kernel(xc_ref, xl_ref, mod_ref, g1_ref, w_ref, b_ref, cw_ref, cb_ref, wco_ref,
                 ycg_ref, sgb_ref, q_ref, k_ref, v_ref, so_ref, gt_ref, *, n_ctx_tiles, ctx_row):
    i = pl.program_id(0)
    x = jnp.where(i < n_ctx_tiles, xc_ref[...], xl_ref[...])
    h = (_rms(x) * g1_ref[...] * (1.0 + mod_ref[0, 1:2, :]) + mod_ref[0, 0:1, :]).astype(BF16)

    def proj(cols):
        return _dot(h, w_ref[:, cols[0]:cols[1]]) + b_ref[:, cols[0]:cols[1]]

    zc = proj(C_CONV)
    d = x.shape[1]
    u = zc[:, 2 * d:3 * d] * zc[:, 0:d]
    rowlen = jnp.where(i < n_ctx_tiles, ctx_row, GRID_W)
    pos = lax.broadcasted_iota(jnp.int32, (TM, 1), 0) & (rowlen - 1)
    u_prev = jnp.where(pos == 0, 0.0, pltpu.roll(u, 1, 0))
    u_next = jnp.where(pos == rowlen - 1, 0.0, pltpu.roll(u, TM - 1, 0))
    uc = u_prev * cw_ref[0:1, :] + u * cw_ref[1:2, :] + u_next * cw_ref[2:3, :] + cb_ref[...]
    yconv = _dot((zc[:, d:2 * d] * uc).astype(BF16), wco_ref[...])

    ycg_ref[...] = (jax.nn.sigmoid(proj(C_GA)) * yconv).astype(BF16)
    sgb_ref[...] = jax.nn.sigmoid(proj(C_GB)).astype(BF16)
    q_ref[...] = (proj(C_Q) * (D_QK ** -0.5)).astype(BF16)
    k_ref[...] = proj(C_K).astype(BF16)
    v_ref[...] = proj(C_V).astype(BF16)
    so_ref[...] = jax.nn.sigmoid(proj(C_O)).astype(BF16)
    gt_ref[...] = proj(C_G)


def _tri_masks():
    row = lax.broadcasted_iota(jnp.int32, (TM, TM), 0)
    col = lax.broadcasted_iota(jnp.int32, (TM, TM), 1)
    return row >= col, row <= col


def _gate_cumsums(gi, gf):
    lower, upper = _tri_masks()
    lf = jax.nn.log_sigmoid(gf)
    fwd_lane = lax.broadcasted_iota(jnp.int32, (1, LANES), 1) < N_HEADS
    bsum = jnp.where(fwd_lane, _dot(lower.astype(F32), lf, HIGHEST), _dot(upper.astype(F32), lf, HIGHEST))
    return lf, bsum, fwd_lane


def _state_kernel(*refs, zero_init, emit_before, emit_after):
    it = iter(refs)
    kf_ref, vf_ref, gf_ref, kb_ref, vb_ref, gb_ref = (next(it) for _ in range(6))
    if not zero_init:
        c0_ref, n0_ref, m0_ref = (next(it) for _ in range(3))
    if emit_before:
        cbf_ref, cbb_ref, nbf_ref, nbb_ref, mbf_ref, mbb_ref = (next(it) for _ in range(6))
    if emit_after:
        ca_ref, na_ref, ma_ref = (next(it) for _ in range(3))
    c_scr, n_scr, m_scr = (next(it) for _ in range(3))
    c = pl.program_id(1)

    @pl.when(c == 0)
    def _():
        if zero_init:
            c_scr[...] = jnp.zeros_like(c_scr)
            n_scr[...] = jnp.zeros_like(n_scr)
            m_scr[...] = jnp.zeros_like(m_scr)
        else:
            c_scr[...] = c0_ref[0]
            n_scr[...] = n0_ref[0]
            m_scr[...] = m0_ref[0]

    if emit_before:
        cbf_ref[0] = c_scr[0:N_HEADS]
        cbb_ref[0] = c_scr[N_HEADS:N_HD]
        nbf_ref[0] = n_scr[0:N_HEADS]
        nbb_ref[0] = n_scr[N_HEADS:N_HD]
        mbf_ref[0] = m_scr[...]
        mbb_ref[0] = m_scr[...]

    fwd_lane = lax.broadcasted_iota(jnp.int32, (1, LANES), 1) < N_HEADS
    gi = jnp.where(fwd_lane, gf_ref[:, 0:LANES], gb_ref[:, 0:LANES])
    gfg = jnp.where(fwd_lane, gf_ref[:, LANES:2 * LANES], gb_ref[:, LANES:2 * LANES])
    lf, bsum, _ = _gate_cumsums(gi, gfg)
    total = jnp.sum(lf, axis=0, keepdims=True)
    g = total - bsum + gi
    m_prev = m_scr[...]
    m_new = jnp.maximum(total + m_prev, jnp.max(g, axis=0, keepdims=True))
    wk = jnp.exp(g - m_new)
    decay = jnp.exp(total + m_prev - m_new)
    for hd in range(N_HD):
        h = hd % N_HEADS
        k_ref, v_ref = (kf_ref, vf_ref) if hd < N_HEADS else (kb_ref, vb_ref)
        wkk = wk[:, hd:hd + 1] * k_ref[:, h * LANES:(h + 1) * LANES].astype(F32)
        upd = lax.dot_general(wkk.astype(BF16), v_ref[:, h * D_V:(h + 1) * D_V],
                              (((0,), (0,)), ((), ())), preferred_element_type=F32)
        dec = decay[:, hd:hd + 1]
        c_scr[hd] = dec * c_scr[hd] + upd[0:D_QK, :]
        n_scr[hd:hd + 1, :] = dec * n_scr[hd:hd + 1, :] + jnp.sum(wkk, axis=0, keepdims=True)
    m_scr[...] = m_new

    if emit_after:
        @pl.when(c == pl.num_programs(1) - 1)
        def _():
            ca_ref[0] = c_scr[...]
            na_ref[0] = n_scr[...]
            ma_ref[0] = m_scr[...]


def _state_scan(k, v, gt, tile0, n_seq, n_chunk, init, emit_before, emit_after):
    d = k.shape[1]
    fwd = lambda s, c: (tile0 + s * n_chunk + c, 0)
    bwd = lambda s, c: (tile0 + s * n_chunk + n_chunk - 1 - c, 0)
    in_specs = [pl.BlockSpec((TM, d), fwd), pl.BlockSpec((TM, d), fwd), pl.BlockSpec((TM, 2 * LANES), fwd),
                pl.BlockSpec((TM, d), bwd), pl.BlockSpec((TM, d), bwd), pl.BlockSpec((TM, 2 * LANES), bwd)]
    args = [k, v, gt, k, v, gt]
    if init is not None:
        in_specs += [pl.BlockSpec((1, N_HD, D_QK, D_V), lambda s, c: (s, 0, 0, 0)),
                     pl.BlockSpec((1, N_HD, LANES), lambda s, c: (s, 0, 0)),
                     pl.BlockSpec((1, 1, LANES), lambda s, c: (s, 0, 0))]
        args += list(init)
    out_specs, out_shape = [], []
    n_tot = n_seq * n_chunk
    if emit_before:
        cf = lambda s, c: (s * n_chunk + c, 0, 0, 0)
        cb = lambda s, c: (s * n_chunk + n_chunk - 1 - c, 0, 0, 0)
        nf = lambda s, c: (s * n_chunk + c, 0, 0)
        nb = lambda s, c: (s * n_chunk + n_chunk - 1 - c, 0, 0)
        out_specs += [pl.BlockSpec((1, N_HEADS, D_QK, D_V), cf), pl.BlockSpec((1, N_HEADS, D_QK, D_V), cb),
                      pl.BlockSpec((1, N_HEADS, LANES), nf), pl.BlockSpec((1, N_HEADS, LANES), nb),
                      pl.BlockSpec((1, 1, LANES), nf), pl.BlockSpec((1, 1, LANES), nb)]
        out_shape += [jax.ShapeDtypeStruct((n_tot, N_HEADS, D_QK, D_V), F32)] * 2
        out_shape += [jax.ShapeDtypeStruct((n_tot, N_HEADS, LANES), F32)] * 2
        out_shape += [jax.ShapeDtypeStruct((n_tot, 1, LANES), F32)] * 2
    if emit_after:
        out_specs += [pl.BlockSpec((1, N_HD, D_QK, D_V), lambda s, c: (s, 0, 0, 0)),
                      pl.BlockSpec((1, N_HD, LANES), lambda s, c: (s, 0, 0)),
                      pl.BlockSpec((1, 1, LANES), lambda s, c: (s, 0, 0))]
        out_shape += [jax.ShapeDtypeStruct((n_seq, N_HD, D_QK, D_V), F32),
                      jax.ShapeDtypeStruct((n_seq, N_HD, LANES), F32),
                      jax.ShapeDtypeStruct((n_seq, 1, LANES), F32)]
    return pl.pallas_call(
        functools.partial(_state_kernel, zero_init=init is None, emit_before=emit_before, emit_after=emit_after),
        grid=(n_seq, n_chunk),
        in_specs=in_specs,
        out_specs=out_specs,
        out_shape=out_shape,
        scratch_shapes=[pltpu.VMEM((N_HD, D_QK, D_V), F32), pltpu.VMEM((N_HD, LANES), F32),
                        pltpu.VMEM((1, LANES), F32)],
        compiler_params=pltpu.CompilerParams(dimension_semantics=("arbitrary", "arbitrary"),
                                             vmem_limit_bytes=VMEM_LIMIT),
        name="mlstm_state_scan",
    )(*args)


def _mix_kernel(q_ref, k_ref, v_ref, so_ref, sgb_ref, ycg_ref, gt_ref, xc_ref, xl_ref, mod_ref,
                cbf_ref, cbb_ref, nbf_ref, nbb_ref, mbf_ref, mbb_ref,
                gmh_ref, wmo_ref, wo_ref, g2_ref, rw_ref, rb_ref,
                x1_ref, h2_ref, route_ref, routet_ref, cnt_ref, hm_scr, carry_scr, *, n_ctx_tiles, n_experts):
    i = pl.program_id(0)
    is_lat = i >= n_ctx_tiles
    lat_f = is_lat.astype(F32)
    lower, upper = _tri_masks()
    gi = gt_ref[:, 0:LANES]
    lf, bsum, fwd_lane = _gate_cumsums(gi, gt_ref[:, LANES:2 * LANES])
    a = gi - bsum
    t_idx = lax.broadcasted_iota(jnp.int32, (TM, LANES), 0)
    pmax, smax = a, a
    step = 1
    while step < TM:
        pmax = jnp.maximum(pmax, jnp.where(t_idx >= step, pltpu.roll(pmax, step, 0), -jnp.inf))
        smax = jnp.maximum(smax, jnp.where(t_idx < TM - step, pltpu.roll(smax, TM - step, 0), -jnp.inf))
        step *= 2
    m_prev = jnp.where(fwd_lane, mbf_ref[0], mbb_ref[0]) * lat_f
    mrow = jnp.maximum(m_prev, jnp.where(fwd_lane, pmax, smax))
    w_inter = jnp.exp(m_prev - mrow)
    e_floor = jnp.exp(-(bsum + mrow))
    a_t = a.T
    n_all = jnp.concatenate([nbf_ref[0], nbb_ref[0], jnp.zeros((LANES - N_HD, LANES), F32)], axis=0) * lat_f
    n_t = n_all.T
    lane0 = lax.broadcasted_iota(jnp.int32, (1, LANES), 1) == 0
    ones_col = jnp.where(lane0, 1.0, 0.0).astype(BF16) * jnp.ones((TM, 1), BF16)
    zpad = jnp.zeros((LANES - D_QK, D_V), F32)

    for h in range(N_HEADS):
        qh = q_ref[:, h * LANES:(h + 1) * LANES]
        kh = k_ref[:, h * LANES:(h + 1) * LANES]
        vext = jnp.concatenate([v_ref[:, h * D_V:(h + 1) * D_V], ones_col], axis=1)
        qk = lax.dot_general(qh, kh, (((1,), (1,)), ((), ())), preferred_element_type=F32)
        hsum = None
        for d in range(2):
            hd = d * N_HEADS + h
            mask = lower if d == 0 else upper
            c_prev = (cbf_ref if d == 0 else cbb_ref)[0, h]
            e = jnp.exp(jnp.where(mask, a_t[hd:hd + 1, :] - mrow[:, hd:hd + 1], -jnp.inf))
            nd = _dot((qk * e).astype(BF16), vext)
            cext = jnp.concatenate(
                [jnp.concatenate([c_prev * lat_f, zpad], axis=0),
                 jnp.where(lane0, n_t[:, hd:hd + 1], 0.0)], axis=1).astype(BF16)
            qc = _dot(qh, cext)
            wi = w_inter[:, hd:hd + 1]
            num = nd[:, 0:D_V] + wi * qc[:, 0:D_V]
            den = nd[:, D_V:D_V + 1] + wi * qc[:, D_V:D_V + 1]
            r = 1.0 / jnp.maximum(jnp.abs(den), e_floor[:, hd:hd + 1])
            hsum = num * r if hsum is None else hsum + num * r
        hs = slice(h * D_V, (h + 1) * D_V)
        hm_scr[:, hs] = (_rms(hsum) * gmh_ref[:, hs] * so_ref[:, hs].astype(F32)).astype(BF16)

    ym = _dot(hm_scr[...], wmo_ref[...])
    mix = (ycg_ref[...].astype(F32) + sgb_ref[...].astype(F32) * ym).astype(BF16)
    x1 = jnp.where(is_lat, xl_ref[...], xc_ref[...]) + mod_ref[0, 2:3, :] * _dot(mix, wo_ref[...])
    x1_ref[...] = x1
    h2 = _rms(x1) * g2_ref[...] * (1.0 + mod_ref[0, 4:5, :]) + mod_ref[0, 3:4, :]
    _store_rows(h2_ref, h2)

    lane = lax.broadcasted_iota(jnp.int32, (TM, LANES), 1)
    work = jnp.where(lane < n_experts, _dot(h2, rw_ref[...], HIGHEST) + rb_ref[...], -jnp.inf)
    sels, exps, idxs = [], [], []
    top = None
    for _ in range(TOP_K):
        mx = jnp.max(work, axis=-1, keepdims=True)
        ix = jnp.min(jnp.where(work == mx, lane, LANES), axis=-1, keepdims=True)
        sel = lane == ix
        work = jnp.where(sel, -jnp.inf, work)
        top = mx if top is None else top
        sels.append(sel)
        idxs.append(ix.astype(F32))
        exps.append(jnp.exp(mx - top))
    inv = 1.0 / functools.reduce(lambda p, q: p + q, exps)

    @pl.when(i == 0)
    def _():
        carry_scr[...] = jnp.zeros_like(carry_scr)

    onehot = functools.reduce(lambda p, q: p + q, [jnp.where(s, 1.0, 0.0) for s in sels])
    row = lax.broadcasted_iota(jnp.int32, (TM, TM), 0)
    col = lax.broadcasted_iota(jnp.int32, (TM, TM), 1)
    before = _dot((row > col).astype(BF16), onehot.astype(BF16)) + carry_scr[...]
    carry_scr[...] += jnp.sum(onehot, axis=0, keepdims=True)
    cnt_ref[...] = carry_scr[...]
    route = jnp.zeros((TM, LANES), F32)
    for j in range(TOP_K):
        slot = jnp.sum(jnp.where(sels[j], before, 0.0), axis=-1, keepdims=True)
        route = jnp.where(lane == j, idxs[j], route)
        route = jnp.where(lane == TOP_K + j, slot, route)
        route = jnp.where(lane == 2 * TOP_K + j, exps[j] * inv, route)
    route_ref[...] = route
    routet_ref[...] = route.T[0:8, :].astype(jnp.int32)


def _dispatch_kernel(pos_ref, h2_ref, xs_ref, sem):
    def copy(t, j):
        return pltpu.make_async_copy(_row_tile(h2_ref, t), _row_tile(xs_ref, pos_ref[j, t]), sem)

    def start(t, carry):
        for j in range(TOP_K):
            copy(t, j).start()
        return carry

    lax.fori_loop(0, TM, start, 0)
    for j in range(TOP_K):
        pltpu.make_async_copy(h2_ref, xs_ref.at[pl.ds(0, TM * ROW_SUB)], sem).wait()


def _expert_kernel(tile_ref, exp_ref, flag_ref, lo_ref, hi_ref,
                   xs_ref, wg_ref, bg_ref, wl_ref, bl_ref, wd_ref, bd_ref, ys_ref, w_scr):
    w = pl.program_id(0)
    flags = flag_ref[w]

    @pl.when((flags & 4) != 0)
    def _():
        w_scr[0] = wg_ref[0].astype(BF16)
        w_scr[1] = wl_ref[0].astype(BF16)
        w_scr[2] = wd_ref[0].astype(BF16)

    @pl.when((flags & 1) != 0)
    def _():
        x = _load_rows(xs_ref).astype(BF16)
        gt =jnp.minimum(_dot(x, w_scr[0]) + bg_ref[0], SWIGLU_LIMIT)
        lin = jnp.clip(_dot(x, w_scr[1]) + bl_ref[0], -SWIGLU_LIMIT, SWIGLU_LIMIT)
        act = gt * jax.nn.sigmoid(SWIGLU_ALPHA * gt) * (lin + 1.0)
        y = _dot(act.astype(BF16), w_scr[2]) + bd_ref[0]
        rows = lax.broadcasted_iota(jnp.int32, (y.shape[0], 1), 0)
        mine = (rows >= lo_ref[w]) & (rows < hi_ref[w])

        @pl.when((flags & 2) != 0)
        def _():
            _store_rows(ys_ref, jnp.where(mine, y, 0.0))

        @pl.when((flags & 2) == 0)
        def _():
            _store_rows(ys_ref, jnp.where(mine, y, _load_rows(ys_ref)))


def _combine_kernel(pos_ref, posn_ref, route_ref, x1_ref, mod_ref, fg_ref, ys_ref, outc_ref, outl_ref,
                    buf, sem, *, n_ctx_tiles):
    i = pl.program_id(0)
    n = pl.num_programs(0)

    def copy(p_ref, slot, t, j):
        return pltpu.make_async_copy(_row_tile(ys_ref, p_ref[j, t]), _row_tile(buf.at[slot, j], t), sem.at[slot])

    def start_all(p_ref, slot):
        def body(t, carry):
            for j in range(TOP_K):
                copy(p_ref, slot, t, j).start()
            return carry
        lax.fori_loop(0, TM, body, 0)

    @pl.when(i == 0)
    def _():
        start_all(pos_ref, 0)

    @pl.when(i + 1 < n)
    def _():
        start_all(posn_ref, (i + 1) % 2)

    slot = i % 2

    for j in range(TOP_K):
        pltpu.make_async_copy(ys_ref.at[pl.ds(0, TM * ROW_SUB)], buf.at[slot, j], sem.at[slot]).wait()
    acc = None
    for j in range(TOP_K):
        term = route_ref[:, 2 * TOP_K + j:2 * TOP_K + j + 1] * _load_rows(buf.at[slot, j])
        acc = term if acc is None else acc + term
    out = _rms(x1_ref[...] + mod_ref[0, 5:6, :] * acc) * fg_ref[...]

    @pl.when(i < n_ctx_tiles)
    def _():
        outc_ref[...] = out

    @pl.when(i >= n_ctx_tiles)
    def _():
        outl_ref[...] = out


def _work_items(counts, n_experts, n_rows, tmx):
    n_items_max = n_rows // tmx + n_experts - 1
    cnt = counts.astype(jnp.int32)
    offs = jnp.concatenate([jnp.zeros((1,), jnp.int32), jnp.cumsum(cnt)])
    first_tile = offs[:-1] // tmx
    n_it = jnp.where(cnt > 0, (offs[1:] - 1) // tmx - first_tile + 1, 0)
    it_start = jnp.concatenate([jnp.zeros((1,), jnp.int32), jnp.cumsum(n_it)])
    total = it_start[-1]
    w = jnp.arange(n_items_max, dtype=jnp.int32)
    wc = jnp.minimum(w, total - 1)
    e = jnp.sum((it_start[None, 1:] <= wc[:, None]).astype(jnp.int32), axis=1)
    e = jnp.minimum(e, n_experts - 1)
    tile = first_tile[e] + wc - it_start[e]
    valid = w < total
    prev = lambda a: jnp.concatenate([jnp.full((1,), -1, jnp.int32), a[:-1]])
    flags = (valid.astype(jnp.int32) + 2 * (valid & (tile != prev(tile))).astype(jnp.int32)
             + 4 * (valid & (e != prev(e))).astype(jnp.int32))
    lo = jnp.clip(offs[e] - tile * tmx, 0, tmx)
    hi = jnp.clip(offs[e + 1] - tile * tmx, 0, tmx)
    return offs, tile, e, flags, lo, hi


def _pack_in_proj(w, b):
    d = w.shape[0]
    o_q, o_k, o_v, o_o, o_g, o_ga, o_gb = 3 * d, 3 * d + 512, 3 * d + 1024, 4 * d + 1024, 5 * d + 1024, \
        5 * d + 1024 + 4 * N_HEADS, 6 * d + 1024 + 4 * N_HEADS

    def pad_heads(m):
        m = m.reshape(m.shape[0], N_HEADS, D_QK)
        return jnp.pad(m, ((0, 0), (0, 0), (0, LANES - D_QK))).reshape(m.shape[0], N_HEADS * LANES)

    def gates(m):
        gz = jnp.zeros((m.shape[0], LANES - N_HD), m.dtype)
        i_f, f_f, i_b, f_b = (m[:, o_g + j * N_HEADS:o_g + (j + 1) * N_HEADS] for j in range(4))
        return jnp.concatenate([i_f, i_b, gz, f_f, f_b, gz], axis=1)

    def pack(m):
        return jnp.concatenate([m[:, 0:o_q], pad_heads(m[:, o_q:o_k]), pad_heads(m[:, o_k:o_v]), m[:, o_v:o_o],
                                m[:, o_o:o_g], m[:, o_ga:o_gb], m[:, o_gb:o_gb + d], gates(m)], axis=1)

    return pack(w).astype(BF16), pack(b.reshape(1, -1))


def kernel(x_prompt, x_sample, c, state_C, state_n, state_m, c_ctx, ada_w, ada_b, norm1_g, norm2_g, w_in, b_in,
           conv_w, conv_b, w_conv_out, mh_norm_g, w_m_out, w_o, router_w, router_b, w_gate, b_gate, w_lin, b_lin,
           w_down, b_down, final_g):
    nb, seq, d = x_prompt.shape
    nd, dseq, _ = x_sample.shape
    n_experts = w_gate.shape[1]
    assert d == _D and w_in.shape[0] == 1 and seq == TM and dseq % TM == 0 and TM % GRID_W == 0
    t_ctx, t_lat = nb * seq, nd * dseq
    n_tok = t_ctx + t_lat
    n_ctx_tiles, n_tiles = t_ctx // TM, n_tok // TM
    lat_chunks = dseq // TM
    n_lat_tiles = n_tiles - n_ctx_tiles

    n_c = 1 + nd
    n_cp = -(-n_c // 8) * 8
    cvec = jnp.concatenate([c_ctx[None, :], c, jnp.zeros((n_cp - n_c, d), F32)], axis=0)
    mod = _modulation(cvec, ada_w[0], ada_b[0]).reshape(n_cp, N_MOD, d)
    mod = jnp.pad(mod, ((0, 0), (0, 8 - N_MOD), (0, 0)))

    def mod_row(tile_tokens):
        ctx_t, per_seq = t_ctx // tile_tokens, dseq // tile_tokens
        return lambda i, *_: (jnp.where(i < ctx_t, 0, 1 + (i - ctx_t) // per_seq), 0, 0)

    x_ctx, x_lat = x_prompt.reshape(t_ctx, d), x_sample.reshape(t_lat, d)
    ctx_spec = pl.BlockSpec((TM, d), lambda i: (jnp.minimum(i, n_ctx_tiles - 1), 0))
    lat_spec = pl.BlockSpec((TM, d), lambda i: (jnp.maximum(i - n_ctx_tiles, 0), 0))
    w_all, b_all = _pack_in_proj(w_in[0], b_in[0])
    tile = lambda cols: pl.BlockSpec((TM, cols), lambda i: (i, 0))
    rows = lambda n: pl.BlockSpec((n * ROW_SUB, LANES), lambda i: (i, 0))
    params = pltpu.CompilerParams(dimension_semantics=("arbitrary",), vmem_limit_bytes=VMEM_LIMIT)
    bf = lambda cols: jax.ShapeDtypeStruct((n_tok, cols), BF16)

    ycg, sgb, q, k, v, so, gt = pl.pallas_call(
        functools.partial(_proj_kernel, n_ctx_tiles=n_ctx_tiles, ctx_row=seq),
        grid=(n_tiles,),
        in_specs=[ctx_spec, lat_spec, pl.BlockSpec((1, 8, d), mod_row(TM)), _const_spec((1, d)),
                  _const_spec((d, W_COLS)), _const_spec((1, W_COLS)), _const_spec((3, d)), _const_spec((1, d)),
                  _const_spec((d, d))],
        out_specs=[tile(d), tile(d), tile(d), tile(d), tile(d), tile(d), tile(2 * LANES)],
        out_shape=[bf(d), bf(d), bf(d), bf(d), bf(d), bf(d), jax.ShapeDtypeStruct((n_tok, 2 * LANES), F32)],
        compiler_params=params,
        name="in_proj_conv",
    )(x_ctx, x_lat, mod, norm1_g, w_all, b_all, conv_w[0], conv_b, w_conv_out[0].astype(BF16))

    c_new, n_new, m_new = _state_scan(k, v, gt, 0, nb, 1, None, False, True)
    init = (state_C[:, 0].astype(F32).reshape(nd, N_HD, D_QK, D_V),
            jnp.pad(state_n[:, 0].astype(F32).reshape(nd, N_HD, D_QK), ((0, 0), (0, 0), (0, LANES - D_QK))),
            jnp.pad(state_m[:, 0].astype(F32).reshape(nd, 1, N_HD), ((0, 0), (0, 0), (0, LANES - N_HD))))
    cbf, cbb, nbf, nbb, mbf, mbb = _state_scan(k, v, gt, n_ctx_tiles, nd, lat_chunks, init, True, False)

    lat_idx = lambda i: jnp.maximum(i - n_ctx_tiles, 0)
    st4 = pl.BlockSpec((1, N_HEADS, D_QK, D_V), lambda i: (lat_idx(i), 0, 0, 0))
    st3 = pl.BlockSpec((1, N_HEADS, LANES), lambda i: (lat_idx(i), 0, 0))
    st1 = pl.BlockSpec((1, 1, LANES), lambda i: (lat_idx(i), 0, 0))
    rw = jnp.pad(router_w[0], ((0, 0), (0, LANES - n_experts)))
    rb = jnp.pad(router_b[0], (0, LANES - n_experts)).reshape(1, LANES)
    x1, h2, route, route_t, counts = pl.pallas_call(
        functools.partial(_mix_kernel, n_ctx_tiles=n_ctx_tiles, n_experts=n_experts),
        grid=(n_tiles,),
        in_specs=[tile(d), tile(d), tile(d), tile(d), tile(d), tile(d), tile(2 * LANES), ctx_spec, lat_spec,
                  pl.BlockSpec((1, 8, d), mod_row(TM)), st4, st4, st3, st3, st1, st1,
                  _const_spec((1, d)), _const_spec((d, d)), _const_spec((d, d)), _const_spec((1, d)),
                  _const_spec((d, LANES)), _const_spec((1, LANES))],
        out_specs=[tile(d), rows(TM), tile(LANES), pl.BlockSpec((8, TM), lambda i: (0, i)),
                   pl.BlockSpec((1, LANES), lambda i: (0, 0))],
        out_shape=[jax.ShapeDtypeStruct((n_tok, d), F32), jax.ShapeDtypeStruct((n_tok * ROW_SUB, LANES), F32),
                   jax.ShapeDtypeStruct((n_tok, LANES), F32), jax.ShapeDtypeStruct((8, n_tok), jnp.int32),
                   jax.ShapeDtypeStruct((1, LANES), F32)],
        scratch_shapes=[pltpu.VMEM((TM, d), BF16), pltpu.VMEM((1, LANES), F32)],
        compiler_params=params,
        name="mlstm_mix_router",
    )(q, k, v, so, sgb, ycg, gt, x_ctx, x_lat, mod, cbf, cbb, nbf, nbb, mbf, mbb,
      mh_norm_g, w_m_out[0].astype(BF16), w_o[0].astype(BF16), norm2_g, rw, rb)

    n_rows = TOP_K * n_tok
    offs, it_tile, it_exp, it_flags, it_lo, it_hi = _work_items(counts[0, :n_experts], n_experts, n_rows, TMX)
    is_exp = route_t[0:TOP_K, :, None] == jnp.arange(n_experts, dtype=jnp.int32)
    pos_t = jnp.sum(jnp.where(is_exp, offs[:n_experts], 0), axis=-1) + route_t[TOP_K:2 * TOP_K]
    pos_spec = lambda f: pl.BlockSpec((TOP_K, TM), f, memory_space=pltpu.SMEM)
    any_spec = pl.BlockSpec(memory_space=pl.ANY)

    xs = pl.pallas_call(
        _dispatch_kernel,
        grid=(n_tiles,),
        in_specs=[pos_spec(lambda i: (0, i)), rows(TM)],
        out_specs=any_spec,
        out_shape=jax.ShapeDtypeStruct((n_rows * ROW_SUB, LANES), F32),
        scratch_shapes=[pltpu.SemaphoreType.DMA],
        compiler_params=params,
        name="moe_dispatch",
    )(pos_t, h2)

    dff = w_gate.shape[-1]
    wspec = lambda a, b_: pl.BlockSpec((1, a, b_), lambda w, tl, ex, *_: (ex[w], 0, 0))
    ys = pl.pallas_call(
        _expert_kernel,
        grid_spec=pltpu.PrefetchScalarGridSpec(
            num_scalar_prefetch=5,
            grid=(it_tile.shape[0],),
            in_specs=[pl.BlockSpec((TMX * ROW_SUB, LANES), lambda w, tl, *_: (tl[w], 0)),
                      wspec(d, dff), wspec(1, dff), wspec(d, dff), wspec(1, dff), wspec(dff, d), wspec(1, d)],
            out_specs=pl.BlockSpec((TMX * ROW_SUB, LANES), lambda w, tl, *_: (tl[w], 0)),
            scratch_shapes=[pltpu.VMEM((3, d, dff), BF16)]),
        out_shape=jax.ShapeDtypeStruct((n_rows * ROW_SUB, LANES), F32),
        compiler_params=params,
        name="moe_experts",
    )(it_tile, it_exp, it_flags, it_lo, it_hi, xs, w_gate[0], b_gate[0].reshape(n_experts, 1, dff), w_lin[0],
      b_lin[0].reshape(n_experts, 1, dff), w_down[0], b_down[0].reshape(n_experts, 1, d))

    ctx_i = lambda i: (jnp.minimum(i, n_ctx_tiles - 1), 0)
    lat_i = lambda i: (jnp.maximum(i - n_ctx_tiles, 0), 0)
    y_prompt, y_sample = pl.pallas_call(
        functools.partial(_combine_kernel, n_ctx_tiles=n_ctx_tiles),
        grid=(n_tiles,),
        in_specs=[pos_spec(lambda i: (0, i)), pos_spec(lambda i: (0, jnp.minimum(i + 1, n_tiles - 1))),
                  tile(LANES), tile(d), pl.BlockSpec((1, 8, d), mod_row(TM)), _const_spec((1, d)), any_spec],
        out_specs=[pl.BlockSpec((TM, d), ctx_i), pl.BlockSpec((TM, d), lat_i)],
        out_shape=[jax.ShapeDtypeStruct((t_ctx, d), F32), jax.ShapeDtypeStruct((t_lat, d), F32)],
        scratch_shapes=[pltpu.VMEM((2, TOP_K, TM * ROW_SUB, LANES), F32), pltpu.SemaphoreType.DMA((2,))],
        compiler_params=params,
        name="moe_combine",
    )(pos_t, pos_t, route, x1, mod, final_g.reshape(1, d), ys)

    y_prompt = y_prompt.reshape(nb, seq, d)
    y_sample = y_sample.reshape(nd, dseq, d)
    new_c = c_new.reshape(nb, 1, 2, N_HEADS, D_QK, D_V)
    new_n = n_new[:, :, :D_QK].reshape(nb, 1, 2, N_HEADS, D_QK)
    new_m = m_new[:, 0, :N_HD].reshape(nb, 1, 2, N_HEADS)
    return (y_prompt, y_sample, new_c, new_n, new_m)
```

```python
import functools

import jax
import jax.numpy as jnp
from jax import lax
from jax.experimental import pallas as pl
from jax.experimental.pallas import tpu as pltpu

F32 = jnp.float32
BF16 = jnp.bfloat16
HIGHEST = lax.Precision.HIGHEST

N_HEADS = 8
D_QK = 64
D_V = 128
GRID_W = 64
TOP_K = 4
SWIGLU_LIMIT = 7.0
SWIGLU_ALPHA = 1.702
EPS = 1e-6
N_MOD = 6

LANES = 128
TM = 256
TMX = 512
N_HD = 2 * N_HEADS
VMEM_LIMIT = 56 * 1024 * 1024

_D = 1024
C_CONV = (0, 3 * _D)
C_Q = (3 * _D, 4 * _D)
C_K = (4 * _D, 5 * _D)
C_V = (5 * _D, 6 * _D)
C_O = (6 * _D, 7 * _D)
C_GA = (7 * _D, 8 * _D)
C_GB = (8 * _D, 9 * _D)
C_G = (9 * _D, 9 * _D + 2 * LANES)
W_COLS = C_G[1]


def _dot(a, b, precision=None):
    return jnp.dot(a, b, preferred_element_type=F32, precision=precision)


def _rms(x):
    return x * lax.rsqrt(jnp.mean(x * x, axis=-1, keepdims=True) + EPS)


ROW_SUB = 8


def _store_rows(ref, val):
    n = val.shape[0]
    for s in range(ROW_SUB):
        ref[pl.ds(s, n, stride=ROW_SUB), :] = val[:, s * LANES:(s + 1) * LANES]


def _load_rows(ref):
    n = ref.shape[0] // ROW_SUB
    return jnp.concatenate([ref[pl.ds(s, n, stride=ROW_SUB), :] for s in range(ROW_SUB)], axis=1)


def _row_tile(ref, r):
    return ref.at[pl.ds(pl.multiple_of(r * ROW_SUB, ROW_SUB), ROW_SUB)]


def _const_spec(shape):
    return pl.BlockSpec(shape, lambda *_: (0,) * len(shape), pipeline_mode=pl.Buffered(1))


def _mod_kernel(c_ref, w_ref, b_ref, o_ref):
    c = c_ref[...]
    o_ref[...] = _dot(c * jax.nn.sigmoid(c), w_ref[...], HIGHEST) + b_ref[...]


def _modulation(cvec, ada_w, ada_b):
    n, d = cvec.shape
    nout = ada_w.shape[1]
    return pl.pallas_call(
        _mod_kernel,
        grid=(nout // d,),
        in_specs=[pl.BlockSpec((n, d), lambda j: (0, 0)),
                  pl.BlockSpec((d, d), lambda j: (0, j)),
                  pl.BlockSpec((1, d), lambda j: (0, j))],
        out_specs=pl.BlockSpec((n, d), lambda j: (0, j)),
        out_shape=jax.ShapeDtypeStruct((n, nout), F32),
        name="adaln_mod",
    )(cvec, ada_w, ada_b.reshape(1, nout))


def _proj_kernel(xc_ref, xl_ref, mod_ref, g1_ref, w_ref, b_ref, cw_ref, cb_ref, wco_ref,
                 ycg_ref, sgb_ref, q_ref, k_ref, v_ref, so_ref, gt_ref, *, n_ctx_tiles, ctx_row):
    i = pl.program_id(0)
    x = jnp.where(i < n_ctx_tiles, xc_ref[...], xl_ref[...])
    h = (_rms(x) * g1_ref[...] * (1.0 + mod_ref[0, 1:2, :]) + mod_ref[0, 0:1, :]).astype(BF16)

    def proj(cols):
        return _dot(h, w_ref[:, cols[0]:cols[1]]) + b_ref[:, cols[0]:cols[1]]

    zc = proj(C_CONV)
    d = x.shape[1]
    u = zc[:, 2 * d:3 * d] * zc[:, 0:d]
    rowlen = jnp.where(i < n_ctx_tiles, ctx_row, GRID_W)
    pos = lax.broadcasted_iota(jnp.int32, (TM, 1), 0) & (rowlen - 1)
    u_prev = jnp.where(pos == 0, 0.0, pltpu.roll(u, 1, 0))
    u_next = jnp.where(pos == rowlen - 1, 0.0, pltpu.roll(u, TM - 1, 0))
    uc = u_prev * cw_ref[0:1, :] + u * cw_ref[1:2, :] + u_next * cw_ref[2:3, :] + cb_ref[...]
    yconv = _dot((zc[:, d:2 * d] * uc).astype(BF16), wco_ref[...])

    ycg_ref[...] = (jax.nn.sigmoid(proj(C_GA)) * yconv).astype(BF16)
    sgb_ref[...] = jax.nn.sigmoid(proj(C_GB)).astype(BF16)
    q_ref[...] = (proj(C_Q) * (D_QK ** -0.5)).astype(BF16)
    k_ref[...] = proj(C_K).astype(BF16)
    v_ref[...] = proj(C_V).astype(BF16)
    so_ref[...] = jax.nn.sigmoid(proj(C_O)).astype(BF16)
    gt_ref[...] = proj(C_G)


def _tri_masks():
    row = lax.broadcasted_iota(jnp.int32, (TM, TM), 0)
    col = lax.broadcasted_iota(jnp.int32, (TM, TM), 1)
    return row >= col, row <= col


def _gate_cumsums(gi, gf):
    lower, upper = _tri_masks()
    lf = jax.nn.log_sigmoid(gf)
    fwd_lane = lax.broadcasted_iota(jnp.int32, (1, LANES), 1) < N_HEADS
    bsum = jnp.where(fwd_lane, _dot(lower.astype(F32), lf, HIGHEST), _dot(upper.astype(F32), lf, HIGHEST))
    return lf, bsum, fwd_lane


def _state_kernel(*refs, zero_init, emit_before, emit_after):
    it = iter(refs)
    kf_ref, vf_ref, gf_ref, kb_ref, vb_ref, gb_ref = (next(it) for _ in range(6))
    if not zero_init:
        c0_ref, n0_ref, m0_ref = (next(it) for _ in range(3))
    if emit_before:
        cbf_ref, cbb_ref, nbf_ref, nbb_ref, mbf_ref, mbb_ref = (next(it) for _ in range(6))
    if emit_after:
        ca_ref, na_ref, ma_ref = (next(it) for _ in range(3))
    c_scr, n_scr, m_scr = (next(it) for _ in range(3))
    c = pl.program_id(1)

    @pl.when(c == 0)
    def _():
        if zero_init:
            c_scr[...] = jnp.zeros_like(c_scr)
            n_scr[...] = jnp.zeros_like(n_scr)
            m_scr[...] = jnp.zeros_like(m_scr)
        else:
            c_scr[...] = c0_ref[0]
            n_scr[...] = n0_ref[0]
            m_scr[...] = m0_ref[0]

    if emit_before:
        cbf_ref[0] = c_scr[0:N_HEADS]
        cbb_ref[0] = c_scr[N_HEADS:N_HD]
        nbf_ref[0] = n_scr[0:N_HEADS]
        nbb_ref[0] = n_scr[N_HEADS:N_HD]
        mbf_ref[0] = m_scr[...]
        mbb_ref[0] = m_scr[...]

    fwd_lane = lax.broadcasted_iota(jnp.int32, (1, LANES), 1) < N_HEADS
    gi = jnp.where(fwd_lane, gf_ref[:, 0:LANES], gb_ref[:, 0:LANES])
    gfg = jnp.where(fwd_lane, gf_ref[:, LANES:2 * LANES], gb_ref[:, LANES:2 * LANES])
    lf, bsum, _ = _gate_cumsums(gi, gfg)
    total = jnp.sum(lf, axis=0, keepdims=True)
    g = total - bsum + gi
    m_prev = m_scr[...]
    m_new = jnp.maximum(total + m_prev, jnp.max(g, axis=0, keepdims=True))
    wk = jnp.exp(g - m_new)
    decay = jnp.exp(total + m_prev - m_new)
    for hd in range(N_HD):
        h = hd % N_HEADS
        k_ref, v_ref = (kf_ref, vf_ref) if hd < N_HEADS else (kb_ref, vb_ref)
        wkk = wk[:, hd:hd + 1] * k_ref[:, h * LANES:(h + 1) * LANES].astype(F32)
        upd = lax.dot_general(wkk.astype(BF16), v_ref[:, h * D_V:(h + 1) * D_V],
                              (((0,), (0,)), ((), ())), preferred_element_type=F32)
        dec = decay[:, hd:hd + 1]
        c_scr[hd] = dec * c_scr[hd] + upd[0:D_QK, :]
        n_scr[hd:hd + 1, :] = dec * n_scr[hd:hd + 1, :] + jnp.sum(wkk, axis=0, keepdims=True)
    m_scr[...] = m_new

    if emit_after:
        @pl.when(c == pl.num_programs(1) - 1)
        def _():
            ca_ref[0] = c_scr[...]
            na_ref[0] = n_scr[...]
            ma_ref[0] = m_scr[...]


def _state_scan(k, v, gt, tile0, n_seq, n_chunk, init, emit_before, emit_after):
    d = k.shape[1]
    fwd = lambda s, c: (tile0 + s * n_chunk + c, 0)
    bwd = lambda s, c: (tile0 + s * n_chunk + n_chunk - 1 - c, 0)
    in_specs = [pl.BlockSpec((TM, d), fwd), pl.BlockSpec((TM, d), fwd), pl.BlockSpec((TM, 2 * LANES), fwd),
                pl.BlockSpec((TM, d), bwd), pl.BlockSpec((TM, d), bwd), pl.BlockSpec((TM, 2 * LANES), bwd)]
    args = [k, v, gt, k, v, gt]
    if init is not None:
        in_specs += [pl.BlockSpec((1, N_HD, D_QK, D_V), lambda s, c: (s, 0, 0, 0)),
                     pl.BlockSpec((1, N_HD, LANES), lambda s, c: (s, 0, 0)),
                     pl.BlockSpec((1, 1, LANES), lambda s, c: (s, 0, 0))]
        args += list(init)
    out_specs, out_shape = [], []
    n_tot = n_seq * n_chunk
    if emit_before:
        cf = lambda s, c: (s * n_chunk + c, 0, 0, 0)
        cb = lambda s, c: (s * n_chunk + n_chunk - 1 - c, 0, 0, 0)
        nf = lambda s, c: (s * n_chunk + c, 0, 0)
        nb = lambda s, c: (s * n_chunk + n_chunk - 1 - c, 0, 0)
        out_specs += [pl.BlockSpec((1, N_HEADS, D_QK, D_V), cf), pl.BlockSpec((1, N_HEADS, D_QK, D_V), cb),
                      pl.BlockSpec((1, N_HEADS, LANES), nf), pl.BlockSpec((1, N_HEADS, LANES), nb),
                      pl.BlockSpec((1, 1, LANES), nf), pl.BlockSpec((1, 1, LANES), nb)]
        out_shape += [jax.ShapeDtypeStruct((n_tot, N_HEADS, D_QK, D_V), F32)] * 2
        out_shape += [jax.ShapeDtypeStruct((n_tot, N_HEADS, LANES), F32)] * 2
        out_shape += [jax.ShapeDtypeStruct((n_tot, 1, LANES), F32)] * 2
    if emit_after:
        out_specs += [pl.BlockSpec((1, N_HD, D_QK, D_V), lambda s, c: (s, 0, 0, 0)),
                      pl.BlockSpec((1, N_HD, LANES), lambda s, c: (s, 0, 0)),
                      pl.BlockSpec((1, 1, LANES), lambda s, c: (s, 0, 0))]
        out_shape += [jax.ShapeDtypeStruct((n_seq, N_HD, D_QK, D_V), F32),
                      jax.ShapeDtypeStruct((n_seq, N_HD, LANES), F32),
                      jax.ShapeDtypeStruct((n_seq, 1, LANES), F32)]
    return pl.pallas_call(
        functools.partial(_state_kernel, zero_init=init is None, emit_before=emit_before, emit_after=emit_after),
        grid=(n_seq, n_chunk),
        in_specs=in_specs,
        out_specs=out_specs,
        out_shape=out_shape,
        scratch_shapes=[pltpu.VMEM((N_HD, D_QK, D_V), F32), pltpu.VMEM((N_HD, LANES), F32),
                        pltpu.VMEM((1, LANES), F32)],
        compiler_params=pltpu.CompilerParams(dimension_semantics=("arbitrary", "arbitrary"),
                                             vmem_limit_bytes=VMEM_LIMIT),
        name="mlstm_state_scan",
    )(*args)


def _mix_kernel(q_ref, k_ref, v_ref, so_ref, sgb_ref, ycg_ref, gt_ref, xc_ref, xl_ref, mod_ref,
                cbf_ref, cbb_ref, nbf_ref, nbb_ref, mbf_ref, mbb_ref,
                gmh_ref, wmo_ref, wo_ref, g2_ref, rw_ref, rb_ref,
                x1_ref, h2_ref, route_ref, routet_ref, cnt_ref, hm_scr, carry_scr, *, n_ctx_tiles, n_experts):
    i = pl.program_id(0)
    is_lat = i >= n_ctx_tiles
    lat_f = is_lat.astype(F32)
    lower, upper = _tri_masks()
    gi = gt_ref[:, 0:LANES]
    lf, bsum, fwd_lane = _gate_cumsums(gi, gt_ref[:, LANES:2 * LANES])
    a = gi - bsum
    t_idx = lax.broadcasted_iota(jnp.int32, (TM, LANES), 0)
    pmax, smax = a, a
    step = 1
    while step < TM:
        pmax = jnp.maximum(pmax, jnp.where(t_idx >= step, pltpu.roll(pmax, step, 0), -jnp.inf))
        smax = jnp.maximum(smax, jnp.where(t_idx < TM - step, pltpu.roll(smax, TM - step, 0), -jnp.inf))
        step *= 2
    m_prev = jnp.where(fwd_lane, mbf_ref[0], mbb_ref[0]) * lat_f
    mrow = jnp.maximum(m_prev, jnp.where(fwd_lane, pmax, smax))
    w_inter = jnp.exp(m_prev - mrow)
    e_floor = jnp.exp(-(bsum + mrow))
    a_t = a.T
    n_all = jnp.concatenate([nbf_ref[0], nbb_ref[0], jnp.zeros((LANES - N_HD, LANES), F32)], axis=0) * lat_f
    n_t = n_all.T
    lane0 = lax.broadcasted_iota(jnp.int32, (1, LANES), 1) == 0
    ones_col = jnp.where(lane0, 1.0, 0.0).astype(BF16) * jnp.ones((TM, 1), BF16)
    zpad = jnp.zeros((LANES - D_QK, D_V), F32)

    for h in range(N_HEADS):
        qh = q_ref[:, h * LANES:(h + 1) * LANES]
        kh = k_ref[:, h * LANES:(h + 1) * LANES]
        vext = jnp.concatenate([v_ref[:, h * D_V:(h + 1) * D_V], ones_col], axis=1)
        qk = lax.dot_general(qh, kh, (((1,), (1,)), ((), ())), preferred_element_type=F32)
        hsum = None
        for d in range(2):
            hd = d * N_HEADS + h
            mask = lower if d == 0 else upper
            c_prev = (cbf_ref if d == 0 else cbb_ref)[0, h]
            e = jnp.exp(jnp.where(mask, a_t[hd:hd + 1, :] - mrow[:, hd:hd + 1], -jnp.inf))
            nd = _dot((qk * e).astype(BF16), vext)
            cext = jnp.concatenate(
                [jnp.concatenate([c_prev * lat_f, zpad], axis=0),
                 jnp.where(lane0, n_t[:, hd:hd + 1], 0.0)], axis=1).astype(BF16)
            qc = _dot(qh, cext)
            wi = w_inter[:, hd:hd + 1]
            num = nd[:, 0:D_V] + wi * qc[:, 0:D_V]
            den = nd[:, D_V:D_V + 1] + wi * qc[:, D_V:D_V + 1]
            r = 1.0 / jnp.maximum(jnp.abs(den), e_floor[:, hd:hd + 1])
            hsum = num * r if hsum is None else hsum + num * r
        hs = slice(h * D_V, (h + 1) * D_V)
        hm_scr[:, hs] = (_rms(hsum) * gmh_ref[:, hs] * so_ref[:, hs].astype(F32)).astype(BF16)

    ym = _dot(hm_scr[...], wmo_ref[...])
    mix = (ycg_ref[...].astype(F32) + sgb_ref[...].astype(F32) * ym).astype(BF16)
    x1 = jnp.where(is_lat, xl_ref[...], xc_ref[...]) + mod_ref[0, 2:3, :] * _dot(mix, wo_ref[...])
    x1_ref[...] = x1
    h2 = _rms(x1) * g2_ref[...] * (1.0 + mod_ref[0, 4:5, :]) + mod_ref[0, 3:4, :]
    _store_rows(h2_ref, h2)

    lane = lax.broadcasted_iota(jnp.int32, (TM, LANES), 1)
    work = jnp.where(lane < n_experts, _dot(h2, rw_ref[...], HIGHEST) + rb_ref[...], -jnp.inf)
    sels, exps, idxs = [], [], []
    top = None
    for _ in range(TOP_K):
        mx = jnp.max(work, axis=-1, keepdims=True)
        ix = jnp.min(jnp.where(work == mx, lane, LANES), axis=-1, keepdims=True)
        sel = lane == ix
        work = jnp.where(sel, -jnp.inf, work)
        top = mx if top is None else top
        sels.append(sel)
        idxs.append(ix.astype(F32))
        exps.append(jnp.exp(mx - top))
    inv = 1.0 / functools.reduce(lambda p, q: p + q, exps)

    @pl.when(i == 0)
    def _():
        carry_scr[...] = jnp.zeros_like(carry_scr)

    onehot = functools.reduce(lambda p, q: p + q, [jnp.where(s, 1.0, 0.0) for s in sels])
    row = lax.broadcasted_iota(jnp.int32, (TM, TM), 0)
    col = lax.broadcasted_iota(jnp.int32, (TM, TM), 1)
    before = _dot((row > col).astype(BF16), onehot.astype(BF16)) + carry_scr[...]
    carry_scr[...] += jnp.sum(onehot, axis=0, keepdims=True)
    cnt_ref[...] = carry_scr[...]
    route = jnp.zeros((TM, LANES), F32)
    for j in range(TOP_K):
        slot = jnp.sum(jnp.where(sels[j], before, 0.0), axis=-1, keepdims=True)
        route = jnp.where(lane == j, idxs[j], route)
        route = jnp.where(lane == TOP_K + j, slot, route)
        route = jnp.where(lane == 2 * TOP_K + j, exps[j] * inv, route)
    route_ref[...] = route
    routet_ref[...] = route.T[0:8, :].astype(jnp.int32)


def _dispatch_kernel(pos_ref, h2_ref, xs_ref, sem):
    def copy(t, j):
        return pltpu.make_async_copy(_row_tile(h2_ref, t), _row_tile(xs_ref, pos_ref[j, t]), sem)

    def start(t, carry):
        for j in range(TOP_K):
            copy(t, j).start(priority=j % 2)
        return carry

    lax.fori_loop(0, TM, start, 0)
    for j in range(TOP_K):
        pltpu.make_async_copy(h2_ref, xs_ref.at[pl.ds(0, TM * ROW_SUB)], sem).wait()


def _expert_kernel(tile_ref, exp_ref, flag_ref, lo_ref, hi_ref,
                   xs_ref, wg_ref, bg_ref, wl_ref, bl_ref, wd_ref, bd_ref, ys_ref, w_scr):
    w = pl.program_id(0)
    flags = flag_ref[w]

    @pl.when((flags & 4) != 0)
    def _():
        w_scr[0] = wg_ref[0].astype(BF16)
        w_scr[1] = wl_ref[0].astype(BF16)
        w_scr[2] = wd_ref[0].astype(BF16)

    @pl.when((flags & 1) != 0)
    def _():
        x = _load_rows(xs_ref).astype(BF16)
        gt =jnp.minimum(_dot(x, w_scr[0]) + bg_ref[0], SWIGLU_LIMIT)
        lin = jnp.clip(_dot(x, w_scr[1]) + bl_ref[0], -SWIGLU_LIMIT, SWIGLU_LIMIT)
        act = gt * jax.nn.sigmoid(SWIGLU_ALPHA * gt) * (lin + 1.0)
        y = _dot(act.astype(BF16), w_scr[2]) + bd_ref[0]
        rows = lax.broadcasted_iota(jnp.int32, (y.shape[0], 1), 0)
        mine = (rows >= lo_ref[w]) & (rows < hi_ref[w])

        @pl.when((flags & 2) != 0)
        def _():
            _store_rows(ys_ref, jnp.where(mine, y, 0.0))

        @pl.when((flags & 2) == 0)
        def _():
            _store_rows(ys_ref, jnp.where(mine, y, _load_rows(ys_ref)))


def _combine_kernel(pos_ref, posn_ref, route_ref, x1_ref, mod_ref, fg_ref, ys_ref, outc_ref, outl_ref,
                    buf, sem, *, n_ctx_tiles):
    i = pl.program_id(0)
    n = pl.num_programs(0)

    def copy(p_ref, slot, t, j):
        return pltpu.make_async_copy(_row_tile(ys_ref, p_ref[j, t]), _row_tile(buf.at[slot, j], t), sem.at[slot])

    def start_all(p_ref, slot):
        def body(t, carry):
            for j in range(TOP_K):
                copy(p_ref, slot, t, j).start(priority=j % 2)
            return carry
        lax.fori_loop(0, TM, body, 0)

    @pl.when(i == 0)
    def _():
        start_all(pos_ref, 0)

    @pl.when(i + 1 < n)
    def _():
        start_all(posn_ref, (i + 1) % 2)

    slot = i % 2

    for j in range(TOP_K):
        pltpu.make_async_copy(ys_ref.at[pl.ds(0, TM * ROW_SUB)], buf.at[slot, j], sem.at[slot]).wait()
    acc = None
    for j in range(TOP_K):
        term = route_ref[:, 2 * TOP_K + j:2 * TOP_K + j + 1] * _load_rows(buf.at[slot, j])
        acc = term if acc is None else acc + term
    out = _rms(x1_ref[...] + mod_ref[0, 5:6, :] * acc) * fg_ref[...]

    @pl.when(i < n_ctx_tiles)
    def _():
        outc_ref[...] = out

    @pl.when(i >= n_ctx_tiles)
    def _():
        outl_ref[...] = out


def _work_items(counts, n_experts, n_rows, tmx):
    n_items_max = n_rows // tmx + n_experts - 1
    cnt = counts.astype(jnp.int32)
    offs = jnp.concatenate([jnp.zeros((1,), jnp.int32), jnp.cumsum(cnt)])
    first_tile = offs[:-1] // tmx
    n_it = jnp.where(cnt > 0, (offs[1:] - 1) // tmx - first_tile + 1, 0)
    it_start = jnp.concatenate([jnp.zeros((1,), jnp.int32), jnp.cumsum(n_it)])
    total = it_start[-1]
    w = jnp.arange(n_items_max, dtype=jnp.int32)
    wc = jnp.minimum(w, total - 1)
    e = jnp.sum((it_start[None, 1:] <= wc[:, None]).astype(jnp.int32), axis=1)
    e = jnp.minimum(e, n_experts - 1)
    tile = first_tile[e] + wc - it_start[e]
    valid = w < total
    prev = lambda a: jnp.concatenate([jnp.full((1,), -1, jnp.int32), a[:-1]])
    flags = (valid.astype(jnp.int32) + 2 * (valid & (tile != prev(tile))).astype(jnp.int32)
             + 4 * (valid & (e != prev(e))).astype(jnp.int32))
    lo = jnp.clip(offs[e] - tile * tmx, 0, tmx)
    hi = jnp.clip(offs[e + 1] - tile * tmx, 0, tmx)
    return offs, tile, e, flags, lo, hi


def _pack_in_proj(w, b):
    d = w.shape[0]
    o_q, o_k, o_v, o_o, o_g, o_ga, o_gb = 3 * d, 3 * d + 512, 3 * d + 1024, 4 * d + 1024, 5 * d + 1024, \
        5 * d + 1024 + 4 * N_HEADS, 6 * d + 1024 + 4 * N_HEADS

    def pad_heads(m):
        m = m.reshape(m.shape[0], N_HEADS, D_QK)
        return jnp.pad(m, ((0, 0), (0, 0), (0, LANES - D_QK))).reshape(m.shape[0], N_HEADS * LANES)

    def gates(m):
        gz = jnp.zeros((m.shape[0], LANES - N_HD), m.dtype)
        i_f, f_f, i_b, f_b = (m[:, o_g + j * N_HEADS:o_g + (j + 1) * N_HEADS] for j in range(4))
        return jnp.concatenate([i_f, i_b, gz, f_f, f_b, gz], axis=1)

    def pack(m):
        return jnp.concatenate([m[:, 0:o_q], pad_heads(m[:, o_q:o_k]), pad_heads(m[:, o_k:o_v]), m[:, o_v:o_o],
                                m[:, o_o:o_g], m[:, o_ga:o_gb], m[:, o_gb:o_gb + d], gates(m)], axis=1)

    return pack(w).astype(BF16), pack(b.reshape(1, -1))


def kernel(x_prompt, x_sample, c, state_C, state_n, state_m, c_ctx, ada_w, ada_b, norm1_g, norm2_g, w_in, b_in,
           conv_w, conv_b, w_conv_out, mh_norm_g, w_m_out, w_o, router_w, router_b, w_gate, b_gate, w_lin, b_lin,
           w_down, b_down, final_g):
    nb, seq, d = x_prompt.shape
    nd, dseq, _ = x_sample.shape
    n_experts = w_gate.shape[1]
    assert d == _D and w_in.shape[0] == 1 and seq == TM and dseq % TM == 0 and TM % GRID_W == 0
    t_ctx, t_lat = nb * seq, nd * dseq
    n_tok = t_ctx + t_lat
    n_ctx_tiles, n_tiles = t_ctx // TM, n_tok // TM
    lat_chunks = dseq // TM
    n_lat_tiles = n_tiles - n_ctx_tiles

    n_c = 1 + nd
    n_cp = -(-n_c // 8) * 8
    cvec = jnp.concatenate([c_ctx[None, :], c, jnp.zeros((n_cp - n_c, d), F32)], axis=0)
    mod = _modulation(cvec, ada_w[0], ada_b[0]).reshape(n_cp, N_MOD, d)
    mod = jnp.pad(mod, ((0, 0), (0, 8 - N_MOD), (0, 0)))

    def mod_row(tile_tokens):
        ctx_t, per_seq = t_ctx // tile_tokens, dseq // tile_tokens
        return lambda i, *_: (jnp.where(i < ctx_t, 0, 1 + (i - ctx_t) // per_seq), 0, 0)

    x_ctx, x_lat = x_prompt.reshape(t_ctx, d), x_sample.reshape(t_lat, d)
    ctx_spec = pl.BlockSpec((TM, d), lambda i: (jnp.minimum(i, n_ctx_tiles - 1), 0))
    lat_spec = pl.BlockSpec((TM, d), lambda i: (jnp.maximum(i - n_ctx_tiles, 0), 0))
    w_all, b_all = _pack_in_proj(w_in[0], b_in[0])
    tile = lambda cols: pl.BlockSpec((TM, cols), lambda i: (i, 0))
    rows = lambda n: pl.BlockSpec((n * ROW_SUB, LANES), lambda i: (i, 0))
    params = pltpu.CompilerParams(dimension_semantics=("arbitrary",), vmem_limit_bytes=VMEM_LIMIT)
    bf = lambda cols: jax.ShapeDtypeStruct((n_tok, cols), BF16)

    ycg, sgb, q, k, v, so, gt = pl.pallas_call(
        functools.partial(_proj_kernel, n_ctx_tiles=n_ctx_tiles, ctx_row=seq),
        grid=(n_tiles,),
        in_specs=[ctx_spec, lat_spec, pl.BlockSpec((1, 8, d), mod_row(TM)), _const_spec((1, d)),
                  _const_spec((d, W_COLS)), _const_spec((1, W_COLS)), _const_spec((3, d)), _const_spec((1, d)),
                  _const_spec((d, d))],
        out_specs=[tile(d), tile(d), tile(d), tile(d), tile(d), tile(d), tile(2 * LANES)],
        out_shape=[bf(d), bf(d), bf(d), bf(d), bf(d), bf(d), jax.ShapeDtypeStruct((n_tok, 2 * LANES), F32)],
        compiler_params=params,
        name="in_proj_conv",
    )(x_ctx, x_lat, mod, norm1_g, w_all, b_all, conv_w[0], conv_b, w_conv_out[0].astype(BF16))

    c_new, n_new, m_new = _state_scan(k, v, gt, 0, nb, 1, None, False, True)
    init = (state_C[:, 0].astype(F32).reshape(nd, N_HD, D_QK, D_V),
            jnp.pad(state_n[:, 0].astype(F32).reshape(nd, N_HD, D_QK), ((0, 0), (0, 0), (0, LANES - D_QK))),
            jnp.pad(state_m[:, 0].astype(F32).reshape(nd, 1, N_HD), ((0, 0), (0, 0), (0, LANES - N_HD))))
    cbf, cbb, nbf, nbb, mbf, mbb = _state_scan(k, v, gt, n_ctx_tiles, nd, lat_chunks, init, True, False)

    lat_idx = lambda i: jnp.maximum(i - n_ctx_tiles, 0)
    st4 = pl.BlockSpec((1, N_HEADS, D_QK, D_V), lambda i: (lat_idx(i), 0, 0, 0))
    st3 = pl.BlockSpec((1, N_HEADS, LANES), lambda i: (lat_idx(i), 0, 0))
    st1 = pl.BlockSpec((1, 1, LANES), lambda i: (lat_idx(i), 0, 0))
    rw = jnp.pad(router_w[0], ((0, 0), (0, LANES - n_experts)))
    rb = jnp.pad(router_b[0], (0, LANES - n_experts)).reshape(1, LANES)
    x1, h2, route, route_t, counts = pl.pallas_call(
        functools.partial(_mix_kernel, n_ctx_tiles=n_ctx_tiles, n_experts=n_experts),
        grid=(n_tiles,),
        in_specs=[tile(d), tile(d), tile(d), tile(d), tile(d), tile(d), tile(2 * LANES), ctx_spec, lat_spec,
                  pl.BlockSpec((1, 8, d), mod_row(TM)), st4, st4, st3, st3, st1, st1,
                  _const_spec((1, d)), _const_spec((d, d)), _const_spec((d, d)), _const_spec((1, d)),
                  _const_spec((d, LANES)), _const_spec((1, LANES))],
        out_specs=[tile(d), rows(TM), tile(LANES), pl.BlockSpec((8, TM), lambda i: (0, i)),
                   pl.BlockSpec((1, LANES), lambda i: (0, 0))],
        out_shape=[jax.ShapeDtypeStruct((n_tok, d), F32), jax.ShapeDtypeStruct((n_tok * ROW_SUB, LANES), F32),
                   jax.ShapeDtypeStruct((n_tok, LANES), F32), jax.ShapeDtypeStruct((8, n_tok), jnp.int32),
                   jax.ShapeDtypeStruct((1, LANES), F32)],
        scratch_shapes=[pltpu.VMEM((TM, d), BF16), pltpu.VMEM((1, LANES), F32)],
        compiler_params=params,
        name="mlstm_mix_router",
    )(q, k, v, so, sgb, ycg, gt, x_ctx, x_lat, mod, cbf, cbb, nbf, nbb, mbf, mbb,
      mh_norm_g, w_m_out[0].astype(BF16), w_o[0].astype(BF16), norm2_g, rw, rb)

    n_rows = TOP_K * n_tok
    offs, it_tile, it_exp, it_flags, it_lo, it_hi = _work_items(counts[0, :n_experts], n_experts, n_rows, TMX)
    is_exp = route_t[0:TOP_K, :, None] == jnp.arange(n_experts, dtype=jnp.int32)
    pos_t = jnp.sum(jnp.where(is_exp, offs[:n_experts], 0), axis=-1) + route_t[TOP_K:2 * TOP_K]
    pos_spec = lambda f: pl.BlockSpec((TOP_K, TM), f, memory_space=pltpu.SMEM)
    any_spec = pl.BlockSpec(memory_space=pl.ANY)

    xs = pl.pallas_call(
        _dispatch_kernel,
        grid=(n_tiles,),
        in_specs=[pos_spec(lambda i: (0, i)), rows(TM)],
        out_specs=any_spec,
        out_shape=jax.ShapeDtypeStruct((n_rows * ROW_SUB, LANES), F32),
        scratch_shapes=[pltpu.SemaphoreType.DMA],
        compiler_params=params,
        name="moe_dispatch",
    )(pos_t, h2)

    dff = w_gate.shape[-1]
    wspec = lambda a, b_: pl.BlockSpec((1, a, b_), lambda w, tl, ex, *_: (ex[w], 0, 0))
    ys = pl.pallas_call(
        _expert_kernel,
        grid_spec=pltpu.PrefetchScalarGridSpec(
            num_scalar_prefetch=5,
            grid=(it_tile.shape[0],),
            in_specs=[pl.BlockSpec((TMX * ROW_SUB, LANES), lambda w, tl, *_: (tl[w], 0)),
                      wspec(d, dff), wspec(1, dff), wspec(d, dff), wspec(1, dff), wspec(dff, d), wspec(1, d)],
            out_specs=pl.BlockSpec((TMX * ROW_SUB, LANES), lambda w, tl, *_: (tl[w], 0)),
            scratch_shapes=[pltpu.VMEM((3, d, dff), BF16)]),
        out_shape=jax.ShapeDtypeStruct((n_rows * ROW_SUB, LANES), F32),
        compiler_params=params,
        name="moe_experts",
    )(it_tile, it_exp, it_flags, it_lo, it_hi, xs, w_gate[0], b_gate[0].reshape(n_experts, 1, dff), w_lin[0],
      b_lin[0].reshape(n_experts, 1, dff), w_down[0], b_down[0].reshape(n_experts, 1, d))

    ctx_i = lambda i: (jnp.minimum(i, n_ctx_tiles - 1), 0)
    lat_i = lambda i: (jnp.maximum(i - n_ctx_tiles, 0), 0)
    y_prompt, y_sample = pl.pallas_call(
        functools.partial(_combine_kernel, n_ctx_tiles=n_ctx_tiles),
        grid=(n_tiles,),
        in_specs=[pos_spec(lambda i: (0, i)), pos_spec(lambda i: (0, jnp.minimum(i + 1, n_tiles - 1))),
                  tile(LANES), tile(d), pl.BlockSpec((1, 8, d), mod_row(TM)), _const_spec((1, d)), any_spec],
        out_specs=[pl.BlockSpec((TM, d), ctx_i), pl.BlockSpec((TM, d), lat_i)],
        out_shape=[jax.ShapeDtypeStruct((t_ctx, d), F32), jax.ShapeDtypeStruct((t_lat, d), F32)],
        scratch_shapes=[pltpu.VMEM((2, TOP_K, TM * ROW_SUB, LANES), F32), pltpu.SemaphoreType.DMA((2,))],
        compiler_params=params,
        name="moe_combine",
    )(pos_t, pos_t, route, x1, mod, final_g.reshape(1, d), ys)

    y_prompt = y_prompt.reshape(nb, seq, d)
    y_sample = y_sample.reshape(nd, dseq, d)
    new_c = c_new.reshape(nb, 1, 2, N_HEADS, D_QK, D_V)
    new_n = n_new[:, :, :D_QK].reshape(nb, 1, 2, N_HEADS, D_QK)
    new_m = m_new[:, 0, :N_HD].reshape(nb, 1, 2, N_HEADS)
    return (y_prompt, y_sample, new_c, new_n, new_m)
```

```python
import functools

import jax
import jax.numpy as jnp
from jax import lax
from jax.experimental import pallas as pl
from jax.experimental.pallas import tpu as pltpu

F32 = jnp.float32
BF16 = jnp.bfloat16
HIGHEST = lax.Precision.HIGHEST

N_HEADS = 8
D_QK = 64
D_V = 128
GRID_W = 64
TOP_K = 4
SWIGLU_LIMIT = 7.0
SWIGLU_ALPHA = 1.702
EPS = 1e-6
N_MOD = 6

LANES = 128
TM = 256
TMX = 512
N_HD = 2 * N_HEADS
VMEM_LIMIT = 56 * 1024 * 1024

_D = 1024
C_CONV = (0, 3 * _D)
C_Q = (3 * _D, 4 * _D)
C_K = (4 * _D, 5 * _D)
C_V = (5 * _D, 6 * _D)
C_O = (6 * _D, 7 * _D)
C_GA = (7 * _D, 8 * _D)
C_GB = (8 * _D, 9 * _D)
C_G = (9 * _D, 9 * _D + 2 * LANES)
W_COLS = C_G[1]


def _dot(a, b, precision=None):
    return jnp.dot(a, b, preferred_element_type=F32, precision=precision)


def _rms(x):
    return x * lax.rsqrt(jnp.mean(x * x, axis=-1, keepdims=True) + EPS)


ROW_SUB = 8


def _store_rows(ref, val):
    n = val.shape[0]
    for s in range(ROW_SUB):
        ref[pl.ds(s, n, stride=ROW_SUB), :] = val[:, s * LANES:(s + 1) * LANES]


def _load_rows(ref):
    n = ref.shape[0] // ROW_SUB
    return jnp.concatenate([ref[pl.ds(s, n, stride=ROW_SUB), :] for s in range(ROW_SUB)], axis=1)


def _row_tile(ref, r):
    return ref.at[pl.ds(pl.multiple_of(r * ROW_SUB, ROW_SUB), ROW_SUB)]


def _const_spec(shape):
    return pl.BlockSpec(shape, lambda *_: (0,) * len(shape), pipeline_mode=pl.Buffered(1))


def _mod_kernel(c_ref, w_ref, b_ref, o_ref):
    c = c_ref[...]
    o_ref[...] = _dot(c * jax.nn.sigmoid(c), w_ref[...], HIGHEST) + b_ref[...]


def _modulation(cvec, ada_w, ada_b):
    n, d = cvec.shape
    nout = ada_w.shape[1]
    return pl.pallas_call(
        _mod_kernel,
        grid=(nout // d,),
        in_specs=[pl.BlockSpec((n, d), lambda j: (0, 0)),
                  pl.BlockSpec((d, d), lambda j: (0, j)),
                  pl.BlockSpec((1, d), lambda j: (0, j))],
        out_specs=pl.BlockSpec((n, d), lambda j: (0, j)),
        out_shape=jax.ShapeDtypeStruct((n, nout), F32),
        name="adaln_mod",
    )(cvec, ada_w, ada_b.reshape(1, nout))


def _proj_kernel(xc_ref, xl_ref, mod_ref, g1_ref, w_ref, b_ref, cw_ref, cb_ref, wco_ref,
                 ycg_ref, sgb_ref, q_ref, k_ref, v_ref, so_ref, gt_ref, *, n_ctx_tiles, ctx_row):
    i = pl.program_id(0)
    x = jnp.where(i < n_ctx_tiles, xc_ref[...], xl_ref[...])
    h = (_rms(x) * g1_ref[...] * (1.0 + mod_ref[0, 1:2, :]) + mod_ref[0, 0:1, :]).astype(BF16)

    def proj(cols):
        return _dot(h, w_ref[:, cols[0]:cols[1]]) + b_ref[:, cols[0]:cols[1]]

    zc = proj(C_CONV)
    d = x.shape[1]
    u = zc[:, 2 * d:3 * d] * zc[:, 0:d]
    rowlen = jnp.where(i < n_ctx_tiles, ctx_row, GRID_W)
    pos = lax.broadcasted_iota(jnp.int32, (TM, 1), 0) & (rowlen - 1)
    u_prev = jnp.where(pos == 0, 0.0, pltpu.roll(u, 1, 0))
    u_next = jnp.where(pos == rowlen - 1, 0.0, pltpu.roll(u, TM - 1, 0))
    uc = u_prev * cw_ref[0:1, :] + u * cw_ref[1:2, :] + u_next * cw_ref[2:3, :] + cb_ref[...]
    yconv = _dot((zc[:, d:2 * d] * uc).astype(BF16), wco_ref[...])

    ycg_ref[...] = (jax.nn.sigmoid(proj(C_GA)) * yconv).astype(BF16)
    sgb_ref[...] = jax.nn.sigmoid(proj(C_GB)).astype(BF16)
    q_ref[...] = (proj(C_Q) * (D_QK ** -0.5)).astype(BF16)
    k_ref[...] = proj(C_K).astype(BF16)
    v_ref[...] = proj(C_V).astype(BF16)
    so_ref[...] = jax.nn.sigmoid(proj(C_O)).astype(BF16)
    gt_ref[...] = proj(C_G)


def _tri_masks():
    row = lax.broadcasted_iota(jnp.int32, (TM, TM), 0)
    col = lax.broadcasted_iota(jnp.int32, (TM, TM), 1)
    return row >= col, row <= col


def _gate_cumsums(gi, gf):
    lower, upper = _tri_masks()
    lf = jax.nn.log_sigmoid(gf)
    fwd_lane = lax.broadcasted_iota(jnp.int32, (1, LANES), 1) < N_HEADS
    bsum = jnp.where(fwd_lane, _dot(lower.astype(F32), lf, HIGHEST), _dot(upper.astype(F32), lf, HIGHEST))
    return lf, bsum, fwd_lane


def _state_kernel(*refs, zero_init, emit_before, emit_after):
    it = iter(refs)
    kf_ref, vf_ref, gf_ref, kb_ref, vb_ref, gb_ref = (next(it) for _ in range(6))
    if not zero_init:
        c0_ref, n0_ref, m0_ref = (next(it) for _ in range(3))
    if emit_before:
        cbf_ref, cbb_ref, nbf_ref, nbb_ref, mbf_ref, mbb_ref = (next(it) for _ in range(6))
    if emit_after:
        ca_ref, na_ref, ma_ref = (next(it) for _ in range(3))
    c_scr, n_scr, m_scr = (next(it) for _ in range(3))
    c = pl.program_id(1)

    @pl.when(c == 0)
    def _():
        if zero_init:
            c_scr[...] = jnp.zeros_like(c_scr)
            n_scr[...] = jnp.zeros_like(n_scr)
            m_scr[...] = jnp.zeros_like(m_scr)
        else:
            c_scr[...] = c0_ref[0]
            n_scr[...] = n0_ref[0]
            m_scr[...] = m0_ref[0]

    if emit_before:
        cbf_ref[0] = c_scr[0:N_HEADS]
        cbb_ref[0] = c_scr[N_HEADS:N_HD]
        nbf_ref[0] = n_scr[0:N_HEADS]
        nbb_ref[0] = n_scr[N_HEADS:N_HD]
        mbf_ref[0] = m_scr[...]
        mbb_ref[0] = m_scr[...]

    fwd_lane = lax.broadcasted_iota(jnp.int32, (1, LANES), 1) < N_HEADS
    gi = jnp.where(fwd_lane, gf_ref[:, 0:LANES], gb_ref[:, 0:LANES])
    gfg = jnp.where(fwd_lane, gf_ref[:, LANES:2 * LANES], gb_ref[:, LANES:2 * LANES])
    lf, bsum, _ = _gate_cumsums(gi, gfg)
    total = jnp.sum(lf, axis=0, keepdims=True)
    g = total - bsum + gi
    m_prev = m_scr[...]
    m_new = jnp.maximum(total + m_prev, jnp.max(g, axis=0, keepdims=True))
    wk = jnp.exp(g - m_new)
    decay = jnp.exp(total + m_prev - m_new)
    for hd in range(N_HD):
        h = hd % N_HEADS
        k_ref, v_ref = (kf_ref, vf_ref) if hd < N_HEADS else (kb_ref, vb_ref)
        wkk = wk[:, hd:hd + 1] * k_ref[:, h * LANES:(h + 1) * LANES].astype(F32)
        vh = v_ref[:, h * D_V:(h + 1) * D_V]
        dec = decay[:, hd:hd + 1]
        tn = (((0,), (0,)), ((), ()))
        if emit_before:
            c_scr[hd] = dec * c_scr[hd] + lax.dot_general(vh, wkk.astype(BF16), tn, preferred_element_type=F32)
        else:
            upd = lax.dot_general(wkk.astype(BF16), vh, tn, preferred_element_type=F32)
            c_scr[hd] = dec * c_scr[hd] + upd[0:D_QK, :]
        n_scr[hd:hd + 1, :] = dec * n_scr[hd:hd + 1, :] + jnp.sum(wkk, axis=0, keepdims=True)
    m_scr[...] = m_new

    if emit_after:
        @pl.when(c == pl.num_programs(1) - 1)
        def _():
            ca_ref[0] = c_scr[...]
            na_ref[0] = n_scr[...]
            ma_ref[0] = m_scr[...]


def _state_scan(k, v, gt, tile0, n_seq, n_chunk, init, emit_before, emit_after):
    d = k.shape[1]
    cshape = (D_V, LANES) if emit_before else (D_QK, D_V)
    fwd = lambda s, c: (tile0 + s * n_chunk + c, 0)
    bwd = lambda s, c: (tile0 + s * n_chunk + n_chunk - 1 - c, 0)
    in_specs = [pl.BlockSpec((TM, d), fwd), pl.BlockSpec((TM, d), fwd), pl.BlockSpec((TM, 2 * LANES), fwd),
                pl.BlockSpec((TM, d), bwd), pl.BlockSpec((TM, d), bwd), pl.BlockSpec((TM, 2 * LANES), bwd)]
    args = [k, v, gt, k, v, gt]
    if init is not None:
        in_specs += [pl.BlockSpec((1, N_HD) + cshape, lambda s, c: (s, 0, 0, 0)),
                     pl.BlockSpec((1, N_HD, LANES), lambda s, c: (s, 0, 0)),
                     pl.BlockSpec((1, 1, LANES), lambda s, c: (s, 0, 0))]
        args += list(init)
    out_specs, out_shape = [], []
    n_tot = n_seq * n_chunk
    if emit_before:
        cf = lambda s, c: (s * n_chunk + c, 0, 0, 0)
        cb = lambda s, c: (s * n_chunk + n_chunk - 1 - c, 0, 0, 0)
        nf = lambda s, c: (s * n_chunk + c, 0, 0)
        nb = lambda s, c: (s * n_chunk + n_chunk - 1 - c, 0, 0)
        out_specs += [pl.BlockSpec((1, N_HEADS) + cshape, cf), pl.BlockSpec((1, N_HEADS) + cshape, cb),
                      pl.BlockSpec((1, N_HEADS, LANES), nf), pl.BlockSpec((1, N_HEADS, LANES), nb),
                      pl.BlockSpec((1, 1, LANES), nf), pl.BlockSpec((1, 1, LANES), nb)]
        out_shape += [jax.ShapeDtypeStruct((n_tot, N_HEADS) + cshape, F32)] * 2
        out_shape += [jax.ShapeDtypeStruct((n_tot, N_HEADS, LANES), F32)] * 2
        out_shape += [jax.ShapeDtypeStruct((n_tot, 1, LANES), F32)] * 2
    if emit_after:
        out_specs += [pl.BlockSpec((1, N_HD, D_QK, D_V), lambda s, c: (s, 0, 0, 0)),
                      pl.BlockSpec((1, N_HD, LANES), lambda s, c: (s, 0, 0)),
                      pl.BlockSpec((1, 1, LANES), lambda s, c: (s, 0, 0))]
        out_shape += [jax.ShapeDtypeStruct((n_seq, N_HD, D_QK, D_V), F32),
                      jax.ShapeDtypeStruct((n_seq, N_HD, LANES), F32),
                      jax.ShapeDtypeStruct((n_seq, 1, LANES), F32)]
    return pl.pallas_call(
        functools.partial(_state_kernel, zero_init=init is None, emit_before=emit_before, emit_after=emit_after),
        grid=(n_seq, n_chunk),
        in_specs=in_specs,
        out_specs=out_specs,
        out_shape=out_shape,
        scratch_shapes=[pltpu.VMEM((N_HD,) + cshape, F32), pltpu.VMEM((N_HD, LANES), F32),
                        pltpu.VMEM((1, LANES), F32)],
        compiler_params=pltpu.CompilerParams(dimension_semantics=("arbitrary", "arbitrary"),
                                             vmem_limit_bytes=VMEM_LIMIT),
        name="mlstm_state_scan",
    )(*args)


def _mix_kernel(q_ref, k_ref, v_ref, so_ref, sgb_ref, ycg_ref, gt_ref, xc_ref, xl_ref, mod_ref,
                cbf_ref, cbb_ref, nbf_ref, nbb_ref, mbf_ref, mbb_ref,
                gmh_ref, wmo_ref, wo_ref, g2_ref, rw_ref, rb_ref,
                x1_ref, h2_ref, route_ref, routet_ref, cnt_ref, hm_scr, carry_scr, *, n_ctx_tiles, n_experts):
    i = pl.program_id(0)
    is_lat = i >= n_ctx_tiles
    lat_f = is_lat.astype(F32)
    lower, upper = _tri_masks()
    gt_t = gt_ref[...].T
    gi_t = gt_t[0:N_HD, :]
    lf_t = jax.nn.log_sigmoid(gt_t[LANES:LANES + N_HD, :])
    fwd_row = lax.broadcasted_iota(jnp.int32, (N_HD, 1), 0) < N_HEADS
    bsum_t = jnp.where(fwd_row, _dot(lf_t, upper.astype(F32), HIGHEST), _dot(lf_t, lower.astype(F32), HIGHEST))
    a_t = gi_t - bsum_t
    t_idx = lax.broadcasted_iota(jnp.int32, (N_HD, TM), 1)
    pmax, smax = a_t, a_t
    step = 1
    while step < TM:
        pmax = jnp.maximum(pmax, jnp.where(t_idx >= step, pltpu.roll(pmax, step, 1), -jnp.inf))
        smax = jnp.maximum(smax, jnp.where(t_idx < TM - step, pltpu.roll(smax, TM - step, 1), -jnp.inf))
        step *= 2
    m_row = jnp.where(lax.broadcasted_iota(jnp.int32, (1, LANES), 1) < N_HEADS, mbf_ref[0], mbb_ref[0]) * lat_f
    m_sq = jnp.where(lax.broadcasted_iota(jnp.int32, (LANES, 1), 0) == 0, m_row, 0.0)
    m_prev = m_sq.T[0:N_HD, 0:1]
    mrow_t = jnp.maximum(m_prev, jnp.where(fwd_row, pmax, smax))
    w_inter_t = jnp.exp(m_prev - mrow_t)
    e_floor_t = jnp.exp(-(bsum_t + mrow_t))
    a_c = jnp.concatenate([a_t, jnp.zeros((LANES - N_HD, TM), F32)], axis=0).T
    row0 = lax.broadcasted_iota(jnp.int32, (LANES, 1), 0) == 0
    ones_rows = jnp.where(row0, 1.0, 0.0).astype(BF16) * jnp.ones((1, TM), BF16)

    for h in range(N_HEADS):
        hs = slice(h * D_V, (h + 1) * D_V)
        qh = q_ref[:, h * LANES:(h + 1) * LANES]
        kh = k_ref[:, h * LANES:(h + 1) * LANES]
        vext_t = jnp.concatenate([v_ref[:, hs].T, ones_rows], axis=0)
        kq = lax.dot_general(kh, qh, (((1,), (1,)), ((), ())), preferred_element_type=F32)
        hsum = None
        for d in range(2):
            hd = d * N_HEADS + h
            mask = upper if d == 0 else lower
            c_t = (cbf_ref if d == 0 else cbb_ref)[0, h] * lat_f
            n_r = (nbf_ref if d == 0 else nbb_ref)[0, h:h + 1, :] * lat_f
            e = jnp.exp(jnp.where(mask, a_c[:, hd:hd + 1] - mrow_t[hd:hd + 1, :], -jnp.inf))
            nd = _dot(vext_t, (kq * e).astype(BF16))
            cext_t = jnp.concatenate([c_t, jnp.where(row0, n_r, 0.0)], axis=0).astype(BF16)
            qc = lax.dot_general(cext_t, qh, (((1,), (1,)), ((), ())), preferred_element_type=F32)
            wi = w_inter_t[hd:hd + 1, :]
            num = nd[0:D_V, :] + wi * qc[0:D_V, :]
            den = nd[D_V:D_V + 1, :] + wi * qc[D_V:D_V + 1, :]
            r = 1.0 / jnp.maximum(jnp.abs(den), e_floor_t[hd:hd + 1, :])
            hsum = num * r if hsum is None else hsum + num * r
        hn = hsum * lax.rsqrt(jnp.mean(hsum * hsum, axis=0, keepdims=True) + EPS)
        hm_scr[:, hs] = (hn.T * gmh_ref[:, hs] * so_ref[:, hs].astype(F32)).astype(BF16)

    ym = _dot(hm_scr[...], wmo_ref[...])
    mix = (ycg_ref[...].astype(F32) + sgb_ref[...].astype(F32) * ym).astype(BF16)
    x1 = jnp.where(is_lat, xl_ref[...], xc_ref[...]) + mod_ref[0, 2:3, :] * _dot(mix, wo_ref[...])
    x1_ref[...] = x1
    h2 = _rms(x1) * g2_ref[...] * (1.0 + mod_ref[0, 4:5, :]) + mod_ref[0, 3:4, :]
    _store_rows(h2_ref, h2)

    lane = lax.broadcasted_iota(jnp.int32, (TM, LANES), 1)
    work = jnp.where(lane < n_experts, _dot(h2, rw_ref[...], HIGHEST) + rb_ref[...], -jnp.inf)
    sels, exps, idxs = [], [], []
    top = None
    for _ in range(TOP_K):
        mx = jnp.max(work, axis=-1, keepdims=True)
        ix = jnp.min(jnp.where(work == mx, lane, LANES), axis=-1, keepdims=True)
        sel = lane == ix
        work = jnp.where(sel, -jnp.inf, work)
        top = mx if top is None else top
        sels.append(sel)
        idxs.append(ix.astype(F32))
        exps.append(jnp.exp(mx - top))
    inv = 1.0 / functools.reduce(lambda p, q: p + q, exps)

    @pl.when(i == 0)
    def _():
        carry_scr[...] = jnp.zeros_like(carry_scr)

    onehot = functools.reduce(lambda p, q: p + q, [jnp.where(s, 1.0, 0.0) for s in sels])
    row = lax.broadcasted_iota(jnp.int32, (TM, TM), 0)
    col = lax.broadcasted_iota(jnp.int32, (TM, TM), 1)
    before = _dot((row > col).astype(BF16), onehot.astype(BF16)) + carry_scr[...]
    carry_scr[...] += jnp.sum(onehot, axis=0, keepdims=True)
    cnt_ref[...] = carry_scr[...]
    route = jnp.zeros((TM, LANES), F32)
    for j in range(TOP_K):
        slot = jnp.sum(jnp.where(sels[j], before, 0.0), axis=-1, keepdims=True)
        route = jnp.where(lane == j, idxs[j], route)
        route = jnp.where(lane == TOP_K + j, slot, route)
        route = jnp.where(lane == 2 * TOP_K + j, exps[j] * inv, route)
    route_ref[...] = route
    routet_ref[...] = route.T[0:8, :].astype(jnp.int32)


def _dispatch_kernel(pos_ref, h2_ref, xs_ref, sem):
    def copy(t, j):
        return pltpu.make_async_copy(_row_tile(h2_ref, t), _row_tile(xs_ref, pos_ref[j, t]), sem)

    def start(t, carry):
        for j in range(TOP_K):
            copy(t, j).start(priority=j % 2)
        return carry

    lax.fori_loop(0, TM, start, 0)
    for j in range(TOP_K):
        pltpu.make_async_copy(h2_ref, xs_ref.at[pl.ds(0, TM * ROW_SUB)], sem).wait()


def _expert_kernel(tile_ref, exp_ref, flag_ref, lo_ref, hi_ref,
                   xs_ref, wg_ref, bg_ref, wl_ref, bl_ref, wd_ref, bd_ref, ys_ref, w_scr):
    w = pl.program_id(0)
    flags = flag_ref[w]

    @pl.when((flags & 4) != 0)
    def _():
        w_scr[0] = wg_ref[0].astype(BF16)
        w_scr[1] = wl_ref[0].astype(BF16)
        w_scr[2] = wd_ref[0].astype(BF16)

    @pl.when((flags & 1) != 0)
    def _():
        x = _load_rows(xs_ref).astype(BF16)
        gt = jnp.minimum(_dot(x, w_scr[0]) + bg_ref[0], SWIGLU_LIMIT)
        lin = jnp.clip(_dot(x, w_scr[1]) + bl_ref[0], -SWIGLU_LIMIT, SWIGLU_LIMIT)
        act = gt * jax.nn.sigmoid(SWIGLU_ALPHA * gt) * (lin + 1.0)
        y = _dot(act.astype(BF16), w_scr[2]) + bd_ref[0]
        rows = lax.broadcasted_iota(jnp.int32, (y.shape[0], 1), 0)
        mine = (rows >= lo_ref[w]) & (rows < hi_ref[w])

        @pl.when((flags & 2) != 0)
        def _():
            _store_rows(ys_ref, jnp.where(mine, y, 0.0))

        @pl.when((flags & 2) == 0)
        def _():
            _store_rows(ys_ref, jnp.where(mine, y, _load_rows(ys_ref)))


def _combine_kernel(pos_ref, posn_ref, route_ref, x1_ref, mod_ref, fg_ref, ys_ref, outc_ref, outl_ref,
                    buf, sem, *, n_ctx_tiles):
    i = pl.program_id(0)
    n = pl.num_programs(0)

    def copy(p_ref, slot, t, j):
        return pltpu.make_async_copy(_row_tile(ys_ref, p_ref[j, t]), _row_tile(buf.at[slot, j], t), sem.at[slot])

    def start_all(p_ref, slot):
        def body(t, carry):
            for j in range(TOP_K):
                copy(p_ref, slot, t, j).start(priority=j % 2)
            return carry
        lax.fori_loop(0, TM, body, 0)

    @pl.when(i == 0)
    def _():
        start_all(pos_ref, 0)

    @pl.when(i + 1 < n)
    def _():
        start_all(posn_ref, (i + 1) % 2)

    slot = i % 2

    for j in range(TOP_K):
        pltpu.make_async_copy(ys_ref.at[pl.ds(0, TM * ROW_SUB)], buf.at[slot, j], sem.at[slot]).wait()
    acc = None
    for j in range(TOP_K):
        term = route_ref[:, 2 * TOP_K + j:2 * TOP_K + j + 1] * _load_rows(buf.at[slot, j])
        acc = term if acc is None else acc + term
    out = _rms(x1_ref[...] + mod_ref[0, 5:6, :] * acc) * fg_ref[...]

    @pl.when(i < n_ctx_tiles)
    def _():
        outc_ref[...] = out

    @pl.when(i >= n_ctx_tiles)
    def _():
        outl_ref[...] = out


def _work_items(counts, n_experts, n_rows, tmx):
    n_items_max = n_rows // tmx + n_experts - 1
    cnt = counts.astype(jnp.int32)
    offs = jnp.concatenate([jnp.zeros((1,), jnp.int32), jnp.cumsum(cnt)])
    first_tile = offs[:-1] // tmx
    n_it = jnp.where(cnt > 0, (offs[1:] - 1) // tmx - first_tile + 1, 0)
    it_start = jnp.concatenate([jnp.zeros((1,), jnp.int32), jnp.cumsum(n_it)])
    total = it_start[-1]
    w = jnp.arange(n_items_max, dtype=jnp.int32)
    wc = jnp.minimum(w, total - 1)
    e = jnp.sum((it_start[None, 1:] <= wc[:, None]).astype(jnp.int32), axis=1)
    e = jnp.minimum(e, n_experts - 1)
    tile = first_tile[e] + wc - it_start[e]
    valid = w < total
    prev = lambda a: jnp.concatenate([jnp.full((1,), -1, jnp.int32), a[:-1]])
    flags = (valid.astype(jnp.int32) + 2 * (valid & (tile != prev(tile))).astype(jnp.int32)
             + 4 * (valid & (e != prev(e))).astype(jnp.int32))
    lo = jnp.clip(offs[e] - tile * tmx, 0, tmx)
    hi = jnp.clip(offs[e + 1] - tile * tmx, 0, tmx)
    return offs, tile, e, flags, lo, hi


def _pack_in_proj(w, b):
    d = w.shape[0]
    o_q, o_k, o_v, o_o, o_g, o_ga, o_gb = 3 * d, 3 * d + 512, 3 * d + 1024, 4 * d + 1024, 5 * d + 1024, \
        5 * d + 1024 + 4 * N_HEADS, 6 * d + 1024 + 4 * N_HEADS

    def pad_heads(m):
        m = m.reshape(m.shape[0], N_HEADS, D_QK)
        return jnp.pad(m, ((0, 0), (0, 0), (0, LANES - D_QK))).reshape(m.shape[0], N_HEADS * LANES)

    def gates(m):
        gz = jnp.zeros((m.shape[0], LANES - N_HD), m.dtype)
        i_f, f_f, i_b, f_b = (m[:, o_g + j * N_HEADS:o_g + (j + 1) * N_HEADS] for j in range(4))
        return jnp.concatenate([i_f, i_b, gz, f_f, f_b, gz], axis=1)

    def pack(m):
        return jnp.concatenate([m[:, 0:o_q], pad_heads(m[:, o_q:o_k]), pad_heads(m[:, o_k:o_v]), m[:, o_v:o_o],
                                m[:, o_o:o_g], m[:, o_ga:o_gb], m[:, o_gb:o_gb + d], gates(m)], axis=1)

    return pack(w).astype(BF16), pack(b.reshape(1, -1))


def kernel(x_prompt, x_sample, c, state_C, state_n, state_m, c_ctx, ada_w, ada_b, norm1_g, norm2_g, w_in, b_in,
           conv_w, conv_b, w_conv_out, mh_norm_g, w_m_out, w_o, router_w, router_b, w_gate, b_gate, w_lin, b_lin,
           w_down, b_down, final_g):
    nb, seq, d = x_prompt.shape
    nd, dseq, _ = x_sample.shape
    n_experts = w_gate.shape[1]
    assert d == _D and w_in.shape[0] == 1 and seq == TM and dseq % TM == 0 and TM % GRID_W == 0
    t_ctx, t_lat = nb * seq, nd * dseq
    n_tok = t_ctx + t_lat
    n_ctx_tiles, n_tiles = t_ctx // TM, n_tok // TM
    lat_chunks = dseq // TM
    n_lat_tiles = n_tiles - n_ctx_tiles

    n_c = 1 + nd
    n_cp = -(-n_c // 8) * 8
    cvec = jnp.concatenate([c_ctx[None, :], c, jnp.zeros((n_cp - n_c, d), F32)], axis=0)
    mod = _modulation(cvec, ada_w[0], ada_b[0]).reshape(n_cp, N_MOD, d)
    mod = jnp.pad(mod, ((0, 0), (0, 8 - N_MOD), (0, 0)))

    def mod_row(tile_tokens):
        ctx_t, per_seq = t_ctx // tile_tokens, dseq // tile_tokens
        return lambda i, *_: (jnp.where(i < ctx_t, 0, 1 + (i - ctx_t) // per_seq), 0, 0)

    x_ctx, x_lat = x_prompt.reshape(t_ctx, d), x_sample.reshape(t_lat, d)
    ctx_spec = pl.BlockSpec((TM, d), lambda i: (jnp.minimum(i, n_ctx_tiles - 1), 0))
    lat_spec = pl.BlockSpec((TM, d), lambda i: (jnp.maximum(i - n_ctx_tiles, 0), 0))
    w_all, b_all = _pack_in_proj(w_in[0], b_in[0])
    tile = lambda cols: pl.BlockSpec((TM, cols), lambda i: (i, 0))
    rows = lambda n: pl.BlockSpec((n * ROW_SUB, LANES), lambda i: (i, 0))
    params = pltpu.CompilerParams(dimension_semantics=("arbitrary",), vmem_limit_bytes=VMEM_LIMIT)
    bf = lambda cols: jax.ShapeDtypeStruct((n_tok, cols), BF16)

    ycg, sgb, q, k, v, so, gt = pl.pallas_call(
        functools.partial(_proj_kernel, n_ctx_tiles=n_ctx_tiles, ctx_row=seq),
        grid=(n_tiles,),
        in_specs=[ctx_spec, lat_spec, pl.BlockSpec((1, 8, d), mod_row(TM)), _const_spec((1, d)),
                  _const_spec((d, W_COLS)), _const_spec((1, W_COLS)), _const_spec((3, d)), _const_spec((1, d)),
                  _const_spec((d, d))],
        out_specs=[tile(d), tile(d), tile(d), tile(d), tile(d), tile(d), tile(2 * LANES)],
        out_shape=[bf(d), bf(d), bf(d), bf(d), bf(d), bf(d), jax.ShapeDtypeStruct((n_tok, 2 * LANES), F32)],
        compiler_params=params,
        name="in_proj_conv",
    )(x_ctx, x_lat, mod, norm1_g, w_all, b_all, conv_w[0], conv_b, w_conv_out[0].astype(BF16))

    c_new, n_new, m_new = _state_scan(k, v, gt, 0, nb, 1, None, False, True)
    init = (jnp.pad(jnp.swapaxes(state_C[:, 0].astype(F32).reshape(nd, N_HD, D_QK, D_V), 2, 3),
                    ((0, 0), (0, 0), (0, 0), (0, LANES - D_QK))),
            jnp.pad(state_n[:, 0].astype(F32).reshape(nd, N_HD, D_QK), ((0, 0), (0, 0), (0, LANES - D_QK))),
            jnp.pad(state_m[:, 0].astype(F32).reshape(nd, 1, N_HD), ((0, 0), (0, 0), (0, LANES - N_HD))))
    cbf, cbb, nbf, nbb, mbf, mbb = _state_scan(k, v, gt, n_ctx_tiles, nd, lat_chunks, init, True, False)

    lat_idx = lambda i: jnp.maximum(i - n_ctx_tiles, 0)
    st4 = pl.BlockSpec((1, N_HEADS, D_V, LANES), lambda i: (lat_idx(i), 0, 0, 0))
    st3 = pl.BlockSpec((1, N_HEADS, LANES), lambda i: (lat_idx(i), 0, 0))
    st1 = pl.BlockSpec((1, 1, LANES), lambda i: (lat_idx(i), 0, 0))
    rw = jnp.pad(router_w[0], ((0, 0), (0, LANES - n_experts)))
    rb = jnp.pad(router_b[0], (0, LANES - n_experts)).reshape(1, LANES)
    x1, h2, route, route_t, counts = pl.pallas_call(
        functools.partial(_mix_kernel, n_ctx_tiles=n_ctx_tiles, n_experts=n_experts),
        grid=(n_tiles,),
        in_specs=[tile(d), tile(d), tile(d), tile(d), tile(d), tile(d), tile(2 * LANES), ctx_spec, lat_spec,
                  pl.BlockSpec((1, 8, d), mod_row(TM)), st4, st4, st3, st3, st1, st1,
                  _const_spec((1, d)), _const_spec((d, d)), _const_spec((d, d)), _const_spec((1, d)),
                  _const_spec((d, LANES)), _const_spec((1, LANES))],
        out_specs=[tile(d), rows(TM), tile(LANES), pl.BlockSpec((8, TM), lambda i: (0, i)),
                   pl.BlockSpec((1, LANES), lambda i: (0, 0))],
        out_shape=[jax.ShapeDtypeStruct((n_tok, d), F32), jax.ShapeDtypeStruct((n_tok * ROW_SUB, LANES), F32),
                   jax.ShapeDtypeStruct((n_tok, LANES), F32), jax.ShapeDtypeStruct((8, n_tok), jnp.int32),
                   jax.ShapeDtypeStruct((1, LANES), F32)],
        scratch_shapes=[pltpu.VMEM((TM, d), BF16), pltpu.VMEM((1, LANES), F32)],
        compiler_params=params,
        name="mlstm_mix_router",
    )(q, k, v, so, sgb, ycg, gt, x_ctx, x_lat, mod, cbf, cbb, nbf, nbb, mbf, mbb,
      mh_norm_g, w_m_out[0].astype(BF16), w_o[0].astype(BF16), norm2_g, rw, rb)

    n_rows = TOP_K * n_tok
    offs, it_tile, it_exp, it_flags, it_lo, it_hi = _work_items(counts[0, :n_experts], n_experts, n_rows, TMX)
    is_exp = route_t[0:TOP_K, :, None] == jnp.arange(n_experts, dtype=jnp.int32)
    pos_t = jnp.sum(jnp.where(is_exp, offs[:n_experts], 0), axis=-1) + route_t[TOP_K:2 * TOP_K]
    pos_spec = lambda f: pl.BlockSpec((TOP_K, TM), f, memory_space=pltpu.SMEM)
    any_spec = pl.BlockSpec(memory_space=pl.ANY)

    xs = pl.pallas_call(
        _dispatch_kernel,
        grid=(n_tiles,),
        in_specs=[pos_spec(lambda i: (0, i)), rows(TM)],
        out_specs=any_spec,
        out_shape=jax.ShapeDtypeStruct((n_rows * ROW_SUB, LANES), F32),
        scratch_shapes=[pltpu.SemaphoreType.DMA],
        compiler_params=params,
        name="moe_dispatch",
    )(pos_t, h2)

    dff = w_gate.shape[-1]
    wspec = lambda a, b_: pl.BlockSpec((1, a, b_), lambda w, tl, ex, *_: (ex[w], 0, 0))
    ys = pl.pallas_call(
        _expert_kernel,
        grid_spec=pltpu.PrefetchScalarGridSpec(
            num_scalar_prefetch=5,
            grid=(it_tile.shape[0],),
            in_specs=[pl.BlockSpec((TMX * ROW_SUB, LANES), lambda w, tl, *_: (tl[w], 0)),
                      wspec(d, dff), wspec(1, dff), wspec(d, dff), wspec(1, dff), wspec(dff, d), wspec(1, d)],
            out_specs=pl.BlockSpec((TMX * ROW_SUB, LANES), lambda w, tl, *_: (tl[w], 0)),
            scratch_shapes=[pltpu.VMEM((3, d, dff), BF16)]),
        out_shape=jax.ShapeDtypeStruct((n_rows * ROW_SUB, LANES), F32),
        compiler_params=params,
        name="moe_experts",
    )(it_tile, it_exp, it_flags, it_lo, it_hi, xs, w_gate[0], b_gate[0].reshape(n_experts, 1, dff), w_lin[0],
      b_lin[0].reshape(n_experts, 1, dff), w_down[0], b_down[0].reshape(n_experts, 1, d))

    ctx_i = lambda i: (jnp.minimum(i, n_ctx_tiles - 1), 0)
    lat_i = lambda i: (jnp.maximum(i - n_ctx_tiles, 0), 0)
    y_prompt, y_sample = pl.pallas_call(
        functools.partial(_combine_kernel, n_ctx_tiles=n_ctx_tiles),
        grid=(n_tiles,),
        in_specs=[pos_spec(lambda i: (0, i)), pos_spec(lambda i: (0, jnp.minimum(i + 1, n_tiles - 1))),
                  tile(LANES), tile(d), pl.BlockSpec((1, 8, d), mod_row(TM)), _const_spec((1, d)), any_spec],
        out_specs=[pl.BlockSpec((TM, d), ctx_i), pl.BlockSpec((TM, d), lat_i)],
        out_shape=[jax.ShapeDtypeStruct((t_ctx, d), F32), jax.ShapeDtypeStruct((t_lat, d), F32)],
        scratch_shapes=[pltpu.VMEM((2, TOP_K, TM * ROW_SUB, LANES), F32), pltpu.SemaphoreType.DMA((2,))],
        compiler_params=params,
        name="moe_combine",
    )(pos_t, pos_t, route, x1, mod, final_g.reshape(1, d), ys)

    y_prompt = y_prompt.reshape(nb, seq, d)
    y_sample = y_sample.reshape(nd, dseq, d)
    new_c = c_new.reshape(nb, 1, 2, N_HEADS, D_QK, D_V)
    new_n = n_new[:, :, :D_QK].reshape(nb, 1, 2, N_HEADS, D_QK)
    new_m = m_new[:, 0, :N_HD].reshape(nb, 1, 2, N_HEADS)
    return (y_prompt, y_sample, new_c, new_n, new_m)
```

```python
import functools

import jax
import jax.numpy as jnp
from jax import lax
from jax.experimental import pallas as pl
from jax.experimental.pallas import tpu as pltpu

F32 = jnp.float32
BF16 = jnp.bfloat16
HIGHEST = lax.Precision.HIGHEST

N_HEADS = 8
D_QK = 64
D_V = 128
GRID_W = 64
TOP_K = 4
SWIGLU_LIMIT = 7.0
SWIGLU_ALPHA = 1.702
EPS = 1e-6
N_MOD = 6

LANES = 128
TM = 256
TMX = 512
ISSUE_UNROLL = 4
N_HD = 2 * N_HEADS
VMEM_LIMIT = 56 * 1024 * 1024

_D = 1024
C_CONV = (0, 3 * _D)
C_Q = (3 * _D, 4 * _D)
C_K = (4 * _D, 5 * _D)
C_V = (5 * _D, 6 * _D)
C_O = (6 * _D, 7 * _D)
C_GA = (7 * _D, 8 * _D)
C_GB = (8 * _D, 9 * _D)
C_G = (9 * _D, 9 * _D + 2 * LANES)
W_COLS = C_G[1]


def _dot(a, b, precision=None):
    return jnp.dot(a, b, preferred_element_type=F32, precision=precision)


def _rms(x):
    return x * lax.rsqrt(jnp.mean(x * x, axis=-1, keepdims=True) + EPS)


ROW_SUB = 8


def _store_rows(ref, val):
    n = val.shape[0]
    for s in range(ROW_SUB):
        ref[pl.ds(s, n, stride=ROW_SUB), :] = val[:, s * LANES:(s + 1) * LANES]


def _load_rows(ref):
    n = ref.shape[0] // ROW_SUB
    return jnp.concatenate([ref[pl.ds(s, n, stride=ROW_SUB), :] for s in range(ROW_SUB)], axis=1)


def _row_tile(ref, r):
    return ref.at[pl.ds(pl.multiple_of(r * ROW_SUB, ROW_SUB), ROW_SUB)]


def _const_spec(shape):
    return pl.BlockSpec(shape, lambda *_: (0,) * len(shape), pipeline_mode=pl.Buffered(1))


def _mod_kernel(c_ref, w_ref, b_ref, o_ref):
    c = c_ref[...]
    o_ref[...] = _dot(c * jax.nn.sigmoid(c), w_ref[...], HIGHEST) + b_ref[...]


def _modulation(cvec, ada_w, ada_b):
    n, d = cvec.shape
    nout = ada_w.shape[1]
    return pl.pallas_call(
        _mod_kernel,
        grid=(nout // d,),
        in_specs=[pl.BlockSpec((n, d), lambda j: (0, 0)),
                  pl.BlockSpec((d, d), lambda j: (0, j)),
                  pl.BlockSpec((1, d), lambda j: (0, j))],
        out_specs=pl.BlockSpec((n, d), lambda j: (0, j)),
        out_shape=jax.ShapeDtypeStruct((n, nout), F32),
        name="adaln_mod",
    )(cvec, ada_w, ada_b.reshape(1, nout))


def _proj_kernel(xc_ref, xl_ref, mod_ref, g1_ref, w_ref, b_ref, cw_ref, cb_ref, wco_ref,
                 ycg_ref, sgb_ref, q_ref, k_ref, v_ref, so_ref, gt_ref, *, n_ctx_tiles, ctx_row):
    i = pl.program_id(0)
    x = jnp.where(i < n_ctx_tiles, xc_ref[...], xl_ref[...])
    h = (_rms(x) * g1_ref[...] * (1.0 + mod_ref[0, 1:2, :]) + mod_ref[0, 0:1, :]).astype(BF16)

    def proj(cols):
        return _dot(h, w_ref[:, cols[0]:cols[1]]) + b_ref[:, cols[0]:cols[1]]

    zc = proj(C_CONV)
    d = x.shape[1]
    u = zc[:, 2 * d:3 * d] * zc[:, 0:d]
    rowlen = jnp.where(i < n_ctx_tiles, ctx_row, GRID_W)
    pos = lax.broadcasted_iota(jnp.int32, (TM, 1), 0) & (rowlen - 1)
    u_prev = jnp.where(pos == 0, 0.0, pltpu.roll(u, 1, 0))
    u_next = jnp.where(pos == rowlen - 1, 0.0, pltpu.roll(u, TM - 1, 0))
    uc = u_prev * cw_ref[0:1, :] + u * cw_ref[1:2, :] + u_next * cw_ref[2:3, :] + cb_ref[...]
    yconv = _dot((zc[:, d:2 * d] * uc).astype(BF16), wco_ref[...])

    ycg_ref[...] = (jax.nn.sigmoid(proj(C_GA)) * yconv).astype(BF16)
    sgb_ref[...] = jax.nn.sigmoid(proj(C_GB)).astype(BF16)
    q_ref[...] = (proj(C_Q) * (D_QK ** -0.5)).astype(BF16)
    k_ref[...] = proj(C_K).astype(BF16)
    v_ref[...] = proj(C_V).astype(BF16)
    so_ref[...] = jax.nn.sigmoid(proj(C_O)).astype(BF16)
    gt_ref[...] = proj(C_G)


def _tri_masks():
    row = lax.broadcasted_iota(jnp.int32, (TM, TM), 0)
    col = lax.broadcasted_iota(jnp.int32, (TM, TM), 1)
    return row >= col, row <= col


def _gate_cumsums(gi, gf):
    lower, upper = _tri_masks()
    lf = jax.nn.log_sigmoid(gf)
    fwd_lane = lax.broadcasted_iota(jnp.int32, (1, LANES), 1) < N_HEADS
    bsum = jnp.where(fwd_lane, _dot(lower.astype(F32), lf, HIGHEST), _dot(upper.astype(F32), lf, HIGHEST))
    return lf, bsum, fwd_lane


def _state_kernel(*refs, zero_init, emit_before, emit_after):
    it = iter(refs)
    kf_ref, vf_ref, gf_ref, kb_ref, vb_ref, gb_ref = (next(it) for _ in range(6))
    if not zero_init:
        c0_ref, n0_ref, m0_ref = (next(it) for _ in range(3))
    if emit_before:
        cbf_ref, cbb_ref, nbf_ref, nbb_ref, mbf_ref, mbb_ref = (next(it) for _ in range(6))
    if emit_after:
        ca_ref, na_ref, ma_ref = (next(it) for _ in range(3))
    c_scr, n_scr, m_scr = (next(it) for _ in range(3))
    c = pl.program_id(1)

    @pl.when(c == 0)
    def _():
        if zero_init:
            c_scr[...] = jnp.zeros_like(c_scr)
            n_scr[...] = jnp.zeros_like(n_scr)
            m_scr[...] = jnp.zeros_like(m_scr)
        else:
            c_scr[...] = c0_ref[0]
            n_scr[...] = n0_ref[0]
            m_scr[...] = m0_ref[0]

    if emit_before:
        cbf_ref[0] = c_scr[0:N_HEADS]
        cbb_ref[0] = c_scr[N_HEADS:N_HD]
        nbf_ref[0] = n_scr[0:N_HEADS]
        nbb_ref[0] = n_scr[N_HEADS:N_HD]
        mbf_ref[0] = m_scr[...]
        mbb_ref[0] = m_scr[...]

    fwd_lane = lax.broadcasted_iota(jnp.int32, (1, LANES), 1) < N_HEADS
    gi = jnp.where(fwd_lane, gf_ref[:, 0:LANES], gb_ref[:, 0:LANES])
    gfg = jnp.where(fwd_lane, gf_ref[:, LANES:2 * LANES], gb_ref[:, LANES:2 * LANES])
    lf, bsum, _ = _gate_cumsums(gi, gfg)
    total = jnp.sum(lf, axis=0, keepdims=True)
    g = total - bsum + gi
    m_prev = m_scr[...]
    m_new = jnp.maximum(total + m_prev, jnp.max(g, axis=0, keepdims=True))
    wk = jnp.exp(g - m_new)
    decay = jnp.exp(total + m_prev - m_new)
    for hd in range(N_HD):
        h = hd % N_HEADS
        k_ref, v_ref = (kf_ref, vf_ref) if hd < N_HEADS else (kb_ref, vb_ref)
        wkk = wk[:, hd:hd + 1] * k_ref[:, h * LANES:(h + 1) * LANES].astype(F32)
        vh = v_ref[:, h * D_V:(h + 1) * D_V]
        dec = decay[:, hd:hd + 1]
        tn = (((0,), (0,)), ((), ()))
        if emit_before:
            c_scr[hd] = dec * c_scr[hd] + lax.dot_general(vh, wkk.astype(BF16), tn, preferred_element_type=F32)
        else:
            upd = lax.dot_general(wkk.astype(BF16), vh, tn, preferred_element_type=F32)
            c_scr[hd] = dec * c_scr[hd] + upd[0:D_QK, :]
        n_scr[hd:hd + 1, :] = dec * n_scr[hd:hd + 1, :] + jnp.sum(wkk, axis=0, keepdims=True)
    m_scr[...] = m_new

    if emit_after:
        @pl.when(c == pl.num_programs(1) - 1)
        def _():
            ca_ref[0] = c_scr[...]
            na_ref[0] = n_scr[...]
            ma_ref[0] = m_scr[...]


def _state_scan(k, v, gt, tile0, n_seq, n_chunk, init, emit_before, emit_after):
    d = k.shape[1]
    cshape = (D_V, LANES) if emit_before else (D_QK, D_V)
    fwd = lambda s, c: (tile0 + s * n_chunk + c, 0)
    bwd = lambda s, c: (tile0 + s * n_chunk + n_chunk - 1 - c, 0)
    in_specs = [pl.BlockSpec((TM, d), fwd), pl.BlockSpec((TM, d), fwd), pl.BlockSpec((TM, 2 * LANES), fwd),
                pl.BlockSpec((TM, d), bwd), pl.BlockSpec((TM, d), bwd), pl.BlockSpec((TM, 2 * LANES), bwd)]
    args = [k, v, gt, k, v, gt]
    if init is not None:
        in_specs += [pl.BlockSpec((1, N_HD) + cshape, lambda s, c: (s, 0, 0, 0)),
                     pl.BlockSpec((1, N_HD, LANES), lambda s, c: (s, 0, 0)),
                     pl.BlockSpec((1, 1, LANES), lambda s, c: (s, 0, 0))]
        args += list(init)
    out_specs, out_shape = [], []
    n_tot = n_seq * n_chunk
    if emit_before:
        cf = lambda s, c: (s * n_chunk + c, 0, 0, 0)
        cb = lambda s, c: (s * n_chunk + n_chunk - 1 - c, 0, 0, 0)
        nf = lambda s, c: (s * n_chunk + c, 0, 0)
        nb = lambda s, c: (s * n_chunk + n_chunk - 1 - c, 0, 0)
        out_specs += [pl.BlockSpec((1, N_HEADS) + cshape, cf), pl.BlockSpec((1, N_HEADS) + cshape, cb),
                      pl.BlockSpec((1, N_HEADS, LANES), nf), pl.BlockSpec((1, N_HEADS, LANES), nb),
                      pl.BlockSpec((1, 1, LANES), nf), pl.BlockSpec((1, 1, LANES), nb)]
        out_shape += [jax.ShapeDtypeStruct((n_tot, N_HEADS) + cshape, F32)] * 2
        out_shape += [jax.ShapeDtypeStruct((n_tot, N_HEADS, LANES), F32)] * 2
        out_shape += [jax.ShapeDtypeStruct((n_tot, 1, LANES), F32)] * 2
    if emit_after:
        out_specs += [pl.BlockSpec((1, N_HD, D_QK, D_V), lambda s, c: (s, 0, 0, 0)),
                      pl.BlockSpec((1, N_HD, LANES), lambda s, c: (s, 0, 0)),
                      pl.BlockSpec((1, 1, LANES), lambda s, c: (s, 0, 0))]
        out_shape += [jax.ShapeDtypeStruct((n_seq, N_HD, D_QK, D_V), F32),
                      jax.ShapeDtypeStruct((n_seq, N_HD, LANES), F32),
                      jax.ShapeDtypeStruct((n_seq, 1, LANES), F32)]
    return pl.pallas_call(
        functools.partial(_state_kernel, zero_init=init is None, emit_before=emit_before, emit_after=emit_after),
        grid=(n_seq, n_chunk),
        in_specs=in_specs,
        out_specs=out_specs,
        out_shape=out_shape,
        scratch_shapes=[pltpu.VMEM((N_HD,) + cshape, F32), pltpu.VMEM((N_HD, LANES), F32),
                        pltpu.VMEM((1, LANES), F32)],
        compiler_params=pltpu.CompilerParams(dimension_semantics=("arbitrary", "arbitrary"),
                                             vmem_limit_bytes=VMEM_LIMIT),
        name="mlstm_state_scan",
    )(*args)


def _mix_kernel(q_ref, k_ref, v_ref, so_ref, sgb_ref, ycg_ref, gt_ref, xc_ref, xl_ref, mod_ref,
                cbf_ref, cbb_ref, nbf_ref, nbb_ref, mbf_ref, mbb_ref,
                gmh_ref, wmo_ref, wo_ref, g2_ref, rw_ref, rb_ref,
                x1_ref, h2_ref, route_ref, routet_ref, cnt_ref, hm_scr, carry_scr, *, n_ctx_tiles, n_experts):
    i = pl.program_id(0)
    is_lat = i >= n_ctx_tiles
    lat_f = is_lat.astype(F32)
    lower, upper = _tri_masks()
    gt_t = gt_ref[...].T
    gi_t = gt_t[0:N_HD, :]
    lf_t = jax.nn.log_sigmoid(gt_t[LANES:LANES + N_HD, :])
    fwd_row = lax.broadcasted_iota(jnp.int32, (N_HD, 1), 0) < N_HEADS
    bsum_t = jnp.where(fwd_row, _dot(lf_t, upper.astype(F32), HIGHEST), _dot(lf_t, lower.astype(F32), HIGHEST))
    a_t = gi_t - bsum_t
    t_idx = lax.broadcasted_iota(jnp.int32, (N_HD, TM), 1)
    pmax, smax = a_t, a_t
    step = 1
    while step < TM:
        pmax = jnp.maximum(pmax, jnp.where(t_idx >= step, pltpu.roll(pmax, step, 1), -jnp.inf))
        smax = jnp.maximum(smax, jnp.where(t_idx < TM - step, pltpu.roll(smax, TM - step, 1), -jnp.inf))
        step *= 2
    m_row = jnp.where(lax.broadcasted_iota(jnp.int32, (1, LANES), 1) < N_HEADS, mbf_ref[0], mbb_ref[0]) * lat_f
    m_sq = jnp.where(lax.broadcasted_iota(jnp.int32, (LANES, 1), 0) == 0, m_row, 0.0)
    m_prev = m_sq.T[0:N_HD, 0:1]
    mrow_t = jnp.maximum(m_prev, jnp.where(fwd_row, pmax, smax))
    w_inter_t = jnp.exp(m_prev - mrow_t)
    e_floor_t = jnp.exp(-(bsum_t + mrow_t))
    a_c = jnp.concatenate([a_t, jnp.zeros((LANES - N_HD, TM), F32)], axis=0).T
    row0 = lax.broadcasted_iota(jnp.int32, (LANES, 1), 0) == 0
    ones_rows = jnp.where(row0, 1.0, 0.0).astype(BF16) * jnp.ones((1, TM), BF16)

    for h in range(N_HEADS):
        hs = slice(h * D_V, (h + 1) * D_V)
        qh = q_ref[:, h * LANES:(h + 1) * LANES]
        kh = k_ref[:, h * LANES:(h + 1) * LANES]
        vext_t = jnp.concatenate([v_ref[:, hs].T, ones_rows], axis=0)
        kq = lax.dot_general(kh, qh, (((1,), (1,)), ((), ())), preferred_element_type=F32)
        hsum = None
        for d in range(2):
            hd = d * N_HEADS + h
            mask = upper if d == 0 else lower
            c_t = (cbf_ref if d == 0 else cbb_ref)[0, h] * lat_f
            n_r = (nbf_ref if d == 0 else nbb_ref)[0, h:h + 1, :] * lat_f
            e = jnp.exp(jnp.where(mask, a_c[:, hd:hd + 1] - mrow_t[hd:hd + 1, :], -jnp.inf))
            nd = _dot(vext_t, (kq * e).astype(BF16))
            cext_t = jnp.concatenate([c_t, jnp.where(row0, n_r, 0.0)], axis=0).astype(BF16)
            qc = lax.dot_general(cext_t, qh, (((1,), (1,)), ((), ())), preferred_element_type=F32)
            wi = w_inter_t[hd:hd + 1, :]
            num = nd[0:D_V, :] + wi * qc[0:D_V, :]
            den = nd[D_V:D_V + 1, :] + wi * qc[D_V:D_V + 1, :]
            r = 1.0 / jnp.maximum(jnp.abs(den), e_floor_t[hd:hd + 1, :])
            hsum = num * r if hsum is None else hsum + num * r
        hn = hsum * lax.rsqrt(jnp.mean(hsum * hsum, axis=0, keepdims=True) + EPS)
        hm_scr[:, hs] = (hn.T * gmh_ref[:, hs] * so_ref[:, hs].astype(F32)).astype(BF16)

    ym = _dot(hm_scr[...], wmo_ref[...])
    mix = (ycg_ref[...].astype(F32) + sgb_ref[...].astype(F32) * ym).astype(BF16)
    x1 = jnp.where(is_lat, xl_ref[...], xc_ref[...]) + mod_ref[0, 2:3, :] * _dot(mix, wo_ref[...])
    x1_ref[...] = x1
    h2 = _rms(x1) * g2_ref[...] * (1.0 + mod_ref[0, 4:5, :]) + mod_ref[0, 3:4, :]
    _store_rows(h2_ref, h2)

    lane = lax.broadcasted_iota(jnp.int32, (TM, LANES), 1)
    work = jnp.where(lane < n_experts, _dot(h2, rw_ref[...], HIGHEST) + rb_ref[...], -jnp.inf)
    sels, exps, idxs = [], [], []
    top = None
    for _ in range(TOP_K):
        mx = jnp.max(work, axis=-1, keepdims=True)
        ix = jnp.min(jnp.where(work == mx, lane, LANES), axis=-1, keepdims=True)
        sel = lane == ix
        work = jnp.where(sel, -jnp.inf, work)
        top = mx if top is None else top
        sels.append(sel)
        idxs.append(ix.astype(F32))
        exps.append(jnp.exp(mx - top))
    inv = 1.0 / functools.reduce(lambda p, q: p + q, exps)

    @pl.when(i == 0)
    def _():
        carry_scr[...] = jnp.zeros_like(carry_scr)

    onehot = functools.reduce(lambda p, q: p + q, [jnp.where(s, 1.0, 0.0) for s in sels])
    row = lax.broadcasted_iota(jnp.int32, (TM, TM), 0)
    col = lax.broadcasted_iota(jnp.int32, (TM, TM), 1)
    before = _dot((row > col).astype(BF16), onehot.astype(BF16)) + carry_scr[...]
    carry_scr[...] += jnp.sum(onehot, axis=0, keepdims=True)
    cnt_ref[...] = carry_scr[...]
    route = jnp.zeros((TM, LANES), F32)
    for j in range(TOP_K):
        slot = jnp.sum(jnp.where(sels[j], before, 0.0), axis=-1, keepdims=True)
        route = jnp.where(lane == j, idxs[j], route)
        route = jnp.where(lane == TOP_K + j, slot, route)
        route = jnp.where(lane == 2 * TOP_K + j, exps[j] * inv, route)
    route_ref[...] = route
    routet_ref[...] = route.T[0:8, :].astype(jnp.int32)


def _dispatch_kernel(pos_ref, h2_ref, xs_ref, sem):
    def copy(t, j):
        return pltpu.make_async_copy(_row_tile(h2_ref, t), _row_tile(xs_ref, pos_ref[TOP_K * t + j]), sem)

    def start(t, carry):
        for j in range(TOP_K):
            copy(t, j).start(priority=j % 2)
        return carry

    lax.fori_loop(0, TM, start, 0, unroll=ISSUE_UNROLL)
    for j in range(TOP_K):
        pltpu.make_async_copy(h2_ref, xs_ref.at[pl.ds(0, TM * ROW_SUB)], sem).wait()


def _expert_kernel(tile_ref, exp_ref, flag_ref, lo_ref, hi_ref,
                   xs_ref, wg_ref, bg_ref, wl_ref, bl_ref, wd_ref, bd_ref, ys_ref, w_scr):
    w = pl.program_id(0)
    flags = flag_ref[w]

    @pl.when((flags & 4) != 0)
    def _():
        w_scr[0] = wg_ref[0].astype(BF16)
        w_scr[1] = wl_ref[0].astype(BF16)
        w_scr[2] = wd_ref[0].astype(BF16)

    @pl.when((flags & 1) != 0)
    def _():
        x = _load_rows(xs_ref).astype(BF16)
        gt = jnp.minimum(_dot(x, w_scr[0]) + bg_ref[0], SWIGLU_LIMIT)
        lin = jnp.clip(_dot(x, w_scr[1]) + bl_ref[0], -SWIGLU_LIMIT, SWIGLU_LIMIT)
        act = gt * jax.nn.sigmoid(SWIGLU_ALPHA * gt) * (lin + 1.0)
        y = _dot(act.astype(BF16), w_scr[2]) + bd_ref[0]
        rows = lax.broadcasted_iota(jnp.int32, (y.shape[0], 1), 0)
        mine = (rows >= lo_ref[w]) & (rows < hi_ref[w])

        @pl.when((flags & 2) != 0)
        def _():
            _store_rows(ys_ref, jnp.where(mine, y, 0.0))

        @pl.when((flags & 2) == 0)
        def _():
            _store_rows(ys_ref, jnp.where(mine, y, _load_rows(ys_ref)))


def _combine_kernel(pos_ref, posn_ref, route_ref, x1_ref, mod_ref, fg_ref, ys_ref, outc_ref, outl_ref,
                    buf, sem, *, n_ctx_tiles):
    i = pl.program_id(0)
    n = pl.num_programs(0)

    def copy(p_ref, slot, t, j):
        return pltpu.make_async_copy(_row_tile(ys_ref, p_ref[TOP_K * t + j]), _row_tile(buf.at[slot, j], t), sem.at[slot])

    def start_all(p_ref, slot):
        def body(t, carry):
            for j in range(TOP_K):
                copy(p_ref, slot, t, j).start(priority=j % 2)
            return carry
        lax.fori_loop(0, TM, body, 0, unroll=ISSUE_UNROLL)

    @pl.when(i == 0)
    def _():
        start_all(pos_ref, 0)

    @pl.when(i + 1 < n)
    def _():
        start_all(posn_ref, (i + 1) % 2)

    slot = i % 2

    for j in range(TOP_K):
        pltpu.make_async_copy(ys_ref.at[pl.ds(0, TM * ROW_SUB)], buf.at[slot, j], sem.at[slot]).wait()
    acc = None
    for j in range(TOP_K):
        term = route_ref[:, 2 * TOP_K + j:2 * TOP_K + j + 1] * _load_rows(buf.at[slot, j])
        acc = term if acc is None else acc + term
    out = _rms(x1_ref[...] + mod_ref[0, 5:6, :] * acc) * fg_ref[...]

    @pl.when(i < n_ctx_tiles)
    def _():
        outc_ref[...] = out

    @pl.when(i >= n_ctx_tiles)
    def _():
        outl_ref[...] = out


def _work_items(counts, n_experts, n_rows, tmx):
    n_items_max = n_rows // tmx + n_experts - 1
    cnt = counts.astype(jnp.int32)
    offs = jnp.concatenate([jnp.zeros((1,), jnp.int32), jnp.cumsum(cnt)])
    first_tile = offs[:-1] // tmx
    n_it = jnp.where(cnt > 0, (offs[1:] - 1) // tmx - first_tile + 1, 0)
    it_start = jnp.concatenate([jnp.zeros((1,), jnp.int32), jnp.cumsum(n_it)])
    total = it_start[-1]
    w = jnp.arange(n_items_max, dtype=jnp.int32)
    wc = jnp.minimum(w, total - 1)
    e = jnp.sum((it_start[None, 1:] <= wc[:, None]).astype(jnp.int32), axis=1)
    e = jnp.minimum(e, n_experts - 1)
    tile = first_tile[e] + wc - it_start[e]
    valid = w < total
    prev = lambda a: jnp.concatenate([jnp.full((1,), -1, jnp.int32), a[:-1]])
    flags = (valid.astype(jnp.int32) + 2 * (valid & (tile != prev(tile))).astype(jnp.int32)
             + 4 * (valid & (e != prev(e))).astype(jnp.int32))
    lo = jnp.clip(offs[e] - tile * tmx, 0, tmx)
    hi = jnp.clip(offs[e + 1] - tile * tmx, 0, tmx)
    return offs, tile, e, flags, lo, hi


def _pack_in_proj(w, b):
    d = w.shape[0]
    o_q, o_k, o_v, o_o, o_g, o_ga, o_gb = 3 * d, 3 * d + 512, 3 * d + 1024, 4 * d + 1024, 5 * d + 1024, \
        5 * d + 1024 + 4 * N_HEADS, 6 * d + 1024 + 4 * N_HEADS

    def pad_heads(m):
        m = m.reshape(m.shape[0], N_HEADS, D_QK)
        return jnp.pad(m, ((0, 0), (0, 0), (0, LANES - D_QK))).reshape(m.shape[0], N_HEADS * LANES)

    def gates(m):
        gz = jnp.zeros((m.shape[0], LANES - N_HD), m.dtype)
        i_f, f_f, i_b, f_b = (m[:, o_g + j * N_HEADS:o_g + (j + 1) * N_HEADS] for j in range(4))
        return jnp.concatenate([i_f, i_b, gz, f_f, f_b, gz], axis=1)

    def pack(m):
        return jnp.concatenate([m[:, 0:o_q], pad_heads(m[:, o_q:o_k]), pad_heads(m[:, o_k:o_v]), m[:, o_v:o_o],
                                m[:, o_o:o_g], m[:, o_ga:o_gb], m[:, o_gb:o_gb + d], gates(m)], axis=1)

    return pack(w.astype(BF16)), pack(b.reshape(1, -1))


def kernel(x_prompt, x_sample, c, state_C, state_n, state_m, c_ctx, ada_w, ada_b, norm1_g, norm2_g, w_in, b_in,
           conv_w, conv_b, w_conv_out, mh_norm_g, w_m_out, w_o, router_w, router_b, w_gate, b_gate, w_lin, b_lin,
           w_down, b_down, final_g):
    nb, seq, d = x_prompt.shape
    nd, dseq, _ = x_sample.shape
    n_experts = w_gate.shape[1]
    assert d == _D and w_in.shape[0] == 1 and seq == TM and dseq % TM == 0 and TM % GRID_W == 0
    t_ctx, t_lat = nb * seq, nd * dseq
    n_tok = t_ctx + t_lat
    n_ctx_tiles, n_tiles = t_ctx // TM, n_tok // TM
    lat_chunks = dseq // TM
    n_lat_tiles = n_tiles - n_ctx_tiles

    n_c = 1 + nd
    n_cp = -(-n_c // 8) * 8
    cvec = jnp.concatenate([c_ctx[None, :], c, jnp.zeros((n_cp - n_c, d), F32)], axis=0)
    mod = _modulation(cvec, ada_w[0], ada_b[0]).reshape(n_cp, N_MOD, d)
    mod = jnp.pad(mod, ((0, 0), (0, 8 - N_MOD), (0, 0)))

    def mod_row(tile_tokens):
        ctx_t, per_seq = t_ctx // tile_tokens, dseq // tile_tokens
        return lambda i, *_: (jnp.where(i < ctx_t, 0, 1 + (i - ctx_t) // per_seq), 0, 0)

    x_ctx, x_lat = x_prompt.reshape(t_ctx, d), x_sample.reshape(t_lat, d)
    ctx_spec = pl.BlockSpec((TM, d), lambda i: (jnp.minimum(i, n_ctx_tiles - 1), 0))
    lat_spec = pl.BlockSpec((TM, d), lambda i: (jnp.maximum(i - n_ctx_tiles, 0), 0))
    w_all, b_all = _pack_in_proj(w_in[0], b_in[0])
    tile = lambda cols: pl.BlockSpec((TM, cols), lambda i: (i, 0))
    rows = lambda n: pl.BlockSpec((n * ROW_SUB, LANES), lambda i: (i, 0))
    params = pltpu.CompilerParams(dimension_semantics=("arbitrary",), vmem_limit_bytes=VMEM_LIMIT)
    bf = lambda cols: jax.ShapeDtypeStruct((n_tok, cols), BF16)

    ycg, sgb, q, k, v, so, gt = pl.pallas_call(
        functools.partial(_proj_kernel, n_ctx_tiles=n_ctx_tiles, ctx_row=seq),
        grid=(n_tiles,),
        in_specs=[ctx_spec, lat_spec, pl.BlockSpec((1, 8, d), mod_row(TM)), _const_spec((1, d)),
                  _const_spec((d, W_COLS)), _const_spec((1, W_COLS)), _const_spec((3, d)), _const_spec((1, d)),
                  _const_spec((d, d))],
        out_specs=[tile(d), tile(d), tile(d), tile(d), tile(d), tile(d), tile(2 * LANES)],
        out_shape=[bf(d), bf(d), bf(d), bf(d), bf(d), bf(d), jax.ShapeDtypeStruct((n_tok, 2 * LANES), F32)],
        compiler_params=params,
        name="in_proj_conv",
    )(x_ctx, x_lat, mod, norm1_g, w_all, b_all, conv_w[0], conv_b, w_conv_out[0].astype(BF16))

    c_new, n_new, m_new = _state_scan(k, v, gt, 0, nb, 1, None, False, True)
    init = (jnp.pad(jnp.swapaxes(state_C[:, 0].astype(F32).reshape(nd, N_HD, D_QK, D_V), 2, 3),
                    ((0, 0), (0, 0), (0, 0), (0, LANES - D_QK))),
            jnp.pad(state_n[:, 0].astype(F32).reshape(nd, N_HD, D_QK), ((0, 0), (0, 0), (0, LANES - D_QK))),
            jnp.pad(state_m[:, 0].astype(F32).reshape(nd, 1, N_HD), ((0, 0), (0, 0), (0, LANES - N_HD))))
    cbf, cbb, nbf, nbb, mbf, mbb = _state_scan(k, v, gt, n_ctx_tiles, nd, lat_chunks, init, True, False)

    lat_idx = lambda i: jnp.maximum(i - n_ctx_tiles, 0)
    st4 = pl.BlockSpec((1, N_HEADS, D_V, LANES), lambda i: (lat_idx(i), 0, 0, 0))
    st3 = pl.BlockSpec((1, N_HEADS, LANES), lambda i: (lat_idx(i), 0, 0))
    st1 = pl.BlockSpec((1, 1, LANES), lambda i: (lat_idx(i), 0, 0))
    rw = jnp.pad(router_w[0], ((0, 0), (0, LANES - n_experts)))
    rb = jnp.pad(router_b[0], (0, LANES - n_experts)).reshape(1, LANES)
    x1, h2, route, route_t, counts = pl.pallas_call(
        functools.partial(_mix_kernel, n_ctx_tiles=n_ctx_tiles, n_experts=n_experts),
        grid=(n_tiles,),
        in_specs=[tile(d), tile(d), tile(d), tile(d), tile(d), tile(d), tile(2 * LANES), ctx_spec, lat_spec,
                  pl.BlockSpec((1, 8, d), mod_row(TM)), st4, st4, st3, st3, st1, st1,
                  _const_spec((1, d)), _const_spec((d, d)), _const_spec((d, d)), _const_spec((1, d)),
                  _const_spec((d, LANES)), _const_spec((1, LANES))],
        out_specs=[tile(d), rows(TM), tile(LANES), pl.BlockSpec((8, TM), lambda i: (0, i)),
                   pl.BlockSpec((1, LANES), lambda i: (0, 0))],
        out_shape=[jax.ShapeDtypeStruct((n_tok, d), F32), jax.ShapeDtypeStruct((n_tok * ROW_SUB, LANES), F32),
                   jax.ShapeDtypeStruct((n_tok, LANES), F32), jax.ShapeDtypeStruct((8, n_tok), jnp.int32),
                   jax.ShapeDtypeStruct((1, LANES), F32)],
        scratch_shapes=[pltpu.VMEM((TM, d), BF16), pltpu.VMEM((1, LANES), F32)],
        compiler_params=params,
        name="mlstm_mix_router",
    )(q, k, v, so, sgb, ycg, gt, x_ctx, x_lat, mod, cbf, cbb, nbf, nbb, mbf, mbb,
      mh_norm_g, w_m_out[0].astype(BF16), w_o[0].astype(BF16), norm2_g, rw, rb)

    n_rows = TOP_K * n_tok
    offs, it_tile, it_exp, it_flags, it_lo, it_hi = _work_items(counts[0, :n_experts], n_experts, n_rows, TMX)
    is_exp = route_t[0:TOP_K, :, None] == jnp.arange(n_experts, dtype=jnp.int32)
    pos_t = jnp.sum(jnp.where(is_exp, offs[:n_experts], 0), axis=-1) + route_t[TOP_K:2 * TOP_K]
    pos = pos_t.T.reshape(n_rows)
    pos_spec = lambda f: pl.BlockSpec((TOP_K * TM,), f, memory_space=pltpu.SMEM)
    any_spec = pl.BlockSpec(memory_space=pl.ANY)

    xs = pl.pallas_call(
        _dispatch_kernel,
        grid=(n_tiles,),
        in_specs=[pos_spec(lambda i: (i,)), rows(TM)],
        out_specs=any_spec,
        out_shape=jax.ShapeDtypeStruct((n_rows * ROW_SUB, LANES), F32),
        scratch_shapes=[pltpu.SemaphoreType.DMA],
        compiler_params=params,
        name="moe_dispatch",
    )(pos, h2)

    dff = w_gate.shape[-1]
    wspec = lambda a, b_: pl.BlockSpec((1, a, b_), lambda w, tl, ex, *_: (ex[w], 0, 0))
    ys = pl.pallas_call(
        _expert_kernel,
        grid_spec=pltpu.PrefetchScalarGridSpec(
            num_scalar_prefetch=5,
            grid=(it_tile.shape[0],),
            in_specs=[pl.BlockSpec((TMX * ROW_SUB, LANES), lambda w, tl, *_: (tl[w], 0)),
                      wspec(d, dff), wspec(1, dff), wspec(d, dff), wspec(1, dff), wspec(dff, d), wspec(1, d)],
            out_specs=pl.BlockSpec((TMX * ROW_SUB, LANES), lambda w, tl, *_: (tl[w], 0)),
            scratch_shapes=[pltpu.VMEM((3, d, dff), BF16)]),
        out_shape=jax.ShapeDtypeStruct((n_rows * ROW_SUB, LANES), F32),
        compiler_params=params,
        name="moe_experts",
    )(it_tile, it_exp, it_flags, it_lo, it_hi, xs, w_gate[0], b_gate[0].reshape(n_experts, 1, dff), w_lin[0],
      b_lin[0].reshape(n_experts, 1, dff), w_down[0], b_down[0].reshape(n_experts, 1, d))

    ctx_i = lambda i: (jnp.minimum(i, n_ctx_tiles - 1), 0)
    lat_i = lambda i: (jnp.maximum(i - n_ctx_tiles, 0), 0)
    y_prompt, y_sample = pl.pallas_call(
        functools.partial(_combine_kernel, n_ctx_tiles=n_ctx_tiles),
        grid=(n_tiles,),
        in_specs=[pos_spec(lambda i: (i,)), pos_spec(lambda i: (jnp.minimum(i + 1, n_tiles - 1),)),
                  tile(LANES), tile(d), pl.BlockSpec((1, 8, d), mod_row(TM)), _const_spec((1, d)), any_spec],
        out_specs=[pl.BlockSpec((TM, d), ctx_i), pl.BlockSpec((TM, d), lat_i)],
        out_shape=[jax.ShapeDtypeStruct((t_ctx, d), F32), jax.ShapeDtypeStruct((t_lat, d), F32)],
        scratch_shapes=[pltpu.VMEM((2, TOP_K, TM * ROW_SUB, LANES), F32), pltpu.SemaphoreType.DMA((2,))],
        compiler_params=params,
        name="moe_combine",
    )(pos, pos, route, x1, mod, final_g.reshape(1, d), ys)

    y_prompt = y_prompt.reshape(nb, seq, d)
    y_sample = y_sample.reshape(nd, dseq, d)
    new_c = c_new.reshape(nb, 1, 2, N_HEADS, D_QK, D_V)
    new_n = n_new[:, :, :D_QK].reshape(nb, 1, 2, N_HEADS, D_QK)
    new_m = m_new[:, 0, :N_HD].reshape(nb, 1, 2, N_HEADS)
    return (y_prompt, y_sample, new_c, new_n, new_m)
```

```python
import functools

import jax
import jax.numpy as jnp
from jax import lax
from jax.experimental import pallas as pl
from jax.experimental.pallas import tpu as pltpu

F32 = jnp.float32
BF16 = jnp.bfloat16
HIGHEST = lax.Precision.HIGHEST

N_HEADS = 8
D_QK = 64
D_V = 128
GRID_W = 64
TOP_K = 4
SWIGLU_LIMIT = 7.0
SWIGLU_ALPHA = 1.702
EPS = 1e-6
N_MOD = 6

LANES = 128
TM = 256
TMX = 512
ISSUE_UNROLL = 4
N_HD = 2 * N_HEADS
VMEM_LIMIT = 56 * 1024 * 1024

_D = 1024
C_CONV = (0, 3 * _D)
C_Q = (3 * _D, 4 * _D)
C_K = (4 * _D, 5 * _D)
C_V = (5 * _D, 6 * _D)
C_O = (6 * _D, 7 * _D)
C_GA = (7 * _D, 8 * _D)
C_GB = (8 * _D, 9 * _D)
C_G = (9 * _D, 9 * _D + 2 * LANES)
W_COLS = C_G[1]


def _dot(a, b, precision=None):
    return jnp.dot(a, b, preferred_element_type=F32, precision=precision)


def _rms(x):
    return x * lax.rsqrt(jnp.mean(x * x, axis=-1, keepdims=True) + EPS)


ROW_SUB = 8


def _store_rows(ref, val):
    n = val.shape[0]
    for s in range(ROW_SUB):
        ref[pl.ds(s, n, stride=ROW_SUB), :] = val[:, s * LANES:(s + 1) * LANES]


def _load_rows(ref):
    n = ref.shape[0] // ROW_SUB
    return jnp.concatenate([ref[pl.ds(s, n, stride=ROW_SUB), :] for s in range(ROW_SUB)], axis=1)


def _row_tile(ref, r):
    return ref.at[pl.ds(pl.multiple_of(r * ROW_SUB, ROW_SUB), ROW_SUB)]


def _const_spec(shape):
    return pl.BlockSpec(shape, lambda *_: (0,) * len(shape), pipeline_mode=pl.Buffered(1))


def _mod_kernel(c_ref, w_ref, b_ref, o_ref):
    c = c_ref[...]
    o_ref[...] = _dot(c * jax.nn.sigmoid(c), w_ref[...], HIGHEST) + b_ref[...]


def _modulation(cvec, ada_w, ada_b):
    n, d = cvec.shape
    nout = ada_w.shape[1]
    return pl.pallas_call(
        _mod_kernel,
        grid=(nout // d,),
        in_specs=[pl.BlockSpec((n, d), lambda j: (0, 0)),
                  pl.BlockSpec((d, d), lambda j: (0, j)),
                  pl.BlockSpec((1, d), lambda j: (0, j))],
        out_specs=pl.BlockSpec((n, d), lambda j: (0, j)),
        out_shape=jax.ShapeDtypeStruct((n, nout), F32),
        name="adaln_mod",
    )(cvec, ada_w, ada_b.reshape(1, nout))


def _proj_kernel(xc_ref, xl_ref, mod_ref, g1_ref, w_ref, b_ref, cw_ref, cb_ref, wco_ref,
                 ycg_ref, sgb_ref, q_ref, k_ref, v_ref, so_ref, gt_ref, *, n_ctx_tiles, ctx_row):
    i = pl.program_id(0)
    x = jnp.where(i < n_ctx_tiles, xc_ref[...], xl_ref[...])
    h = (_rms(x) * g1_ref[...] * (1.0 + mod_ref[0, 1:2, :]) + mod_ref[0, 0:1, :]).astype(BF16)

    def proj(cols):
        return _dot(h, w_ref[:, cols[0]:cols[1]]) + b_ref[:, cols[0]:cols[1]]

    zc = proj(C_CONV)
    d = x.shape[1]
    u = zc[:, 2 * d:3 * d] * zc[:, 0:d]
    rowlen = jnp.where(i < n_ctx_tiles, ctx_row, GRID_W)
    pos = lax.broadcasted_iota(jnp.int32, (TM, 1), 0) & (rowlen - 1)
    u_prev = jnp.where(pos == 0, 0.0, pltpu.roll(u, 1, 0))
    u_next = jnp.where(pos == rowlen - 1, 0.0, pltpu.roll(u, TM - 1, 0))
    uc = u_prev * cw_ref[0:1, :] + u * cw_ref[1:2, :] + u_next * cw_ref[2:3, :] + cb_ref[...]
    yconv = _dot((zc[:, d:2 * d] * uc).astype(BF16), wco_ref[...])

    ycg_ref[...] = (jax.nn.sigmoid(proj(C_GA)) * yconv).astype(BF16)
    sgb_ref[...] = jax.nn.sigmoid(proj(C_GB)).astype(BF16)
    q_ref[...] = (proj(C_Q) * (D_QK ** -0.5)).astype(BF16)
    k_ref[...] = proj(C_K).astype(BF16)
    v_ref[...] = proj(C_V).astype(BF16)
    so_ref[...] = jax.nn.sigmoid(proj(C_O)).astype(BF16)
    gt_ref[...] = proj(C_G)


def _tri_masks():
    row = lax.broadcasted_iota(jnp.int32, (TM, TM), 0)
    col = lax.broadcasted_iota(jnp.int32, (TM, TM), 1)
    return row >= col, row <= col


def _gate_cumsums(gi, gf):
    lower, upper = _tri_masks()
    lf = jax.nn.log_sigmoid(gf)
    fwd_lane = lax.broadcasted_iota(jnp.int32, (1, LANES), 1) < N_HEADS
    bsum = jnp.where(fwd_lane, _dot(lower.astype(F32), lf, HIGHEST), _dot(upper.astype(F32), lf, HIGHEST))
    return lf, bsum, fwd_lane


def _state_kernel(*refs, zero_init, emit_before, emit_after):
    it = iter(refs)
    kf_ref, vf_ref, gf_ref, kb_ref, vb_ref, gb_ref = (next(it) for _ in range(6))
    if not zero_init:
        c0_ref, n0_ref, m0_ref = (next(it) for _ in range(3))
    if emit_before:
        cbf_ref, cbb_ref, nbf_ref, nbb_ref, mbf_ref, mbb_ref = (next(it) for _ in range(6))
    if emit_after:
        ca_ref, na_ref, ma_ref = (next(it) for _ in range(3))
    c_scr, n_scr, m_scr = (next(it) for _ in range(3))
    c = pl.program_id(1)

    @pl.when(c == 0)
    def _():
        if zero_init:
            c_scr[...] = jnp.zeros_like(c_scr)
            n_scr[...] = jnp.zeros_like(n_scr)
            m_scr[...] = jnp.zeros_like(m_scr)
        else:
            c_scr[...] = c0_ref[0]
            n_scr[...] = n0_ref[0]
            m_scr[...] = m0_ref[0]

    if emit_before:
        cbf_ref[0] = c_scr[0:N_HEADS]
        cbb_ref[0] = c_scr[N_HEADS:N_HD]
        nbf_ref[0] = n_scr[0:N_HEADS]
        nbb_ref[0] = n_scr[N_HEADS:N_HD]
        mbf_ref[0] = m_scr[...]
        mbb_ref[0] = m_scr[...]

    fwd_lane = lax.broadcasted_iota(jnp.int32, (1, LANES), 1) < N_HEADS
    gi = jnp.where(fwd_lane, gf_ref[:, 0:LANES], gb_ref[:, 0:LANES])
    gfg = jnp.where(fwd_lane, gf_ref[:, LANES:2 * LANES], gb_ref[:, LANES:2 * LANES])
    lf, bsum, _ = _gate_cumsums(gi, gfg)
    total = jnp.sum(lf, axis=0, keepdims=True)
    g = total - bsum + gi
    m_prev = m_scr[...]
    m_new = jnp.maximum(total + m_prev, jnp.max(g, axis=0, keepdims=True))
    wk = jnp.exp(g - m_new)
    decay = jnp.exp(total + m_prev - m_new)
    for hd in range(N_HD):
        h = hd % N_HEADS
        k_ref, v_ref = (kf_ref, vf_ref) if hd < N_HEADS else (kb_ref, vb_ref)
        wkk = wk[:, hd:hd + 1] * k_ref[:, h * LANES:(h + 1) * LANES].astype(F32)
        vh = v_ref[:, h * D_V:(h + 1) * D_V]
        dec = decay[:, hd:hd + 1]
        tn = (((0,), (0,)), ((), ()))
        if emit_before:
            c_scr[hd] = dec * c_scr[hd] + lax.dot_general(vh, wkk.astype(BF16), tn, preferred_element_type=F32)
        else:
            upd = lax.dot_general(wkk.astype(BF16), vh, tn, preferred_element_type=F32)
            c_scr[hd] = dec * c_scr[hd] + upd[0:D_QK, :]
        n_scr[hd:hd + 1, :] = dec * n_scr[hd:hd + 1, :] + jnp.sum(wkk, axis=0, keepdims=True)
    m_scr[...] = m_new

    if emit_after:
        @pl.when(c == pl.num_programs(1) - 1)
        def _():
            ca_ref[0] = c_scr[...]
            na_ref[0] = n_scr[...]
            ma_ref[0] = m_scr[...]


def _state_scan(k, v, gt, tile0, n_seq, n_chunk, init, emit_before, emit_after):
    d = k.shape[1]
    cshape = (D_V, LANES) if emit_before else (D_QK, D_V)
    fwd = lambda s, c: (tile0 + s * n_chunk + c, 0)
    bwd = lambda s, c: (tile0 + s * n_chunk + n_chunk - 1 - c, 0)
    in_specs = [pl.BlockSpec((TM, d), fwd), pl.BlockSpec((TM, d), fwd), pl.BlockSpec((TM, 2 * LANES), fwd),
                pl.BlockSpec((TM, d), bwd), pl.BlockSpec((TM, d), bwd), pl.BlockSpec((TM, 2 * LANES), bwd)]
    args = [k, v, gt, k, v, gt]
    if init is not None:
        in_specs += [pl.BlockSpec((1, N_HD) + cshape, lambda s, c: (s, 0, 0, 0)),
                     pl.BlockSpec((1, N_HD, LANES), lambda s, c: (s, 0, 0)),
                     pl.BlockSpec((1, 1, LANES), lambda s, c: (s, 0, 0))]
        args += list(init)
    out_specs, out_shape = [], []
    n_tot = n_seq * n_chunk
    if emit_before:
        cf = lambda s, c: (s * n_chunk + c, 0, 0, 0)
        cb = lambda s, c: (s * n_chunk + n_chunk - 1 - c, 0, 0, 0)
        nf = lambda s, c: (s * n_chunk + c, 0, 0)
        nb = lambda s, c: (s * n_chunk + n_chunk - 1 - c, 0, 0)
        out_specs += [pl.BlockSpec((1, N_HEADS) + cshape, cf), pl.BlockSpec((1, N_HEADS) + cshape, cb),
                      pl.BlockSpec((1, N_HEADS, LANES), nf), pl.BlockSpec((1, N_HEADS, LANES), nb),
                      pl.BlockSpec((1, 1, LANES), nf), pl.BlockSpec((1, 1, LANES), nb)]
        out_shape += [jax.ShapeDtypeStruct((n_tot, N_HEADS) + cshape, F32)] * 2
        out_shape += [jax.ShapeDtypeStruct((n_tot, N_HEADS, LANES), F32)] * 2
        out_shape += [jax.ShapeDtypeStruct((n_tot, 1, LANES), F32)] * 2
    if emit_after:
        out_specs += [pl.BlockSpec((1, N_HD, D_QK, D_V), lambda s, c: (s, 0, 0, 0)),
                      pl.BlockSpec((1, N_HD, LANES), lambda s, c: (s, 0, 0)),
                      pl.BlockSpec((1, 1, LANES), lambda s, c: (s, 0, 0))]
        out_shape += [jax.ShapeDtypeStruct((n_seq, N_HD, D_QK, D_V), F32),
                      jax.ShapeDtypeStruct((n_seq, N_HD, LANES), F32),
                      jax.ShapeDtypeStruct((n_seq, 1, LANES), F32)]
    return pl.pallas_call(
        functools.partial(_state_kernel, zero_init=init is None, emit_before=emit_before, emit_after=emit_after),
        grid=(n_seq, n_chunk),
        in_specs=in_specs,
        out_specs=out_specs,
        out_shape=out_shape,
        scratch_shapes=[pltpu.VMEM((N_HD,) + cshape, F32), pltpu.VMEM((N_HD, LANES), F32),
                        pltpu.VMEM((1, LANES), F32)],
        compiler_params=pltpu.CompilerParams(dimension_semantics=("arbitrary", "arbitrary"),
                                             vmem_limit_bytes=VMEM_LIMIT),
        name="mlstm_state_scan",
    )(*args)


def _mix_kernel(q_ref, k_ref, v_ref, so_ref, sgb_ref, ycg_ref, gt_ref, xc_ref, xl_ref, mod_ref,
                cbf_ref, cbb_ref, nbf_ref, nbb_ref, mbf_ref, mbb_ref,
                gmh_ref, wmo_ref, wo_ref, g2_ref, rw_ref, rb_ref,
                x1_ref, h2_ref, route_ref, routet_ref, cnt_ref, hm_scr, carry_scr, *, n_ctx_tiles, n_experts):
    i = pl.program_id(0)
    is_lat = i >= n_ctx_tiles
    lat_f = is_lat.astype(F32)
    lower, upper = _tri_masks()
    gt_t = gt_ref[...].T
    gi_t = gt_t[0:N_HD, :]
    lf_t = jax.nn.log_sigmoid(gt_t[LANES:LANES + N_HD, :])
    fwd_row = lax.broadcasted_iota(jnp.int32, (N_HD, 1), 0) < N_HEADS
    bsum_t = jnp.where(fwd_row, _dot(lf_t, upper.astype(F32), HIGHEST), _dot(lf_t, lower.astype(F32), HIGHEST))
    a_t = gi_t - bsum_t
    m_row = jnp.where(lax.broadcasted_iota(jnp.int32, (1, LANES), 1) < N_HEADS, mbf_ref[0], mbb_ref[0]) * lat_f
    m_sq = jnp.where(lax.broadcasted_iota(jnp.int32, (LANES, 1), 0) == 0, m_row, 0.0)
    m_prev = m_sq.T[0:N_HD, 0:1]
    a_c = jnp.concatenate([a_t, jnp.zeros((LANES - N_HD, TM), F32)], axis=0).T
    row0 = lax.broadcasted_iota(jnp.int32, (LANES, 1), 0) == 0
    ones_rows = jnp.where(row0, 1.0, 0.0).astype(BF16) * jnp.ones((1, TM), BF16)

    for h in range(N_HEADS):
        hs = slice(h * D_V, (h + 1) * D_V)
        qh = q_ref[:, h * LANES:(h + 1) * LANES]
        kh = k_ref[:, h * LANES:(h + 1) * LANES]
        vext_t = jnp.concatenate([v_ref[:, hs].T, ones_rows], axis=0)
        kq = lax.dot_general(kh, qh, (((1,), (1,)), ((), ())), preferred_element_type=F32)
        hsum = None
        for d in range(2):
            hd = d * N_HEADS + h
            mask = upper if d == 0 else lower
            c_t = (cbf_ref if d == 0 else cbb_ref)[0, h] * lat_f
            n_r = (nbf_ref if d == 0 else nbb_ref)[0, h:h + 1, :] * lat_f
            a_b = jnp.where(mask, a_c[:, hd:hd + 1], -jnp.inf)
            mrow = jnp.maximum(m_prev[hd:hd + 1, :], jnp.max(a_b, axis=0, keepdims=True))
            e = jnp.exp(a_b - mrow)
            nd = _dot(vext_t, (kq * e).astype(BF16))
            cext_t = jnp.concatenate([c_t, jnp.where(row0, n_r, 0.0)], axis=0).astype(BF16)
            qc = lax.dot_general(cext_t, qh, (((1,), (1,)), ((), ())), preferred_element_type=F32)
            wi = jnp.exp(m_prev[hd:hd + 1, :] - mrow)
            num = nd[0:D_V, :] + wi * qc[0:D_V, :]
            den = nd[D_V:D_V + 1, :] + wi * qc[D_V:D_V + 1, :]
            r = 1.0 / jnp.maximum(jnp.abs(den), jnp.exp(-(bsum_t[hd:hd + 1, :] + mrow)))
            hsum = num * r if hsum is None else hsum + num * r
        hn = hsum * lax.rsqrt(jnp.mean(hsum * hsum, axis=0, keepdims=True) + EPS)
        hm_scr[:, hs] = (hn.T * gmh_ref[:, hs] * so_ref[:, hs].astype(F32)).astype(BF16)

    ym = _dot(hm_scr[...], wmo_ref[...])
    mix = (ycg_ref[...].astype(F32) + sgb_ref[...].astype(F32) * ym).astype(BF16)
    x1 = jnp.where(is_lat, xl_ref[...], xc_ref[...]) + mod_ref[0, 2:3, :] * _dot(mix, wo_ref[...])
    x1_ref[...] = x1
    h2 = _rms(x1) * g2_ref[...] * (1.0 + mod_ref[0, 4:5, :]) + mod_ref[0, 3:4, :]
    _store_rows(h2_ref, h2)

    lane = lax.broadcasted_iota(jnp.int32, (TM, LANES), 1)
    rw = rw_ref[...]
    h2_hi, rw_hi = h2.astype(BF16), rw.astype(BF16)
    h2_lo, rw_lo = (h2 - h2_hi.astype(F32)).astype(BF16), (rw - rw_hi.astype(F32)).astype(BF16)
    logits = _dot(h2_hi, rw_hi) + (_dot(h2_hi, rw_lo) + _dot(h2_lo, rw_hi)) + rb_ref[...]
    work = jnp.where(lane < n_experts, logits, -jnp.inf)
    sels, exps, idxs = [], [], []
    top = None
    for _ in range(TOP_K):
        mx = jnp.max(work, axis=-1, keepdims=True)
        ix = jnp.min(jnp.where(work == mx, lane, LANES), axis=-1, keepdims=True)
        sel = lane == ix
        work = jnp.where(sel, -jnp.inf, work)
        top = mx if top is None else top
        sels.append(sel)
        idxs.append(ix.astype(F32))
        exps.append(jnp.exp(mx - top))
    inv = 1.0 / functools.reduce(lambda p, q: p + q, exps)

    @pl.when(i == 0)
    def _():
        carry_scr[...] = jnp.zeros_like(carry_scr)

    onehot = functools.reduce(lambda p, q: p + q, [jnp.where(s, 1.0, 0.0) for s in sels])
    row = lax.broadcasted_iota(jnp.int32, (TM, TM), 0)
    col = lax.broadcasted_iota(jnp.int32, (TM, TM), 1)
    before = _dot((row > col).astype(BF16), onehot.astype(BF16)) + carry_scr[...]
    carry_scr[...] += jnp.sum(onehot, axis=0, keepdims=True)
    cnt_ref[...] = carry_scr[...]
    route = jnp.zeros((TM, LANES), F32)
    for j in range(TOP_K):
        slot = jnp.sum(jnp.where(sels[j], before, 0.0), axis=-1, keepdims=True)
        route = jnp.where(lane == j, idxs[j], route)
        route = jnp.where(lane == TOP_K + j, slot, route)
        route = jnp.where(lane == 2 * TOP_K + j, exps[j] * inv, route)
    route_ref[...] = route
    routet_ref[...] = route.T[0:8, :].astype(jnp.int32)


def _dispatch_kernel(pos_ref, h2_ref, xs_ref, sem):
    def copy(t, j):
        return pltpu.make_async_copy(_row_tile(h2_ref, t), _row_tile(xs_ref, pos_ref[TOP_K * t + j]), sem)

    def start(t, carry):
        for j in range(TOP_K):
            copy(t, j).start(priority=j % 2)
        return carry

    lax.fori_loop(0, TM, start, 0, unroll=ISSUE_UNROLL)
    for j in range(TOP_K):
        pltpu.make_async_copy(h2_ref, xs_ref.at[pl.ds(0, TM * ROW_SUB)], sem).wait()


def _expert_kernel(tile_ref, exp_ref, flag_ref, lo_ref, hi_ref,
                   xs_ref, wg_ref, bg_ref, wl_ref, bl_ref, wd_ref, bd_ref, ys_ref, w_scr):
    w = pl.program_id(0)
    flags = flag_ref[w]

    @pl.when((flags & 4) != 0)
    def _():
        w_scr[0] = wg_ref[0].astype(BF16)
        w_scr[1] = wl_ref[0].astype(BF16)
        w_scr[2] = wd_ref[0].astype(BF16)

    @pl.when((flags & 1) != 0)
    def _():
        x = _load_rows(xs_ref).astype(BF16)
        gt = jnp.minimum(_dot(x, w_scr[0]) + bg_ref[0], SWIGLU_LIMIT)
        lin = jnp.clip(_dot(x, w_scr[1]) + bl_ref[0], -SWIGLU_LIMIT, SWIGLU_LIMIT)
        act = gt * jax.nn.sigmoid(SWIGLU_ALPHA * gt) * (lin + 1.0)
        y = _dot(act.astype(BF16), w_scr[2]) + bd_ref[0]
        rows = lax.broadcasted_iota(jnp.int32, (y.shape[0], 1), 0)
        mine = (rows >= lo_ref[w]) & (rows < hi_ref[w])

        @pl.when((flags & 2) != 0)
        def _():
            _store_rows(ys_ref, jnp.where(mine, y, 0.0))

        @pl.when((flags & 2) == 0)
        def _():
            _store_rows(ys_ref, jnp.where(mine, y, _load_rows(ys_ref)))


def _combine_kernel(pos_ref, posn_ref, route_ref, x1_ref, mod_ref, fg_ref, ys_ref, outc_ref, outl_ref,
                    buf, sem, *, n_ctx_tiles):
    i = pl.program_id(0)
    n = pl.num_programs(0)

    def copy(p_ref, slot, t, j):
        return pltpu.make_async_copy(_row_tile(ys_ref, p_ref[TOP_K * t + j]), _row_tile(buf.at[slot, j], t), sem.at[slot])

    def start_all(p_ref, slot):
        def body(t, carry):
            for j in range(TOP_K):
                copy(p_ref, slot, t, j).start(priority=j % 2)
            return carry
        lax.fori_loop(0, TM, body, 0, unroll=ISSUE_UNROLL)

    @pl.when(i == 0)
    def _():
        start_all(pos_ref, 0)

    @pl.when(i + 1 < n)
    def _():
        start_all(posn_ref, (i + 1) % 2)

    slot = i % 2

    for j in range(TOP_K):
        pltpu.make_async_copy(ys_ref.at[pl.ds(0, TM * ROW_SUB)], buf.at[slot, j], sem.at[slot]).wait()
    acc = None
    for j in range(TOP_K):
        term = route_ref[:, 2 * TOP_K + j:2 * TOP_K + j + 1] * _load_rows(buf.at[slot, j])
        acc = term if acc is None else acc + term
    out = _rms(x1_ref[...] + mod_ref[0, 5:6, :] * acc) * fg_ref[...]

    @pl.when(i < n_ctx_tiles)
    def _():
        outc_ref[...] = out

    @pl.when(i >= n_ctx_tiles)
    def _():
        outl_ref[...] = out


def _work_items(counts, n_experts, n_rows, tmx):
    n_items_max = n_rows // tmx + n_experts - 1
    cnt = counts.astype(jnp.int32)
    offs = jnp.concatenate([jnp.zeros((1,), jnp.int32), jnp.cumsum(cnt)])
    first_tile = offs[:-1] // tmx
    n_it = jnp.where(cnt > 0, (offs[1:] - 1) // tmx - first_tile + 1, 0)
    it_start = jnp.concatenate([jnp.zeros((1,), jnp.int32), jnp.cumsum(n_it)])
    total = it_start[-1]
    w = jnp.arange(n_items_max, dtype=jnp.int32)
    wc = jnp.minimum(w, total - 1)
    e = jnp.sum((it_start[None, 1:] <= wc[:, None]).astype(jnp.int32), axis=1)
    e = jnp.minimum(e, n_experts - 1)
    is_e = e[:, None] == jnp.arange(n_experts, dtype=jnp.int32)
    at_e = lambda a: jnp.sum(jnp.where(is_e, a[None, :], 0), axis=1)
    tile = at_e(first_tile) + wc - at_e(it_start[:-1])
    valid = w < total
    prev = lambda a: jnp.concatenate([jnp.full((1,), -1, jnp.int32), a[:-1]])
    flags = (valid.astype(jnp.int32) + 2 * (valid & (tile != prev(tile))).astype(jnp.int32)
             + 4 * (valid & (e != prev(e))).astype(jnp.int32))
    lo = jnp.clip(at_e(offs[:-1]) - tile * tmx, 0, tmx)
    hi = jnp.clip(at_e(offs[1:]) - tile * tmx, 0, tmx)
    return offs, tile, e, flags, lo, hi


def _pack_in_proj(w, b):
    d = w.shape[0]
    o_q, o_k, o_v, o_o, o_g, o_ga, o_gb = 3 * d, 3 * d + 512, 3 * d + 1024, 4 * d + 1024, 5 * d + 1024, \
        5 * d + 1024 + 4 * N_HEADS, 6 * d + 1024 + 4 * N_HEADS

    def pad_heads(m):
        m = m.reshape(m.shape[0], N_HEADS, D_QK)
        return jnp.pad(m, ((0, 0), (0, 0), (0, LANES - D_QK))).reshape(m.shape[0], N_HEADS * LANES)

    def gates(m):
        gz = jnp.zeros((m.shape[0], LANES - N_HD), m.dtype)
        i_f, f_f, i_b, f_b = (m[:, o_g + j * N_HEADS:o_g + (j + 1) * N_HEADS] for j in range(4))
        return jnp.concatenate([i_f, i_b, gz, f_f, f_b, gz], axis=1)

    def pack(m):
        return jnp.concatenate([m[:, 0:o_q], pad_heads(m[:, o_q:o_k]), pad_heads(m[:, o_k:o_v]), m[:, o_v:o_o],
                                m[:, o_o:o_g], m[:, o_ga:o_gb], m[:, o_gb:o_gb + d], gates(m)], axis=1)

    return pack(w.astype(BF16)), pack(b.reshape(1, -1))


def kernel(x_prompt, x_sample, c, state_C, state_n, state_m, c_ctx, ada_w, ada_b, norm1_g, norm2_g, w_in, b_in,
           conv_w, conv_b, w_conv_out, mh_norm_g, w_m_out, w_o, router_w, router_b, w_gate, b_gate, w_lin, b_lin,
           w_down, b_down, final_g):
    nb, seq, d = x_prompt.shape
    nd, dseq, _ = x_sample.shape
    n_experts = w_gate.shape[1]
    assert d == _D and w_in.shape[0] == 1 and seq == TM and dseq % TM == 0 and TM % GRID_W == 0
    t_ctx, t_lat = nb * seq, nd * dseq
    n_tok = t_ctx + t_lat
    n_ctx_tiles, n_tiles = t_ctx // TM, n_tok // TM
    lat_chunks = dseq // TM
    n_lat_tiles = n_tiles - n_ctx_tiles

    n_c = 1 + nd
    n_cp = -(-n_c // 8) * 8
    cvec = jnp.concatenate([c_ctx[None, :], c, jnp.zeros((n_cp - n_c, d), F32)], axis=0)
    mod = _modulation(cvec, ada_w[0], ada_b[0]).reshape(n_cp, N_MOD, d)
    mod = jnp.pad(mod, ((0, 0), (0, 8 - N_MOD), (0, 0)))

    def mod_row(tile_tokens):
        ctx_t, per_seq = t_ctx // tile_tokens, dseq // tile_tokens
        return lambda i, *_: (jnp.where(i < ctx_t, 0, 1 + (i - ctx_t) // per_seq), 0, 0)

    x_ctx, x_lat = x_prompt.reshape(t_ctx, d), x_sample.reshape(t_lat, d)
    ctx_spec = pl.BlockSpec((TM, d), lambda i: (jnp.minimum(i, n_ctx_tiles - 1), 0))
    lat_spec = pl.BlockSpec((TM, d), lambda i: (jnp.maximum(i - n_ctx_tiles, 0), 0))
    w_all, b_all = _pack_in_proj(w_in[0], b_in[0])
    tile = lambda cols: pl.BlockSpec((TM, cols), lambda i: (i, 0))
    rows = lambda n: pl.BlockSpec((n * ROW_SUB, LANES), lambda i: (i, 0))
    params = pltpu.CompilerParams(dimension_semantics=("arbitrary",), vmem_limit_bytes=VMEM_LIMIT)
    bf = lambda cols: jax.ShapeDtypeStruct((n_tok, cols), BF16)

    ycg, sgb, q, k, v, so, gt = pl.pallas_call(
        functools.partial(_proj_kernel, n_ctx_tiles=n_ctx_tiles, ctx_row=seq),
        grid=(n_tiles,),
        in_specs=[ctx_spec, lat_spec, pl.BlockSpec((1, 8, d), mod_row(TM)), _const_spec((1, d)),
                  _const_spec((d, W_COLS)), _const_spec((1, W_COLS)), _const_spec((3, d)), _const_spec((1, d)),
                  _const_spec((d, d))],
        out_specs=[tile(d), tile(d), tile(d), tile(d), tile(d), tile(d), tile(2 * LANES)],
        out_shape=[bf(d), bf(d), bf(d), bf(d), bf(d), bf(d), jax.ShapeDtypeStruct((n_tok, 2 * LANES), F32)],
        compiler_params=params,
        name="in_proj_conv",
    )(x_ctx, x_lat, mod, norm1_g, w_all, b_all, conv_w[0], conv_b, w_conv_out[0].astype(BF16))

    c_new, n_new, m_new = _state_scan(k, v, gt, 0, nb, 1, None, False, True)
    init = (jnp.pad(jnp.swapaxes(state_C[:, 0].astype(F32).reshape(nd, N_HD, D_QK, D_V), 2, 3),
                    ((0, 0), (0, 0), (0, 0), (0, LANES - D_QK))),
            jnp.pad(state_n[:, 0].astype(F32).reshape(nd, N_HD, D_QK), ((0, 0), (0, 0), (0, LANES - D_QK))),
            jnp.pad(state_m[:, 0].astype(F32).reshape(nd, 1, N_HD), ((0, 0), (0, 0), (0, LANES - N_HD))))
    cbf, cbb, nbf, nbb, mbf, mbb = _state_scan(k, v, gt, n_ctx_tiles, nd, lat_chunks, init, True, False)

    lat_idx = lambda i: jnp.maximum(i - n_ctx_tiles, 0)
    st4 = pl.BlockSpec((1, N_HEADS, D_V, LANES), lambda i: (lat_idx(i), 0, 0, 0))
    st3 = pl.BlockSpec((1, N_HEADS, LANES), lambda i: (lat_idx(i), 0, 0))
    st1 = pl.BlockSpec((1, 1, LANES), lambda i: (lat_idx(i), 0, 0))
    rw = jnp.pad(router_w[0], ((0, 0), (0, LANES - n_experts)))
    rb = jnp.pad(router_b[0], (0, LANES - n_experts)).reshape(1, LANES)
    x1, h2, route, route_t, counts = pl.pallas_call(
        functools.partial(_mix_kernel, n_ctx_tiles=n_ctx_tiles, n_experts=n_experts),
        grid=(n_tiles,),
        in_specs=[tile(d), tile(d), tile(d), tile(d), tile(d), tile(d), tile(2 * LANES), ctx_spec, lat_spec,
                  pl.BlockSpec((1, 8, d), mod_row(TM)), st4, st4, st3, st3, st1, st1,
                  _const_spec((1, d)), _const_spec((d, d)), _const_spec((d, d)), _const_spec((1, d)),
                  _const_spec((d, LANES)), _const_spec((1, LANES))],
        out_specs=[tile(d), rows(TM), tile(LANES), pl.BlockSpec((8, TM), lambda i: (0, i)),
                   pl.BlockSpec((1, LANES), lambda i: (0, 0))],
        out_shape=[jax.ShapeDtypeStruct((n_tok, d), F32), jax.ShapeDtypeStruct((n_tok * ROW_SUB, LANES), F32),
                   jax.ShapeDtypeStruct((n_tok, LANES), F32), jax.ShapeDtypeStruct((8, n_tok), jnp.int32),
                   jax.ShapeDtypeStruct((1, LANES), F32)],
        scratch_shapes=[pltpu.VMEM((TM, d), BF16), pltpu.VMEM((1, LANES), F32)],
        compiler_params=params,
        name="mlstm_mix_router",
    )(q, k, v, so, sgb, ycg, gt, x_ctx, x_lat, mod, cbf, cbb, nbf, nbb, mbf, mbb,
      mh_norm_g, w_m_out[0].astype(BF16), w_o[0].astype(BF16), norm2_g, rw, rb)

    n_rows = TOP_K * n_tok
    offs, it_tile, it_exp, it_flags, it_lo, it_hi = _work_items(counts[0, :n_experts], n_experts, n_rows, TMX)
    is_exp = route_t[0:TOP_K, :, None] == jnp.arange(n_experts, dtype=jnp.int32)
    pos_t = jnp.sum(jnp.where(is_exp, offs[:n_experts], 0), axis=-1) + route_t[TOP_K:2 * TOP_K]
    pos = pos_t.T.reshape(n_rows)
    pos_spec = lambda f: pl.BlockSpec((TOP_K * TM,), f, memory_space=pltpu.SMEM)
    any_spec = pl.BlockSpec(memory_space=pl.ANY)

    xs = pl.pallas_call(
        _dispatch_kernel,
        grid=(n_tiles,),
        in_specs=[pos_spec(lambda i: (i,)), rows(TM)],
        out_specs=any_spec,
        out_shape=jax.ShapeDtypeStruct((n_rows * ROW_SUB, LANES), F32),
        scratch_shapes=[pltpu.SemaphoreType.DMA],
        compiler_params=params,
        name="moe_dispatch",
    )(pos, h2)

    dff = w_gate.shape[-1]
    wspec = lambda a, b_: pl.BlockSpec((1, a, b_), lambda w, tl, ex, *_: (ex[w], 0, 0))
    ys = pl.pallas_call(
        _expert_kernel,
        grid_spec=pltpu.PrefetchScalarGridSpec(
            num_scalar_prefetch=5,
            grid=(it_tile.shape[0],),
            in_specs=[pl.BlockSpec((TMX * ROW_SUB, LANES), lambda w, tl, *_: (tl[w], 0)),
                      wspec(d, dff), wspec(1, dff), wspec(d, dff), wspec(1, dff), wspec(dff, d), wspec(1, d)],
            out_specs=pl.BlockSpec((TMX * ROW_SUB, LANES), lambda w, tl, *_: (tl[w], 0)),
            scratch_shapes=[pltpu.VMEM((3, d, dff), BF16)]),
        out_shape=jax.ShapeDtypeStruct((n_rows * ROW_SUB, LANES), F32),
        compiler_params=params,
        name="moe_experts",
    )(it_tile, it_exp, it_flags, it_lo, it_hi, xs, w_gate[0], b_gate[0].reshape(n_experts, 1, dff), w_lin[0],
      b_lin[0].reshape(n_experts, 1, dff), w_down[0], b_down[0].reshape(n_experts, 1, d))

    ctx_i = lambda i: (jnp.minimum(i, n_ctx_tiles - 1), 0)
    lat_i = lambda i: (jnp.maximum(i - n_ctx_tiles, 0), 0)
    y_prompt, y_sample = pl.pallas_call(
        functools.partial(_combine_kernel, n_ctx_tiles=n_ctx_tiles),
        grid=(n_tiles,),
        in_specs=[pos_spec(lambda i: (i,)), pos_spec(lambda i: (jnp.minimum(i + 1, n_tiles - 1),)),
                  tile(LANES), tile(d), pl.BlockSpec((1, 8, d), mod_row(TM)), _const_spec((1, d)), any_spec],
        out_specs=[pl.BlockSpec((TM, d), ctx_i), pl.BlockSpec((TM, d), lat_i)],
        out_shape=[jax.ShapeDtypeStruct((t_ctx, d), F32), jax.ShapeDtypeStruct((t_lat, d), F32)],
        scratch_shapes=[pltpu.VMEM((2, TOP_K, TM * ROW_SUB, LANES), F32), pltpu.SemaphoreType.DMA((2,))],
        compiler_params=params,
        name="moe_combine",
    )(pos, pos, route, x1, mod, final_g.reshape(1, d), ys)

    y_prompt = y_prompt.reshape(nb, seq, d)
    y_sample = y_sample.reshape(nd, dseq, d)
    new_c = c_new.reshape(nb, 1, 2, N_HEADS, D_QK, D_V)
    new_n = n_new[:, :, :D_QK].reshape(nb, 1, 2, N_HEADS, D_QK)
    new_m = m_new[:, 0, :N_HD].reshape(nb, 1, 2, N_HEADS)
    return (y_prompt, y_sample, new_c, new_n, new_m)
```

```python
import functools

import jax
import jax.numpy as jnp
from jax import lax
from jax.experimental import pallas as pl
from jax.experimental.pallas import tpu as pltpu

F32 = jnp.float32
BF16 = jnp.bfloat16
HIGHEST = lax.Precision.HIGHEST

N_HEADS = 8
D_QK = 64
D_V = 128
GRID_W = 64
TOP_K = 4
SWIGLU_LIMIT = 7.0
SWIGLU_ALPHA = 1.702
EPS = 1e-6
N_MOD = 6

LANES = 128
TM = 256
TMX = 512
ISSUE_UNROLL = 4
N_HD = 2 * N_HEADS
VMEM_LIMIT = 56 * 1024 * 1024

_D = 1024
C_CONV = (0, 3 * _D)
D_QKH = N_HEADS * D_QK
C_Q = (3 * _D, 3 * _D + D_QKH)
C_K = (C_Q[1], C_Q[1] + D_QKH)
C_V = (C_K[1], C_K[1] + _D)
C_O = (C_V[1], C_V[1] + _D)
C_GA = (C_O[1], C_O[1] + _D)
C_GB = (C_GA[1], C_GA[1] + _D)
C_G = (C_GB[1], C_GB[1] + 2 * LANES)
W_COLS = C_G[1]


def _head_lanes(h):
    lane = lax.broadcasted_iota(jnp.int32, (1, LANES), 1)
    return (lane < D_QK) if h % 2 == 0 else (lane >= D_QK)


def _dot(a, b, precision=None):
    return jnp.dot(a, b, preferred_element_type=F32, precision=precision)


def _rms(x):
    return x * lax.rsqrt(jnp.mean(x * x, axis=-1, keepdims=True) + EPS)


ROW_SUB = 8


def _store_rows(ref, val):
    n = val.shape[0]
    for s in range(ROW_SUB):
        ref[pl.ds(s, n, stride=ROW_SUB), :] = val[:, s * LANES:(s + 1) * LANES]


def _load_rows(ref):
    n = ref.shape[0] // ROW_SUB
    return jnp.concatenate([ref[pl.ds(s, n, stride=ROW_SUB), :] for s in range(ROW_SUB)], axis=1)


def _row_tile(ref, r):
    return ref.at[pl.ds(pl.multiple_of(r * ROW_SUB, ROW_SUB), ROW_SUB)]


def _const_spec(shape):
    return pl.BlockSpec(shape, lambda *_: (0,) * len(shape), pipeline_mode=pl.Buffered(1))


def _mod_kernel(c_ref, w_ref, b_ref, o_ref):
    c = c_ref[...]
    o_ref[...] = _dot(c * jax.nn.sigmoid(c), w_ref[...], HIGHEST) + b_ref[...]


def _modulation(cvec, ada_w, ada_b):
    n, d = cvec.shape
    nout = ada_w.shape[1]
    return pl.pallas_call(
        _mod_kernel,
        grid=(nout // d,),
        in_specs=[pl.BlockSpec((n, d), lambda j: (0, 0)),
                  pl.BlockSpec((d, d), lambda j: (0, j)),
                  pl.BlockSpec((1, d), lambda j: (0, j))],
        out_specs=pl.BlockSpec((n, d), lambda j: (0, j)),
        out_shape=jax.ShapeDtypeStruct((n, nout), F32),
        name="adaln_mod",
    )(cvec, ada_w, ada_b.reshape(1, nout))


def _proj_kernel(xc_ref, xl_ref, mod_ref, g1_ref, w_ref, b_ref, cw_ref, cb_ref, wco_ref,
                 ycg_ref, sgb_ref, q_ref, k_ref, v_ref, so_ref, gt_ref, *, n_ctx_tiles, ctx_row):
    i = pl.program_id(0)
    x = jnp.where(i < n_ctx_tiles, xc_ref[...], xl_ref[...])
    h = (_rms(x) * g1_ref[...] * (1.0 + mod_ref[0, 1:2, :]) + mod_ref[0, 0:1, :]).astype(BF16)

    def proj(cols):
        return _dot(h, w_ref[:, cols[0]:cols[1]]) + b_ref[:, cols[0]:cols[1]]

    zc = proj(C_CONV)
    d = x.shape[1]
    u = zc[:, 2 * d:3 * d] * zc[:, 0:d]
    rowlen = jnp.where(i < n_ctx_tiles, ctx_row, GRID_W)
    pos = lax.broadcasted_iota(jnp.int32, (TM, 1), 0) & (rowlen - 1)
    u_prev = jnp.where(pos == 0, 0.0, pltpu.roll(u, 1, 0))
    u_next = jnp.where(pos == rowlen - 1, 0.0, pltpu.roll(u, TM - 1, 0))
    uc = u_prev * cw_ref[0:1, :] + u * cw_ref[1:2, :] + u_next * cw_ref[2:3, :] + cb_ref[...]
    yconv = _dot((zc[:, d:2 * d] * uc).astype(BF16), wco_ref[...])

    ycg_ref[...] = (jax.nn.sigmoid(proj(C_GA)) * yconv).astype(BF16)
    sgb_ref[...] = jax.nn.sigmoid(proj(C_GB)).astype(BF16)
    q_ref[...] = (proj(C_Q) * (D_QK ** -0.5)).astype(BF16)
    k_ref[...] = proj(C_K).astype(BF16)
    v_ref[...] = proj(C_V).astype(BF16)
    so_ref[...] = jax.nn.sigmoid(proj(C_O)).astype(BF16)
    gt_ref[...] = proj(C_G)


def _tri_masks():
    row = lax.broadcasted_iota(jnp.int32, (TM, TM), 0)
    col = lax.broadcasted_iota(jnp.int32, (TM, TM), 1)
    return row >= col, row <= col


def _gate_cumsums(gi, gf):
    lower, upper = _tri_masks()
    lf = jax.nn.log_sigmoid(gf)
    fwd_lane = lax.broadcasted_iota(jnp.int32, (1, LANES), 1) < N_HEADS
    bsum = jnp.where(fwd_lane, _dot(lower.astype(F32), lf, HIGHEST), _dot(upper.astype(F32), lf, HIGHEST))
    return lf, bsum, fwd_lane


def _state_kernel(*refs, zero_init, emit_before, emit_after):
    it = iter(refs)
    kf_ref, vf_ref, gf_ref, kb_ref, vb_ref, gb_ref = (next(it) for _ in range(6))
    if not zero_init:
        c0_ref, n0_ref, m0_ref = (next(it) for _ in range(3))
    if emit_before:
        cbf_ref, cbb_ref, nbf_ref, nbb_ref, mbf_ref, mbb_ref = (next(it) for _ in range(6))
    if emit_after:
        ca_ref, na_ref, ma_ref = (next(it) for _ in range(3))
    c_scr, n_scr, m_scr = (next(it) for _ in range(3))
    c = pl.program_id(1)

    @pl.when(c == 0)
    def _():
        if zero_init:
            c_scr[...] = jnp.zeros_like(c_scr)
            n_scr[...] = jnp.zeros_like(n_scr)
            m_scr[...] = jnp.zeros_like(m_scr)
        else:
            c_scr[...] = c0_ref[0]
            n_scr[...] = n0_ref[0]
            m_scr[...] = m0_ref[0]

    if emit_before:
        cbf_ref[0] = c_scr[0:N_HEADS]
        cbb_ref[0] = c_scr[N_HEADS:N_HD]
        nbf_ref[0] = n_scr[0:N_HEADS]
        nbb_ref[0] = n_scr[N_HEADS:N_HD]
        mbf_ref[0] = m_scr[...]
        mbb_ref[0] = m_scr[...]

    fwd_lane = lax.broadcasted_iota(jnp.int32, (1, LANES), 1) < N_HEADS
    gi = jnp.where(fwd_lane, gf_ref[:, 0:LANES], gb_ref[:, 0:LANES])
    gfg = jnp.where(fwd_lane, gf_ref[:, LANES:2 * LANES], gb_ref[:, LANES:2 * LANES])
    lf, bsum, _ = _gate_cumsums(gi, gfg)
    total = jnp.sum(lf, axis=0, keepdims=True)
    g = total - bsum + gi
    m_prev = m_scr[...]
    m_new = jnp.maximum(total + m_prev, jnp.max(g, axis=0, keepdims=True))
    wk = jnp.exp(g - m_new)
    decay = jnp.exp(total + m_prev - m_new)
    for hd in range(N_HD):
        h = hd % N_HEADS
        k_ref, v_ref = (kf_ref, vf_ref) if hd < N_HEADS else (kb_ref, vb_ref)
        k_tile = k_ref[:, (h // 2) * LANES:(h // 2 + 1) * LANES].astype(F32)
        wkk = wk[:, hd:hd + 1] * jnp.where(_head_lanes(h), k_tile, 0.0)
        vh = v_ref[:, h * D_V:(h + 1) * D_V]
        dec = decay[:, hd:hd + 1]
        tn = (((0,), (0,)), ((), ()))
        if emit_before:
            c_scr[hd] = dec * c_scr[hd] + lax.dot_general(vh, wkk.astype(BF16), tn, preferred_element_type=F32)
        else:
            upd = lax.dot_general(wkk.astype(BF16), vh, tn, preferred_element_type=F32)
            c_scr[hd] = dec * c_scr[hd] + upd[(h % 2) * D_QK:(h % 2 + 1) * D_QK, :]
        n_scr[hd:hd + 1, :] = dec * n_scr[hd:hd + 1, :] + jnp.sum(wkk, axis=0, keepdims=True)
    m_scr[...] = m_new

    if emit_after:
        @pl.when(c == pl.num_programs(1) - 1)
        def _():
            ca_ref[0] = c_scr[...]
            na_ref[0] = n_scr[...]
            ma_ref[0] = m_scr[...]


def _state_scan(k, v, gt, tile0, n_seq, n_chunk, init, emit_before, emit_after):
    d = k.shape[1]
    cshape = (D_V, LANES) if emit_before else (D_QK, D_V)
    fwd = lambda s, c: (tile0 + s * n_chunk + c, 0)
    bwd = lambda s, c: (tile0 + s * n_chunk + n_chunk - 1 - c, 0)
    dv = v.shape[1]
    in_specs = [pl.BlockSpec((TM, d), fwd), pl.BlockSpec((TM, dv), fwd), pl.BlockSpec((TM, 2 * LANES), fwd),
                pl.BlockSpec((TM, d), bwd), pl.BlockSpec((TM, dv), bwd), pl.BlockSpec((TM, 2 * LANES), bwd)]
    args = [k, v, gt, k, v, gt]
    if init is not None:
        in_specs += [pl.BlockSpec((1, N_HD) + cshape, lambda s, c: (s, 0, 0, 0)),
                     pl.BlockSpec((1, N_HD, LANES), lambda s, c: (s, 0, 0)),
                     pl.BlockSpec((1, 1, LANES), lambda s, c: (s, 0, 0))]
        args += list(init)
    out_specs, out_shape = [], []
    n_tot = n_seq * n_chunk
    if emit_before:
        cf = lambda s, c: (s * n_chunk + c, 0, 0, 0)
        cb = lambda s, c: (s * n_chunk + n_chunk - 1 - c, 0, 0, 0)
        nf = lambda s, c: (s * n_chunk + c, 0, 0)
        nb = lambda s, c: (s * n_chunk + n_chunk - 1 - c, 0, 0)
        out_specs += [pl.BlockSpec((1, N_HEADS) + cshape, cf), pl.BlockSpec((1, N_HEADS) + cshape, cb),
                      pl.BlockSpec((1, N_HEADS, LANES), nf), pl.BlockSpec((1, N_HEADS, LANES), nb),
                      pl.BlockSpec((1, 1, LANES), nf), pl.BlockSpec((1, 1, LANES), nb)]
        out_shape += [jax.ShapeDtypeStruct((n_tot, N_HEADS) + cshape, F32)] * 2
        out_shape += [jax.ShapeDtypeStruct((n_tot, N_HEADS, LANES), F32)] * 2
        out_shape += [jax.ShapeDtypeStruct((n_tot, 1, LANES), F32)] * 2
    if emit_after:
        out_specs += [pl.BlockSpec((1, N_HD, D_QK, D_V), lambda s, c: (s, 0, 0, 0)),
                      pl.BlockSpec((1, N_HD, LANES), lambda s, c: (s, 0, 0)),
                      pl.BlockSpec((1, 1, LANES), lambda s, c: (s, 0, 0))]
        out_shape += [jax.ShapeDtypeStruct((n_seq, N_HD, D_QK, D_V), F32),
                      jax.ShapeDtypeStruct((n_seq, N_HD, LANES), F32),
                      jax.ShapeDtypeStruct((n_seq, 1, LANES), F32)]
    return pl.pallas_call(
        functools.partial(_state_kernel, zero_init=init is None, emit_before=emit_before, emit_after=emit_after),
        grid=(n_seq, n_chunk),
        in_specs=in_specs,
        out_specs=out_specs,
        out_shape=out_shape,
        scratch_shapes=[pltpu.VMEM((N_HD,) + cshape, F32), pltpu.VMEM((N_HD, LANES), F32),
                        pltpu.VMEM((1, LANES), F32)],
        compiler_params=pltpu.CompilerParams(dimension_semantics=("arbitrary", "arbitrary"),
                                             vmem_limit_bytes=VMEM_LIMIT),
        name="mlstm_state_scan",
    )(*args)


def _mix_kernel(q_ref, k_ref, v_ref, so_ref, sgb_ref, ycg_ref, gt_ref, xc_ref, xl_ref, mod_ref,
                cbf_ref, cbb_ref, nbf_ref, nbb_ref, mbf_ref, mbb_ref,
                gmh_ref, wmo_ref, wo_ref, g2_ref, rw_ref, rb_ref,
                x1_ref, h2_ref, route_ref, routet_ref, cnt_ref, hm_scr, carry_scr, *, n_ctx_tiles, n_experts):
    i = pl.program_id(0)
    is_lat = i >= n_ctx_tiles
    lat_f = is_lat.astype(F32)
    lower, upper = _tri_masks()
    gt_t = gt_ref[...].T
    gi_t = gt_t[0:N_HD, :]
    lf_t = jax.nn.log_sigmoid(gt_t[LANES:LANES + N_HD, :])
    fwd_row = lax.broadcasted_iota(jnp.int32, (N_HD, 1), 0) < N_HEADS
    bsum_t = jnp.where(fwd_row, _dot(lf_t, upper.astype(F32), HIGHEST), _dot(lf_t, lower.astype(F32), HIGHEST))
    a_t = gi_t - bsum_t
    m_row = jnp.where(lax.broadcasted_iota(jnp.int32, (1, LANES), 1) < N_HEADS, mbf_ref[0], mbb_ref[0]) * lat_f
    m_sq = jnp.where(lax.broadcasted_iota(jnp.int32, (LANES, 1), 0) == 0, m_row, 0.0)
    m_prev = m_sq.T[0:N_HD, 0:1]
    a_c = jnp.concatenate([a_t, jnp.zeros((LANES - N_HD, TM), F32)], axis=0).T
    row0 = lax.broadcasted_iota(jnp.int32, (LANES, 1), 0) == 0
    ones_rows = jnp.where(row0, 1.0, 0.0).astype(BF16) * jnp.ones((1, TM), BF16)

    for h in range(N_HEADS):
        hs = slice(h * D_V, (h + 1) * D_V)
        pair = slice((h // 2) * LANES, (h // 2 + 1) * LANES)
        qh = jnp.where(_head_lanes(h), q_ref[:, pair], jnp.zeros((), BF16))
        kh = k_ref[:, pair]
        vext_t = jnp.concatenate([v_ref[:, hs].T, ones_rows], axis=0)
        kq = lax.dot_general(kh, qh, (((1,), (1,)), ((), ())), preferred_element_type=F32)
        hsum = None
        for d in range(2):
            hd = d * N_HEADS + h
            mask = upper if d == 0 else lower
            c_t = (cbf_ref if d == 0 else cbb_ref)[0, h] * lat_f
            n_r = (nbf_ref if d == 0 else nbb_ref)[0, h:h + 1, :] * lat_f
            a_b = jnp.where(mask, a_c[:, hd:hd + 1], -jnp.inf)
            mrow = jnp.maximum(m_prev[hd:hd + 1, :], jnp.max(a_b, axis=0, keepdims=True))
            e = jnp.exp(a_b - mrow)
            nd = _dot(vext_t, (kq * e).astype(BF16))
            cext_t = jnp.concatenate([c_t, jnp.where(row0, n_r, 0.0)], axis=0).astype(BF16)
            qc = lax.dot_general(cext_t, qh, (((1,), (1,)), ((), ())), preferred_element_type=F32)
            wi = jnp.exp(m_prev[hd:hd + 1, :] - mrow)
            num = nd[0:D_V, :] + wi * qc[0:D_V, :]
            den = nd[D_V:D_V + 1, :] + wi * qc[D_V:D_V + 1, :]
            r = 1.0 / jnp.maximum(jnp.abs(den), jnp.exp(-(bsum_t[hd:hd + 1, :] + mrow)))
            hsum = num * r if hsum is None else hsum + num * r
        hn = hsum * lax.rsqrt(jnp.mean(hsum * hsum, axis=0, keepdims=True) + EPS)
        hm_scr[:, hs] = (hn.T * gmh_ref[:, hs] * so_ref[:, hs].astype(F32)).astype(BF16)

    ym = _dot(hm_scr[...], wmo_ref[...])
    mix = (ycg_ref[...].astype(F32) + sgb_ref[...].astype(F32) * ym).astype(BF16)
    x1 = jnp.where(is_lat, xl_ref[...], xc_ref[...]) + mod_ref[0, 2:3, :] * _dot(mix, wo_ref[...])
    x1_ref[...] = x1
    h2 = _rms(x1) * g2_ref[...] * (1.0 + mod_ref[0, 4:5, :]) + mod_ref[0, 3:4, :]
    _store_rows(h2_ref, h2)

    lane = lax.broadcasted_iota(jnp.int32, (TM, LANES), 1)
    rw = rw_ref[...]
    h2_hi, rw_hi = h2.astype(BF16), rw.astype(BF16)
    h2_lo, rw_lo = (h2 - h2_hi.astype(F32)).astype(BF16), (rw - rw_hi.astype(F32)).astype(BF16)
    logits = _dot(h2_hi, rw_hi) + (_dot(h2_hi, rw_lo) + _dot(h2_lo, rw_hi)) + rb_ref[...]
    work = jnp.where(lane < n_experts, logits, -jnp.inf)
    sels, exps, idxs = [], [], []
    top = None
    for _ in range(TOP_K):
        mx = jnp.max(work, axis=-1, keepdims=True)
        ix = jnp.min(jnp.where(work == mx, lane, LANES), axis=-1, keepdims=True)
        sel = lane == ix
        work = jnp.where(sel, -jnp.inf, work)
        top = mx if top is None else top
        sels.append(sel)
        idxs.append(ix.astype(F32))
        exps.append(jnp.exp(mx - top))
    inv = 1.0 / functools.reduce(lambda p, q: p + q, exps)

    @pl.when(i == 0)
    def _():
        carry_scr[...] = jnp.zeros_like(carry_scr)

    onehot = functools.reduce(lambda p, q: p + q, [jnp.where(s, 1.0, 0.0) for s in sels])
    row = lax.broadcasted_iota(jnp.int32, (TM, TM), 0)
    col = lax.broadcasted_iota(jnp.int32, (TM, TM), 1)
    before = _dot((row > col).astype(BF16), onehot.astype(BF16)) + carry_scr[...]
    carry_scr[...] += jnp.sum(onehot, axis=0, keepdims=True)
    cnt_ref[...] = carry_scr[...]
    route = jnp.zeros((TM, LANES), F32)
    for j in range(TOP_K):
        slot = jnp.sum(jnp.where(sels[j], before, 0.0), axis=-1, keepdims=True)
        route = jnp.where(lane == j, idxs[j], route)
        route = jnp.where(lane == TOP_K + j, slot, route)
        route = jnp.where(lane == 2 * TOP_K + j, exps[j] * inv, route)
    route_ref[...] = route
    routet_ref[...] = route.T[0:8, :].astype(jnp.int32)


def _dispatch_kernel(pos_ref, h2_ref, xs_ref, sem):
    def copy(t, j):
        return pltpu.make_async_copy(_row_tile(h2_ref, t), _row_tile(xs_ref, pos_ref[TOP_K * t + j]), sem)

    def start(t, carry):
        for j in range(TOP_K):
            copy(t, j).start(priority=j % 2)
        return carry

    lax.fori_loop(0, TM, start, 0, unroll=ISSUE_UNROLL)
    for j in range(TOP_K):
        pltpu.make_async_copy(h2_ref, xs_ref.at[pl.ds(0, TM * ROW_SUB)], sem).wait()


def _expert_kernel(tile_ref, exp_ref, flag_ref, lo_ref, hi_ref,
                   xs_ref, wg_ref, bg_ref, wl_ref, bl_ref, wd_ref, bd_ref, ys_ref, w_scr):
    w = pl.program_id(0)
    flags = flag_ref[w]

    @pl.when((flags & 4) != 0)
    def _():
        w_scr[0] = wg_ref[0].astype(BF16)
        w_scr[1] = wl_ref[0].astype(BF16)
        w_scr[2] = wd_ref[0].astype(BF16)

    @pl.when((flags & 1) != 0)
    def _():
        x = _load_rows(xs_ref).astype(BF16)
        gt = jnp.minimum(_dot(x, w_scr[0]) + bg_ref[0], SWIGLU_LIMIT)
        lin = jnp.clip(_dot(x, w_scr[1]) + bl_ref[0], -SWIGLU_LIMIT, SWIGLU_LIMIT)
        act = gt * jax.nn.sigmoid(SWIGLU_ALPHA * gt) * (lin + 1.0)
        y = _dot(act.astype(BF16), w_scr[2]) + bd_ref[0]
        rows = lax.broadcasted_iota(jnp.int32, (y.shape[0], 1), 0)
        mine = (rows >= lo_ref[w]) & (rows < hi_ref[w])

        @pl.when((flags & 2) != 0)
        def _():
            _store_rows(ys_ref, jnp.where(mine, y, 0.0))

        @pl.when((flags & 2) == 0)
        def _():
            _store_rows(ys_ref, jnp.where(mine, y, _load_rows(ys_ref)))


def _combine_kernel(pos_ref, posn_ref, route_ref, x1_ref, mod_ref, fg_ref, ys_ref, outc_ref, outl_ref,
                    buf, sem, *, n_ctx_tiles):
    i = pl.program_id(0)
    n = pl.num_programs(0)

    def copy(p_ref, slot, t, j):
        return pltpu.make_async_copy(_row_tile(ys_ref, p_ref[TOP_K * t + j]), _row_tile(buf.at[slot, j], t), sem.at[slot])

    def start_all(p_ref, slot):
        def body(t, carry):
            for j in range(TOP_K):
                copy(p_ref, slot, t, j).start(priority=j % 2)
            return carry
        lax.fori_loop(0, TM, body, 0, unroll=ISSUE_UNROLL)

    @pl.when(i == 0)
    def _():
        start_all(pos_ref, 0)

    @pl.when(i + 1 < n)
    def _():
        start_all(posn_ref, (i + 1) % 2)

    slot = i % 2

    for j in range(TOP_K):
        pltpu.make_async_copy(ys_ref.at[pl.ds(0, TM * ROW_SUB)], buf.at[slot, j], sem.at[slot]).wait()
    acc = None
    for j in range(TOP_K):
        term = route_ref[:, 2 * TOP_K + j:2 * TOP_K + j + 1] * _load_rows(buf.at[slot, j])
        acc = term if acc is None else acc + term
    out = _rms(x1_ref[...] + mod_ref[0, 5:6, :] * acc) * fg_ref[...]

    @pl.when(i < n_ctx_tiles)
    def _():
        outc_ref[...] = out

    @pl.when(i >= n_ctx_tiles)
    def _():
        outl_ref[...] = out


def _work_items(counts, n_experts, n_rows, tmx):
    n_items_max = n_rows // tmx + n_experts - 1
    cnt = counts.astype(jnp.int32)
    offs = jnp.concatenate([jnp.zeros((1,), jnp.int32), jnp.cumsum(cnt)])
    first_tile = offs[:-1] // tmx
    n_it = jnp.where(cnt > 0, (offs[1:] - 1) // tmx - first_tile + 1, 0)
    it_start = jnp.concatenate([jnp.zeros((1,), jnp.int32), jnp.cumsum(n_it)])
    total = it_start[-1]
    w = jnp.arange(n_items_max, dtype=jnp.int32)
    wc = jnp.minimum(w, total - 1)
    e = jnp.sum((it_start[None, 1:] <= wc[:, None]).astype(jnp.int32), axis=1)
    e = jnp.minimum(e, n_experts - 1)
    is_e = e[:, None] == jnp.arange(n_experts, dtype=jnp.int32)
    at_e = lambda a: jnp.sum(jnp.where(is_e, a[None, :], 0), axis=1)
    tile = at_e(first_tile) + wc - at_e(it_start[:-1])
    valid = w < total
    prev = lambda a: jnp.concatenate([jnp.full((1,), -1, jnp.int32), a[:-1]])
    flags = (valid.astype(jnp.int32) + 2 * (valid & (tile != prev(tile))).astype(jnp.int32)
             + 4 * (valid & (e != prev(e))).astype(jnp.int32))
    lo = jnp.clip(at_e(offs[:-1]) - tile * tmx, 0, tmx)
    hi = jnp.clip(at_e(offs[1:]) - tile * tmx, 0, tmx)
    return offs, tile, e, flags, lo, hi


def _pack_in_proj(w, b):
    d = w.shape[0]
    o_g = 5 * d + 2 * D_QKH
    o_ga, o_gb = o_g + 4 * N_HEADS, o_g + 4 * N_HEADS + d

    def gates(m):
        gz = jnp.zeros((m.shape[0], LANES - N_HD), m.dtype)
        i_f, f_f, i_b, f_b = (m[:, o_g + j * N_HEADS:o_g + (j + 1) * N_HEADS] for j in range(4))
        return jnp.concatenate([i_f, i_b, gz, f_f, f_b, gz], axis=1)

    def pack(m):
        return jnp.concatenate([m[:, 0:o_g], m[:, o_ga:o_gb], m[:, o_gb:o_gb + d], gates(m)], axis=1)

    return pack(w.astype(BF16)), pack(b.reshape(1, -1))


def kernel(x_prompt, x_sample, c, state_C, state_n, state_m, c_ctx, ada_w, ada_b, norm1_g, norm2_g, w_in, b_in,
           conv_w, conv_b, w_conv_out, mh_norm_g, w_m_out, w_o, router_w, router_b, w_gate, b_gate, w_lin, b_lin,
           w_down, b_down, final_g):
    nb, seq, d = x_prompt.shape
    nd, dseq, _ = x_sample.shape
    n_experts = w_gate.shape[1]
    assert d == _D and w_in.shape[0] == 1 and seq == TM and dseq % TM == 0 and TM % GRID_W == 0
    t_ctx, t_lat = nb * seq, nd * dseq
    n_tok = t_ctx + t_lat
    n_ctx_tiles, n_tiles = t_ctx // TM, n_tok // TM
    lat_chunks = dseq // TM
    n_lat_tiles = n_tiles - n_ctx_tiles

    n_c = 1 + nd
    n_cp = -(-n_c // 8) * 8
    cvec = jnp.concatenate([c_ctx[None, :], c, jnp.zeros((n_cp - n_c, d), F32)], axis=0)
    mod = _modulation(cvec, ada_w[0], ada_b[0]).reshape(n_cp, N_MOD, d)
    mod = jnp.pad(mod, ((0, 0), (0, 8 - N_MOD), (0, 0)))

    def mod_row(tile_tokens):
        ctx_t, per_seq = t_ctx // tile_tokens, dseq // tile_tokens
        return lambda i, *_: (jnp.where(i < ctx_t, 0, 1 + (i - ctx_t) // per_seq), 0, 0)

    x_ctx, x_lat = x_prompt.reshape(t_ctx, d), x_sample.reshape(t_lat, d)
    ctx_spec = pl.BlockSpec((TM, d), lambda i: (jnp.minimum(i, n_ctx_tiles - 1), 0))
    lat_spec = pl.BlockSpec((TM, d), lambda i: (jnp.maximum(i - n_ctx_tiles, 0), 0))
    w_all, b_all = _pack_in_proj(w_in[0], b_in[0])
    tile = lambda cols: pl.BlockSpec((TM, cols), lambda i: (i, 0))
    rows = lambda n: pl.BlockSpec((n * ROW_SUB, LANES), lambda i: (i, 0))
    params = pltpu.CompilerParams(dimension_semantics=("arbitrary",), vmem_limit_bytes=VMEM_LIMIT)
    bf = lambda cols: jax.ShapeDtypeStruct((n_tok, cols), BF16)

    ycg, sgb, q, k, v, so, gt = pl.pallas_call(
        functools.partial(_proj_kernel, n_ctx_tiles=n_ctx_tiles, ctx_row=seq),
        grid=(n_tiles,),
        in_specs=[ctx_spec, lat_spec, pl.BlockSpec((1, 8, d), mod_row(TM)), _const_spec((1, d)),
                  _const_spec((d, W_COLS)), _const_spec((1, W_COLS)), _const_spec((3, d)), _const_spec((1, d)),
                  _const_spec((d, d))],
        out_specs=[tile(d), tile(d), tile(D_QKH), tile(D_QKH), tile(d), tile(d), tile(2 * LANES)],
        out_shape=[bf(d), bf(d), bf(D_QKH), bf(D_QKH), bf(d), bf(d),
                   jax.ShapeDtypeStruct((n_tok, 2 * LANES), F32)],
        compiler_params=params,
        name="in_proj_conv",
    )(x_ctx, x_lat, mod, norm1_g, w_all, b_all, conv_w[0], conv_b, w_conv_out[0].astype(BF16))

    c_new, n_new, m_new = _state_scan(k, v, gt, 0, nb, 1, None, False, True)
    def on_head_lanes(a):
        even = lax.broadcasted_iota(jnp.int32, (N_HD,) + (1,) * (a.ndim - 2), 0) % 2 == 0
        lo = jnp.pad(a, [(0, 0)] * (a.ndim - 1) + [(0, LANES - D_QK)])
        hi = jnp.pad(a, [(0, 0)] * (a.ndim - 1) + [(LANES - D_QK, 0)])
        return jnp.where(even, lo, hi)

    init = (on_head_lanes(jnp.swapaxes(state_C[:, 0].astype(F32).reshape(nd, N_HD, D_QK, D_V), 2, 3)),
            on_head_lanes(state_n[:, 0].astype(F32).reshape(nd, N_HD, D_QK)),
            jnp.pad(state_m[:, 0].astype(F32).reshape(nd, 1, N_HD), ((0, 0), (0, 0), (0, LANES - N_HD))))
    cbf, cbb, nbf, nbb, mbf, mbb = _state_scan(k, v, gt, n_ctx_tiles, nd, lat_chunks, init, True, False)

    lat_idx = lambda i: jnp.maximum(i - n_ctx_tiles, 0)
    st4 = pl.BlockSpec((1, N_HEADS, D_V, LANES), lambda i: (lat_idx(i), 0, 0, 0))
    st3 = pl.BlockSpec((1, N_HEADS, LANES), lambda i: (lat_idx(i), 0, 0))
    st1 = pl.BlockSpec((1, 1, LANES), lambda i: (lat_idx(i), 0, 0))
    rw = jnp.pad(router_w[0], ((0, 0), (0, LANES - n_experts)))
    rb = jnp.pad(router_b[0], (0, LANES - n_experts)).reshape(1, LANES)
    x1, h2, route, route_t, counts = pl.pallas_call(
        functools.partial(_mix_kernel, n_ctx_tiles=n_ctx_tiles, n_experts=n_experts),
        grid=(n_tiles,),
        in_specs=[tile(D_QKH), tile(D_QKH), tile(d), tile(d), tile(d), tile(d), tile(2 * LANES), ctx_spec, lat_spec,
                  pl.BlockSpec((1, 8, d), mod_row(TM)), st4, st4, st3, st3, st1, st1,
                  _const_spec((1, d)), _const_spec((d, d)), _const_spec((d, d)), _const_spec((1, d)),
                  _const_spec((d, LANES)), _const_spec((1, LANES))],
        out_specs=[tile(d), rows(TM), tile(LANES), pl.BlockSpec((8, TM), lambda i: (0, i)),
                   pl.BlockSpec((1, LANES), lambda i: (0, 0))],
        out_shape=[jax.ShapeDtypeStruct((n_tok, d), F32), jax.ShapeDtypeStruct((n_tok * ROW_SUB, LANES), F32),
                   jax.ShapeDtypeStruct((n_tok, LANES), F32), jax.ShapeDtypeStruct((8, n_tok), jnp.int32),
                   jax.ShapeDtypeStruct((1, LANES), F32)],
        scratch_shapes=[pltpu.VMEM((TM, d), BF16), pltpu.VMEM((1, LANES), F32)],
        compiler_params=params,
        name="mlstm_mix_router",
    )(q, k, v, so, sgb, ycg, gt, x_ctx, x_lat, mod, cbf, cbb, nbf, nbb, mbf, mbb,
      mh_norm_g, w_m_out[0].astype(BF16), w_o[0].astype(BF16), norm2_g, rw, rb)

    n_rows = TOP_K * n_tok
    offs, it_tile, it_exp, it_flags, it_lo, it_hi = _work_items(counts[0, :n_experts], n_experts, n_rows, TMX)
    is_exp = route_t[0:TOP_K, :, None] == jnp.arange(n_experts, dtype=jnp.int32)
    pos_t = jnp.sum(jnp.where(is_exp, offs[:n_experts], 0), axis=-1) + route_t[TOP_K:2 * TOP_K]
    pos = pos_t.T.reshape(n_rows)
    pos_spec = lambda f: pl.BlockSpec((TOP_K * TM,), f, memory_space=pltpu.SMEM)
    any_spec = pl.BlockSpec(memory_space=pl.ANY)

    xs = pl.pallas_call(
        _dispatch_kernel,
        grid=(n_tiles,),
        in_specs=[pos_spec(lambda i: (i,)), rows(TM)],
        out_specs=any_spec,
        out_shape=jax.ShapeDtypeStruct((n_rows * ROW_SUB, LANES), F32),
        scratch_shapes=[pltpu.SemaphoreType.DMA],
        compiler_params=params,
        name="moe_dispatch",
    )(pos, h2)

    dff = w_gate.shape[-1]
    wspec = lambda a, b_: pl.BlockSpec((1, a, b_), lambda w, tl, ex, *_: (ex[w], 0, 0))
    ys = pl.pallas_call(
        _expert_kernel,
        grid_spec=pltpu.PrefetchScalarGridSpec(
            num_scalar_prefetch=5,
            grid=(it_tile.shape[0],),
            in_specs=[pl.BlockSpec((TMX * ROW_SUB, LANES), lambda w, tl, *_: (tl[w], 0)),
                      wspec(d, dff), wspec(1, dff), wspec(d, dff), wspec(1, dff), wspec(dff, d), wspec(1, d)],
            out_specs=pl.BlockSpec((TMX * ROW_SUB, LANES), lambda w, tl, *_: (tl[w], 0)),
            scratch_shapes=[pltpu.VMEM((3, d, dff), BF16)]),
        out_shape=jax.ShapeDtypeStruct((n_rows * ROW_SUB, LANES), F32),
        compiler_params=params,
        name="moe_experts",
    )(it_tile, it_exp, it_flags, it_lo, it_hi, xs, w_gate[0], b_gate[0].reshape(n_experts, 1, dff), w_lin[0],
      b_lin[0].reshape(n_experts, 1, dff), w_down[0], b_down[0].reshape(n_experts, 1, d))

    ctx_i = lambda i: (jnp.minimum(i, n_ctx_tiles - 1), 0)
    lat_i = lambda i: (jnp.maximum(i - n_ctx_tiles, 0), 0)
    y_prompt, y_sample = pl.pallas_call(
        functools.partial(_combine_kernel, n_ctx_tiles=n_ctx_tiles),
        grid=(n_tiles,),
        in_specs=[pos_spec(lambda i: (i,)), pos_spec(lambda i: (jnp.minimum(i + 1, n_tiles - 1),)),
                  tile(LANES), tile(d), pl.BlockSpec((1, 8, d), mod_row(TM)), _const_spec((1, d)), any_spec],
        out_specs=[pl.BlockSpec((TM, d), ctx_i), pl.BlockSpec((TM, d), lat_i)],
        out_shape=[jax.ShapeDtypeStruct((t_ctx, d), F32), jax.ShapeDtypeStruct((t_lat, d), F32)],
        scratch_shapes=[pltpu.VMEM((2, TOP_K, TM * ROW_SUB, LANES), F32), pltpu.SemaphoreType.DMA((2,))],
        compiler_params=params,
        name="moe_combine",
    )(pos, pos, route, x1, mod, final_g.reshape(1, d), ys)

    y_prompt = y_prompt.reshape(nb, seq, d)
    y_sample = y_sample.reshape(nd, dseq, d)
    new_c = c_new.reshape(nb, 1, 2, N_HEADS, D_QK, D_V)
    odd_head = (lax.broadcasted_iota(jnp.int32, (1, N_HD, 1), 1) % 2) == 1
    new_n = jnp.where(odd_head, n_new[:, :, D_QK:], n_new[:, :, :D_QK]).reshape(nb, 1, 2, N_HEADS, D_QK)
    new_m = m_new[:, 0, :N_HD].reshape(nb, 1, 2, N_HEADS)
    return (y_prompt, y_sample, new_c, new_n, new_m)
```

```python
import functools

import jax
import jax.numpy as jnp
from jax import lax
from jax.experimental import pallas as pl
from jax.experimental.pallas import tpu as pltpu

F32 = jnp.float32
BF16 = jnp.bfloat16
HIGHEST = lax.Precision.HIGHEST

N_HEADS = 8
D_QK = 64
D_V = 128
GRID_W = 64
TOP_K = 4
SWIGLU_LIMIT = 7.0
SWIGLU_ALPHA = 1.702
EPS = 1e-6
N_MOD = 6

LANES = 128
TM = 256
TMX = 512
ISSUE_UNROLL = 4
N_HD = 2 * N_HEADS
VMEM_LIMIT = 56 * 1024 * 1024

_D = 1024
C_CONV = (0, 3 * _D)
D_QKH = N_HEADS * D_QK
C_Q = (3 * _D, 3 * _D + D_QKH)
C_K = (C_Q[1], C_Q[1] + D_QKH)
C_V = (C_K[1], C_K[1] + _D)
C_O = (C_V[1], C_V[1] + _D)
C_GA = (C_O[1], C_O[1] + _D)
C_GB = (C_GA[1], C_GA[1] + _D)
C_G = (C_GB[1], C_GB[1] + 2 * LANES)
W_COLS = C_G[1]


def _head_lanes(h):
    lane = lax.broadcasted_iota(jnp.int32, (1, LANES), 1)
    return (lane < D_QK) if h % 2 == 0 else (lane >= D_QK)


def _dot(a, b, precision=None):
    return jnp.dot(a, b, preferred_element_type=F32, precision=precision)


def _rms(x):
    return x * lax.rsqrt(jnp.mean(x * x, axis=-1, keepdims=True) + EPS)


ROW_SUB = 8


def _store_rows(ref, val):
    n = val.shape[0]
    for s in range(ROW_SUB):
        ref[pl.ds(s, n, stride=ROW_SUB), :] = val[:, s * LANES:(s + 1) * LANES]


def _load_rows(ref):
    n = ref.shape[0] // ROW_SUB
    return jnp.concatenate([ref[pl.ds(s, n, stride=ROW_SUB), :] for s in range(ROW_SUB)], axis=1)


def _row_tile(ref, r):
    return ref.at[pl.ds(pl.multiple_of(r * ROW_SUB, ROW_SUB), ROW_SUB)]


def _const_spec(shape):
    return pl.BlockSpec(shape, lambda *_: (0,) * len(shape), pipeline_mode=pl.Buffered(1))


def _mod_kernel(c_ref, w_ref, b_ref, o_ref):
    c = c_ref[...]
    o_ref[...] = _dot(c * jax.nn.sigmoid(c), w_ref[...], HIGHEST) + b_ref[...]


def _modulation(cvec, ada_w, ada_b):
    n, d = cvec.shape
    nout = ada_w.shape[1]
    return pl.pallas_call(
        _mod_kernel,
        grid=(nout // d,),
        in_specs=[pl.BlockSpec((n, d), lambda j: (0, 0)),
                  pl.BlockSpec((d, d), lambda j: (0, j)),
                  pl.BlockSpec((1, d), lambda j: (0, j))],
        out_specs=pl.BlockSpec((n, d), lambda j: (0, j)),
        out_shape=jax.ShapeDtypeStruct((n, nout), F32),
        name="adaln_mod",
    )(cvec, ada_w, ada_b.reshape(1, nout))


def _proj_kernel(xc_ref, xl_ref, mod_ref, g1_ref, w_ref, b_ref, cw_ref, cb_ref, wco_ref,
                 ycg_ref, sgb_ref, q_ref, k_ref, v_ref, so_ref, gt_ref, *, n_ctx_tiles, ctx_row):
    i = pl.program_id(0)
    x = jnp.where(i < n_ctx_tiles, xc_ref[...], xl_ref[...])
    h = (_rms(x) * g1_ref[...] * (1.0 + mod_ref[0, 1:2, :]) + mod_ref[0, 0:1, :]).astype(BF16)

    def proj(cols):
        return _dot(h, w_ref[:, cols[0]:cols[1]]) + b_ref[:, cols[0]:cols[1]]

    zc = proj(C_CONV)
    d = x.shape[1]
    u = zc[:, 2 * d:3 * d] * zc[:, 0:d]
    rowlen = jnp.where(i < n_ctx_tiles, ctx_row, GRID_W)
    pos = lax.broadcasted_iota(jnp.int32, (TM, 1), 0) & (rowlen - 1)
    u_prev = jnp.where(pos == 0, 0.0, pltpu.roll(u, 1, 0))
    u_next = jnp.where(pos == rowlen - 1, 0.0, pltpu.roll(u, TM - 1, 0))
    uc = u_prev * cw_ref[0:1, :] + u * cw_ref[1:2, :] + u_next * cw_ref[2:3, :] + cb_ref[...]
    yconv = _dot((zc[:, d:2 * d] * uc).astype(BF16), wco_ref[...])

    ycg_ref[...] = (jax.nn.sigmoid(proj(C_GA)) * yconv).astype(BF16)
    sgb_ref[...] = jax.nn.sigmoid(proj(C_GB)).astype(BF16)
    q_ref[...] = (proj(C_Q) * (D_QK ** -0.5)).astype(BF16)
    k_ref[...] = proj(C_K).astype(BF16)
    v_ref[...] = proj(C_V).astype(BF16)
    so_ref[...] = jax.nn.sigmoid(proj(C_O)).astype(BF16)
    gt_ref[...] = proj(C_G)


def _tri_masks():
    row = lax.broadcasted_iota(jnp.int32, (TM, TM), 0)
    col = lax.broadcasted_iota(jnp.int32, (TM, TM), 1)
    return row >= col, row <= col


def _gate_cumsums(gi, gf):
    lower, upper = _tri_masks()
    lf = jax.nn.log_sigmoid(gf)
    fwd_lane = lax.broadcasted_iota(jnp.int32, (1, LANES), 1) < N_HEADS
    bsum = jnp.where(fwd_lane, _dot(lower.astype(F32), lf, HIGHEST), _dot(upper.astype(F32), lf, HIGHEST))
    return lf, bsum, fwd_lane


def _state_kernel(*refs, zero_init, emit_before, emit_after):
    it = iter(refs)
    kf_ref, vf_ref, gf_ref, kb_ref, vb_ref, gb_ref = (next(it) for _ in range(6))
    if not zero_init:
        c0_ref, n0_ref, m0_ref = (next(it) for _ in range(3))
    if emit_before:
        cbf_ref, cbb_ref, nbf_ref, nbb_ref, mbf_ref, mbb_ref = (next(it) for _ in range(6))
    if emit_after:
        ca_ref, na_ref, ma_ref = (next(it) for _ in range(3))
    c_scr, n_scr, m_scr = (next(it) for _ in range(3))
    c = pl.program_id(1)

    @pl.when(c == 0)
    def _():
        if zero_init:
            c_scr[...] = jnp.zeros_like(c_scr)
            n_scr[...] = jnp.zeros_like(n_scr)
            m_scr[...] = jnp.zeros_like(m_scr)
        else:
            c_scr[...] = c0_ref[0]
            n_scr[...] = n0_ref[0]
            m_scr[...] = m0_ref[0]

    if emit_before:
        cbf_ref[0] = c_scr[0:N_HEADS]
        cbb_ref[0] = c_scr[N_HEADS:N_HD]
        nbf_ref[0] = n_scr[0:N_HEADS]
        nbb_ref[0] = n_scr[N_HEADS:N_HD]
        mbf_ref[0] = m_scr[...]
        mbb_ref[0] = m_scr[...]

    fwd_lane = lax.broadcasted_iota(jnp.int32, (1, LANES), 1) < N_HEADS
    gi = jnp.where(fwd_lane, gf_ref[:, 0:LANES], gb_ref[:, 0:LANES])
    gfg = jnp.where(fwd_lane, gf_ref[:, LANES:2 * LANES], gb_ref[:, LANES:2 * LANES])
    lf, bsum, _ = _gate_cumsums(gi, gfg)
    total = jnp.sum(lf, axis=0, keepdims=True)
    g = total - bsum + gi
    m_prev = m_scr[...]
    m_new = jnp.maximum(total + m_prev, jnp.max(g, axis=0, keepdims=True))
    wk = jnp.exp(g - m_new)
    decay = jnp.exp(total + m_prev - m_new)
    for hd in range(N_HD):
        h = hd % N_HEADS
        k_ref, v_ref = (kf_ref, vf_ref) if hd < N_HEADS else (kb_ref, vb_ref)
        k_tile = k_ref[:, (h // 2) * LANES:(h // 2 + 1) * LANES].astype(F32)
        wkk = wk[:, hd:hd + 1] * jnp.where(_head_lanes(h), k_tile, 0.0)
        vh = v_ref[:, h * D_V:(h + 1) * D_V]
        dec = decay[:, hd:hd + 1]
        tn = (((0,), (0,)), ((), ()))
        if emit_before:
            c_scr[hd] = dec * c_scr[hd] + lax.dot_general(vh, wkk.astype(BF16), tn, preferred_element_type=F32)
        else:
            upd = lax.dot_general(wkk.astype(BF16), vh, tn, preferred_element_type=F32)
            c_scr[hd] = dec * c_scr[hd] + upd[(h % 2) * D_QK:(h % 2 + 1) * D_QK, :]
        n_scr[hd:hd + 1, :] = dec * n_scr[hd:hd + 1, :] + jnp.sum(wkk, axis=0, keepdims=True)
    m_scr[...] = m_new

    if emit_after:
        @pl.when(c == pl.num_programs(1) - 1)
        def _():
            ca_ref[0] = c_scr[...]
            na_ref[0] = n_scr[...]
            ma_ref[0] = m_scr[...]


def _state_scan(k, v, gt, tile0, n_seq, n_chunk, init, emit_before, emit_after):
    d = k.shape[1]
    cshape = (D_V, LANES) if emit_before else (D_QK, D_V)
    fwd = lambda s, c: (tile0 + s * n_chunk + c, 0)
    bwd = lambda s, c: (tile0 + s * n_chunk + n_chunk - 1 - c, 0)
    dv = v.shape[1]
    in_specs = [pl.BlockSpec((TM, d), fwd), pl.BlockSpec((TM, dv), fwd), pl.BlockSpec((TM, 2 * LANES), fwd),
                pl.BlockSpec((TM, d), bwd), pl.BlockSpec((TM, dv), bwd), pl.BlockSpec((TM, 2 * LANES), bwd)]
    args = [k, v, gt, k, v, gt]
    if init is not None:
        in_specs += [pl.BlockSpec((1, N_HD) + cshape, lambda s, c: (s, 0, 0, 0)),
                     pl.BlockSpec((1, N_HD, LANES), lambda s, c: (s, 0, 0)),
                     pl.BlockSpec((1, 1, LANES), lambda s, c: (s, 0, 0))]
        args += list(init)
    out_specs, out_shape = [], []
    n_tot = n_seq * n_chunk
    if emit_before:
        cf = lambda s, c: (s * n_chunk + c, 0, 0, 0)
        cb = lambda s, c: (s * n_chunk + n_chunk - 1 - c, 0, 0, 0)
        nf = lambda s, c: (s * n_chunk + c, 0, 0)
        nb = lambda s, c: (s * n_chunk + n_chunk - 1 - c, 0, 0)
        out_specs += [pl.BlockSpec((1, N_HEADS) + cshape, cf), pl.BlockSpec((1, N_HEADS) + cshape, cb),
                      pl.BlockSpec((1, N_HEADS, LANES), nf), pl.BlockSpec((1, N_HEADS, LANES), nb),
                      pl.BlockSpec((1, 1, LANES), nf), pl.BlockSpec((1, 1, LANES), nb)]
        out_shape += [jax.ShapeDtypeStruct((n_tot, N_HEADS) + cshape, F32)] * 2
        out_shape += [jax.ShapeDtypeStruct((n_tot, N_HEADS, LANES), F32)] * 2
        out_shape += [jax.ShapeDtypeStruct((n_tot, 1, LANES), F32)] * 2
    if emit_after:
        out_specs += [pl.BlockSpec((1, N_HD, D_QK, D_V), lambda s, c: (s, 0, 0, 0)),
                      pl.BlockSpec((1, N_HD, LANES), lambda s, c: (s, 0, 0)),
                      pl.BlockSpec((1, 1, LANES), lambda s, c: (s, 0, 0))]
        out_shape += [jax.ShapeDtypeStruct((n_seq, N_HD, D_QK, D_V), F32),
                      jax.ShapeDtypeStruct((n_seq, N_HD, LANES), F32),
                      jax.ShapeDtypeStruct((n_seq, 1, LANES), F32)]
    return pl.pallas_call(
        functools.partial(_state_kernel, zero_init=init is None, emit_before=emit_before, emit_after=emit_after),
        grid=(n_seq, n_chunk),
        in_specs=in_specs,
        out_specs=out_specs,
        out_shape=out_shape,
        scratch_shapes=[pltpu.VMEM((N_HD,) + cshape, F32), pltpu.VMEM((N_HD, LANES), F32),
                        pltpu.VMEM((1, LANES), F32)],
        compiler_params=pltpu.CompilerParams(dimension_semantics=("arbitrary", "arbitrary"),
                                             vmem_limit_bytes=VMEM_LIMIT),
        name="mlstm_state_scan",
    )(*args)


def _mix_kernel(q_ref, k_ref, v_ref, so_ref, sgb_ref, ycg_ref, gt_ref, xc_ref, xl_ref, mod_ref,
                cbf_ref, cbb_ref, nbf_ref, nbb_ref, mbf_ref, mbb_ref,
                gmh_ref, wmo_ref, wo_ref, g2_ref, rw_ref, rb_ref,
                x1_ref, h2_ref, route_ref, routet_ref, cnt_ref, hm_scr, carry_scr, *, n_ctx_tiles, n_experts):
    i = pl.program_id(0)
    is_lat = i >= n_ctx_tiles
    lat_f = is_lat.astype(F32)
    lower, upper = _tri_masks()
    gt_t = gt_ref[...].T
    gi_t = gt_t[0:N_HD, :]
    lf_t = jax.nn.log_sigmoid(gt_t[LANES:LANES + N_HD, :])
    fwd_row = lax.broadcasted_iota(jnp.int32, (N_HD, 1), 0) < N_HEADS
    bsum_t = jnp.where(fwd_row, _dot(lf_t, upper.astype(F32), HIGHEST), _dot(lf_t, lower.astype(F32), HIGHEST))
    a_t = gi_t - bsum_t
    m_row = jnp.where(lax.broadcasted_iota(jnp.int32, (1, LANES), 1) < N_HEADS, mbf_ref[0], mbb_ref[0]) * lat_f
    m_sq = jnp.where(lax.broadcasted_iota(jnp.int32, (LANES, 1), 0) == 0, m_row, 0.0)
    m_prev = m_sq.T[0:N_HD, 0:1]
    a_c = jnp.concatenate([a_t, jnp.zeros((LANES - N_HD, TM), F32)], axis=0).T
    row0 = lax.broadcasted_iota(jnp.int32, (LANES, 1), 0) == 0
    ones_rows = jnp.where(row0, 1.0, 0.0).astype(BF16) * jnp.ones((1, TM), BF16)

    for h in range(N_HEADS):
        hs = slice(h * D_V, (h + 1) * D_V)
        pair = slice((h // 2) * LANES, (h // 2 + 1) * LANES)
        qh = jnp.where(_head_lanes(h), q_ref[:, pair], jnp.zeros((), BF16))
        kh = k_ref[:, pair]
        vext_t = jnp.concatenate([v_ref[:, hs].T, ones_rows], axis=0)
        kq = lax.dot_general(kh, qh, (((1,), (1,)), ((), ())), preferred_element_type=F32)
        hsum = None
        for d in range(2):
            hd = d * N_HEADS + h
            mask = upper if d == 0 else lower
            c_t = (cbf_ref if d == 0 else cbb_ref)[0, h] * lat_f
            n_r = (nbf_ref if d == 0 else nbb_ref)[0, h:h + 1, :] * lat_f
            a_b = jnp.where(mask, a_c[:, hd:hd + 1], -jnp.inf)
            mrow = jnp.maximum(m_prev[hd:hd + 1, :], jnp.max(a_b, axis=0, keepdims=True))
            e = jnp.exp(a_b - mrow)
            nd = _dot(vext_t, (kq * e).astype(BF16))
            cext_t = jnp.concatenate([c_t, jnp.where(row0, n_r, 0.0)], axis=0).astype(BF16)
            qc = lax.dot_general(cext_t, qh, (((1,), (1,)), ((), ())), preferred_element_type=F32)
            wi = jnp.exp(m_prev[hd:hd + 1, :] - mrow)
            num = nd[0:D_V, :] + wi * qc[0:D_V, :]
            den = nd[D_V:D_V + 1, :] + wi * qc[D_V:D_V + 1, :]
            r = 1.0 / jnp.maximum(jnp.abs(den), jnp.exp(-(bsum_t[hd:hd + 1, :] + mrow)))
            hsum = num * r if hsum is None else hsum + num * r
        hn = hsum * lax.rsqrt(jnp.mean(hsum * hsum, axis=0, keepdims=True) + EPS)
        hm_scr[:, hs] = (hn.T * gmh_ref[:, hs] * so_ref[:, hs].astype(F32)).astype(BF16)

    ym = _dot(hm_scr[...], wmo_ref[...])
    mix = (ycg_ref[...].astype(F32) + sgb_ref[...].astype(F32) * ym).astype(BF16)
    x1 = jnp.where(is_lat, xl_ref[...], xc_ref[...]) + mod_ref[0, 2:3, :] * _dot(mix, wo_ref[...])
    x1_ref[...] = x1
    h2 = _rms(x1) * g2_ref[...] * (1.0 + mod_ref[0, 4:5, :]) + mod_ref[0, 3:4, :]
    _store_rows(h2_ref, h2)

    lane = lax.broadcasted_iota(jnp.int32, (TM, LANES), 1)
    rw = rw_ref[...]
    h2_hi, rw_hi = h2.astype(BF16), rw.astype(BF16)
    h2_lo, rw_lo = (h2 - h2_hi.astype(F32)).astype(BF16), (rw - rw_hi.astype(F32)).astype(BF16)
    logits = _dot(h2_hi, rw_hi) + (_dot(h2_hi, rw_lo) + _dot(h2_lo, rw_hi)) + rb_ref[...]
    work = jnp.where(lane < n_experts, logits, -jnp.inf)
    sels, exps, idxs = [], [], []
    top = None
    for _ in range(TOP_K):
        mx = jnp.max(work, axis=-1, keepdims=True)
        ix = jnp.min(jnp.where(work == mx, lane, LANES), axis=-1, keepdims=True)
        sel = lane == ix
        work = jnp.where(sel, -jnp.inf, work)
        top = mx if top is None else top
        sels.append(sel)
        idxs.append(ix.astype(F32))
        exps.append(jnp.exp(mx - top))
    inv = 1.0 / functools.reduce(lambda p, q: p + q, exps)

    @pl.when(i == 0)
    def _():
        carry_scr[...] = jnp.zeros_like(carry_scr)

    onehot = functools.reduce(lambda p, q: p + q, [jnp.where(s, 1.0, 0.0) for s in sels])
    row = lax.broadcasted_iota(jnp.int32, (TM, TM), 0)
    col = lax.broadcasted_iota(jnp.int32, (TM, TM), 1)
    before = _dot((row > col).astype(BF16), onehot.astype(BF16)) + carry_scr[...]
    carry_scr[...] += jnp.sum(onehot, axis=0, keepdims=True)
    cnt_ref[...] = carry_scr[...]
    route = jnp.zeros((TM, LANES), F32)
    for j in range(TOP_K):
        slot = jnp.sum(jnp.where(sels[j], before, 0.0), axis=-1, keepdims=True)
        route = jnp.where(lane == j, idxs[j], route)
        route = jnp.where(lane == TOP_K + j, slot, route)
        route = jnp.where(lane == 2 * TOP_K + j, exps[j] * inv, route)
    route_ref[...] = route
    routet_ref[...] = route.T[0:8, :].astype(jnp.int32)


def _dispatch_kernel(pos_ref, h2_ref, xs_ref, sem):
    def copy(t, j):
        return pltpu.make_async_copy(_row_tile(h2_ref, t), _row_tile(xs_ref, pos_ref[TOP_K * t + j]), sem)

    def start(t, carry):
        for j in range(TOP_K):
            copy(t, j).start(priority=j % 2)
        return carry

    lax.fori_loop(0, TM, start, 0, unroll=ISSUE_UNROLL)
    for j in range(TOP_K):
        pltpu.make_async_copy(h2_ref, xs_ref.at[pl.ds(0, TM * ROW_SUB)], sem).wait()


def _expert_kernel(tile_ref, exp_ref, flag_ref, lo_ref, hi_ref, nxt_ref,
                   xs_ref, wg_hbm, bg_ref, wl_hbm, bl_ref, wd_hbm, bd_ref, ys_ref, w_scr, wf_scr, wsem):
    w = pl.program_id(0)
    flags = flag_ref[w]

    def fetch(e, slot):
        return [pltpu.make_async_copy(hbm.at[e], wf_scr.at[slot, m], wsem.at[slot])
                for m, hbm in enumerate((wg_hbm, wl_hbm, wd_hbm))]

    @pl.when(w == 0)
    def _():
        for cp in fetch(exp_ref[0], 0):
            cp.start()

    @pl.when((flags & 4) != 0)
    def _():
        slot = (flags >> 4) & 1
        for cp in fetch(exp_ref[w], slot):
            cp.wait()
        for m in range(3):
            w_scr[m] = wf_scr[slot, m].astype(BF16)

        @pl.when((flags & 8) != 0)
        def _():
            for cp in fetch(nxt_ref[w], 1 - slot):
                cp.start()

    @pl.when((flags & 1) != 0)
    def _():
        x = _load_rows(xs_ref).astype(BF16)
        gt = jnp.minimum(_dot(x, w_scr[0]) + bg_ref[0], SWIGLU_LIMIT)
        lin = jnp.clip(_dot(x, w_scr[1]) + bl_ref[0], -SWIGLU_LIMIT, SWIGLU_LIMIT)
        act = gt * jax.nn.sigmoid(SWIGLU_ALPHA * gt) * (lin + 1.0)
        y = _dot(act.astype(BF16), w_scr[2]) + bd_ref[0]
        rows = lax.broadcasted_iota(jnp.int32, (y.shape[0], 1), 0)
        mine = (rows >= lo_ref[w]) & (rows < hi_ref[w])

        @pl.when((flags & 2) != 0)
        def _():
            _store_rows(ys_ref, jnp.where(mine, y, 0.0))

        @pl.when((flags & 2) == 0)
        def _():
            _store_rows(ys_ref, jnp.where(mine, y, _load_rows(ys_ref)))


def _combine_kernel(pos_ref, posn_ref, route_ref, x1_ref, mod_ref, fg_ref, ys_ref, outc_ref, outl_ref,
                    buf, sem, *, n_ctx_tiles):
    i = pl.program_id(0)
    n = pl.num_programs(0)

    def copy(p_ref, slot, t, j):
        return pltpu.make_async_copy(_row_tile(ys_ref, p_ref[TOP_K * t + j]), _row_tile(buf.at[slot, j], t), sem.at[slot])

    def start_all(p_ref, slot):
        def body(t, carry):
            for j in range(TOP_K):
                copy(p_ref, slot, t, j).start(priority=j % 2)
            return carry
        lax.fori_loop(0, TM, body, 0, unroll=ISSUE_UNROLL)

    @pl.when(i == 0)
    def _():
        start_all(pos_ref, 0)

    @pl.when(i + 1 < n)
    def _():
        start_all(posn_ref, (i + 1) % 2)

    slot = i % 2

    for j in range(TOP_K):
        pltpu.make_async_copy(ys_ref.at[pl.ds(0, TM * ROW_SUB)], buf.at[slot, j], sem.at[slot]).wait()
    acc = None
    for j in range(TOP_K):
        term = route_ref[:, 2 * TOP_K + j:2 * TOP_K + j + 1] * _load_rows(buf.at[slot, j])
        acc = term if acc is None else acc + term
    out = _rms(x1_ref[...] + mod_ref[0, 5:6, :] * acc) * fg_ref[...]

    @pl.when(i < n_ctx_tiles)
    def _():
        outc_ref[...] = out

    @pl.when(i >= n_ctx_tiles)
    def _():
        outl_ref[...] = out


def _work_items(counts, n_experts, n_rows, tmx):
    n_items_max = n_rows // tmx + n_experts - 1
    cnt = counts.astype(jnp.int32)
    offs = jnp.concatenate([jnp.zeros((1,), jnp.int32), jnp.cumsum(cnt)])
    first_tile = offs[:-1] // tmx
    n_it = jnp.where(cnt > 0, (offs[1:] - 1) // tmx - first_tile + 1, 0)
    it_start = jnp.concatenate([jnp.zeros((1,), jnp.int32), jnp.cumsum(n_it)])
    total = it_start[-1]
    w = jnp.arange(n_items_max, dtype=jnp.int32)
    wc = jnp.minimum(w, total - 1)
    e = jnp.sum((it_start[None, 1:] <= wc[:, None]).astype(jnp.int32), axis=1)
    e = jnp.minimum(e, n_experts - 1)
    is_e = e[:, None] == jnp.arange(n_experts, dtype=jnp.int32)
    at_e = lambda a: jnp.sum(jnp.where(is_e, a[None, :], 0), axis=1)
    tile = at_e(first_tile) + wc - at_e(it_start[:-1])
    valid = w < total
    prev = lambda a: jnp.concatenate([jnp.full((1,), -1, jnp.int32), a[:-1]])
    ids = jnp.arange(n_experts, dtype=jnp.int32)
    later = (ids[None, :] > ids[:, None]) & (cnt > 0)[None, :]
    nxt = jnp.min(jnp.where(later, ids[None, :], n_experts), axis=1)
    has_next = nxt < n_experts
    slot = (jnp.cumsum((cnt > 0).astype(jnp.int32)) - 1) & 1
    flags = (valid.astype(jnp.int32) + 2 * (valid & (tile != prev(tile))).astype(jnp.int32)
             + 4 * (valid & (e != prev(e))).astype(jnp.int32)
             + 8 * at_e(has_next.astype(jnp.int32)) + 16 * at_e(slot))
    lo = jnp.clip(at_e(offs[:-1]) - tile * tmx, 0, tmx)
    hi = jnp.clip(at_e(offs[1:]) - tile * tmx, 0, tmx)
    return offs, tile, e, flags, lo, hi, at_e(jnp.minimum(nxt, n_experts - 1))


def _pack_in_proj(w, b):
    d = w.shape[0]
    o_g = 5 * d + 2 * D_QKH
    o_ga, o_gb = o_g + 4 * N_HEADS, o_g + 4 * N_HEADS + d

    def gates(m):
        gz = jnp.zeros((m.shape[0], LANES - N_HD), m.dtype)
        i_f, f_f, i_b, f_b = (m[:, o_g + j * N_HEADS:o_g + (j + 1) * N_HEADS] for j in range(4))
        return jnp.concatenate([i_f, i_b, gz, f_f, f_b, gz], axis=1)

    def pack(m):
        return jnp.concatenate([m[:, 0:o_g], m[:, o_ga:o_gb], m[:, o_gb:o_gb + d], gates(m)], axis=1)

    return pack(w.astype(BF16)), pack(b.reshape(1, -1))


def kernel(x_prompt, x_sample, c, state_C, state_n, state_m, c_ctx, ada_w, ada_b, norm1_g, norm2_g, w_in, b_in,
           conv_w, conv_b, w_conv_out, mh_norm_g, w_m_out, w_o, router_w, router_b, w_gate, b_gate, w_lin, b_lin,
           w_down, b_down, final_g):
    nb, seq, d = x_prompt.shape
    nd, dseq, _ = x_sample.shape
    n_experts = w_gate.shape[1]
    assert d == _D and w_in.shape[0] == 1 and seq == TM and dseq % TM == 0 and TM % GRID_W == 0
    assert w_gate.shape[-1] == d
    t_ctx, t_lat = nb * seq, nd * dseq
    n_tok = t_ctx + t_lat
    n_ctx_tiles, n_tiles = t_ctx // TM, n_tok // TM
    lat_chunks = dseq // TM
    n_lat_tiles = n_tiles - n_ctx_tiles

    n_c = 1 + nd
    n_cp = -(-n_c // 8) * 8
    cvec = jnp.concatenate([c_ctx[None, :], c, jnp.zeros((n_cp - n_c, d), F32)], axis=0)
    mod = _modulation(cvec, ada_w[0], ada_b[0]).reshape(n_cp, N_MOD, d)
    mod = jnp.pad(mod, ((0, 0), (0, 8 - N_MOD), (0, 0)))

    def mod_row(tile_tokens):
        ctx_t, per_seq = t_ctx // tile_tokens, dseq // tile_tokens
        return lambda i, *_: (jnp.where(i < ctx_t, 0, 1 + (i - ctx_t) // per_seq), 0, 0)

    x_ctx, x_lat = x_prompt.reshape(t_ctx, d), x_sample.reshape(t_lat, d)
    ctx_spec = pl.BlockSpec((TM, d), lambda i: (jnp.minimum(i, n_ctx_tiles - 1), 0))
    lat_spec = pl.BlockSpec((TM, d), lambda i: (jnp.maximum(i - n_ctx_tiles, 0), 0))
    w_all, b_all = _pack_in_proj(w_in[0], b_in[0])
    tile = lambda cols: pl.BlockSpec((TM, cols), lambda i: (i, 0))
    rows = lambda n: pl.BlockSpec((n * ROW_SUB, LANES), lambda i: (i, 0))
    params = pltpu.CompilerParams(dimension_semantics=("arbitrary",), vmem_limit_bytes=VMEM_LIMIT)
    bf = lambda cols: jax.ShapeDtypeStruct((n_tok, cols), BF16)

    ycg, sgb, q, k, v, so, gt = pl.pallas_call(
        functools.partial(_proj_kernel, n_ctx_tiles=n_ctx_tiles, ctx_row=seq),
        grid=(n_tiles,),
        in_specs=[ctx_spec, lat_spec, pl.BlockSpec((1, 8, d), mod_row(TM)), _const_spec((1, d)),
                  _const_spec((d, W_COLS)), _const_spec((1, W_COLS)), _const_spec((3, d)), _const_spec((1, d)),
                  _const_spec((d, d))],
        out_specs=[tile(d), tile(d), tile(D_QKH), tile(D_QKH), tile(d), tile(d), tile(2 * LANES)],
        out_shape=[bf(d), bf(d), bf(D_QKH), bf(D_QKH), bf(d), bf(d),
                   jax.ShapeDtypeStruct((n_tok, 2 * LANES), F32)],
        compiler_params=params,
        name="in_proj_conv",
    )(x_ctx, x_lat, mod, norm1_g, w_all, b_all, conv_w[0], conv_b, w_conv_out[0].astype(BF16))

    c_new, n_new, m_new = _state_scan(k, v, gt, 0, nb, 1, None, False, True)
    def on_head_lanes(a):
        even = lax.broadcasted_iota(jnp.int32, (N_HD,) + (1,) * (a.ndim - 2), 0) % 2 == 0
        lo = jnp.pad(a, [(0, 0)] * (a.ndim - 1) + [(0, LANES - D_QK)])
        hi = jnp.pad(a, [(0, 0)] * (a.ndim - 1) + [(LANES - D_QK, 0)])
        return jnp.where(even, lo, hi)

    init = (on_head_lanes(jnp.swapaxes(state_C[:, 0].astype(F32).reshape(nd, N_HD, D_QK, D_V), 2, 3)),
            on_head_lanes(state_n[:, 0].astype(F32).reshape(nd, N_HD, D_QK)),
            jnp.pad(state_m[:, 0].astype(F32).reshape(nd, 1, N_HD), ((0, 0), (0, 0), (0, LANES - N_HD))))
    cbf, cbb, nbf, nbb, mbf, mbb = _state_scan(k, v, gt, n_ctx_tiles, nd, lat_chunks, init, True, False)

    lat_idx = lambda i: jnp.maximum(i - n_ctx_tiles, 0)
    st4 = pl.BlockSpec((1, N_HEADS, D_V, LANES), lambda i: (lat_idx(i), 0, 0, 0))
    st3 = pl.BlockSpec((1, N_HEADS, LANES), lambda i: (lat_idx(i), 0, 0))
    st1 = pl.BlockSpec((1, 1, LANES), lambda i: (lat_idx(i), 0, 0))
    rw = jnp.pad(router_w[0], ((0, 0), (0, LANES - n_experts)))
    rb = jnp.pad(router_b[0], (0, LANES - n_experts)).reshape(1, LANES)
    x1, h2, route, route_t, counts = pl.pallas_call(
        functools.partial(_mix_kernel, n_ctx_tiles=n_ctx_tiles, n_experts=n_experts),
        grid=(n_tiles,),
        in_specs=[tile(D_QKH), tile(D_QKH), tile(d), tile(d), tile(d), tile(d), tile(2 * LANES), ctx_spec, lat_spec,
                  pl.BlockSpec((1, 8, d), mod_row(TM)), st4, st4, st3, st3, st1, st1,
                  _const_spec((1, d)), _const_spec((d, d)), _const_spec((d, d)), _const_spec((1, d)),
                  _const_spec((d, LANES)), _const_spec((1, LANES))],
        out_specs=[tile(d), rows(TM), tile(LANES), pl.BlockSpec((8, TM), lambda i: (0, i)),
                   pl.BlockSpec((1, LANES), lambda i: (0, 0))],
        out_shape=[jax.ShapeDtypeStruct((n_tok, d), F32), jax.ShapeDtypeStruct((n_tok * ROW_SUB, LANES), F32),
                   jax.ShapeDtypeStruct((n_tok, LANES), F32), jax.ShapeDtypeStruct((8, n_tok), jnp.int32),
                   jax.ShapeDtypeStruct((1, LANES), F32)],
        scratch_shapes=[pltpu.VMEM((TM, d), BF16), pltpu.VMEM((1, LANES), F32)],
        compiler_params=params,
        name="mlstm_mix_router",
    )(q, k, v, so, sgb, ycg, gt, x_ctx, x_lat, mod, cbf, cbb, nbf, nbb, mbf, mbb,
      mh_norm_g, w_m_out[0].astype(BF16), w_o[0].astype(BF16), norm2_g, rw, rb)

    n_rows = TOP_K * n_tok
    offs, it_tile, it_exp, it_flags, it_lo, it_hi, it_nxt = _work_items(
        counts[0, :n_experts], n_experts, n_rows, TMX)
    is_exp = route_t[0:TOP_K, :, None] == jnp.arange(n_experts, dtype=jnp.int32)
    pos_t = jnp.sum(jnp.where(is_exp, offs[:n_experts], 0), axis=-1) + route_t[TOP_K:2 * TOP_K]
    pos = pos_t.T.reshape(n_rows)
    pos_spec = lambda f: pl.BlockSpec((TOP_K * TM,), f, memory_space=pltpu.SMEM)
    any_spec = pl.BlockSpec(memory_space=pl.ANY)

    xs = pl.pallas_call(
        _dispatch_kernel,
        grid=(n_tiles,),
        in_specs=[pos_spec(lambda i: (i,)), rows(TM)],
        out_specs=any_spec,
        out_shape=jax.ShapeDtypeStruct((n_rows * ROW_SUB, LANES), F32),
        scratch_shapes=[pltpu.SemaphoreType.DMA],
        compiler_params=params,
        name="moe_dispatch",
    )(pos, h2)

    dff = w_gate.shape[-1]
    wspec = lambda a, b_: pl.BlockSpec((1, a, b_), lambda w, tl, ex, *_: (ex[w], 0, 0))
    ys = pl.pallas_call(
        _expert_kernel,
        grid_spec=pltpu.PrefetchScalarGridSpec(
            num_scalar_prefetch=6,
            grid=(it_tile.shape[0],),
            in_specs=[pl.BlockSpec((TMX * ROW_SUB, LANES), lambda w, tl, *_: (tl[w], 0)),
                      any_spec, wspec(1, dff), any_spec, wspec(1, dff), any_spec, wspec(1, d)],
            out_specs=pl.BlockSpec((TMX * ROW_SUB, LANES), lambda w, tl, *_: (tl[w], 0)),
            scratch_shapes=[pltpu.VMEM((3, d, dff), BF16), pltpu.VMEM((2, 3, d, dff), F32),
                            pltpu.SemaphoreType.DMA((2,))]),
        out_shape=jax.ShapeDtypeStruct((n_rows * ROW_SUB, LANES), F32),
        compiler_params=params,
        name="moe_experts",
    )(it_tile, it_exp, it_flags, it_lo, it_hi, it_nxt, xs, w_gate[0], b_gate[0].reshape(n_experts, 1, dff), w_lin[0],
      b_lin[0].reshape(n_experts, 1, dff), w_down[0], b_down[0].reshape(n_experts, 1, d))

    ctx_i = lambda i: (jnp.minimum(i, n_ctx_tiles - 1), 0)
    lat_i = lambda i: (jnp.maximum(i - n_ctx_tiles, 0), 0)
    y_prompt, y_sample = pl.pallas_call(
        functools.partial(_combine_kernel, n_ctx_tiles=n_ctx_tiles),
        grid=(n_tiles,),
        in_specs=[pos_spec(lambda i: (i,)), pos_spec(lambda i: (jnp.minimum(i + 1, n_tiles - 1),)),
                  tile(LANES), tile(d), pl.BlockSpec((1, 8, d), mod_row(TM)), _const_spec((1, d)), any_spec],
        out_specs=[pl.BlockSpec((TM, d), ctx_i), pl.BlockSpec((TM, d), lat_i)],
        out_shape=[jax.ShapeDtypeStruct((t_ctx, d), F32), jax.ShapeDtypeStruct((t_lat, d), F32)],
        scratch_shapes=[pltpu.VMEM((2, TOP_K, TM * ROW_SUB, LANES), F32), pltpu.SemaphoreType.DMA((2,))],
        compiler_params=params,
        name="moe_combine",
    )(pos, pos, route, x1, mod, final_g.reshape(1, d), ys)

    y_prompt = y_prompt.reshape(nb, seq, d)
    y_sample = y_sample.reshape(nd, dseq, d)
    new_c = c_new.reshape(nb, 1, 2, N_HEADS, D_QK, D_V)
    odd_head = (lax.broadcasted_iota(jnp.int32, (1, N_HD, 1), 1) % 2) == 1
    new_n = jnp.where(odd_head, n_new[:, :, D_QK:], n_new[:, :, :D_QK]).reshape(nb, 1, 2, N_HEADS, D_QK)
    new_m = m_new[:, 0, :N_HD].reshape(nb, 1, 2, N_HEADS)
    return (y_prompt, y_sample, new_c, new_n, new_m)
```

```python
import functools

import jax
import jax.numpy as jnp
from jax import lax
from jax.experimental import pallas as pl
from jax.experimental.pallas import tpu as pltpu

F32 = jnp.float32
BF16 = jnp.bfloat16
HIGHEST = lax.Precision.HIGHEST

N_HEADS = 8
D_QK = 64
D_V = 128
GRID_W = 64
TOP_K = 4
SWIGLU_LIMIT = 7.0
SWIGLU_ALPHA = 1.702
EPS = 1e-6
N_MOD = 6

LANES = 128
TM = 256
TMX = 512
ISSUE_UNROLL = 4
TM_DISPATCH = 1024
TM_COMBINE = 512
N_HD = 2 * N_HEADS
VMEM_LIMIT = 56 * 1024 * 1024

_D = 1024
C_CONV = (0, 3 * _D)
D_QKH = N_HEADS * D_QK
C_Q = (3 * _D, 3 * _D + D_QKH)
C_K = (C_Q[1], C_Q[1] + D_QKH)
C_V = (C_K[1], C_K[1] + _D)
C_O = (C_V[1], C_V[1] + _D)
C_GA = (C_O[1], C_O[1] + _D)
C_GB = (C_GA[1], C_GA[1] + _D)
C_G = (C_GB[1], C_GB[1] + 2 * LANES)
W_COLS = C_G[1]


def _head_lanes(h):
    lane = lax.broadcasted_iota(jnp.int32, (1, LANES), 1)
    return (lane < D_QK) if h % 2 == 0 else (lane >= D_QK)


def _dot(a, b, precision=None):
    return jnp.dot(a, b, preferred_element_type=F32, precision=precision)


def _rms(x):
    return x * lax.rsqrt(jnp.mean(x * x, axis=-1, keepdims=True) + EPS)


ROW_SUB = 8


def _store_rows(ref, val):
    n = val.shape[0]
    for s in range(ROW_SUB):
        ref[pl.ds(s, n, stride=ROW_SUB), :] = val[:, s * LANES:(s + 1) * LANES]


def _load_rows(ref):
    n = ref.shape[0] // ROW_SUB
    return jnp.concatenate([ref[pl.ds(s, n, stride=ROW_SUB), :] for s in range(ROW_SUB)], axis=1)


def _row_tile(ref, r):
    return ref.at[pl.ds(pl.multiple_of(r * ROW_SUB, ROW_SUB), ROW_SUB)]


def _const_spec(shape):
    return pl.BlockSpec(shape, lambda *_: (0,) * len(shape), pipeline_mode=pl.Buffered(1))


def _mod_kernel(c_ref, w_ref, b_ref, o_ref):
    c = c_ref[...]
    o_ref[...] = _dot(c * jax.nn.sigmoid(c), w_ref[...], HIGHEST) + b_ref[...]


def _modulation(cvec, ada_w, ada_b):
    n, d = cvec.shape
    nout = ada_w.shape[1]
    return pl.pallas_call(
        _mod_kernel,
        grid=(nout // d,),
        in_specs=[pl.BlockSpec((n, d), lambda j: (0, 0)),
                  pl.BlockSpec((d, d), lambda j: (0, j)),
                  pl.BlockSpec((1, d), lambda j: (0, j))],
        out_specs=pl.BlockSpec((n, d), lambda j: (0, j)),
        out_shape=jax.ShapeDtypeStruct((n, nout), F32),
        name="adaln_mod",
    )(cvec, ada_w, ada_b.reshape(1, nout))


def _proj_kernel(xc_ref, xl_ref, mod_ref, g1_ref, w_ref, b_ref, cw_ref, cb_ref, wco_ref,
                 ycg_ref, sgb_ref, q_ref, k_ref, v_ref, so_ref, gt_ref, *, n_ctx_tiles, ctx_row):
    i = pl.program_id(0)
    x = jnp.where(i < n_ctx_tiles, xc_ref[...], xl_ref[...])
    h = (_rms(x) * g1_ref[...] * (1.0 + mod_ref[0, 1:2, :]) + mod_ref[0, 0:1, :]).astype(BF16)

    def proj(cols):
        return _dot(h, w_ref[:, cols[0]:cols[1]]) + b_ref[:, cols[0]:cols[1]]

    zc = proj(C_CONV)
    d = x.shape[1]
    u = zc[:, 2 * d:3 * d] * zc[:, 0:d]
    rowlen = jnp.where(i < n_ctx_tiles, ctx_row, GRID_W)
    pos = lax.broadcasted_iota(jnp.int32, (TM, 1), 0) & (rowlen - 1)
    u_prev = jnp.where(pos == 0, 0.0, pltpu.roll(u, 1, 0))
    u_next = jnp.where(pos == rowlen - 1, 0.0, pltpu.roll(u, TM - 1, 0))
    uc = u_prev * cw_ref[0:1, :] + u * cw_ref[1:2, :] + u_next * cw_ref[2:3, :] + cb_ref[...]
    yconv = _dot((zc[:, d:2 * d] * uc).astype(BF16), wco_ref[...])

    ycg_ref[...] = (jax.nn.sigmoid(proj(C_GA)) * yconv).astype(BF16)
    sgb_ref[...] = jax.nn.sigmoid(proj(C_GB)).astype(BF16)
    q_ref[...] = (proj(C_Q) * (D_QK ** -0.5)).astype(BF16)
    k_ref[...] = proj(C_K).astype(BF16)
    v_ref[...] = proj(C_V).astype(BF16)
    so_ref[...] = jax.nn.sigmoid(proj(C_O)).astype(BF16)
    gt_ref[...] = proj(C_G)


def _tri_masks():
    row = lax.broadcasted_iota(jnp.int32, (TM, TM), 0)
    col = lax.broadcasted_iota(jnp.int32, (TM, TM), 1)
    return row >= col, row <= col


def _gate_cumsums(gi, gf):
    lower, upper = _tri_masks()
    lf = jax.nn.log_sigmoid(gf)
    fwd_lane = lax.broadcasted_iota(jnp.int32, (1, LANES), 1) < N_HEADS
    bsum = jnp.where(fwd_lane, _dot(lower.astype(F32), lf, HIGHEST), _dot(upper.astype(F32), lf, HIGHEST))
    return lf, bsum, fwd_lane


def _state_kernel(*refs, zero_init, emit_before, emit_after):
    it = iter(refs)
    kf_ref, vf_ref, gf_ref, kb_ref, vb_ref, gb_ref = (next(it) for _ in range(6))
    if not zero_init:
        c0_ref, n0_ref, m0_ref = (next(it) for _ in range(3))
    if emit_before:
        cbf_ref, cbb_ref, nbf_ref, nbb_ref, mbf_ref, mbb_ref = (next(it) for _ in range(6))
    if emit_after:
        ca_ref, na_ref, ma_ref = (next(it) for _ in range(3))
    c_scr, n_scr, m_scr = (next(it) for _ in range(3))
    c = pl.program_id(1)

    @pl.when(c == 0)
    def _():
        if zero_init:
            c_scr[...] = jnp.zeros_like(c_scr)
            n_scr[...] = jnp.zeros_like(n_scr)
            m_scr[...] = jnp.zeros_like(m_scr)
        else:
            c_scr[...] = c0_ref[0]
            n_scr[...] = n0_ref[0]
            m_scr[...] = m0_ref[0]

    if emit_before:
        cbf_ref[0] = c_scr[0:N_HEADS]
        cbb_ref[0] = c_scr[N_HEADS:N_HD]
        nbf_ref[0] = n_scr[0:N_HEADS]
        nbb_ref[0] = n_scr[N_HEADS:N_HD]
        mbf_ref[0] = m_scr[...]
        mbb_ref[0] = m_scr[...]

    fwd_lane = lax.broadcasted_iota(jnp.int32, (1, LANES), 1) < N_HEADS
    gi = jnp.where(fwd_lane, gf_ref[:, 0:LANES], gb_ref[:, 0:LANES])
    gfg = jnp.where(fwd_lane, gf_ref[:, LANES:2 * LANES], gb_ref[:, LANES:2 * LANES])
    lf, bsum, _ = _gate_cumsums(gi, gfg)
    total = jnp.sum(lf, axis=0, keepdims=True)
    g = total - bsum + gi
    m_prev = m_scr[...]
    m_new = jnp.maximum(total + m_prev, jnp.max(g, axis=0, keepdims=True))
    wk = jnp.exp(g - m_new)
    decay = jnp.exp(total + m_prev - m_new)
    for hd in range(N_HD):
        h = hd % N_HEADS
        k_ref, v_ref = (kf_ref, vf_ref) if hd < N_HEADS else (kb_ref, vb_ref)
        k_tile = k_ref[:, (h // 2) * LANES:(h // 2 + 1) * LANES].astype(F32)
        wkk = wk[:, hd:hd + 1] * jnp.where(_head_lanes(h), k_tile, 0.0)
        vh = v_ref[:, h * D_V:(h + 1) * D_V]
        dec = decay[:, hd:hd + 1]
        tn = (((0,), (0,)), ((), ()))
        if emit_before:
            c_scr[hd] = dec * c_scr[hd] + lax.dot_general(vh, wkk.astype(BF16), tn, preferred_element_type=F32)
        else:
            upd = lax.dot_general(wkk.astype(BF16), vh, tn, preferred_element_type=F32)
            c_scr[hd] = dec * c_scr[hd] + upd[(h % 2) * D_QK:(h % 2 + 1) * D_QK, :]
        n_scr[hd:hd + 1, :] = dec * n_scr[hd:hd + 1, :] + jnp.sum(wkk, axis=0, keepdims=True)
    m_scr[...] = m_new

    if emit_after:
        @pl.when(c == pl.num_programs(1) - 1)
        def _():
            ca_ref[0] = c_scr[...]
            na_ref[0] = n_scr[...]
            ma_ref[0] = m_scr[...]


def _state_scan(k, v, gt, tile0, n_seq, n_chunk, init, emit_before, emit_after):
    d = k.shape[1]
    cshape = (D_V, LANES) if emit_before else (D_QK, D_V)
    fwd = lambda s, c: (tile0 + s * n_chunk + c, 0)
    bwd = lambda s, c: (tile0 + s * n_chunk + n_chunk - 1 - c, 0)
    dv = v.shape[1]
    in_specs = [pl.BlockSpec((TM, d), fwd), pl.BlockSpec((TM, dv), fwd), pl.BlockSpec((TM, 2 * LANES), fwd),
                pl.BlockSpec((TM, d), bwd), pl.BlockSpec((TM, dv), bwd), pl.BlockSpec((TM, 2 * LANES), bwd)]
    args = [k, v, gt, k, v, gt]
    if init is not None:
        in_specs += [pl.BlockSpec((1, N_HD) + cshape, lambda s, c: (s, 0, 0, 0)),
                     pl.BlockSpec((1, N_HD, LANES), lambda s, c: (s, 0, 0)),
                     pl.BlockSpec((1, 1, LANES), lambda s, c: (s, 0, 0))]
        args += list(init)
    out_specs, out_shape = [], []
    n_tot = n_seq * n_chunk
    if emit_before:
        cf = lambda s, c: (s * n_chunk + c, 0, 0, 0)
        cb = lambda s, c: (s * n_chunk + n_chunk - 1 - c, 0, 0, 0)
        nf = lambda s, c: (s * n_chunk + c, 0, 0)
        nb = lambda s, c: (s * n_chunk + n_chunk - 1 - c, 0, 0)
        out_specs += [pl.BlockSpec((1, N_HEADS) + cshape, cf), pl.BlockSpec((1, N_HEADS) + cshape, cb),
                      pl.BlockSpec((1, N_HEADS, LANES), nf), pl.BlockSpec((1, N_HEADS, LANES), nb),
                      pl.BlockSpec((1, 1, LANES), nf), pl.BlockSpec((1, 1, LANES), nb)]
        out_shape += [jax.ShapeDtypeStruct((n_tot, N_HEADS) + cshape, F32)] * 2
        out_shape += [jax.ShapeDtypeStruct((n_tot, N_HEADS, LANES), F32)] * 2
        out_shape += [jax.ShapeDtypeStruct((n_tot, 1, LANES), F32)] * 2
    if emit_after:
        out_specs += [pl.BlockSpec((1, N_HD, D_QK, D_V), lambda s, c: (s, 0, 0, 0)),
                      pl.BlockSpec((1, N_HD, LANES), lambda s, c: (s, 0, 0)),
                      pl.BlockSpec((1, 1, LANES), lambda s, c: (s, 0, 0))]
        out_shape += [jax.ShapeDtypeStruct((n_seq, N_HD, D_QK, D_V), F32),
                      jax.ShapeDtypeStruct((n_seq, N_HD, LANES), F32),
                      jax.ShapeDtypeStruct((n_seq, 1, LANES), F32)]
    return pl.pallas_call(
        functools.partial(_state_kernel, zero_init=init is None, emit_before=emit_before, emit_after=emit_after),
        grid=(n_seq, n_chunk),
        in_specs=in_specs,
        out_specs=out_specs,
        out_shape=out_shape,
        scratch_shapes=[pltpu.VMEM((N_HD,) + cshape, F32), pltpu.VMEM((N_HD, LANES), F32),
                        pltpu.VMEM((1, LANES), F32)],
        compiler_params=pltpu.CompilerParams(dimension_semantics=("arbitrary", "arbitrary"),
                                             vmem_limit_bytes=VMEM_LIMIT),
        name="mlstm_state_scan",
    )(*args)


def _mix_kernel(q_ref, k_ref, v_ref, so_ref, sgb_ref, ycg_ref, gt_ref, xc_ref, xl_ref, mod_ref,
                cbf_ref, cbb_ref, nbf_ref, nbb_ref, mbf_ref, mbb_ref,
                gmh_ref, wmo_ref, wo_ref, g2_ref, rw_ref, rb_ref,
                x1_ref, h2_ref, route_ref, routet_ref, cnt_ref, hm_scr, carry_scr, *, n_ctx_tiles, n_experts):
    i = pl.program_id(0)
    is_lat = i >= n_ctx_tiles
    lat_f = is_lat.astype(F32)
    lower, upper = _tri_masks()
    gt_t = gt_ref[...].T
    gi_t = gt_t[0:N_HD, :]
    lf_t = jax.nn.log_sigmoid(gt_t[LANES:LANES + N_HD, :])
    fwd_row = lax.broadcasted_iota(jnp.int32, (N_HD, 1), 0) < N_HEADS
    bsum_t = jnp.where(fwd_row, _dot(lf_t, upper.astype(F32), HIGHEST), _dot(lf_t, lower.astype(F32), HIGHEST))
    a_t = gi_t - bsum_t
    m_row = jnp.where(lax.broadcasted_iota(jnp.int32, (1, LANES), 1) < N_HEADS, mbf_ref[0], mbb_ref[0]) * lat_f
    m_sq = jnp.where(lax.broadcasted_iota(jnp.int32, (LANES, 1), 0) == 0, m_row, 0.0)
    m_prev = m_sq.T[0:N_HD, 0:1]
    a_c = jnp.concatenate([a_t, jnp.zeros((LANES - N_HD, TM), F32)], axis=0).T
    row0 = lax.broadcasted_iota(jnp.int32, (LANES, 1), 0) == 0
    ones_rows = jnp.where(row0, 1.0, 0.0).astype(BF16) * jnp.ones((1, TM), BF16)

    for h in range(N_HEADS):
        hs = slice(h * D_V, (h + 1) * D_V)
        pair = slice((h // 2) * LANES, (h // 2 + 1) * LANES)
        qh = jnp.where(_head_lanes(h), q_ref[:, pair], jnp.zeros((), BF16))
        kh = k_ref[:, pair]
        vext_t = jnp.concatenate([v_ref[:, hs].T, ones_rows], axis=0)
        kq = lax.dot_general(kh, qh, (((1,), (1,)), ((), ())), preferred_element_type=F32)
        hsum = None
        for d in range(2):
            hd = d * N_HEADS + h
            mask = upper if d == 0 else lower
            c_t = (cbf_ref if d == 0 else cbb_ref)[0, h] * lat_f
            n_r = (nbf_ref if d == 0 else nbb_ref)[0, h:h + 1, :] * lat_f
            a_b = jnp.where(mask, a_c[:, hd:hd + 1], -jnp.inf)
            mrow = jnp.maximum(m_prev[hd:hd + 1, :], jnp.max(a_b, axis=0, keepdims=True))
            e = jnp.exp(a_b - mrow)
            nd = _dot(vext_t, (kq * e).astype(BF16))
            cext_t = jnp.concatenate([c_t, jnp.where(row0, n_r, 0.0)], axis=0).astype(BF16)
            qc = lax.dot_general(cext_t, qh, (((1,), (1,)), ((), ())), preferred_element_type=F32)
            wi = jnp.exp(m_prev[hd:hd + 1, :] - mrow)
            num = nd[0:D_V, :] + wi * qc[0:D_V, :]
            den = nd[D_V:D_V + 1, :] + wi * qc[D_V:D_V + 1, :]
            r = 1.0 / jnp.maximum(jnp.abs(den), jnp.exp(-(bsum_t[hd:hd + 1, :] + mrow)))
            hsum = num * r if hsum is None else hsum + num * r
        hn = hsum * lax.rsqrt(jnp.mean(hsum * hsum, axis=0, keepdims=True) + EPS)
        hm_scr[:, hs] = (hn.T * gmh_ref[:, hs] * so_ref[:, hs].astype(F32)).astype(BF16)

    ym = _dot(hm_scr[...], wmo_ref[...])
    mix = (ycg_ref[...].astype(F32) + sgb_ref[...].astype(F32) * ym).astype(BF16)
    x1 = jnp.where(is_lat, xl_ref[...], xc_ref[...]) + mod_ref[0, 2:3, :] * _dot(mix, wo_ref[...])
    x1_ref[...] = x1
    h2 = _rms(x1) * g2_ref[...] * (1.0 + mod_ref[0, 4:5, :]) + mod_ref[0, 3:4, :]
    _store_rows(h2_ref, h2)

    lane = lax.broadcasted_iota(jnp.int32, (TM, LANES), 1)
    rw = rw_ref[...]
    h2_hi, rw_hi = h2.astype(BF16), rw.astype(BF16)
    h2_lo, rw_lo = (h2 - h2_hi.astype(F32)).astype(BF16), (rw - rw_hi.astype(F32)).astype(BF16)
    logits = _dot(h2_hi, rw_hi) + (_dot(h2_hi, rw_lo) + _dot(h2_lo, rw_hi)) + rb_ref[...]
    work = jnp.where(lane < n_experts, logits, -jnp.inf)
    sels, exps, idxs = [], [], []
    top = None
    for _ in range(TOP_K):
        mx = jnp.max(work, axis=-1, keepdims=True)
        ix = jnp.min(jnp.where(work == mx, lane, LANES), axis=-1, keepdims=True)
        sel = lane == ix
        work = jnp.where(sel, -jnp.inf, work)
        top = mx if top is None else top
        sels.append(sel)
        idxs.append(ix.astype(F32))
        exps.append(jnp.exp(mx - top))
    inv = 1.0 / functools.reduce(lambda p, q: p + q, exps)

    @pl.when(i == 0)
    def _():
        carry_scr[...] = jnp.zeros_like(carry_scr)

    onehot = functools.reduce(lambda p, q: p + q, [jnp.where(s, 1.0, 0.0) for s in sels])
    row = lax.broadcasted_iota(jnp.int32, (TM, TM), 0)
    col = lax.broadcasted_iota(jnp.int32, (TM, TM), 1)
    before = _dot((row > col).astype(BF16), onehot.astype(BF16)) + carry_scr[...]
    carry_scr[...] += jnp.sum(onehot, axis=0, keepdims=True)
    cnt_ref[...] = carry_scr[...]
    route = jnp.zeros((TM, LANES), F32)
    for j in range(TOP_K):
        slot = jnp.sum(jnp.where(sels[j], before, 0.0), axis=-1, keepdims=True)
        route = jnp.where(lane == j, idxs[j], route)
        route = jnp.where(lane == TOP_K + j, slot, route)
        route = jnp.where(lane == 2 * TOP_K + j, exps[j] * inv, route)
    route_ref[...] = route
    routet_ref[...] = route.T[0:8, :].astype(jnp.int32)


def _dispatch_kernel(pos_ref, h2_ref, xs_ref, sem):
    def copy(t, j):
        return pltpu.make_async_copy(_row_tile(h2_ref, t), _row_tile(xs_ref, pos_ref[TOP_K * t + j]), sem)

    def start(t, carry):
        for j in range(TOP_K):
            copy(t, j).start(priority=j % 2)
        return carry

    tm = h2_ref.shape[0] // ROW_SUB
    lax.fori_loop(0, tm, start, 0, unroll=ISSUE_UNROLL)
    for j in range(TOP_K):
        pltpu.make_async_copy(h2_ref, xs_ref.at[pl.ds(0, tm * ROW_SUB)], sem).wait()


def _expert_kernel(tile_ref, exp_ref, flag_ref, lo_ref, hi_ref, nxt_ref,
                   xs_ref, wg_hbm, bg_ref, wl_hbm, bl_ref, wd_hbm, bd_ref, ys_ref, w_scr, wf_scr, wsem):
    w = pl.program_id(0)
    flags = flag_ref[w]

    def fetch(e, slot):
        return [pltpu.make_async_copy(hbm.at[e], wf_scr.at[slot, m], wsem.at[slot])
                for m, hbm in enumerate((wg_hbm, wl_hbm, wd_hbm))]

    @pl.when(w == 0)
    def _():
        for cp in fetch(exp_ref[0], 0):
            cp.start()

    @pl.when((flags & 4) != 0)
    def _():
        slot = (flags >> 4) & 1
        for cp in fetch(exp_ref[w], slot):
            cp.wait()
        for m in range(3):
            w_scr[m] = wf_scr[slot, m].astype(BF16)

        @pl.when((flags & 8) != 0)
        def _():
            for cp in fetch(nxt_ref[w], 1 - slot):
                cp.start()

    @pl.when((flags & 1) != 0)
    def _():
        x = _load_rows(xs_ref).astype(BF16)
        gt = jnp.minimum(_dot(x, w_scr[0]) + bg_ref[0], SWIGLU_LIMIT)
        lin = jnp.clip(_dot(x, w_scr[1]) + bl_ref[0], -SWIGLU_LIMIT, SWIGLU_LIMIT)
        act = gt * jax.nn.sigmoid(SWIGLU_ALPHA * gt) * (lin + 1.0)
        y = _dot(act.astype(BF16), w_scr[2]) + bd_ref[0]
        rows = lax.broadcasted_iota(jnp.int32, (y.shape[0], 1), 0)
        mine = (rows >= lo_ref[w]) & (rows < hi_ref[w])

        @pl.when((flags & 2) != 0)
        def _():
            _store_rows(ys_ref, jnp.where(mine, y, 0.0))

        @pl.when((flags & 2) == 0)
        def _():
            _store_rows(ys_ref, jnp.where(mine, y, _load_rows(ys_ref)))


def _combine_kernel(pos_ref, posn_ref, route_ref, x1_ref, mod_ref, fg_ref, ys_ref, outc_ref, outl_ref,
                    buf, sem, *, n_ctx_tiles):
    i = pl.program_id(0)
    n = pl.num_programs(0)
    tm = x1_ref.shape[0]

    def copy(p_ref, slot, t, j):
        return pltpu.make_async_copy(_row_tile(ys_ref, p_ref[TOP_K * t + j]), _row_tile(buf.at[slot, j], t), sem.at[slot])

    def start_all(p_ref, slot):
        def body(t, carry):
            for j in range(TOP_K):
                copy(p_ref, slot, t, j).start(priority=j % 2)
            return carry
        lax.fori_loop(0, tm, body, 0, unroll=ISSUE_UNROLL)

    @pl.when(i == 0)
    def _():
        start_all(pos_ref, 0)

    @pl.when(i + 1 < n)
    def _():
        start_all(posn_ref, (i + 1) % 2)

    slot = i % 2

    for j in range(TOP_K):
        pltpu.make_async_copy(ys_ref.at[pl.ds(0, tm * ROW_SUB)], buf.at[slot, j], sem.at[slot]).wait()
    acc = None
    for j in range(TOP_K):
        term = route_ref[:, 2 * TOP_K + j:2 * TOP_K + j + 1] * _load_rows(buf.at[slot, j])
        acc = term if acc is None else acc + term
    out = _rms(x1_ref[...] + mod_ref[0, 5:6, :] * acc) * fg_ref[...]

    @pl.when(i < n_ctx_tiles)
    def _():
        outc_ref[...] = out

    @pl.when(i >= n_ctx_tiles)
    def _():
        outl_ref[...] = out


def _work_items(counts, n_experts, n_rows, tmx):
    n_items_max = n_rows // tmx + n_experts - 1
    cnt = counts.astype(jnp.int32)
    offs = jnp.concatenate([jnp.zeros((1,), jnp.int32), jnp.cumsum(cnt)])
    first_tile = offs[:-1] // tmx
    n_it = jnp.where(cnt > 0, (offs[1:] - 1) // tmx - first_tile + 1, 0)
    it_start = jnp.concatenate([jnp.zeros((1,), jnp.int32), jnp.cumsum(n_it)])
    total = it_start[-1]
    w = jnp.arange(n_items_max, dtype=jnp.int32)
    wc = jnp.minimum(w, total - 1)
    e = jnp.sum((it_start[None, 1:] <= wc[:, None]).astype(jnp.int32), axis=1)
    e = jnp.minimum(e, n_experts - 1)
    is_e = e[:, None] == jnp.arange(n_experts, dtype=jnp.int32)
    at_e = lambda a: jnp.sum(jnp.where(is_e, a[None, :], 0), axis=1)
    tile = at_e(first_tile) + wc - at_e(it_start[:-1])
    valid = w < total
    prev = lambda a: jnp.concatenate([jnp.full((1,), -1, jnp.int32), a[:-1]])
    ids = jnp.arange(n_experts, dtype=jnp.int32)
    later = (ids[None, :] > ids[:, None]) & (cnt > 0)[None, :]
    nxt = jnp.min(jnp.where(later, ids[None, :], n_experts), axis=1)
    has_next = nxt < n_experts
    slot = (jnp.cumsum((cnt > 0).astype(jnp.int32)) - 1) & 1
    flags = (valid.astype(jnp.int32) + 2 * (valid & (tile != prev(tile))).astype(jnp.int32)
             + 4 * (valid & (e != prev(e))).astype(jnp.int32)
             + 8 * at_e(has_next.astype(jnp.int32)) + 16 * at_e(slot))
    lo = jnp.clip(at_e(offs[:-1]) - tile * tmx, 0, tmx)
    hi = jnp.clip(at_e(offs[1:]) - tile * tmx, 0, tmx)
    return offs, tile, e, flags, lo, hi, at_e(jnp.minimum(nxt, n_experts - 1))


def _pack_in_proj(w, b):
    d = w.shape[0]
    o_g = 5 * d + 2 * D_QKH
    o_ga, o_gb = o_g + 4 * N_HEADS, o_g + 4 * N_HEADS + d

    def gates(m):
        gz = jnp.zeros((m.shape[0], LANES - N_HD), m.dtype)
        i_f, f_f, i_b, f_b = (m[:, o_g + j * N_HEADS:o_g + (j + 1) * N_HEADS] for j in range(4))
        return jnp.concatenate([i_f, i_b, gz, f_f, f_b, gz], axis=1)

    def pack(m):
        return jnp.concatenate([m[:, 0:o_g], m[:, o_ga:o_gb], m[:, o_gb:o_gb + d], gates(m)], axis=1)

    return pack(w.astype(BF16)), pack(b.reshape(1, -1))


def kernel(x_prompt, x_sample, c, state_C, state_n, state_m, c_ctx, ada_w, ada_b, norm1_g, norm2_g, w_in, b_in,
           conv_w, conv_b, w_conv_out, mh_norm_g, w_m_out, w_o, router_w, router_b, w_gate, b_gate, w_lin, b_lin,
           w_down, b_down, final_g):
    nb, seq, d = x_prompt.shape
    nd, dseq, _ = x_sample.shape
    n_experts = w_gate.shape[1]
    assert d == _D and w_in.shape[0] == 1 and seq == TM and dseq % TM == 0 and TM % GRID_W == 0
    assert w_gate.shape[-1] == d
    t_ctx, t_lat = nb * seq, nd * dseq
    n_tok = t_ctx + t_lat
    n_ctx_tiles, n_tiles = t_ctx // TM, n_tok // TM
    lat_chunks = dseq // TM
    n_lat_tiles = n_tiles - n_ctx_tiles

    n_c = 1 + nd
    n_cp = -(-n_c // 8) * 8
    cvec = jnp.concatenate([c_ctx[None, :], c, jnp.zeros((n_cp - n_c, d), F32)], axis=0)
    mod = _modulation(cvec, ada_w[0], ada_b[0]).reshape(n_cp, N_MOD, d)
    mod = jnp.pad(mod, ((0, 0), (0, 8 - N_MOD), (0, 0)))

    def mod_row(tile_tokens):
        ctx_t, per_seq = t_ctx // tile_tokens, dseq // tile_tokens
        return lambda i, *_: (jnp.where(i < ctx_t, 0, 1 + (i - ctx_t) // per_seq), 0, 0)

    x_ctx, x_lat = x_prompt.reshape(t_ctx, d), x_sample.reshape(t_lat, d)
    ctx_spec = pl.BlockSpec((TM, d), lambda i: (jnp.minimum(i, n_ctx_tiles - 1), 0))
    lat_spec = pl.BlockSpec((TM, d), lambda i: (jnp.maximum(i - n_ctx_tiles, 0), 0))
    w_all, b_all = _pack_in_proj(w_in[0], b_in[0])
    tile = lambda cols: pl.BlockSpec((TM, cols), lambda i: (i, 0))
    rows = lambda n: pl.BlockSpec((n * ROW_SUB, LANES), lambda i: (i, 0))
    params = pltpu.CompilerParams(dimension_semantics=("arbitrary",), vmem_limit_bytes=VMEM_LIMIT)
    bf = lambda cols: jax.ShapeDtypeStruct((n_tok, cols), BF16)

    ycg, sgb, q, k, v, so, gt = pl.pallas_call(
        functools.partial(_proj_kernel, n_ctx_tiles=n_ctx_tiles, ctx_row=seq),
        grid=(n_tiles,),
        in_specs=[ctx_spec, lat_spec, pl.BlockSpec((1, 8, d), mod_row(TM)), _const_spec((1, d)),
                  _const_spec((d, W_COLS)), _const_spec((1, W_COLS)), _const_spec((3, d)), _const_spec((1, d)),
                  _const_spec((d, d))],
        out_specs=[tile(d), tile(d), tile(D_QKH), tile(D_QKH), tile(d), tile(d), tile(2 * LANES)],
        out_shape=[bf(d), bf(d), bf(D_QKH), bf(D_QKH), bf(d), bf(d),
                   jax.ShapeDtypeStruct((n_tok, 2 * LANES), F32)],
        compiler_params=params,
        name="in_proj_conv",
    )(x_ctx, x_lat, mod, norm1_g, w_all, b_all, conv_w[0], conv_b, w_conv_out[0].astype(BF16))

    c_new, n_new, m_new = _state_scan(k, v, gt, 0, nb, 1, None, False, True)
    def on_head_lanes(a):
        even = lax.broadcasted_iota(jnp.int32, (N_HD,) + (1,) * (a.ndim - 2), 0) % 2 == 0
        lo = jnp.pad(a, [(0, 0)] * (a.ndim - 1) + [(0, LANES - D_QK)])
        hi = jnp.pad(a, [(0, 0)] * (a.ndim - 1) + [(LANES - D_QK, 0)])
        return jnp.where(even, lo, hi)

    init = (on_head_lanes(jnp.swapaxes(state_C[:, 0].astype(F32).reshape(nd, N_HD, D_QK, D_V), 2, 3)),
            on_head_lanes(state_n[:, 0].astype(F32).reshape(nd, N_HD, D_QK)),
            jnp.pad(state_m[:, 0].astype(F32).reshape(nd, 1, N_HD), ((0, 0), (0, 0), (0, LANES - N_HD))))
    cbf, cbb, nbf, nbb, mbf, mbb = _state_scan(k, v, gt, n_ctx_tiles, nd, lat_chunks, init, True, False)

    lat_idx = lambda i: jnp.maximum(i - n_ctx_tiles, 0)
    st4 = pl.BlockSpec((1, N_HEADS, D_V, LANES), lambda i: (lat_idx(i), 0, 0, 0))
    st3 = pl.BlockSpec((1, N_HEADS, LANES), lambda i: (lat_idx(i), 0, 0))
    st1 = pl.BlockSpec((1, 1, LANES), lambda i: (lat_idx(i), 0, 0))
    rw = jnp.pad(router_w[0], ((0, 0), (0, LANES - n_experts)))
    rb = jnp.pad(router_b[0], (0, LANES - n_experts)).reshape(1, LANES)
    x1, h2, route, route_t, counts = pl.pallas_call(
        functools.partial(_mix_kernel, n_ctx_tiles=n_ctx_tiles, n_experts=n_experts),
        grid=(n_tiles,),
        in_specs=[tile(D_QKH), tile(D_QKH), tile(d), tile(d), tile(d), tile(d), tile(2 * LANES), ctx_spec, lat_spec,
                  pl.BlockSpec((1, 8, d), mod_row(TM)), st4, st4, st3, st3, st1, st1,
                  _const_spec((1, d)), _const_spec((d, d)), _const_spec((d, d)), _const_spec((1, d)),
                  _const_spec((d, LANES)), _const_spec((1, LANES))],
        out_specs=[tile(d), rows(TM), tile(LANES), pl.BlockSpec((8, TM), lambda i: (0, i)),
                   pl.BlockSpec((1, LANES), lambda i: (0, 0))],
        out_shape=[jax.ShapeDtypeStruct((n_tok, d), F32), jax.ShapeDtypeStruct((n_tok * ROW_SUB, LANES), F32),
                   jax.ShapeDtypeStruct((n_tok, LANES), F32), jax.ShapeDtypeStruct((8, n_tok), jnp.int32),
                   jax.ShapeDtypeStruct((1, LANES), F32)],
        scratch_shapes=[pltpu.VMEM((TM, d), BF16), pltpu.VMEM((1, LANES), F32)],
        compiler_params=params,
        name="mlstm_mix_router",
    )(q, k, v, so, sgb, ycg, gt, x_ctx, x_lat, mod, cbf, cbb, nbf, nbb, mbf, mbb,
      mh_norm_g, w_m_out[0].astype(BF16), w_o[0].astype(BF16), norm2_g, rw, rb)

    n_rows = TOP_K * n_tok
    offs, it_tile, it_exp, it_flags, it_lo, it_hi, it_nxt = _work_items(
        counts[0, :n_experts], n_experts, n_rows, TMX)
    is_exp = route_t[0:TOP_K, :, None] == jnp.arange(n_experts, dtype=jnp.int32)
    pos_t = jnp.sum(jnp.where(is_exp, offs[:n_experts], 0), axis=-1) + route_t[TOP_K:2 * TOP_K]
    pos = pos_t.T.reshape(n_rows)
    pos_spec = lambda tm, f: pl.BlockSpec((TOP_K * tm,), f, memory_space=pltpu.SMEM)
    any_spec = pl.BlockSpec(memory_space=pl.ANY)
    tmd = next(t for t in (TM_DISPATCH, TM) if n_tok % t == 0)
    tmc = next(t for t in (TM_COMBINE, TM) if t_ctx % t == 0 and dseq % t == 0)

    xs = pl.pallas_call(
        _dispatch_kernel,
        grid=(n_tok // tmd,),
        in_specs=[pos_spec(tmd, lambda i: (i,)), rows(tmd)],
        out_specs=any_spec,
        out_shape=jax.ShapeDtypeStruct((n_rows * ROW_SUB, LANES), F32),
        scratch_shapes=[pltpu.SemaphoreType.DMA],
        compiler_params=params,
        name="moe_dispatch",
    )(pos, h2)

    dff = w_gate.shape[-1]
    wspec = lambda a, b_: pl.BlockSpec((1, a, b_), lambda w, tl, ex, *_: (ex[w], 0, 0))
    ys = pl.pallas_call(
        _expert_kernel,
        grid_spec=pltpu.PrefetchScalarGridSpec(
            num_scalar_prefetch=6,
            grid=(it_tile.shape[0],),
            in_specs=[pl.BlockSpec((TMX * ROW_SUB, LANES), lambda w, tl, *_: (tl[w], 0)),
                      any_spec, wspec(1, dff), any_spec, wspec(1, dff), any_spec, wspec(1, d)],
            out_specs=pl.BlockSpec((TMX * ROW_SUB, LANES), lambda w, tl, *_: (tl[w], 0)),
            scratch_shapes=[pltpu.VMEM((3, d, dff), BF16), pltpu.VMEM((2, 3, d, dff), F32),
                            pltpu.SemaphoreType.DMA((2,))]),
        out_shape=jax.ShapeDtypeStruct((n_rows * ROW_SUB, LANES), F32),
        compiler_params=params,
        name="moe_experts",
    )(it_tile, it_exp, it_flags, it_lo, it_hi, it_nxt, xs, w_gate[0], b_gate[0].reshape(n_experts, 1, dff), w_lin[0],
      b_lin[0].reshape(n_experts, 1, dff), w_down[0], b_down[0].reshape(n_experts, 1, d))

    nc_ctx, nc_all = t_ctx // tmc, n_tok // tmc
    ctx_i = lambda i: (jnp.minimum(i, nc_ctx - 1), 0)
    lat_i = lambda i: (jnp.maximum(i - nc_ctx, 0), 0)
    ctile = lambda cols: pl.BlockSpec((tmc, cols), lambda i: (i, 0))
    y_prompt, y_sample = pl.pallas_call(
        functools.partial(_combine_kernel, n_ctx_tiles=nc_ctx),
        grid=(nc_all,),
        in_specs=[pos_spec(tmc, lambda i: (i,)), pos_spec(tmc, lambda i: (jnp.minimum(i + 1, nc_all - 1),)),
                  ctile(LANES), ctile(d), pl.BlockSpec((1, 8, d), mod_row(tmc)), _const_spec((1, d)), any_spec],
        out_specs=[pl.BlockSpec((tmc, d), ctx_i), pl.BlockSpec((tmc, d), lat_i)],
        out_shape=[jax.ShapeDtypeStruct((t_ctx, d), F32), jax.ShapeDtypeStruct((t_lat, d), F32)],
        scratch_shapes=[pltpu.VMEM((2, TOP_K, tmc * ROW_SUB, LANES), F32), pltpu.SemaphoreType.DMA((2,))],
        compiler_params=params,
        name="moe_combine",
    )(pos, pos, route, x1, mod, final_g.reshape(1, d), ys)

    y_prompt = y_prompt.reshape(nb, seq, d)
    y_sample = y_sample.reshape(nd, dseq, d)
    new_c = c_new.reshape(nb, 1, 2, N_HEADS, D_QK, D_V)
    odd_head = (lax.broadcasted_iota(jnp.int32, (1, N_HD, 1), 1) % 2) == 1
    new_n = jnp.where(odd_head, n_new[:, :, D_QK:], n_new[:, :, :D_QK]).reshape(nb, 1, 2, N_HEADS, D_QK)
    new_m = m_new[:, 0, :N_HD].reshape(nb, 1, 2, N_HEADS)
    return (y_prompt, y_sample, new_c, new_n, new_m)
```

```python
import functools

import jax
import jax.numpy as jnp
from jax import lax
from jax.experimental import pallas as pl
from jax.experimental.pallas import tpu as pltpu

F32 = jnp.float32
BF16 = jnp.bfloat16
HIGHEST = lax.Precision.HIGHEST

N_HEADS = 8
D_QK = 64
D_V = 128
GRID_W = 64
TOP_K = 4
SWIGLU_LIMIT = 7.0
SWIGLU_ALPHA = 1.702
EPS = 1e-6
N_MOD = 6

LANES = 128
ROW_SUB = 8
TM = 256
TMX = 512
TM_DISPATCH = 2048
TM_COMBINE = 512
ISSUE_UNROLL = 4
N_HD = 2 * N_HEADS
VMEM_LIMIT = 56 * 1024 * 1024

_D = 1024
C_CONV = (0, 3 * _D)
D_QKH = N_HEADS * D_QK
C_Q = (3 * _D, 3 * _D + D_QKH)
C_K = (C_Q[1], C_Q[1] + D_QKH)
C_V = (C_K[1], C_K[1] + _D)
C_O = (C_V[1], C_V[1] + _D)
C_GA = (C_O[1], C_O[1] + _D)
C_GB = (C_GA[1], C_GA[1] + _D)
C_G = (C_GB[1], C_GB[1] + 2 * LANES)
W_COLS = C_G[1]


def _head_lanes(h):
    lane = lax.broadcasted_iota(jnp.int32, (1, LANES), 1)
    return (lane < D_QK) if h % 2 == 0 else (lane >= D_QK)


def _dot(a, b, precision=None):
    return jnp.dot(a, b, preferred_element_type=F32, precision=precision)


def _rms(x):
    return x * lax.rsqrt(jnp.mean(x * x, axis=-1, keepdims=True) + EPS)


def _store_rows(ref, val):
    n = val.shape[0]
    for s in range(ROW_SUB):
        ref[pl.ds(s, n, stride=ROW_SUB), :] = val[:, s * LANES:(s + 1) * LANES]


def _load_rows(ref):
    n = ref.shape[0] // ROW_SUB
    return jnp.concatenate([ref[pl.ds(s, n, stride=ROW_SUB), :] for s in range(ROW_SUB)], axis=1)


def _row_tile(ref, r):
    return ref.at[pl.ds(pl.multiple_of(r * ROW_SUB, ROW_SUB), ROW_SUB)]


def _const_spec(shape):
    return pl.BlockSpec(shape, lambda *_: (0,) * len(shape), pipeline_mode=pl.Buffered(1))


def _mod_kernel(c_ref, w_ref, b_ref, o_ref):
    c = c_ref[...]
    o_ref[...] = _dot(c * jax.nn.sigmoid(c), w_ref[...], HIGHEST) + b_ref[...]


def _modulation(cvec, ada_w, ada_b):
    n, d = cvec.shape
    nout = ada_w.shape[1]
    return pl.pallas_call(
        _mod_kernel,
        grid=(nout // d,),
        in_specs=[pl.BlockSpec((n, d), lambda j: (0, 0)),
                  pl.BlockSpec((d, d), lambda j: (0, j)),
                  pl.BlockSpec((1, d), lambda j: (0, j))],
        out_specs=pl.BlockSpec((n, d), lambda j: (0, j)),
        out_shape=jax.ShapeDtypeStruct((n, nout), F32),
        name="adaln_mod",
    )(cvec, ada_w, ada_b.reshape(1, nout))


def _proj_kernel(xc_ref, xl_ref, mod_ref, g1_ref, w_ref, b_ref, cw_ref, cb_ref, wco_ref,
                 ycg_ref, sgb_ref, q_ref, k_ref, v_ref, so_ref, gt_ref, *, n_ctx_tiles, ctx_row):
    i = pl.program_id(0)
    x = jnp.where(i < n_ctx_tiles, xc_ref[...], xl_ref[...])
    h = (_rms(x) * g1_ref[...] * (1.0 + mod_ref[0, 1:2, :]) + mod_ref[0, 0:1, :]).astype(BF16)

    def proj(cols):
        return _dot(h, w_ref[:, cols[0]:cols[1]]) + b_ref[:, cols[0]:cols[1]]

    zc = proj(C_CONV)
    d = x.shape[1]
    u = zc[:, 2 * d:3 * d] * zc[:, 0:d]
    rowlen = jnp.where(i < n_ctx_tiles, ctx_row, GRID_W)
    pos = lax.broadcasted_iota(jnp.int32, (TM, 1), 0) & (rowlen - 1)
    u_prev = jnp.where(pos == 0, 0.0, pltpu.roll(u, 1, 0))
    u_next = jnp.where(pos == rowlen - 1, 0.0, pltpu.roll(u, TM - 1, 0))
    uc = u_prev * cw_ref[0:1, :] + u * cw_ref[1:2, :] + u_next * cw_ref[2:3, :] + cb_ref[...]
    yconv = _dot((zc[:, d:2 * d] * uc).astype(BF16), wco_ref[...])

    ycg_ref[...] = (jax.nn.sigmoid(proj(C_GA)) * yconv).astype(BF16)
    sgb_ref[...] = jax.nn.sigmoid(proj(C_GB)).astype(BF16)
    q_ref[...] = (proj(C_Q) * (D_QK ** -0.5)).astype(BF16)
    k_ref[...] = proj(C_K).astype(BF16)
    v_ref[...] = proj(C_V).astype(BF16)
    so_ref[...] = jax.nn.sigmoid(proj(C_O)).astype(BF16)
    gt_ref[...] = proj(C_G)


def _tri_masks():
    row = lax.broadcasted_iota(jnp.int32, (TM, TM), 0)
    col = lax.broadcasted_iota(jnp.int32, (TM, TM), 1)
    return row >= col, row <= col


def _gate_cumsums(gf):
    lower, upper = _tri_masks()
    lf = jax.nn.log_sigmoid(gf)
    fwd_lane = lax.broadcasted_iota(jnp.int32, (1, LANES), 1) < N_HEADS
    bsum = jnp.where(fwd_lane, _dot(lower.astype(F32), lf, HIGHEST), _dot(upper.astype(F32), lf, HIGHEST))
    return lf, bsum


def _state_kernel(*refs, zero_init, emit_before, emit_after):
    it = iter(refs)
    kf_ref, vf_ref, gf_ref, kb_ref, vb_ref, gb_ref = (next(it) for _ in range(6))
    if not zero_init:
        c0_ref, n0_ref, m0_ref = (next(it) for _ in range(3))
    if emit_before:
        cbf_ref, cbb_ref, nbf_ref, nbb_ref, mbf_ref, mbb_ref = (next(it) for _ in range(6))
    if emit_after:
        ca_ref, na_ref, ma_ref = (next(it) for _ in range(3))
    c_scr, n_scr, m_scr = (next(it) for _ in range(3))
    c = pl.program_id(1)

    @pl.when(c == 0)
    def _():
        if zero_init:
            c_scr[...] = jnp.zeros_like(c_scr)
            n_scr[...] = jnp.zeros_like(n_scr)
            m_scr[...] = jnp.zeros_like(m_scr)
        else:
            c_scr[...] = c0_ref[0]
            n_scr[...] = n0_ref[0]
            m_scr[...] = m0_ref[0]

    if emit_before:
        cbf_ref[0] = c_scr[0:N_HEADS]
        cbb_ref[0] = c_scr[N_HEADS:N_HD]
        nbf_ref[0] = n_scr[0:N_HEADS]
        nbb_ref[0] = n_scr[N_HEADS:N_HD]
        mbf_ref[0] = m_scr[...]
        mbb_ref[0] = m_scr[...]

    fwd_lane = lax.broadcasted_iota(jnp.int32, (1, LANES), 1) < N_HEADS
    gi = jnp.where(fwd_lane, gf_ref[:, 0:LANES], gb_ref[:, 0:LANES])
    gfg = jnp.where(fwd_lane, gf_ref[:, LANES:2 * LANES], gb_ref[:, LANES:2 * LANES])
    lf, bsum = _gate_cumsums(gfg)
    total = jnp.sum(lf, axis=0, keepdims=True)
    g = total - bsum + gi
    m_prev = m_scr[...]
    m_new = jnp.maximum(total + m_prev, jnp.max(g, axis=0, keepdims=True))
    wk = jnp.exp(g - m_new)
    decay = jnp.exp(total + m_prev - m_new)
    for hd in range(N_HD):
        h = hd % N_HEADS
        k_ref, v_ref = (kf_ref, vf_ref) if hd < N_HEADS else (kb_ref, vb_ref)
        k_tile = k_ref[:, (h // 2) * LANES:(h // 2 + 1) * LANES].astype(F32)
        wkk = wk[:, hd:hd + 1] * jnp.where(_head_lanes(h), k_tile, 0.0)
        vh = v_ref[:, h * D_V:(h + 1) * D_V]
        dec = decay[:, hd:hd + 1]
        tn = (((0,), (0,)), ((), ()))
        if emit_before:
            c_scr[hd] = dec * c_scr[hd] + lax.dot_general(vh, wkk.astype(BF16), tn, preferred_element_type=F32)
        else:
            upd = lax.dot_general(wkk.astype(BF16), vh, tn, preferred_element_type=F32)
            c_scr[hd] = dec * c_scr[hd] + upd[(h % 2) * D_QK:(h % 2 + 1) * D_QK, :]
        n_scr[hd:hd + 1, :] = dec * n_scr[hd:hd + 1, :] + jnp.sum(wkk, axis=0, keepdims=True)
    m_scr[...] = m_new

    if emit_after:
        @pl.when(c == pl.num_programs(1) - 1)
        def _():
            ca_ref[0] = c_scr[...]
            na_ref[0] = n_scr[...]
            ma_ref[0] = m_scr[...]


def _state_scan(k, v, gt, tile0, n_seq, n_chunk, init, emit_before, emit_after):
    d = k.shape[1]
    cshape = (D_V, LANES) if emit_before else (D_QK, D_V)
    fwd = lambda s, c: (tile0 + s * n_chunk + c, 0)
    bwd = lambda s, c: (tile0 + s * n_chunk + n_chunk - 1 - c, 0)
    dv = v.shape[1]
    in_specs = [pl.BlockSpec((TM, d), fwd), pl.BlockSpec((TM, dv), fwd), pl.BlockSpec((TM, 2 * LANES), fwd),
                pl.BlockSpec((TM, d), bwd), pl.BlockSpec((TM, dv), bwd), pl.BlockSpec((TM, 2 * LANES), bwd)]
    args = [k, v, gt, k, v, gt]
    if init is not None:
        in_specs += [pl.BlockSpec((1, N_HD) + cshape, lambda s, c: (s, 0, 0, 0)),
                     pl.BlockSpec((1, N_HD, LANES), lambda s, c: (s, 0, 0)),
                     pl.BlockSpec((1, 1, LANES), lambda s, c: (s, 0, 0))]
        args += list(init)
    out_specs, out_shape = [], []
    n_tot = n_seq * n_chunk
    if emit_before:
        cf = lambda s, c: (s * n_chunk + c, 0, 0, 0)
        cb = lambda s, c: (s * n_chunk + n_chunk - 1 - c, 0, 0, 0)
        nf = lambda s, c: (s * n_chunk + c, 0, 0)
        nb = lambda s, c: (s * n_chunk + n_chunk - 1 - c, 0, 0)
        out_specs += [pl.BlockSpec((1, N_HEADS) + cshape, cf), pl.BlockSpec((1, N_HEADS) + cshape, cb),
                      pl.BlockSpec((1, N_HEADS, LANES), nf), pl.BlockSpec((1, N_HEADS, LANES), nb),
                      pl.BlockSpec((1, 1, LANES), nf), pl.BlockSpec((1, 1, LANES), nb)]
        out_shape += [jax.ShapeDtypeStruct((n_tot, N_HEADS) + cshape, F32)] * 2
        out_shape += [jax.ShapeDtypeStruct((n_tot, N_HEADS, LANES), F32)] * 2
        out_shape += [jax.ShapeDtypeStruct((n_tot, 1, LANES), F32)] * 2
    if emit_after:
        out_specs += [pl.BlockSpec((1, N_HD, D_QK, D_V), lambda s, c: (s, 0, 0, 0)),
                      pl.BlockSpec((1, N_HD, LANES), lambda s, c: (s, 0, 0)),
                      pl.BlockSpec((1, 1, LANES), lambda s, c: (s, 0, 0))]
        out_shape += [jax.ShapeDtypeStruct((n_seq, N_HD, D_QK, D_V), F32),
                      jax.ShapeDtypeStruct((n_seq, N_HD, LANES), F32),
                      jax.ShapeDtypeStruct((n_seq, 1, LANES), F32)]
    return pl.pallas_call(
        functools.partial(_state_kernel, zero_init=init is None, emit_before=emit_before, emit_after=emit_after),
        grid=(n_seq, n_chunk),
        in_specs=in_specs,
        out_specs=out_specs,
        out_shape=out_shape,
        scratch_shapes=[pltpu.VMEM((N_HD,) + cshape, F32), pltpu.VMEM((N_HD, LANES), F32),
                        pltpu.VMEM((1, LANES), F32)],
        compiler_params=pltpu.CompilerParams(dimension_semantics=("arbitrary", "arbitrary"),
                                             vmem_limit_bytes=VMEM_LIMIT),
        name="mlstm_state_scan",
    )(*args)


def _mix_kernel(q_ref, k_ref, v_ref, so_ref, sgb_ref, ycg_ref, gt_ref, xc_ref, xl_ref, mod_ref,
                cbf_ref, cbb_ref, nbf_ref, nbb_ref, mbf_ref, mbb_ref,
                gmh_ref, wmo_ref, wo_ref, g2_ref, rw_ref, rb_ref,
                x1_ref, h2_ref, route_ref, routet_ref, cnt_ref, hm_scr, carry_scr, *, n_ctx_tiles, n_experts):
    i = pl.program_id(0)
    is_lat = i >= n_ctx_tiles
    lat_f = is_lat.astype(F32)
    lower, upper = _tri_masks()
    gt_t = gt_ref[...].T
    gi_t = gt_t[0:N_HD, :]
    lf_t = jax.nn.log_sigmoid(gt_t[LANES:LANES + N_HD, :])
    fwd_row = lax.broadcasted_iota(jnp.int32, (N_HD, 1), 0) < N_HEADS
    bsum_t = jnp.where(fwd_row, _dot(lf_t, upper.astype(F32), HIGHEST), _dot(lf_t, lower.astype(F32), HIGHEST))
    a_t = gi_t - bsum_t
    m_row = jnp.where(lax.broadcasted_iota(jnp.int32, (1, LANES), 1) < N_HEADS, mbf_ref[0], mbb_ref[0]) * lat_f
    m_sq = jnp.where(lax.broadcasted_iota(jnp.int32, (LANES, 1), 0) == 0, m_row, 0.0)
    m_prev = m_sq.T[0:N_HD, 0:1]
    a_c = jnp.concatenate([a_t, jnp.zeros((LANES - N_HD, TM), F32)], axis=0).T
    row0 = lax.broadcasted_iota(jnp.int32, (LANES, 1), 0) == 0
    ones_rows = jnp.where(row0, 1.0, 0.0).astype(BF16) * jnp.ones((1, TM), BF16)

    for h in range(N_HEADS):
        hs = slice(h * D_V, (h + 1) * D_V)
        pair = slice((h // 2) * LANES, (h // 2 + 1) * LANES)
        qh = jnp.where(_head_lanes(h), q_ref[:, pair], jnp.zeros((), BF16))
        kh = k_ref[:, pair]
        vext_t = jnp.concatenate([v_ref[:, hs].T, ones_rows], axis=0)
        kq = lax.dot_general(kh, qh, (((1,), (1,)), ((), ())), preferred_element_type=F32)
        hsum = None
        for d in range(2):
            hd = d * N_HEADS + h
            mask = upper if d == 0 else lower
            c_t = (cbf_ref if d == 0 else cbb_ref)[0, h] * lat_f
            n_r = (nbf_ref if d == 0 else nbb_ref)[0, h:h + 1, :] * lat_f
            a_b = jnp.where(mask, a_c[:, hd:hd + 1], -jnp.inf)
            mrow = jnp.maximum(m_prev[hd:hd + 1, :], jnp.max(a_b, axis=0, keepdims=True))
            e = jnp.exp(a_b - mrow)
            nd = _dot(vext_t, (kq * e).astype(BF16))
            cext_t = jnp.concatenate([c_t, jnp.where(row0, n_r, 0.0)], axis=0).astype(BF16)
            qc = lax.dot_general(cext_t, qh, (((1,), (1,)), ((), ())), preferred_element_type=F32)
            wi = jnp.exp(m_prev[hd:hd + 1, :] - mrow)
            num = nd[0:D_V, :] + wi * qc[0:D_V, :]
            den = nd[D_V:D_V + 1, :] + wi * qc[D_V:D_V + 1, :]
            r = 1.0 / jnp.maximum(jnp.abs(den), jnp.exp(-(bsum_t[hd:hd + 1, :] + mrow)))
            hsum = num * r if hsum is None else hsum + num * r
        hn = hsum * lax.rsqrt(jnp.mean(hsum * hsum, axis=0, keepdims=True) + EPS)
        hm_scr[:, hs] = (hn.T * gmh_ref[:, hs] * so_ref[:, hs].astype(F32)).astype(BF16)

    ym = _dot(hm_scr[...], wmo_ref[...])
    mix = (ycg_ref[...].astype(F32) + sgb_ref[...].astype(F32) * ym).astype(BF16)
    x1 = jnp.where(is_lat, xl_ref[...], xc_ref[...]) + mod_ref[0, 2:3, :] * _dot(mix, wo_ref[...])
    x1_ref[...] = x1
    h2 = _rms(x1) * g2_ref[...] * (1.0 + mod_ref[0, 4:5, :]) + mod_ref[0, 3:4, :]
    _store_rows(h2_ref, h2)

    lane = lax.broadcasted_iota(jnp.int32, (TM, LANES), 1)
    rw = rw_ref[...]
    h2_hi, rw_hi = h2.astype(BF16), rw.astype(BF16)
    h2_lo, rw_lo = (h2 - h2_hi.astype(F32)).astype(BF16), (rw - rw_hi.astype(F32)).astype(BF16)
    logits = _dot(h2_hi, rw_hi) + (_dot(h2_hi, rw_lo) + _dot(h2_lo, rw_hi)) + rb_ref[...]
    work = jnp.where(lane < n_experts, logits, -jnp.inf)
    sels, exps, idxs = [], [], []
    top = None
    for _ in range(TOP_K):
        mx = jnp.max(work, axis=-1, keepdims=True)
        ix = jnp.min(jnp.where(work == mx, lane, LANES), axis=-1, keepdims=True)
        sel = lane == ix
        work = jnp.where(sel, -jnp.inf, work)
        top = mx if top is None else top
        sels.append(sel)
        idxs.append(ix.astype(F32))
        exps.append(jnp.exp(mx - top))
    inv = 1.0 / functools.reduce(lambda p, q: p + q, exps)

    @pl.when(i == 0)
    def _():
        carry_scr[...] = jnp.zeros_like(carry_scr)

    onehot = functools.reduce(lambda p, q: p + q, [jnp.where(s, 1.0, 0.0) for s in sels])
    row = lax.broadcasted_iota(jnp.int32, (TM, TM), 0)
    col = lax.broadcasted_iota(jnp.int32, (TM, TM), 1)
    before = _dot((row > col).astype(BF16), onehot.astype(BF16)) + carry_scr[...]
    carry_scr[...] += jnp.sum(onehot, axis=0, keepdims=True)
    cnt_ref[...] = carry_scr[...]
    route = jnp.zeros((TM, LANES), F32)
    for j in range(TOP_K):
        slot = jnp.sum(jnp.where(sels[j], before, 0.0), axis=-1, keepdims=True)
        route = jnp.where(lane == j, idxs[j], route)
        route = jnp.where(lane == TOP_K + j, slot, route)
        route = jnp.where(lane == 2 * TOP_K + j, exps[j] * inv, route)
    route_ref[...] = route
    routet_ref[...] = route.T[0:8, :].astype(jnp.int32)


def _dispatch_kernel(pos_ref, h2_ref, xs_ref, sem):
    def copy(t, j):
        return pltpu.make_async_copy(_row_tile(h2_ref, t), _row_tile(xs_ref, pos_ref[TOP_K * t + j]), sem)

    def start(t, carry):
        for j in range(TOP_K):
            copy(t, j).start(priority=j % 2)
        return carry

    tm = h2_ref.shape[0] // ROW_SUB
    lax.fori_loop(0, tm, start, 0, unroll=ISSUE_UNROLL)
    for j in range(TOP_K):
        pltpu.make_async_copy(h2_ref, xs_ref.at[pl.ds(0, tm * ROW_SUB)], sem).wait()


def _expert_kernel(tile_ref, exp_ref, flag_ref, lo_ref, hi_ref, nxt_ref,
                   xs_ref, wg_hbm, bg_ref, wl_hbm, bl_ref, wd_hbm, bd_ref, ys_ref, w_scr, wf_scr, wsem):
    w = pl.program_id(0)
    flags = flag_ref[w]

    def fetch(e, slot):
        return [pltpu.make_async_copy(hbm.at[e], wf_scr.at[slot, m], wsem.at[slot])
                for m, hbm in enumerate((wg_hbm, wl_hbm, wd_hbm))]

    @pl.when(w == 0)
    def _():
        for cp in fetch(exp_ref[0], 0):
            cp.start()

    @pl.when((flags & 4) != 0)
    def _():
        slot = (flags >> 4) & 1
        for cp in fetch(exp_ref[w], slot):
            cp.wait()
        for m in range(3):
            w_scr[m] = wf_scr[slot, m].astype(BF16)

        @pl.when((flags & 8) != 0)
        def _():
            for cp in fetch(nxt_ref[w], 1 - slot):
                cp.start()

    @pl.when((flags & 1) != 0)
    def _():
        x = _load_rows(xs_ref).astype(BF16)
        gt = jnp.minimum(_dot(x, w_scr[0]) + bg_ref[0], SWIGLU_LIMIT)
        lin = jnp.clip(_dot(x, w_scr[1]) + bl_ref[0], -SWIGLU_LIMIT, SWIGLU_LIMIT)
        act = gt * jax.nn.sigmoid(SWIGLU_ALPHA * gt) * (lin + 1.0)
        y = _dot(act.astype(BF16), w_scr[2]) + bd_ref[0]
        rows = lax.broadcasted_iota(jnp.int32, (y.shape[0], 1), 0)
        mine = (rows >= lo_ref[w]) & (rows < hi_ref[w])

        @pl.when((flags & 2) != 0)
        def _():
            _store_rows(ys_ref, jnp.where(mine, y, 0.0))

        @pl.when((flags & 2) == 0)
        def _():
            _store_rows(ys_ref, jnp.where(mine, y, _load_rows(ys_ref)))


def _combine_kernel(pos_ref, posn_ref, route_ref, x1_ref, mod_ref, fg_ref, ys_ref, outc_ref, outl_ref,
                    buf, sem, *, n_ctx_tiles):
    i = pl.program_id(0)
    n = pl.num_programs(0)
    tm = x1_ref.shape[0]

    def copy(p_ref, slot, t, j):
        return pltpu.make_async_copy(_row_tile(ys_ref, p_ref[TOP_K * t + j]), _row_tile(buf.at[slot, j], t), sem.at[slot])

    def start_all(p_ref, slot):
        def body(t, carry):
            for j in range(TOP_K):
                copy(p_ref, slot, t, j).start(priority=j % 2)
            return carry
        lax.fori_loop(0, tm, body, 0, unroll=ISSUE_UNROLL)

    @pl.when(i == 0)
    def _():
        start_all(pos_ref, 0)

    @pl.when(i + 1 < n)
    def _():
        start_all(posn_ref, (i + 1) % 2)

    slot = i % 2

    for j in range(TOP_K):
        pltpu.make_async_copy(ys_ref.at[pl.ds(0, tm * ROW_SUB)], buf.at[slot, j], sem.at[slot]).wait()
    acc = None
    for j in range(TOP_K):
        term = route_ref[:, 2 * TOP_K + j:2 * TOP_K + j + 1] * _load_rows(buf.at[slot, j])
        acc = term if acc is None else acc + term
    out = _rms(x1_ref[...] + mod_ref[0, 5:6, :] * acc) * fg_ref[...]

    @pl.when(i < n_ctx_tiles)
    def _():
        outc_ref[...] = out

    @pl.when(i >= n_ctx_tiles)
    def _():
        outl_ref[...] = out


def _work_items(counts, n_experts, n_rows, tmx):
    n_items_max = n_rows // tmx + n_experts - 1
    cnt = counts.astype(jnp.int32)
    offs = jnp.concatenate([jnp.zeros((1,), jnp.int32), jnp.cumsum(cnt)])
    first_tile = offs[:-1] // tmx
    n_it = jnp.where(cnt > 0, (offs[1:] - 1) // tmx - first_tile + 1, 0)
    it_start = jnp.concatenate([jnp.zeros((1,), jnp.int32), jnp.cumsum(n_it)])
    total = it_start[-1]
    w = jnp.arange(n_items_max, dtype=jnp.int32)
    wc = jnp.minimum(w, total - 1)
    e = jnp.sum((it_start[None, 1:] <= wc[:, None]).astype(jnp.int32), axis=1)
    e = jnp.minimum(e, n_experts - 1)
    is_e = e[:, None] == jnp.arange(n_experts, dtype=jnp.int32)
    at_e = lambda a: jnp.sum(jnp.where(is_e, a[None, :], 0), axis=1)
    tile = at_e(first_tile) + wc - at_e(it_start[:-1])
    valid = w < total
    prev = lambda a: jnp.concatenate([jnp.full((1,), -1, jnp.int32), a[:-1]])
    ids = jnp.arange(n_experts, dtype=jnp.int32)
    later = (ids[None, :] > ids[:, None]) & (cnt > 0)[None, :]
    nxt = jnp.min(jnp.where(later, ids[None, :], n_experts), axis=1)
    has_next = nxt < n_experts
    slot = (jnp.cumsum((cnt > 0).astype(jnp.int32)) - 1) & 1
    flags = (valid.astype(jnp.int32) + 2 * (valid & (tile != prev(tile))).astype(jnp.int32)
             + 4 * (valid & (e != prev(e))).astype(jnp.int32)
             + 8 * at_e(has_next.astype(jnp.int32)) + 16 * at_e(slot))
    lo = jnp.clip(at_e(offs[:-1]) - tile * tmx, 0, tmx)
    hi = jnp.clip(at_e(offs[1:]) - tile * tmx, 0, tmx)
    return offs, tile, e, flags, lo, hi, at_e(jnp.minimum(nxt, n_experts - 1))


def _pack_in_proj(w, b):
    d = w.shape[0]
    o_g = 5 * d + 2 * D_QKH
    o_ga, o_gb = o_g + 4 * N_HEADS, o_g + 4 * N_HEADS + d

    def gates(m):
        gz = jnp.zeros((m.shape[0], LANES - N_HD), m.dtype)
        i_f, f_f, i_b, f_b = (m[:, o_g + j * N_HEADS:o_g + (j + 1) * N_HEADS] for j in range(4))
        return jnp.concatenate([i_f, i_b, gz, f_f, f_b, gz], axis=1)

    def pack(m):
        return jnp.concatenate([m[:, 0:o_g], m[:, o_ga:o_gb], m[:, o_gb:o_gb + d], gates(m)], axis=1)

    return pack(w.astype(BF16)), pack(b.reshape(1, -1))


def kernel(x_prompt, x_sample, c, state_C, state_n, state_m, c_ctx, ada_w, ada_b, norm1_g, norm2_g, w_in, b_in,
           conv_w, conv_b, w_conv_out, mh_norm_g, w_m_out, w_o, router_w, router_b, w_gate, b_gate, w_lin, b_lin,
           w_down, b_down, final_g):
    nb, seq, d = x_prompt.shape
    nd, dseq, _ = x_sample.shape
    n_experts = w_gate.shape[1]
    assert d == _D and w_in.shape[0] == 1 and seq == TM and dseq % TM == 0 and TM % GRID_W == 0
    assert w_gate.shape[-1] == d
    t_ctx, t_lat = nb * seq, nd * dseq
    n_tok = t_ctx + t_lat
    n_ctx_tiles, n_tiles = t_ctx // TM, n_tok // TM
    lat_chunks = dseq // TM
    n_lat_tiles = n_tiles - n_ctx_tiles

    n_c = 1 + nd
    n_cp = -(-n_c // 8) * 8
    cvec = jnp.concatenate([c_ctx[None, :], c, jnp.zeros((n_cp - n_c, d), F32)], axis=0)
    mod = _modulation(cvec, ada_w[0], ada_b[0]).reshape(n_cp, N_MOD, d)
    mod = jnp.pad(mod, ((0, 0), (0, 8 - N_MOD), (0, 0)))

    def mod_row(tile_tokens):
        ctx_t, per_seq = t_ctx // tile_tokens, dseq // tile_tokens
        return lambda i, *_: (jnp.where(i < ctx_t, 0, 1 + (i - ctx_t) // per_seq), 0, 0)

    x_ctx, x_lat = x_prompt.reshape(t_ctx, d), x_sample.reshape(t_lat, d)
    ctx_spec = pl.BlockSpec((TM, d), lambda i: (jnp.minimum(i, n_ctx_tiles - 1), 0))
    lat_spec = pl.BlockSpec((TM, d), lambda i: (jnp.maximum(i - n_ctx_tiles, 0), 0))
    w_all, b_all = _pack_in_proj(w_in[0], b_in[0])
    tile = lambda cols: pl.BlockSpec((TM, cols), lambda i: (i, 0))
    rows = lambda n: pl.BlockSpec((n * ROW_SUB, LANES), lambda i: (i, 0))
    params = pltpu.CompilerParams(dimension_semantics=("arbitrary",), vmem_limit_bytes=VMEM_LIMIT)
    bf = lambda cols: jax.ShapeDtypeStruct((n_tok, cols), BF16)

    ycg, sgb, q, k, v, so, gt = pl.pallas_call(
        functools.partial(_proj_kernel, n_ctx_tiles=n_ctx_tiles, ctx_row=seq),
        grid=(n_tiles,),
        in_specs=[ctx_spec, lat_spec, pl.BlockSpec((1, 8, d), mod_row(TM)), _const_spec((1, d)),
                  _const_spec((d, W_COLS)), _const_spec((1, W_COLS)), _const_spec((3, d)), _const_spec((1, d)),
                  _const_spec((d, d))],
        out_specs=[tile(d), tile(d), tile(D_QKH), tile(D_QKH), tile(d), tile(d), tile(2 * LANES)],
        out_shape=[bf(d), bf(d), bf(D_QKH), bf(D_QKH), bf(d), bf(d),
                   jax.ShapeDtypeStruct((n_tok, 2 * LANES), F32)],
        compiler_params=params,
        name="in_proj_conv",
    )(x_ctx, x_lat, mod, norm1_g, w_all, b_all, conv_w[0], conv_b, w_conv_out[0].astype(BF16))

    c_new, n_new, m_new = _state_scan(k, v, gt, 0, nb, 1, None, False, True)
    def on_head_lanes(a):
        even = lax.broadcasted_iota(jnp.int32, (N_HD,) + (1,) * (a.ndim - 2), 0) % 2 == 0
        lo = jnp.pad(a, [(0, 0)] * (a.ndim - 1) + [(0, LANES - D_QK)])
        hi = jnp.pad(a, [(0, 0)] * (a.ndim - 1) + [(LANES - D_QK, 0)])
        return jnp.where(even, lo, hi)

    init = (on_head_lanes(jnp.swapaxes(state_C[:, 0].astype(F32).reshape(nd, N_HD, D_QK, D_V), 2, 3)),
            on_head_lanes(state_n[:, 0].astype(F32).reshape(nd, N_HD, D_QK)),
            jnp.pad(state_m[:, 0].astype(F32).reshape(nd, 1, N_HD), ((0, 0), (0, 0), (0, LANES - N_HD))))
    cbf, cbb, nbf, nbb, mbf, mbb = _state_scan(k, v, gt, n_ctx_tiles, nd, lat_chunks, init, True, False)

    lat_idx = lambda i: jnp.maximum(i - n_ctx_tiles, 0)
    st4 = pl.BlockSpec((1, N_HEADS, D_V, LANES), lambda i: (lat_idx(i), 0, 0, 0))
    st3 = pl.BlockSpec((1, N_HEADS, LANES), lambda i: (lat_idx(i), 0, 0))
    st1 = pl.BlockSpec((1, 1, LANES), lambda i: (lat_idx(i), 0, 0))
    rw = jnp.pad(router_w[0], ((0, 0), (0, LANES - n_experts)))
    rb = jnp.pad(router_b[0], (0, LANES - n_experts)).reshape(1, LANES)
    x1, h2, route, route_t, counts = pl.pallas_call(
        functools.partial(_mix_kernel, n_ctx_tiles=n_ctx_tiles, n_experts=n_experts),
        grid=(n_tiles,),
        in_specs=[tile(D_QKH), tile(D_QKH), tile(d), tile(d), tile(d), tile(d), tile(2 * LANES), ctx_spec, lat_spec,
                  pl.BlockSpec((1, 8, d), mod_row(TM)), st4, st4, st3, st3, st1, st1,
                  _const_spec((1, d)), _const_spec((d, d)), _const_spec((d, d)), _const_spec((1, d)),
                  _const_spec((d, LANES)), _const_spec((1, LANES))],
        out_specs=[tile(d), rows(TM), tile(LANES), pl.BlockSpec((8, TM), lambda i: (0, i)),
                   pl.BlockSpec((1, LANES), lambda i: (0, 0))],
        out_shape=[jax.ShapeDtypeStruct((n_tok, d), F32), jax.ShapeDtypeStruct((n_tok * ROW_SUB, LANES), F32),
                   jax.ShapeDtypeStruct((n_tok, LANES), F32), jax.ShapeDtypeStruct((8, n_tok), jnp.int32),
                   jax.ShapeDtypeStruct((1, LANES), F32)],
        scratch_shapes=[pltpu.VMEM((TM, d), BF16), pltpu.VMEM((1, LANES), F32)],
        compiler_params=params,
        name="mlstm_mix_router",
    )(q, k, v, so, sgb, ycg, gt, x_ctx, x_lat, mod, cbf, cbb, nbf, nbb, mbf, mbb,
      mh_norm_g, w_m_out[0].astype(BF16), w_o[0].astype(BF16), norm2_g, rw, rb)

    n_rows = TOP_K * n_tok
    offs, it_tile, it_exp, it_flags, it_lo, it_hi, it_nxt = _work_items(
        counts[0, :n_experts], n_experts, n_rows, TMX)
    is_exp = route_t[0:TOP_K, :, None] == jnp.arange(n_experts, dtype=jnp.int32)
    pos_t = jnp.sum(jnp.where(is_exp, offs[:n_experts], 0), axis=-1) + route_t[TOP_K:2 * TOP_K]
    pos = pos_t.T.reshape(n_rows)
    pos_spec = lambda tm, f: pl.BlockSpec((TOP_K * tm,), f, memory_space=pltpu.SMEM)
    any_spec = pl.BlockSpec(memory_space=pl.ANY)
    tmd = next(t for t in (TM_DISPATCH, TM) if n_tok % t == 0)
    tmc = next(t for t in (TM_COMBINE, TM) if t_ctx % t == 0 and dseq % t == 0)

    xs = pl.pallas_call(
        _dispatch_kernel,
        grid=(n_tok // tmd,),
        in_specs=[pos_spec(tmd, lambda i: (i,)), rows(tmd)],
        out_specs=any_spec,
        out_shape=jax.ShapeDtypeStruct((n_rows * ROW_SUB, LANES), F32),
        scratch_shapes=[pltpu.SemaphoreType.DMA],
        compiler_params=params,
        name="moe_dispatch",
    )(pos, h2)

    dff = w_gate.shape[-1]
    wspec = lambda a, b_: pl.BlockSpec((1, a, b_), lambda w, tl, ex, *_: (ex[w], 0, 0))
    ys = pl.pallas_call(
        _expert_kernel,
        grid_spec=pltpu.PrefetchScalarGridSpec(
            num_scalar_prefetch=6,
            grid=(it_tile.shape[0],),
            in_specs=[pl.BlockSpec((TMX * ROW_SUB, LANES), lambda w, tl, *_: (tl[w], 0)),
                      any_spec, wspec(1, dff), any_spec, wspec(1, dff), any_spec, wspec(1, d)],
            out_specs=pl.BlockSpec((TMX * ROW_SUB, LANES), lambda w, tl, *_: (tl[w], 0)),
            scratch_shapes=[pltpu.VMEM((3, d, dff), BF16), pltpu.VMEM((2, 3, d, dff), F32),
                            pltpu.SemaphoreType.DMA((2,))]),
        out_shape=jax.ShapeDtypeStruct((n_rows * ROW_SUB, LANES), F32),
        compiler_params=params,
        name="moe_experts",
    )(it_tile, it_exp, it_flags, it_lo, it_hi, it_nxt, xs, w_gate[0], b_gate[0].reshape(n_experts, 1, dff), w_lin[0],
      b_lin[0].reshape(n_experts, 1, dff), w_down[0], b_down[0].reshape(n_experts, 1, d))

    nc_ctx, nc_all = t_ctx // tmc, n_tok // tmc
    ctx_i = lambda i: (jnp.minimum(i, nc_ctx - 1), 0)
    lat_i = lambda i: (jnp.maximum(i - nc_ctx, 0), 0)
    ctile = lambda cols: pl.BlockSpec((tmc, cols), lambda i: (i, 0))
    y_prompt, y_sample = pl.pallas_call(
        functools.partial(_combine_kernel, n_ctx_tiles=nc_ctx),
        grid=(nc_all,),
        in_specs=[pos_spec(tmc, lambda i: (i,)), pos_spec(tmc, lambda i: (jnp.minimum(i + 1, nc_all - 1),)),
                  ctile(LANES), ctile(d), pl.BlockSpec((1, 8, d), mod_row(tmc)), _const_spec((1, d)), any_spec],
        out_specs=[pl.BlockSpec((tmc, d), ctx_i), pl.BlockSpec((tmc, d), lat_i)],
        out_shape=[jax.ShapeDtypeStruct((t_ctx, d), F32), jax.ShapeDtypeStruct((t_lat, d), F32)],
        scratch_shapes=[pltpu.VMEM((2, TOP_K, tmc * ROW_SUB, LANES), F32), pltpu.SemaphoreType.DMA((2,))],
        compiler_params=params,
        name="moe_combine",
    )(pos, pos, route, x1, mod, final_g.reshape(1, d), ys)

    y_prompt = y_prompt.reshape(nb, seq, d)
    y_sample = y_sample.reshape(nd, dseq, d)
    new_c = c_new.reshape(nb, 1, 2, N_HEADS, D_QK, D_V)
    odd_head = (lax.broadcasted_iota(jnp.int32, (1, N_HD, 1), 1) % 2) == 1
    new_n = jnp.where(odd_head, n_new[:, :, D_QK:], n_new[:, :, :D_QK]).reshape(nb, 1, 2, N_HEADS, D_QK)
    new_m = m_new[:, 0, :N_HD].reshape(nb, 1, 2, N_HEADS)
    return (y_prompt, y_sample, new_c, new_n, new_m)
```

```python
import functools

import jax
import jax.numpy as jnp
from jax import lax
from jax.experimental import pallas as pl
from jax.experimental.pallas import tpu as pltpu

F32 = jnp.float32
BF16 = jnp.bfloat16
HIGHEST = lax.Precision.HIGHEST

N_HEADS = 8
D_QK = 64
D_V = 128
GRID_W = 64
TOP_K = 4
SWIGLU_LIMIT = 7.0
SWIGLU_ALPHA = 1.702
EPS = 1e-6
N_MOD = 6

LANES = 128
ROW_SUB = 8
TM = 256
TMX = 512
TM_DISPATCH = 2048
TM_COMBINE = 512
ISSUE_UNROLL = 4
N_HD = 2 * N_HEADS
VMEM_LIMIT = 56 * 1024 * 1024

_D = 1024
C_CONV = (0, 3 * _D)
D_QKH = N_HEADS * D_QK
C_Q = (3 * _D, 3 * _D + D_QKH)
C_K = (C_Q[1], C_Q[1] + D_QKH)
C_V = (C_K[1], C_K[1] + _D)
C_O = (C_V[1], C_V[1] + _D)
C_GA = (C_O[1], C_O[1] + _D)
C_GB = (C_GA[1], C_GA[1] + _D)
C_G = (C_GB[1], C_GB[1] + 2 * LANES)
W_COLS = C_G[1]


def _head_lanes(h):
    lane = lax.broadcasted_iota(jnp.int32, (1, LANES), 1)
    return (lane < D_QK) if h % 2 == 0 else (lane >= D_QK)


def _dot(a, b, precision=None):
    return jnp.dot(a, b, preferred_element_type=F32, precision=precision)


def _rms(x):
    return x * lax.rsqrt(jnp.mean(x * x, axis=-1, keepdims=True) + EPS)


def _store_rows(ref, val, r0=0):
    n = val.shape[0]
    for s in range(ROW_SUB):
        ref[pl.ds(r0 * ROW_SUB + s, n, stride=ROW_SUB), :] = val[:, s * LANES:(s + 1) * LANES]


def _load_rows(ref, r0=0, n=None):
    n = ref.shape[0] // ROW_SUB if n is None else n
    return jnp.concatenate([ref[pl.ds(r0 * ROW_SUB + s, n, stride=ROW_SUB), :] for s in range(ROW_SUB)], axis=1)


def _row_tile(ref, r):
    return ref.at[pl.ds(pl.multiple_of(r * ROW_SUB, ROW_SUB), ROW_SUB)]


def _const_spec(shape):
    return pl.BlockSpec(shape, lambda *_: (0,) * len(shape), pipeline_mode=pl.Buffered(1))


def _mod_kernel(c_ref, w_ref, b_ref, o_ref):
    c = c_ref[...]
    o_ref[...] = _dot(c * jax.nn.sigmoid(c), w_ref[...], HIGHEST) + b_ref[...]


def _modulation(cvec, ada_w, ada_b):
    n, d = cvec.shape
    nout = ada_w.shape[1]
    return pl.pallas_call(
        _mod_kernel,
        grid=(nout // d,),
        in_specs=[pl.BlockSpec((n, d), lambda j: (0, 0)),
                  pl.BlockSpec((d, d), lambda j: (0, j)),
                  pl.BlockSpec((1, d), lambda j: (0, j))],
        out_specs=pl.BlockSpec((n, d), lambda j: (0, j)),
        out_shape=jax.ShapeDtypeStruct((n, nout), F32),
        name="adaln_mod",
    )(cvec, ada_w, ada_b.reshape(1, nout))


def _proj_kernel(xc_ref, xl_ref, mod_ref, g1_ref, w_ref, b_ref, cw_ref, cb_ref, wco_ref,
                 ycg_ref, sgb_ref, q_ref, k_ref, v_ref, so_ref, gt_ref, *, n_ctx_tiles, ctx_row):
    i = pl.program_id(0)
    x = jnp.where(i < n_ctx_tiles, xc_ref[...], xl_ref[...])
    h = (_rms(x) * g1_ref[...] * (1.0 + mod_ref[0, 1:2, :]) + mod_ref[0, 0:1, :]).astype(BF16)

    def proj(cols):
        return _dot(h, w_ref[:, cols[0]:cols[1]]) + b_ref[:, cols[0]:cols[1]]

    zc = proj(C_CONV)
    d = x.shape[1]
    u = zc[:, 2 * d:3 * d] * zc[:, 0:d]
    rowlen = jnp.where(i < n_ctx_tiles, ctx_row, GRID_W)
    pos = lax.broadcasted_iota(jnp.int32, (TM, 1), 0) & (rowlen - 1)
    u_prev = jnp.where(pos == 0, 0.0, pltpu.roll(u, 1, 0))
    u_next = jnp.where(pos == rowlen - 1, 0.0, pltpu.roll(u, TM - 1, 0))
    uc = u_prev * cw_ref[0:1, :] + u * cw_ref[1:2, :] + u_next * cw_ref[2:3, :] + cb_ref[...]
    yconv = _dot((zc[:, d:2 * d] * uc).astype(BF16), wco_ref[...])

    ycg_ref[...] = (jax.nn.sigmoid(proj(C_GA)) * yconv).astype(BF16)
    sgb_ref[...] = jax.nn.sigmoid(proj(C_GB)).astype(BF16)
    q_ref[...] = (proj(C_Q) * (D_QK ** -0.5)).astype(BF16)
    k_ref[...] = proj(C_K).astype(BF16)
    v_ref[...] = proj(C_V).astype(BF16)
    so_ref[...] = jax.nn.sigmoid(proj(C_O)).astype(BF16)
    gt_ref[...] = proj(C_G)


def _tri_masks():
    row = lax.broadcasted_iota(jnp.int32, (TM, TM), 0)
    col = lax.broadcasted_iota(jnp.int32, (TM, TM), 1)
    return row >= col, row <= col


def _gate_cumsums(gf):
    lower, upper = _tri_masks()
    lf = jax.nn.log_sigmoid(gf)
    fwd_lane = lax.broadcasted_iota(jnp.int32, (1, LANES), 1) < N_HEADS
    bsum = jnp.where(fwd_lane, _dot(lower.astype(F32), lf, HIGHEST), _dot(upper.astype(F32), lf, HIGHEST))
    return lf, bsum


def _state_kernel(*refs, zero_init, emit_before, emit_after):
    it = iter(refs)
    kf_ref, vf_ref, gf_ref, kb_ref, vb_ref, gb_ref = (next(it) for _ in range(6))
    if not zero_init:
        c0_ref, n0_ref, m0_ref = (next(it) for _ in range(3))
    if emit_before:
        cbf_ref, cbb_ref, nbf_ref, nbb_ref, mbf_ref, mbb_ref = (next(it) for _ in range(6))
    if emit_after:
        ca_ref, na_ref, ma_ref = (next(it) for _ in range(3))
    c_scr, n_scr, m_scr = (next(it) for _ in range(3))
    c = pl.program_id(1)

    @pl.when(c == 0)
    def _():
        if zero_init:
            c_scr[...] = jnp.zeros_like(c_scr)
            n_scr[...] = jnp.zeros_like(n_scr)
            m_scr[...] = jnp.zeros_like(m_scr)
        else:
            c_scr[...] = c0_ref[0]
            n_scr[...] = n0_ref[0]
            m_scr[...] = m0_ref[0]

    if emit_before:
        cbf_ref[0] = c_scr[0:N_HEADS]
        cbb_ref[0] = c_scr[N_HEADS:N_HD]
        nbf_ref[0] = n_scr[0:N_HEADS]
        nbb_ref[0] = n_scr[N_HEADS:N_HD]
        mbf_ref[0] = m_scr[...]
        mbb_ref[0] = m_scr[...]

    fwd_lane = lax.broadcasted_iota(jnp.int32, (1, LANES), 1) < N_HEADS
    gi = jnp.where(fwd_lane, gf_ref[:, 0:LANES], gb_ref[:, 0:LANES])
    gfg = jnp.where(fwd_lane, gf_ref[:, LANES:2 * LANES], gb_ref[:, LANES:2 * LANES])
    lf, bsum = _gate_cumsums(gfg)
    total = jnp.sum(lf, axis=0, keepdims=True)
    g = total - bsum + gi
    m_prev = m_scr[...]
    m_new = jnp.maximum(total + m_prev, jnp.max(g, axis=0, keepdims=True))
    wk = jnp.exp(g - m_new)
    decay = jnp.exp(total + m_prev - m_new)
    for hd in range(N_HD):
        h = hd % N_HEADS
        k_ref, v_ref = (kf_ref, vf_ref) if hd < N_HEADS else (kb_ref, vb_ref)
        k_tile = k_ref[:, (h // 2) * LANES:(h // 2 + 1) * LANES].astype(F32)
        wkk = wk[:, hd:hd + 1] * jnp.where(_head_lanes(h), k_tile, 0.0)
        vh = v_ref[:, h * D_V:(h + 1) * D_V]
        dec = decay[:, hd:hd + 1]
        tn = (((0,), (0,)), ((), ()))
        if emit_before:
            c_scr[hd] = dec * c_scr[hd] + lax.dot_general(vh, wkk.astype(BF16), tn, preferred_element_type=F32)
        else:
            upd = lax.dot_general(wkk.astype(BF16), vh, tn, preferred_element_type=F32)
            c_scr[hd] = dec * c_scr[hd] + upd[(h % 2) * D_QK:(h % 2 + 1) * D_QK, :]
        n_scr[hd:hd + 1, :] = dec * n_scr[hd:hd + 1, :] + jnp.sum(wkk, axis=0, keepdims=True)
    m_scr[...] = m_new

    if emit_after:
        @pl.when(c == pl.num_programs(1) - 1)
        def _():
            ca_ref[0] = c_scr[...]
            na_ref[0] = n_scr[...]
            ma_ref[0] = m_scr[...]


def _state_scan(k, v, gt, tile0, n_seq, n_chunk, init, emit_before, emit_after):
    d = k.shape[1]
    cshape = (D_V, LANES) if emit_before else (D_QK, D_V)
    fwd = lambda s, c: (tile0 + s * n_chunk + c, 0)
    bwd = lambda s, c: (tile0 + s * n_chunk + n_chunk - 1 - c, 0)
    dv = v.shape[1]
    in_specs = [pl.BlockSpec((TM, d), fwd), pl.BlockSpec((TM, dv), fwd), pl.BlockSpec((TM, 2 * LANES), fwd),
                pl.BlockSpec((TM, d), bwd), pl.BlockSpec((TM, dv), bwd), pl.BlockSpec((TM, 2 * LANES), bwd)]
    args = [k, v, gt, k, v, gt]
    if init is not None:
        in_specs += [pl.BlockSpec((1, N_HD) + cshape, lambda s, c: (s, 0, 0, 0)),
                     pl.BlockSpec((1, N_HD, LANES), lambda s, c: (s, 0, 0)),
                     pl.BlockSpec((1, 1, LANES), lambda s, c: (s, 0, 0))]
        args += list(init)
    out_specs, out_shape = [], []
    n_tot = n_seq * n_chunk
    if emit_before:
        cf = lambda s, c: (s * n_chunk + c, 0, 0, 0)
        cb = lambda s, c: (s * n_chunk + n_chunk - 1 - c, 0, 0, 0)
        nf = lambda s, c: (s * n_chunk + c, 0, 0)
        nb = lambda s, c: (s * n_chunk + n_chunk - 1 - c, 0, 0)
        out_specs += [pl.BlockSpec((1, N_HEADS) + cshape, cf), pl.BlockSpec((1, N_HEADS) + cshape, cb),
                      pl.BlockSpec((1, N_HEADS, LANES), nf), pl.BlockSpec((1, N_HEADS, LANES), nb),
                      pl.BlockSpec((1, 1, LANES), nf), pl.BlockSpec((1, 1, LANES), nb)]
        out_shape += [jax.ShapeDtypeStruct((n_tot, N_HEADS) + cshape, F32)] * 2
        out_shape += [jax.ShapeDtypeStruct((n_tot, N_HEADS, LANES), F32)] * 2
        out_shape += [jax.ShapeDtypeStruct((n_tot, 1, LANES), F32)] * 2
    if emit_after:
        out_specs += [pl.BlockSpec((1, N_HD, D_QK, D_V), lambda s, c: (s, 0, 0, 0)),
                      pl.BlockSpec((1, N_HD, LANES), lambda s, c: (s, 0, 0)),
                      pl.BlockSpec((1, 1, LANES), lambda s, c: (s, 0, 0))]
        out_shape += [jax.ShapeDtypeStruct((n_seq, N_HD, D_QK, D_V), F32),
                      jax.ShapeDtypeStruct((n_seq, N_HD, LANES), F32),
                      jax.ShapeDtypeStruct((n_seq, 1, LANES), F32)]
    return pl.pallas_call(
        functools.partial(_state_kernel, zero_init=init is None, emit_before=emit_before, emit_after=emit_after),
        grid=(n_seq, n_chunk),
        in_specs=in_specs,
        out_specs=out_specs,
        out_shape=out_shape,
        scratch_shapes=[pltpu.VMEM((N_HD,) + cshape, F32), pltpu.VMEM((N_HD, LANES), F32),
                        pltpu.VMEM((1, LANES), F32)],
        compiler_params=pltpu.CompilerParams(dimension_semantics=("arbitrary", "arbitrary"),
                                             vmem_limit_bytes=VMEM_LIMIT),
        name="mlstm_state_scan",
    )(*args)


def _mix_kernel(q_ref, k_ref, v_ref, so_ref, sgb_ref, ycg_ref, gt_ref, xc_ref, xl_ref, mod_ref,
                cbf_ref, cbb_ref, nbf_ref, nbb_ref, mbf_ref, mbb_ref,
                gmh_ref, wmo_ref, wo_ref, g2_ref, rw_ref, rb_ref,
                x1_ref, h2_ref, route_ref, routet_ref, cnt_ref, hm_scr, carry_scr, *, n_ctx_tiles, n_experts):
    i = pl.program_id(0)
    is_lat = i >= n_ctx_tiles
    lat_f = is_lat.astype(F32)
    lower, upper = _tri_masks()
    gt_t = gt_ref[...].T
    gi_t = gt_t[0:N_HD, :]
    lf_t = jax.nn.log_sigmoid(gt_t[LANES:LANES + N_HD, :])
    fwd_row = lax.broadcasted_iota(jnp.int32, (N_HD, 1), 0) < N_HEADS
    bsum_t = jnp.where(fwd_row, _dot(lf_t, upper.astype(F32), HIGHEST), _dot(lf_t, lower.astype(F32), HIGHEST))
    a_t = gi_t - bsum_t
    m_row = jnp.where(lax.broadcasted_iota(jnp.int32, (1, LANES), 1) < N_HEADS, mbf_ref[0], mbb_ref[0]) * lat_f
    m_sq = jnp.where(lax.broadcasted_iota(jnp.int32, (LANES, 1), 0) == 0, m_row, 0.0)
    m_prev = m_sq.T[0:N_HD, 0:1]
    a_c = jnp.concatenate([a_t, jnp.zeros((LANES - N_HD, TM), F32)], axis=0).T
    row0 = lax.broadcasted_iota(jnp.int32, (LANES, 1), 0) == 0
    ones_rows = jnp.where(row0, 1.0, 0.0).astype(BF16) * jnp.ones((1, TM), BF16)

    for h in range(N_HEADS):
        hs = slice(h * D_V, (h + 1) * D_V)
        pair = slice((h // 2) * LANES, (h // 2 + 1) * LANES)
        qh = jnp.where(_head_lanes(h), q_ref[:, pair], jnp.zeros((), BF16))
        kh = k_ref[:, pair]
        vext_t = jnp.concatenate([v_ref[:, hs].T, ones_rows], axis=0)
        kq = lax.dot_general(kh, qh, (((1,), (1,)), ((), ())), preferred_element_type=F32)
        hsum = None
        for d in range(2):
            hd = d * N_HEADS + h
            mask = upper if d == 0 else lower
            c_t = (cbf_ref if d == 0 else cbb_ref)[0, h] * lat_f
            n_r = (nbf_ref if d == 0 else nbb_ref)[0, h:h + 1, :] * lat_f
            a_b = jnp.where(mask, a_c[:, hd:hd + 1], -jnp.inf)
            mrow = jnp.maximum(m_prev[hd:hd + 1, :], jnp.max(a_b, axis=0, keepdims=True))
            e = jnp.exp(a_b - mrow)
            nd = _dot(vext_t, (kq * e).astype(BF16))
            cext_t = jnp.concatenate([c_t, jnp.where(row0, n_r, 0.0)], axis=0).astype(BF16)
            qc = lax.dot_general(cext_t, qh, (((1,), (1,)), ((), ())), preferred_element_type=F32)
            wi = jnp.exp(m_prev[hd:hd + 1, :] - mrow)
            num = nd[0:D_V, :] + wi * qc[0:D_V, :]
            den = nd[D_V:D_V + 1, :] + wi * qc[D_V:D_V + 1, :]
            r = 1.0 / jnp.maximum(jnp.abs(den), jnp.exp(-(bsum_t[hd:hd + 1, :] + mrow)))
            hsum = num * r if hsum is None else hsum + num * r
        hn = hsum * lax.rsqrt(jnp.mean(hsum * hsum, axis=0, keepdims=True) + EPS)
        hm_scr[:, hs] = (hn.T * gmh_ref[:, hs] * so_ref[:, hs].astype(F32)).astype(BF16)

    ym = _dot(hm_scr[...], wmo_ref[...])
    mix = (ycg_ref[...].astype(F32) + sgb_ref[...].astype(F32) * ym).astype(BF16)
    x1 = jnp.where(is_lat, xl_ref[...], xc_ref[...]) + mod_ref[0, 2:3, :] * _dot(mix, wo_ref[...])
    x1_ref[...] = x1
    h2 = _rms(x1) * g2_ref[...] * (1.0 + mod_ref[0, 4:5, :]) + mod_ref[0, 3:4, :]
    _store_rows(h2_ref, h2)

    lane = lax.broadcasted_iota(jnp.int32, (TM, LANES), 1)
    rw = rw_ref[...]
    h2_hi, rw_hi = h2.astype(BF16), rw.astype(BF16)
    h2_lo, rw_lo = (h2 - h2_hi.astype(F32)).astype(BF16), (rw - rw_hi.astype(F32)).astype(BF16)
    logits = _dot(h2_hi, rw_hi) + (_dot(h2_hi, rw_lo) + _dot(h2_lo, rw_hi)) + rb_ref[...]
    work = jnp.where(lane < n_experts, logits, -jnp.inf)
    sels, exps, idxs = [], [], []
    top = None
    for _ in range(TOP_K):
        mx = jnp.max(work, axis=-1, keepdims=True)
        ix = jnp.min(jnp.where(work == mx, lane, LANES), axis=-1, keepdims=True)
        sel = lane == ix
        work = jnp.where(sel, -jnp.inf, work)
        top = mx if top is None else top
        sels.append(sel)
        idxs.append(ix.astype(F32))
        exps.append(jnp.exp(mx - top))
    inv = 1.0 / functools.reduce(lambda p, q: p + q, exps)

    @pl.when(i == 0)
    def _():
        carry_scr[...] = jnp.zeros_like(carry_scr)

    onehot = functools.reduce(lambda p, q: p + q, [jnp.where(s, 1.0, 0.0) for s in sels])
    row = lax.broadcasted_iota(jnp.int32, (TM, TM), 0)
    col = lax.broadcasted_iota(jnp.int32, (TM, TM), 1)
    before = _dot((row > col).astype(BF16), onehot.astype(BF16)) + carry_scr[...]
    carry_scr[...] += jnp.sum(onehot, axis=0, keepdims=True)
    cnt_ref[...] = carry_scr[...]
    route = jnp.zeros((TM, LANES), F32)
    for j in range(TOP_K):
        slot = jnp.sum(jnp.where(sels[j], before, 0.0), axis=-1, keepdims=True)
        route = jnp.where(lane == j, idxs[j], route)
        route = jnp.where(lane == TOP_K + j, slot, route)
        route = jnp.where(lane == 2 * TOP_K + j, exps[j] * inv, route)
    route_ref[...] = route
    routet_ref[...] = route.T[0:8, :].astype(jnp.int32)


def _dispatch_kernel(pos_ref, h2_ref, xs_ref, sem):
    def copy(t, j):
        return pltpu.make_async_copy(_row_tile(h2_ref, t), _row_tile(xs_ref, pos_ref[TOP_K * t + j]), sem)

    def start(t, carry):
        for j in range(TOP_K):
            copy(t, j).start(priority=j % 2)
        return carry

    tm = h2_ref.shape[0] // ROW_SUB
    lax.fori_loop(0, tm, start, 0, unroll=ISSUE_UNROLL)
    for j in range(TOP_K):
        pltpu.make_async_copy(h2_ref, xs_ref.at[pl.ds(0, tm * ROW_SUB)], sem).wait()


def _expert_kernel(tile_ref, exp_ref, flag_ref, lo_ref, hi_ref, nxt_ref,
                   xs_ref, wg_hbm, bg_ref, wl_hbm, bl_ref, wd_hbm, bd_ref, ys_ref, w_scr, wf_scr, wsem):
    w = pl.program_id(0)
    flags = flag_ref[w]

    def fetch(e, slot):
        return [pltpu.make_async_copy(hbm.at[e], wf_scr.at[slot, m], wsem.at[slot])
                for m, hbm in enumerate((wg_hbm, wl_hbm, wd_hbm))]

    @pl.when(w == 0)
    def _():
        for cp in fetch(exp_ref[0], 0):
            cp.start()

    @pl.when((flags & 4) != 0)
    def _():
        slot = (flags >> 4) & 1
        for cp in fetch(exp_ref[w], slot):
            cp.wait()
        for m in range(3):
            w_scr[m] = wf_scr[slot, m].astype(BF16)

        @pl.when((flags & 8) != 0)
        def _():
            for cp in fetch(nxt_ref[w], 1 - slot):
                cp.start()

    tmx = ys_ref.shape[0] // ROW_SUB

    def run(r0, m):
        x = _load_rows(xs_ref, r0, m).astype(BF16)
        gt = jnp.minimum(_dot(x, w_scr[0]) + bg_ref[0], SWIGLU_LIMIT)
        lin = jnp.clip(_dot(x, w_scr[1]) + bl_ref[0], -SWIGLU_LIMIT, SWIGLU_LIMIT)
        act = gt * jax.nn.sigmoid(SWIGLU_ALPHA * gt) * (lin + 1.0)
        y = _dot(act.astype(BF16), w_scr[2]) + bd_ref[0]
        rows = r0 + lax.broadcasted_iota(jnp.int32, (m, 1), 0)
        mine = (rows >= lo_ref[w]) & (rows < hi_ref[w])

        @pl.when((flags & 2) != 0)
        def _():
            _store_rows(ys_ref, jnp.where(mine, y, 0.0), r0)
            if m < tmx:
                _store_rows(ys_ref, jnp.zeros((tmx - m, y.shape[1]), F32), m if r0 == 0 else 0)

        @pl.when((flags & 2) == 0)
        def _():
            _store_rows(ys_ref, jnp.where(mine, y, _load_rows(ys_ref, r0, m)), r0)

    half = (flags >> 5) & 3
    valid = (flags & 1) != 0
    pl.when(valid & (half == 0))(lambda: run(0, tmx))
    pl.when(valid & (half == 1))(lambda: run(0, tmx // 2))
    pl.when(valid & (half == 2))(lambda: run(tmx // 2, tmx // 2))


def _combine_kernel(pos_ref, posn_ref, route_ref, x1_ref, mod_ref, fg_ref, ys_ref, outc_ref, outl_ref,
                    buf, sem, *, n_ctx_tiles):
    i = pl.program_id(0)
    n = pl.num_programs(0)
    tm = x1_ref.shape[0]

    def copy(p_ref, slot, t, j):
        return pltpu.make_async_copy(_row_tile(ys_ref, p_ref[TOP_K * t + j]), _row_tile(buf.at[slot, j], t), sem.at[slot])

    def start_all(p_ref, slot):
        def body(t, carry):
            for j in range(TOP_K):
                copy(p_ref, slot, t, j).start(priority=j % 2)
            return carry
        lax.fori_loop(0, tm, body, 0, unroll=ISSUE_UNROLL)

    @pl.when(i == 0)
    def _():
        start_all(pos_ref, 0)

    @pl.when(i + 1 < n)
    def _():
        start_all(posn_ref, (i + 1) % 2)

    slot = i % 2

    for j in range(TOP_K):
        pltpu.make_async_copy(ys_ref.at[pl.ds(0, tm * ROW_SUB)], buf.at[slot, j], sem.at[slot]).wait()
    acc = None
    for j in range(TOP_K):
        term = route_ref[:, 2 * TOP_K + j:2 * TOP_K + j + 1] * _load_rows(buf.at[slot, j])
        acc = term if acc is None else acc + term
    out = _rms(x1_ref[...] + mod_ref[0, 5:6, :] * acc) * fg_ref[...]

    @pl.when(i < n_ctx_tiles)
    def _():
        outc_ref[...] = out

    @pl.when(i >= n_ctx_tiles)
    def _():
        outl_ref[...] = out


def _work_items(counts, n_experts, n_rows, tmx):
    n_items_max = n_rows // tmx + n_experts - 1
    cnt = counts.astype(jnp.int32)
    offs = jnp.concatenate([jnp.zeros((1,), jnp.int32), jnp.cumsum(cnt)])
    first_tile = offs[:-1] // tmx
    n_it = jnp.where(cnt > 0, (offs[1:] - 1) // tmx - first_tile + 1, 0)
    it_start = jnp.concatenate([jnp.zeros((1,), jnp.int32), jnp.cumsum(n_it)])
    total = it_start[-1]
    w = jnp.arange(n_items_max, dtype=jnp.int32)
    wc = jnp.minimum(w, total - 1)
    e = jnp.sum((it_start[None, 1:] <= wc[:, None]).astype(jnp.int32), axis=1)
    e = jnp.minimum(e, n_experts - 1)
    is_e = e[:, None] == jnp.arange(n_experts, dtype=jnp.int32)
    at_e = lambda a: jnp.sum(jnp.where(is_e, a[None, :], 0), axis=1)
    tile = at_e(first_tile) + wc - at_e(it_start[:-1])
    valid = w < total
    prev = lambda a: jnp.concatenate([jnp.full((1,), -1, jnp.int32), a[:-1]])
    ids = jnp.arange(n_experts, dtype=jnp.int32)
    later = (ids[None, :] > ids[:, None]) & (cnt > 0)[None, :]
    nxt = jnp.min(jnp.where(later, ids[None, :], n_experts), axis=1)
    has_next = nxt < n_experts
    slot = (jnp.cumsum((cnt > 0).astype(jnp.int32)) - 1) & 1
    flags = (valid.astype(jnp.int32) + 2 * (valid & (tile != prev(tile))).astype(jnp.int32)
             + 4 * (valid & (e != prev(e))).astype(jnp.int32)
             + 8 * at_e(has_next.astype(jnp.int32)) + 16 * at_e(slot))
    lo = jnp.clip(at_e(offs[:-1]) - tile * tmx, 0, tmx)
    hi = jnp.clip(at_e(offs[1:]) - tile * tmx, 0, tmx)
    flags = flags + 32 * jnp.where(hi <= tmx // 2, 1, jnp.where(lo >= tmx // 2, 2, 0))
    return offs, tile, e, flags, lo, hi, at_e(jnp.minimum(nxt, n_experts - 1))


def _pack_in_proj(w, b):
    d = w.shape[0]
    o_g = 5 * d + 2 * D_QKH
    o_ga, o_gb = o_g + 4 * N_HEADS, o_g + 4 * N_HEADS + d

    def gates(m):
        gz = jnp.zeros((m.shape[0], LANES - N_HD), m.dtype)
        i_f, f_f, i_b, f_b = (m[:, o_g + j * N_HEADS:o_g + (j + 1) * N_HEADS] for j in range(4))
        return jnp.concatenate([i_f, i_b, gz, f_f, f_b, gz], axis=1)

    def pack(m):
        return jnp.concatenate([m[:, 0:o_g], m[:, o_ga:o_gb], m[:, o_gb:o_gb + d], gates(m)], axis=1)

    return pack(w.astype(BF16)), pack(b.reshape(1, -1))


def kernel(x_prompt, x_sample, c, state_C, state_n, state_m, c_ctx, ada_w, ada_b, norm1_g, norm2_g, w_in, b_in,
           conv_w, conv_b, w_conv_out, mh_norm_g, w_m_out, w_o, router_w, router_b, w_gate, b_gate, w_lin, b_lin,
           w_down, b_down, final_g):
    nb, seq, d = x_prompt.shape
    nd, dseq, _ = x_sample.shape
    n_experts = w_gate.shape[1]
    assert d == _D and w_in.shape[0] == 1 and seq == TM and dseq % TM == 0 and TM % GRID_W == 0
    assert w_gate.shape[-1] == d
    t_ctx, t_lat = nb * seq, nd * dseq
    n_tok = t_ctx + t_lat
    n_ctx_tiles, n_tiles = t_ctx // TM, n_tok // TM
    lat_chunks = dseq // TM
    n_lat_tiles = n_tiles - n_ctx_tiles

    n_c = 1 + nd
    n_cp = -(-n_c // 8) * 8
    cvec = jnp.concatenate([c_ctx[None, :], c, jnp.zeros((n_cp - n_c, d), F32)], axis=0)
    mod = _modulation(cvec, ada_w[0], ada_b[0]).reshape(n_cp, N_MOD, d)
    mod = jnp.pad(mod, ((0, 0), (0, 8 - N_MOD), (0, 0)))

    def mod_row(tile_tokens):
        ctx_t, per_seq = t_ctx // tile_tokens, dseq // tile_tokens
        return lambda i, *_: (jnp.where(i < ctx_t, 0, 1 + (i - ctx_t) // per_seq), 0, 0)

    x_ctx, x_lat = x_prompt.reshape(t_ctx, d), x_sample.reshape(t_lat, d)
    ctx_spec = pl.BlockSpec((TM, d), lambda i: (jnp.minimum(i, n_ctx_tiles - 1), 0))
    lat_spec = pl.BlockSpec((TM, d), lambda i: (jnp.maximum(i - n_ctx_tiles, 0), 0))
    w_all, b_all = _pack_in_proj(w_in[0], b_in[0])
    tile = lambda cols: pl.BlockSpec((TM, cols), lambda i: (i, 0))
    rows = lambda n: pl.BlockSpec((n * ROW_SUB, LANES), lambda i: (i, 0))
    params = pltpu.CompilerParams(dimension_semantics=("arbitrary",), vmem_limit_bytes=VMEM_LIMIT)
    bf = lambda cols: jax.ShapeDtypeStruct((n_tok, cols), BF16)

    ycg, sgb, q, k, v, so, gt = pl.pallas_call(
        functools.partial(_proj_kernel, n_ctx_tiles=n_ctx_tiles, ctx_row=seq),
        grid=(n_tiles,),
        in_specs=[ctx_spec, lat_spec, pl.BlockSpec((1, 8, d), mod_row(TM)), _const_spec((1, d)),
                  _const_spec((d, W_COLS)), _const_spec((1, W_COLS)), _const_spec((3, d)), _const_spec((1, d)),
                  _const_spec((d, d))],
        out_specs=[tile(d), tile(d), tile(D_QKH), tile(D_QKH), tile(d), tile(d), tile(2 * LANES)],
        out_shape=[bf(d), bf(d), bf(D_QKH), bf(D_QKH), bf(d), bf(d),
                   jax.ShapeDtypeStruct((n_tok, 2 * LANES), F32)],
        compiler_params=params,
        name="in_proj_conv",
    )(x_ctx, x_lat, mod, norm1_g, w_all, b_all, conv_w[0], conv_b, w_conv_out[0].astype(BF16))

    c_new, n_new, m_new = _state_scan(k, v, gt, 0, nb, 1, None, False, True)
    def on_head_lanes(a):
        even = lax.broadcasted_iota(jnp.int32, (N_HD,) + (1,) * (a.ndim - 2), 0) % 2 == 0
        lo = jnp.pad(a, [(0, 0)] * (a.ndim - 1) + [(0, LANES - D_QK)])
        hi = jnp.pad(a, [(0, 0)] * (a.ndim - 1) + [(LANES - D_QK, 0)])
        return jnp.where(even, lo, hi)

    init = (on_head_lanes(jnp.swapaxes(state_C[:, 0].astype(F32).reshape(nd, N_HD, D_QK, D_V), 2, 3)),
            on_head_lanes(state_n[:, 0].astype(F32).reshape(nd, N_HD, D_QK)),
            jnp.pad(state_m[:, 0].astype(F32).reshape(nd, 1, N_HD), ((0, 0), (0, 0), (0, LANES - N_HD))))
    cbf, cbb, nbf, nbb, mbf, mbb = _state_scan(k, v, gt, n_ctx_tiles, nd, lat_chunks, init, True, False)

    lat_idx = lambda i: jnp.maximum(i - n_ctx_tiles, 0)
    st4 = pl.BlockSpec((1, N_HEADS, D_V, LANES), lambda i: (lat_idx(i), 0, 0, 0))
    st3 = pl.BlockSpec((1, N_HEADS, LANES), lambda i: (lat_idx(i), 0, 0))
    st1 = pl.BlockSpec((1, 1, LANES), lambda i: (lat_idx(i), 0, 0))
    rw = jnp.pad(router_w[0], ((0, 0), (0, LANES - n_experts)))
    rb = jnp.pad(router_b[0], (0, LANES - n_experts)).reshape(1, LANES)
    x1, h2, route, route_t, counts = pl.pallas_call(
        functools.partial(_mix_kernel, n_ctx_tiles=n_ctx_tiles, n_experts=n_experts),
        grid=(n_tiles,),
        in_specs=[tile(D_QKH), tile(D_QKH), tile(d), tile(d), tile(d), tile(d), tile(2 * LANES), ctx_spec, lat_spec,
                  pl.BlockSpec((1, 8, d), mod_row(TM)), st4, st4, st3, st3, st1, st1,
                  _const_spec((1, d)), _const_spec((d, d)), _const_spec((d, d)), _const_spec((1, d)),
                  _const_spec((d, LANES)), _const_spec((1, LANES))],
        out_specs=[tile(d), rows(TM), tile(LANES), pl.BlockSpec((8, TM), lambda i: (0, i)),
                   pl.BlockSpec((1, LANES), lambda i: (0, 0))],
        out_shape=[jax.ShapeDtypeStruct((n_tok, d), F32), jax.ShapeDtypeStruct((n_tok * ROW_SUB, LANES), F32),
                   jax.ShapeDtypeStruct((n_tok, LANES), F32), jax.ShapeDtypeStruct((8, n_tok), jnp.int32),
                   jax.ShapeDtypeStruct((1, LANES), F32)],
        scratch_shapes=[pltpu.VMEM((TM, d), BF16), pltpu.VMEM((1, LANES), F32)],
        compiler_params=params,
        name="mlstm_mix_router",
    )(q, k, v, so, sgb, ycg, gt, x_ctx, x_lat, mod, cbf, cbb, nbf, nbb, mbf, mbb,
      mh_norm_g, w_m_out[0].astype(BF16), w_o[0].astype(BF16), norm2_g, rw, rb)

    n_rows = TOP_K * n_tok
    offs, it_tile, it_exp, it_flags, it_lo, it_hi, it_nxt = _work_items(
        counts[0, :n_experts], n_experts, n_rows, TMX)
    is_exp = route_t[0:TOP_K, :, None] == jnp.arange(n_experts, dtype=jnp.int32)
    pos_t = jnp.sum(jnp.where(is_exp, offs[:n_experts], 0), axis=-1) + route_t[TOP_K:2 * TOP_K]
    pos = pos_t.T.reshape(n_rows)
    pos_spec = lambda tm, f: pl.BlockSpec((TOP_K * tm,), f, memory_space=pltpu.SMEM)
    any_spec = pl.BlockSpec(memory_space=pl.ANY)
    tmd = next(t for t in (TM_DISPATCH, TM) if n_tok % t == 0)
    tmc = next(t for t in (TM_COMBINE, TM) if t_ctx % t == 0 and dseq % t == 0)

    xs = pl.pallas_call(
        _dispatch_kernel,
        grid=(n_tok // tmd,),
        in_specs=[pos_spec(tmd, lambda i: (i,)), rows(tmd)],
        out_specs=any_spec,
        out_shape=jax.ShapeDtypeStruct((n_rows * ROW_SUB, LANES), F32),
        scratch_shapes=[pltpu.SemaphoreType.DMA],
        compiler_params=params,
        name="moe_dispatch",
    )(pos, h2)

    dff = w_gate.shape[-1]
    wspec = lambda a, b_: pl.BlockSpec((1, a, b_), lambda w, tl, ex, *_: (ex[w], 0, 0))
    ys = pl.pallas_call(
        _expert_kernel,
        grid_spec=pltpu.PrefetchScalarGridSpec(
            num_scalar_prefetch=6,
            grid=(it_tile.shape[0],),
            in_specs=[pl.BlockSpec((TMX * ROW_SUB, LANES), lambda w, tl, *_: (tl[w], 0)),
                      any_spec, wspec(1, dff), any_spec, wspec(1, dff), any_spec, wspec(1, d)],
            out_specs=pl.BlockSpec((TMX * ROW_SUB, LANES), lambda w, tl, *_: (tl[w], 0)),
            scratch_shapes=[pltpu.VMEM((3, d, dff), BF16), pltpu.VMEM((2, 3, d, dff), F32),
                            pltpu.SemaphoreType.DMA((2,))]),
        out_shape=jax.ShapeDtypeStruct((n_rows * ROW_SUB, LANES), F32),
        compiler_params=params,
        name="moe_experts",
    )(it_tile, it_exp, it_flags, it_lo, it_hi, it_nxt, xs, w_gate[0], b_gate[0].reshape(n_experts, 1, dff), w_lin[0],
      b_lin[0].reshape(n_experts, 1, dff), w_down[0], b_down[0].reshape(n_experts, 1, d))

    nc_ctx, nc_all = t_ctx // tmc, n_tok // tmc
    ctx_i = lambda i: (jnp.minimum(i, nc_ctx - 1), 0)
    lat_i = lambda i: (jnp.maximum(i - nc_ctx, 0), 0)
    ctile = lambda cols: pl.BlockSpec((tmc, cols), lambda i: (i, 0))
    y_prompt, y_sample = pl.pallas_call(
        functools.partial(_combine_kernel, n_ctx_tiles=nc_ctx),
        grid=(nc_all,),
        in_specs=[pos_spec(tmc, lambda i: (i,)), pos_spec(tmc, lambda i: (jnp.minimum(i + 1, nc_all - 1),)),
                  ctile(LANES), ctile(d), pl.BlockSpec((1, 8, d), mod_row(tmc)), _const_spec((1, d)), any_spec],
        out_specs=[pl.BlockSpec((tmc, d), ctx_i), pl.BlockSpec((tmc, d), lat_i)],
        out_shape=[jax.ShapeDtypeStruct((t_ctx, d), F32), jax.ShapeDtypeStruct((t_lat, d), F32)],
        scratch_shapes=[pltpu.VMEM((2, TOP_K, tmc * ROW_SUB, LANES), F32), pltpu.SemaphoreType.DMA((2,))],
        compiler_params=params,
        name="moe_combine",
    )(pos, pos, route, x1, mod, final_g.reshape(1, d), ys)

    y_prompt = y_prompt.reshape(nb, seq, d)
    y_sample = y_sample.reshape(nd, dseq, d)
    new_c = c_new.reshape(nb, 1, 2, N_HEADS, D_QK, D_V)
    odd_head = (lax.broadcasted_iota(jnp.int32, (1, N_HD, 1), 1) % 2) == 1
    new_n = jnp.where(odd_head, n_new[:, :, D_QK:], n_new[:, :, :D_QK]).reshape(nb, 1, 2, N_HEADS, D_QK)
    new_m = m_new[:, 0, :N_HD].reshape(nb, 1, 2, N_HEADS)
    return (y_prompt, y_sample, new_c, new_n, new_m)
```

```python
import functools

import jax
import jax.numpy as jnp
from jax import lax
from jax.experimental import pallas as pl
from jax.experimental.pallas import tpu as pltpu

F32 = jnp.float32
BF16 = jnp.bfloat16
HIGHEST = lax.Precision.HIGHEST

N_HEADS = 8
D_QK = 64
D_V = 128
GRID_W = 64
TOP_K = 4
SWIGLU_LIMIT = 7.0
SWIGLU_ALPHA = 1.702
EPS = 1e-6
N_MOD = 6

LANES = 128
ROW_SUB = 8
TM = 256
TMX = 512
TM_PROJ = 512
TM_DISPATCH = 2048
TM_COMBINE = 512
ISSUE_UNROLL = 4
N_HD = 2 * N_HEADS
VMEM_LIMIT = 56 * 1024 * 1024

_D = 1024
C_CONV = (0, 3 * _D)
D_QKH = N_HEADS * D_QK
C_Q = (3 * _D, 3 * _D + D_QKH)
C_K = (C_Q[1], C_Q[1] + D_QKH)
C_V = (C_K[1], C_K[1] + _D)
C_O = (C_V[1], C_V[1] + _D)
C_GA = (C_O[1], C_O[1] + _D)
C_GB = (C_GA[1], C_GA[1] + _D)
C_G = (C_GB[1], C_GB[1] + 2 * LANES)
W_COLS = C_G[1]


def _head_lanes(h):
    lane = lax.broadcasted_iota(jnp.int32, (1, LANES), 1)
    return (lane < D_QK) if h % 2 == 0 else (lane >= D_QK)


def _dot(a, b, precision=None):
    return jnp.dot(a, b, preferred_element_type=F32, precision=precision)


def _rms(x):
    return x * lax.rsqrt(jnp.mean(x * x, axis=-1, keepdims=True) + EPS)


def _store_rows(ref, val, r0=0):
    n = val.shape[0]
    for s in range(ROW_SUB):
        ref[pl.ds(r0 * ROW_SUB + s, n, stride=ROW_SUB), :] = val[:, s * LANES:(s + 1) * LANES]


def _load_rows(ref, r0=0, n=None):
    n = ref.shape[0] // ROW_SUB if n is None else n
    return jnp.concatenate([ref[pl.ds(r0 * ROW_SUB + s, n, stride=ROW_SUB), :] for s in range(ROW_SUB)], axis=1)


def _row_tile(ref, r):
    return ref.at[pl.ds(pl.multiple_of(r * ROW_SUB, ROW_SUB), ROW_SUB)]


def _const_spec(shape):
    return pl.BlockSpec(shape, lambda *_: (0,) * len(shape), pipeline_mode=pl.Buffered(1))


def _mod_kernel(c_ref, w_ref, b_ref, o_ref):
    c = c_ref[...]
    o_ref[...] = _dot(c * jax.nn.sigmoid(c), w_ref[...], HIGHEST) + b_ref[...]


def _modulation(cvec, ada_w, ada_b):
    n, d = cvec.shape
    nout = ada_w.shape[1]
    return pl.pallas_call(
        _mod_kernel,
        grid=(nout // d,),
        in_specs=[pl.BlockSpec((n, d), lambda j: (0, 0)),
                  pl.BlockSpec((d, d), lambda j: (0, j)),
                  pl.BlockSpec((1, d), lambda j: (0, j))],
        out_specs=pl.BlockSpec((n, d), lambda j: (0, j)),
        out_shape=jax.ShapeDtypeStruct((n, nout), F32),
        name="adaln_mod",
    )(cvec, ada_w, ada_b.reshape(1, nout))


def _proj_kernel(xc_ref, xl_ref, mod_ref, g1_ref, w_ref, b_ref, cw_ref, cb_ref, wco_ref,
                 ycg_ref, sgb_ref, q_ref, k_ref, v_ref, so_ref, gt_ref, *, n_ctx_tiles, ctx_row):
    i = pl.program_id(0)
    x = jnp.where(i < n_ctx_tiles, xc_ref[...], xl_ref[...])
    h = (_rms(x) * g1_ref[...] * (1.0 + mod_ref[0, 1:2, :]) + mod_ref[0, 0:1, :]).astype(BF16)

    def proj(cols):
        return _dot(h, w_ref[:, cols[0]:cols[1]]) + b_ref[:, cols[0]:cols[1]]

    zc = proj(C_CONV)
    d = x.shape[1]
    u = zc[:, 2 * d:3 * d] * zc[:, 0:d]
    tm = x.shape[0]
    rowlen = jnp.where(i < n_ctx_tiles, ctx_row, GRID_W)
    pos = lax.broadcasted_iota(jnp.int32, (tm, 1), 0) & (rowlen - 1)
    u_prev = jnp.where(pos == 0, 0.0, pltpu.roll(u, 1, 0))
    u_next = jnp.where(pos == rowlen - 1, 0.0, pltpu.roll(u, tm - 1, 0))
    uc = u_prev * cw_ref[0:1, :] + u * cw_ref[1:2, :] + u_next * cw_ref[2:3, :] + cb_ref[...]
    yconv = _dot((zc[:, d:2 * d] * uc).astype(BF16), wco_ref[...])

    ycg_ref[...] = (jax.nn.sigmoid(proj(C_GA)) * yconv).astype(BF16)
    sgb_ref[...] = jax.nn.sigmoid(proj(C_GB)).astype(BF16)
    q_ref[...] = (proj(C_Q) * (D_QK ** -0.5)).astype(BF16)
    k_ref[...] = proj(C_K).astype(BF16)
    v_ref[...] = proj(C_V).astype(BF16)
    so_ref[...] = jax.nn.sigmoid(proj(C_O)).astype(BF16)
    gt_ref[...] = proj(C_G)


def _tri_masks():
    row = lax.broadcasted_iota(jnp.int32, (TM, TM), 0)
    col = lax.broadcasted_iota(jnp.int32, (TM, TM), 1)
    return row >= col, row <= col


def _gate_cumsums(gf):
    lower, upper = _tri_masks()
    lf = jax.nn.log_sigmoid(gf)
    fwd_lane = lax.broadcasted_iota(jnp.int32, (1, LANES), 1) < N_HEADS
    bsum = jnp.where(fwd_lane, _dot(lower.astype(F32), lf, HIGHEST), _dot(upper.astype(F32), lf, HIGHEST))
    return lf, bsum


def _state_kernel(*refs, zero_init, emit_before, emit_after):
    it = iter(refs)
    kf_ref, vf_ref, gf_ref, kb_ref, vb_ref, gb_ref = (next(it) for _ in range(6))
    if not zero_init:
        c0_ref, n0_ref, m0_ref = (next(it) for _ in range(3))
    if emit_before:
        cbf_ref, cbb_ref, nbf_ref, nbb_ref, mbf_ref, mbb_ref = (next(it) for _ in range(6))
    if emit_after:
        ca_ref, na_ref, ma_ref = (next(it) for _ in range(3))
    c_scr, n_scr, m_scr = (next(it) for _ in range(3))
    c = pl.program_id(1)

    @pl.when(c == 0)
    def _():
        if zero_init:
            c_scr[...] = jnp.zeros_like(c_scr)
            n_scr[...] = jnp.zeros_like(n_scr)
            m_scr[...] = jnp.zeros_like(m_scr)
        else:
            c_scr[...] = c0_ref[0]
            n_scr[...] = n0_ref[0]
            m_scr[...] = m0_ref[0]

    if emit_before:
        cbf_ref[0] = c_scr[0:N_HEADS]
        cbb_ref[0] = c_scr[N_HEADS:N_HD]
        nbf_ref[0] = n_scr[0:N_HEADS]
        nbb_ref[0] = n_scr[N_HEADS:N_HD]
        mbf_ref[0] = m_scr[...]
        mbb_ref[0] = m_scr[...]

    fwd_lane = lax.broadcasted_iota(jnp.int32, (1, LANES), 1) < N_HEADS
    gi = jnp.where(fwd_lane, gf_ref[:, 0:LANES], gb_ref[:, 0:LANES])
    gfg = jnp.where(fwd_lane, gf_ref[:, LANES:2 * LANES], gb_ref[:, LANES:2 * LANES])
    lf, bsum = _gate_cumsums(gfg)
    total = jnp.sum(lf, axis=0, keepdims=True)
    g = total - bsum + gi
    m_prev = m_scr[...]
    m_new = jnp.maximum(total + m_prev, jnp.max(g, axis=0, keepdims=True))
    wk = jnp.exp(g - m_new)
    decay = jnp.exp(total + m_prev - m_new)
    for hd in range(N_HD):
        h = hd % N_HEADS
        k_ref, v_ref = (kf_ref, vf_ref) if hd < N_HEADS else (kb_ref, vb_ref)
        k_tile = k_ref[:, (h // 2) * LANES:(h // 2 + 1) * LANES].astype(F32)
        wkk = wk[:, hd:hd + 1] * jnp.where(_head_lanes(h), k_tile, 0.0)
        vh = v_ref[:, h * D_V:(h + 1) * D_V]
        dec = decay[:, hd:hd + 1]
        tn = (((0,), (0,)), ((), ()))
        if emit_before:
            c_scr[hd] = dec * c_scr[hd] + lax.dot_general(vh, wkk.astype(BF16), tn, preferred_element_type=F32)
        else:
            upd = lax.dot_general(wkk.astype(BF16), vh, tn, preferred_element_type=F32)
            c_scr[hd] = dec * c_scr[hd] + upd[(h % 2) * D_QK:(h % 2 + 1) * D_QK, :]
        n_scr[hd:hd + 1, :] = dec * n_scr[hd:hd + 1, :] + jnp.sum(wkk, axis=0, keepdims=True)
    m_scr[...] = m_new

    if emit_after:
        @pl.when(c == pl.num_programs(1) - 1)
        def _():
            ca_ref[0] = c_scr[...]
            na_ref[0] = n_scr[...]
            ma_ref[0] = m_scr[...]


def _state_scan(k, v, gt, tile0, n_seq, n_chunk, init, emit_before, emit_after):
    d = k.shape[1]
    cshape = (D_V, LANES) if emit_before else (D_QK, D_V)
    fwd = lambda s, c: (tile0 + s * n_chunk + c, 0)
    bwd = lambda s, c: (tile0 + s * n_chunk + n_chunk - 1 - c, 0)
    dv = v.shape[1]
    in_specs = [pl.BlockSpec((TM, d), fwd), pl.BlockSpec((TM, dv), fwd), pl.BlockSpec((TM, 2 * LANES), fwd),
                pl.BlockSpec((TM, d), bwd), pl.BlockSpec((TM, dv), bwd), pl.BlockSpec((TM, 2 * LANES), bwd)]
    args = [k, v, gt, k, v, gt]
    if init is not None:
        in_specs += [pl.BlockSpec((1, N_HD) + cshape, lambda s, c: (s, 0, 0, 0)),
                     pl.BlockSpec((1, N_HD, LANES), lambda s, c: (s, 0, 0)),
                     pl.BlockSpec((1, 1, LANES), lambda s, c: (s, 0, 0))]
        args += list(init)
    out_specs, out_shape = [], []
    n_tot = n_seq * n_chunk
    if emit_before:
        cf = lambda s, c: (s * n_chunk + c, 0, 0, 0)
        cb = lambda s, c: (s * n_chunk + n_chunk - 1 - c, 0, 0, 0)
        nf = lambda s, c: (s * n_chunk + c, 0, 0)
        nb = lambda s, c: (s * n_chunk + n_chunk - 1 - c, 0, 0)
        out_specs += [pl.BlockSpec((1, N_HEADS) + cshape, cf), pl.BlockSpec((1, N_HEADS) + cshape, cb),
                      pl.BlockSpec((1, N_HEADS, LANES), nf), pl.BlockSpec((1, N_HEADS, LANES), nb),
                      pl.BlockSpec((1, 1, LANES), nf), pl.BlockSpec((1, 1, LANES), nb)]
        out_shape += [jax.ShapeDtypeStruct((n_tot, N_HEADS) + cshape, F32)] * 2
        out_shape += [jax.ShapeDtypeStruct((n_tot, N_HEADS, LANES), F32)] * 2
        out_shape += [jax.ShapeDtypeStruct((n_tot, 1, LANES), F32)] * 2
    if emit_after:
        out_specs += [pl.BlockSpec((1, N_HD, D_QK, D_V), lambda s, c: (s, 0, 0, 0)),
                      pl.BlockSpec((1, N_HD, LANES), lambda s, c: (s, 0, 0)),
                      pl.BlockSpec((1, 1, LANES), lambda s, c: (s, 0, 0))]
        out_shape += [jax.ShapeDtypeStruct((n_seq, N_HD, D_QK, D_V), F32),
                      jax.ShapeDtypeStruct((n_seq, N_HD, LANES), F32),
                      jax.ShapeDtypeStruct((n_seq, 1, LANES), F32)]
    return pl.pallas_call(
        functools.partial(_state_kernel, zero_init=init is None, emit_before=emit_before, emit_after=emit_after),
        grid=(n_seq, n_chunk),
        in_specs=in_specs,
        out_specs=out_specs,
        out_shape=out_shape,
        scratch_shapes=[pltpu.VMEM((N_HD,) + cshape, F32), pltpu.VMEM((N_HD, LANES), F32),
                        pltpu.VMEM((1, LANES), F32)],
        compiler_params=pltpu.CompilerParams(dimension_semantics=("arbitrary", "arbitrary"),
                                             vmem_limit_bytes=VMEM_LIMIT),
        name="mlstm_state_scan",
    )(*args)


def _mix_kernel(q_ref, k_ref, v_ref, so_ref, sgb_ref, ycg_ref, gt_ref, xc_ref, xl_ref, mod_ref,
                cbf_ref, cbb_ref, nbf_ref, nbb_ref, mbf_ref, mbb_ref,
                gmh_ref, wmo_ref, wo_ref, g2_ref, rw_ref, rb_ref,
                x1_ref, h2_ref, route_ref, routet_ref, cnt_ref, hm_scr, carry_scr, *, n_ctx_tiles, n_experts):
    i = pl.program_id(0)
    is_lat = i >= n_ctx_tiles
    lat_f = is_lat.astype(F32)
    lower, upper = _tri_masks()
    gt_t = gt_ref[...].T
    gi_t = gt_t[0:N_HD, :]
    lf_t = jax.nn.log_sigmoid(gt_t[LANES:LANES + N_HD, :])
    fwd_row = lax.broadcasted_iota(jnp.int32, (N_HD, 1), 0) < N_HEADS
    bsum_t = jnp.where(fwd_row, _dot(lf_t, upper.astype(F32), HIGHEST), _dot(lf_t, lower.astype(F32), HIGHEST))
    a_t = gi_t - bsum_t
    m_row = jnp.where(lax.broadcasted_iota(jnp.int32, (1, LANES), 1) < N_HEADS, mbf_ref[0], mbb_ref[0]) * lat_f
    m_sq = jnp.where(lax.broadcasted_iota(jnp.int32, (LANES, 1), 0) == 0, m_row, 0.0)
    m_prev = m_sq.T[0:N_HD, 0:1]
    a_c = jnp.concatenate([a_t, jnp.zeros((LANES - N_HD, TM), F32)], axis=0).T
    row0 = lax.broadcasted_iota(jnp.int32, (LANES, 1), 0) == 0
    ones_rows = jnp.where(row0, 1.0, 0.0).astype(BF16) * jnp.ones((1, TM), BF16)

    for h in range(N_HEADS):
        hs = slice(h * D_V, (h + 1) * D_V)
        pair = slice((h // 2) * LANES, (h // 2 + 1) * LANES)
        qh = jnp.where(_head_lanes(h), q_ref[:, pair], jnp.zeros((), BF16))
        kh = k_ref[:, pair]
        vext_t = jnp.concatenate([v_ref[:, hs].T, ones_rows], axis=0)
        kq = lax.dot_general(kh, qh, (((1,), (1,)), ((), ())), preferred_element_type=F32)
        hsum = None
        for d in range(2):
            hd = d * N_HEADS + h
            mask = upper if d == 0 else lower
            c_t = (cbf_ref if d == 0 else cbb_ref)[0, h] * lat_f
            n_r = (nbf_ref if d == 0 else nbb_ref)[0, h:h + 1, :] * lat_f
            a_b = jnp.where(mask, a_c[:, hd:hd + 1], -jnp.inf)
            mrow = jnp.maximum(m_prev[hd:hd + 1, :], jnp.max(a_b, axis=0, keepdims=True))
            e = jnp.exp(a_b - mrow)
            nd = _dot(vext_t, (kq * e).astype(BF16))
            cext_t = jnp.concatenate([c_t, jnp.where(row0, n_r, 0.0)], axis=0).astype(BF16)
            qc = lax.dot_general(cext_t, qh, (((1,), (1,)), ((), ())), preferred_element_type=F32)
            wi = jnp.exp(m_prev[hd:hd + 1, :] - mrow)
            num = nd[0:D_V, :] + wi * qc[0:D_V, :]
            den = nd[D_V:D_V + 1, :] + wi * qc[D_V:D_V + 1, :]
            r = 1.0 / jnp.maximum(jnp.abs(den), jnp.exp(-(bsum_t[hd:hd + 1, :] + mrow)))
            hsum = num * r if hsum is None else hsum + num * r
        hn = hsum * lax.rsqrt(jnp.mean(hsum * hsum, axis=0, keepdims=True) + EPS)
        hm_scr[:, hs] = (hn.T * gmh_ref[:, hs] * so_ref[:, hs].astype(F32)).astype(BF16)

    ym = _dot(hm_scr[...], wmo_ref[...])
    mix = (ycg_ref[...].astype(F32) + sgb_ref[...].astype(F32) * ym).astype(BF16)
    x1 = jnp.where(is_lat, xl_ref[...], xc_ref[...]) + mod_ref[0, 2:3, :] * _dot(mix, wo_ref[...])
    x1_ref[...] = x1
    h2 = _rms(x1) * g2_ref[...] * (1.0 + mod_ref[0, 4:5, :]) + mod_ref[0, 3:4, :]
    _store_rows(h2_ref, h2)

    lane = lax.broadcasted_iota(jnp.int32, (TM, LANES), 1)
    rw = rw_ref[...]
    h2_hi, rw_hi = h2.astype(BF16), rw.astype(BF16)
    h2_lo, rw_lo = (h2 - h2_hi.astype(F32)).astype(BF16), (rw - rw_hi.astype(F32)).astype(BF16)
    logits = _dot(h2_hi, rw_hi) + (_dot(h2_hi, rw_lo) + _dot(h2_lo, rw_hi)) + rb_ref[...]
    work = jnp.where(lane < n_experts, logits, -jnp.inf)
    sels, exps, idxs = [], [], []
    top = None
    for _ in range(TOP_K):
        mx = jnp.max(work, axis=-1, keepdims=True)
        ix = jnp.min(jnp.where(work == mx, lane, LANES), axis=-1, keepdims=True)
        sel = lane == ix
        work = jnp.where(sel, -jnp.inf, work)
        top = mx if top is None else top
        sels.append(sel)
        idxs.append(ix.astype(F32))
        exps.append(jnp.exp(mx - top))
    inv = 1.0 / functools.reduce(lambda p, q: p + q, exps)

    @pl.when(i == 0)
    def _():
        carry_scr[...] = jnp.zeros_like(carry_scr)

    onehot = functools.reduce(lambda p, q: p + q, [jnp.where(s, 1.0, 0.0) for s in sels])
    row = lax.broadcasted_iota(jnp.int32, (TM, TM), 0)
    col = lax.broadcasted_iota(jnp.int32, (TM, TM), 1)
    before = _dot((row > col).astype(BF16), onehot.astype(BF16)) + carry_scr[...]
    carry_scr[...] += jnp.sum(onehot, axis=0, keepdims=True)
    cnt_ref[...] = carry_scr[...]
    route = jnp.zeros((TM, LANES), F32)
    for j in range(TOP_K):
        slot = jnp.sum(jnp.where(sels[j], before, 0.0), axis=-1, keepdims=True)
        route = jnp.where(lane == j, idxs[j], route)
        route = jnp.where(lane == TOP_K + j, slot, route)
        route = jnp.where(lane == 2 * TOP_K + j, exps[j] * inv, route)
    route_ref[...] = route
    routet_ref[...] = route.T[0:8, :].astype(jnp.int32)


def _dispatch_kernel(pos_ref, h2_ref, xs_ref, sem):
    def copy(t, j):
        return pltpu.make_async_copy(_row_tile(h2_ref, t), _row_tile(xs_ref, pos_ref[TOP_K * t + j]), sem)

    def start(t, carry):
        for j in range(TOP_K):
            copy(t, j).start(priority=j % 2)
        return carry

    tm = h2_ref.shape[0] // ROW_SUB
    lax.fori_loop(0, tm, start, 0, unroll=ISSUE_UNROLL)
    for j in range(TOP_K):
        pltpu.make_async_copy(h2_ref, xs_ref.at[pl.ds(0, tm * ROW_SUB)], sem).wait()


def _expert_kernel(tile_ref, exp_ref, flag_ref, lo_ref, hi_ref, nxt_ref,
                   xs_ref, wg_hbm, bg_ref, wl_hbm, bl_ref, wd_hbm, bd_ref, ys_ref, w_scr, wf_scr, wsem):
    w = pl.program_id(0)
    flags = flag_ref[w]

    def fetch(e, slot):
        return [pltpu.make_async_copy(hbm.at[e], wf_scr.at[slot, m], wsem.at[slot])
                for m, hbm in enumerate((wg_hbm, wl_hbm, wd_hbm))]

    @pl.when(w == 0)
    def _():
        for cp in fetch(exp_ref[0], 0):
            cp.start()

    @pl.when((flags & 4) != 0)
    def _():
        slot = (flags >> 4) & 1
        for cp in fetch(exp_ref[w], slot):
            cp.wait()
        for m in range(3):
            w_scr[m] = wf_scr[slot, m].astype(BF16)

        @pl.when((flags & 8) != 0)
        def _():
            for cp in fetch(nxt_ref[w], 1 - slot):
                cp.start()

    tmx = ys_ref.shape[0] // ROW_SUB

    def run(r0, m):
        x = _load_rows(xs_ref, r0, m).astype(BF16)
        gt = jnp.minimum(_dot(x, w_scr[0]) + bg_ref[0], SWIGLU_LIMIT)
        lin = jnp.clip(_dot(x, w_scr[1]) + bl_ref[0], -SWIGLU_LIMIT, SWIGLU_LIMIT)
        act = gt * jax.nn.sigmoid(SWIGLU_ALPHA * gt) * (lin + 1.0)
        y = _dot(act.astype(BF16), w_scr[2]) + bd_ref[0]
        rows = r0 + lax.broadcasted_iota(jnp.int32, (m, 1), 0)
        mine = (rows >= lo_ref[w]) & (rows < hi_ref[w])

        @pl.when((flags & 2) != 0)
        def _():
            _store_rows(ys_ref, jnp.where(mine, y, 0.0), r0)
            if m < tmx:
                _store_rows(ys_ref, jnp.zeros((tmx - m, y.shape[1]), F32), m if r0 == 0 else 0)

        @pl.when((flags & 2) == 0)
        def _():
            _store_rows(ys_ref, jnp.where(mine, y, _load_rows(ys_ref, r0, m)), r0)

    half = (flags >> 5) & 3
    valid = (flags & 1) != 0
    pl.when(valid & (half == 0))(lambda: run(0, tmx))
    pl.when(valid & (half == 1))(lambda: run(0, tmx // 2))
    pl.when(valid & (half == 2))(lambda: run(tmx // 2, tmx // 2))


def _combine_kernel(pos_ref, posn_ref, route_ref, x1_ref, mod_ref, fg_ref, ys_ref, outc_ref, outl_ref,
                    buf, sem, *, n_ctx_tiles):
    i = pl.program_id(0)
    n = pl.num_programs(0)
    tm = x1_ref.shape[0]

    def copy(p_ref, slot, t, j):
        return pltpu.make_async_copy(_row_tile(ys_ref, p_ref[TOP_K * t + j]), _row_tile(buf.at[slot, j], t), sem.at[slot])

    def start_all(p_ref, slot):
        def body(t, carry):
            for j in range(TOP_K):
                copy(p_ref, slot, t, j).start(priority=j % 2)
            return carry
        lax.fori_loop(0, tm, body, 0, unroll=ISSUE_UNROLL)

    @pl.when(i == 0)
    def _():
        start_all(pos_ref, 0)

    @pl.when(i + 1 < n)
    def _():
        start_all(posn_ref, (i + 1) % 2)

    slot = i % 2

    for j in range(TOP_K):
        pltpu.make_async_copy(ys_ref.at[pl.ds(0, tm * ROW_SUB)], buf.at[slot, j], sem.at[slot]).wait()
    acc = None
    for j in range(TOP_K):
        term = route_ref[:, 2 * TOP_K + j:2 * TOP_K + j + 1] * _load_rows(buf.at[slot, j])
        acc = term if acc is None else acc + term
    out = _rms(x1_ref[...] + mod_ref[0, 5:6, :] * acc) * fg_ref[...]

    @pl.when(i < n_ctx_tiles)
    def _():
        outc_ref[...] = out

    @pl.when(i >= n_ctx_tiles)
    def _():
        outl_ref[...] = out


def _work_items(counts, n_experts, n_rows, tmx):
    n_items_max = n_rows // tmx + n_experts - 1
    cnt = counts.astype(jnp.int32)
    offs = jnp.concatenate([jnp.zeros((1,), jnp.int32), jnp.cumsum(cnt)])
    first_tile = offs[:-1] // tmx
    n_it = jnp.where(cnt > 0, (offs[1:] - 1) // tmx - first_tile + 1, 0)
    it_start = jnp.concatenate([jnp.zeros((1,), jnp.int32), jnp.cumsum(n_it)])
    total = it_start[-1]
    w = jnp.arange(n_items_max, dtype=jnp.int32)
    wc = jnp.minimum(w, total - 1)
    e = jnp.sum((it_start[None, 1:] <= wc[:, None]).astype(jnp.int32), axis=1)
    e = jnp.minimum(e, n_experts - 1)
    is_e = e[:, None] == jnp.arange(n_experts, dtype=jnp.int32)
    at_e = lambda a: jnp.sum(jnp.where(is_e, a[None, :], 0), axis=1)
    tile = at_e(first_tile) + wc - at_e(it_start[:-1])
    valid = w < total
    prev = lambda a: jnp.concatenate([jnp.full((1,), -1, jnp.int32), a[:-1]])
    ids = jnp.arange(n_experts, dtype=jnp.int32)
    later = (ids[None, :] > ids[:, None]) & (cnt > 0)[None, :]
    nxt = jnp.min(jnp.where(later, ids[None, :], n_experts), axis=1)
    has_next = nxt < n_experts
    slot = (jnp.cumsum((cnt > 0).astype(jnp.int32)) - 1) & 1
    flags = (valid.astype(jnp.int32) + 2 * (valid & (tile != prev(tile))).astype(jnp.int32)
             + 4 * (valid & (e != prev(e))).astype(jnp.int32)
             + 8 * at_e(has_next.astype(jnp.int32)) + 16 * at_e(slot))
    lo = jnp.clip(at_e(offs[:-1]) - tile * tmx, 0, tmx)
    hi = jnp.clip(at_e(offs[1:]) - tile * tmx, 0, tmx)
    flags = flags + 32 * jnp.where(hi <= tmx // 2, 1, jnp.where(lo >= tmx // 2, 2, 0))
    return offs, tile, e, flags, lo, hi, at_e(jnp.minimum(nxt, n_experts - 1))


def _pack_in_proj(w, b):
    d = w.shape[0]
    o_g = 5 * d + 2 * D_QKH
    o_ga, o_gb = o_g + 4 * N_HEADS, o_g + 4 * N_HEADS + d

    def gates(m):
        gz = jnp.zeros((m.shape[0], LANES - N_HD), m.dtype)
        i_f, f_f, i_b, f_b = (m[:, o_g + j * N_HEADS:o_g + (j + 1) * N_HEADS] for j in range(4))
        return jnp.concatenate([i_f, i_b, gz, f_f, f_b, gz], axis=1)

    def pack(m):
        return jnp.concatenate([m[:, 0:o_g], m[:, o_ga:o_gb], m[:, o_gb:o_gb + d], gates(m)], axis=1)

    return pack(w.astype(BF16)), pack(b.reshape(1, -1))


def kernel(x_prompt, x_sample, c, state_C, state_n, state_m, c_ctx, ada_w, ada_b, norm1_g, norm2_g, w_in, b_in,
           conv_w, conv_b, w_conv_out, mh_norm_g, w_m_out, w_o, router_w, router_b, w_gate, b_gate, w_lin, b_lin,
           w_down, b_down, final_g):
    nb, seq, d = x_prompt.shape
    nd, dseq, _ = x_sample.shape
    n_experts = w_gate.shape[1]
    assert d == _D and w_in.shape[0] == 1 and seq == TM and dseq % TM == 0 and TM % GRID_W == 0
    assert w_gate.shape[-1] == d
    t_ctx, t_lat = nb * seq, nd * dseq
    n_tok = t_ctx + t_lat
    n_ctx_tiles, n_tiles = t_ctx // TM, n_tok // TM
    lat_chunks = dseq // TM
    n_lat_tiles = n_tiles - n_ctx_tiles

    n_c = 1 + nd
    n_cp = -(-n_c // 8) * 8
    cvec = jnp.concatenate([c_ctx[None, :], c, jnp.zeros((n_cp - n_c, d), F32)], axis=0)
    mod = _modulation(cvec, ada_w[0], ada_b[0]).reshape(n_cp, N_MOD, d)
    mod = jnp.pad(mod, ((0, 0), (0, 8 - N_MOD), (0, 0)))

    def mod_row(tile_tokens):
        ctx_t, per_seq = t_ctx // tile_tokens, dseq // tile_tokens
        return lambda i, *_: (jnp.where(i < ctx_t, 0, 1 + (i - ctx_t) // per_seq), 0, 0)

    x_ctx, x_lat = x_prompt.reshape(t_ctx, d), x_sample.reshape(t_lat, d)
    ctx_spec = pl.BlockSpec((TM, d), lambda i: (jnp.minimum(i, n_ctx_tiles - 1), 0))
    lat_spec = pl.BlockSpec((TM, d), lambda i: (jnp.maximum(i - n_ctx_tiles, 0), 0))
    w_all, b_all = _pack_in_proj(w_in[0], b_in[0])
    tile = lambda cols: pl.BlockSpec((TM, cols), lambda i: (i, 0))
    rows = lambda n: pl.BlockSpec((n * ROW_SUB, LANES), lambda i: (i, 0))
    params = pltpu.CompilerParams(dimension_semantics=("arbitrary",), vmem_limit_bytes=VMEM_LIMIT)
    bf = lambda cols: jax.ShapeDtypeStruct((n_tok, cols), BF16)

    tmp = next(t for t in (TM_PROJ, TM) if t_ctx % t == 0 and dseq % t == 0)
    np_ctx = t_ctx // tmp
    ptile = lambda cols: pl.BlockSpec((tmp, cols), lambda i: (i, 0))
    ycg, sgb, q, k, v, so, gt = pl.pallas_call(
        functools.partial(_proj_kernel, n_ctx_tiles=np_ctx, ctx_row=seq),
        grid=(n_tok // tmp,),
        in_specs=[pl.BlockSpec((tmp, d), lambda i: (jnp.minimum(i, np_ctx - 1), 0)),
                  pl.BlockSpec((tmp, d), lambda i: (jnp.maximum(i - np_ctx, 0), 0)),
                  pl.BlockSpec((1, 8, d), mod_row(tmp)), _const_spec((1, d)),
                  _const_spec((d, W_COLS)), _const_spec((1, W_COLS)), _const_spec((3, d)), _const_spec((1, d)),
                  _const_spec((d, d))],
        out_specs=[ptile(d), ptile(d), ptile(D_QKH), ptile(D_QKH), ptile(d), ptile(d), ptile(2 * LANES)],
        out_shape=[bf(d), bf(d), bf(D_QKH), bf(D_QKH), bf(d), bf(d),
                   jax.ShapeDtypeStruct((n_tok, 2 * LANES), F32)],
        compiler_params=params,
        name="in_proj_conv",
    )(x_ctx, x_lat, mod, norm1_g, w_all, b_all, conv_w[0], conv_b, w_conv_out[0].astype(BF16))

    c_new, n_new, m_new = _state_scan(k, v, gt, 0, nb, 1, None, False, True)
    def on_head_lanes(a):
        even = lax.broadcasted_iota(jnp.int32, (N_HD,) + (1,) * (a.ndim - 2), 0) % 2 == 0
        lo = jnp.pad(a, [(0, 0)] * (a.ndim - 1) + [(0, LANES - D_QK)])
        hi = jnp.pad(a, [(0, 0)] * (a.ndim - 1) + [(LANES - D_QK, 0)])
        return jnp.where(even, lo, hi)

    init = (on_head_lanes(jnp.swapaxes(state_C[:, 0].astype(F32).reshape(nd, N_HD, D_QK, D_V), 2, 3)),
            on_head_lanes(state_n[:, 0].astype(F32).reshape(nd, N_HD, D_QK)),
            jnp.pad(state_m[:, 0].astype(F32).reshape(nd, 1, N_HD), ((0, 0), (0, 0), (0, LANES - N_HD))))
    cbf, cbb, nbf, nbb, mbf, mbb = _state_scan(k, v, gt, n_ctx_tiles, nd, lat_chunks, init, True, False)

    lat_idx = lambda i: jnp.maximum(i - n_ctx_tiles, 0)
    st4 = pl.BlockSpec((1, N_HEADS, D_V, LANES), lambda i: (lat_idx(i), 0, 0, 0))
    st3 = pl.BlockSpec((1, N_HEADS, LANES), lambda i: (lat_idx(i), 0, 0))
    st1 = pl.BlockSpec((1, 1, LANES), lambda i: (lat_idx(i), 0, 0))
    rw = jnp.pad(router_w[0], ((0, 0), (0, LANES - n_experts)))
    rb = jnp.pad(router_b[0], (0, LANES - n_experts)).reshape(1, LANES)
    x1, h2, route, route_t, counts = pl.pallas_call(
        functools.partial(_mix_kernel, n_ctx_tiles=n_ctx_tiles, n_experts=n_experts),
        grid=(n_tiles,),
        in_specs=[tile(D_QKH), tile(D_QKH), tile(d), tile(d), tile(d), tile(d), tile(2 * LANES), ctx_spec, lat_spec,
                  pl.BlockSpec((1, 8, d), mod_row(TM)), st4, st4, st3, st3, st1, st1,
                  _const_spec((1, d)), _const_spec((d, d)), _const_spec((d, d)), _const_spec((1, d)),
                  _const_spec((d, LANES)), _const_spec((1, LANES))],
        out_specs=[tile(d), rows(TM), tile(LANES), pl.BlockSpec((8, TM), lambda i: (0, i)),
                   pl.BlockSpec((1, LANES), lambda i: (0, 0))],
        out_shape=[jax.ShapeDtypeStruct((n_tok, d), F32), jax.ShapeDtypeStruct((n_tok * ROW_SUB, LANES), F32),
                   jax.ShapeDtypeStruct((n_tok, LANES), F32), jax.ShapeDtypeStruct((8, n_tok), jnp.int32),
                   jax.ShapeDtypeStruct((1, LANES), F32)],
        scratch_shapes=[pltpu.VMEM((TM, d), BF16), pltpu.VMEM((1, LANES), F32)],
        compiler_params=params,
        name="mlstm_mix_router",
    )(q, k, v, so, sgb, ycg, gt, x_ctx, x_lat, mod, cbf, cbb, nbf, nbb, mbf, mbb,
      mh_norm_g, w_m_out[0].astype(BF16), w_o[0].astype(BF16), norm2_g, rw, rb)

    n_rows = TOP_K * n_tok
    offs, it_tile, it_exp, it_flags, it_lo, it_hi, it_nxt = _work_items(
        counts[0, :n_experts], n_experts, n_rows, TMX)
    is_exp = route_t[0:TOP_K, :, None] == jnp.arange(n_experts, dtype=jnp.int32)
    pos_t = jnp.sum(jnp.where(is_exp, offs[:n_experts], 0), axis=-1) + route_t[TOP_K:2 * TOP_K]
    pos = pos_t.T.reshape(n_rows)
    pos_spec = lambda tm, f: pl.BlockSpec((TOP_K * tm,), f, memory_space=pltpu.SMEM)
    any_spec = pl.BlockSpec(memory_space=pl.ANY)
    tmd = next(t for t in (TM_DISPATCH, TM) if n_tok % t == 0)
    tmc = next(t for t in (TM_COMBINE, TM) if t_ctx % t == 0 and dseq % t == 0)

    xs = pl.pallas_call(
        _dispatch_kernel,
        grid=(n_tok // tmd,),
        in_specs=[pos_spec(tmd, lambda i: (i,)), rows(tmd)],
        out_specs=any_spec,
        out_shape=jax.ShapeDtypeStruct((n_rows * ROW_SUB, LANES), F32),
        scratch_shapes=[pltpu.SemaphoreType.DMA],
        compiler_params=params,
        name="moe_dispatch",
    )(pos, h2)

    dff = w_gate.shape[-1]
    wspec = lambda a, b_: pl.BlockSpec((1, a, b_), lambda w, tl, ex, *_: (ex[w], 0, 0))
    ys = pl.pallas_call(
        _expert_kernel,
        grid_spec=pltpu.PrefetchScalarGridSpec(
            num_scalar_prefetch=6,
            grid=(it_tile.shape[0],),
            in_specs=[pl.BlockSpec((TMX * ROW_SUB, LANES), lambda w, tl, *_: (tl[w], 0)),
                      any_spec, wspec(1, dff), any_spec, wspec(1, dff), any_spec, wspec(1, d)],
            out_specs=pl.BlockSpec((TMX * ROW_SUB, LANES), lambda w, tl, *_: (tl[w], 0)),
            scratch_shapes=[pltpu.VMEM((3, d, dff), BF16), pltpu.VMEM((2, 3, d, dff), F32),
                            pltpu.SemaphoreType.DMA((2,))]),
        out_shape=jax.ShapeDtypeStruct((n_rows * ROW_SUB, LANES), F32),
        compiler_params=params,
        name="moe_experts",
    )(it_tile, it_exp, it_flags, it_lo, it_hi, it_nxt, xs, w_gate[0], b_gate[0].reshape(n_experts, 1, dff), w_lin[0],
      b_lin[0].reshape(n_experts, 1, dff), w_down[0], b_down[0].reshape(n_experts, 1, d))

    nc_ctx, nc_all = t_ctx // tmc, n_tok // tmc
    ctx_i = lambda i: (jnp.minimum(i, nc_ctx - 1), 0)
    lat_i = lambda i: (jnp.maximum(i - nc_ctx, 0), 0)
    ctile = lambda cols: pl.BlockSpec((tmc, cols), lambda i: (i, 0))
    y_prompt, y_sample = pl.pallas_call(
        functools.partial(_combine_kernel, n_ctx_tiles=nc_ctx),
        grid=(nc_all,),
        in_specs=[pos_spec(tmc, lambda i: (i,)), pos_spec(tmc, lambda i: (jnp.minimum(i + 1, nc_all - 1),)),
                  ctile(LANES), ctile(d), pl.BlockSpec((1, 8, d), mod_row(tmc)), _const_spec((1, d)), any_spec],
        out_specs=[pl.BlockSpec((tmc, d), ctx_i), pl.BlockSpec((tmc, d), lat_i)],
        out_shape=[jax.ShapeDtypeStruct((t_ctx, d), F32), jax.ShapeDtypeStruct((t_lat, d), F32)],
        scratch_shapes=[pltpu.VMEM((2, TOP_K, tmc * ROW_SUB, LANES), F32), pltpu.SemaphoreType.DMA((2,))],
        compiler_params=params,
        name="moe_combine",
    )(pos, pos, route, x1, mod, final_g.reshape(1, d), ys)

    y_prompt = y_prompt.reshape(nb, seq, d)
    y_sample = y_sample.reshape(nd, dseq, d)
    new_c = c_new.reshape(nb, 1, 2, N_HEADS, D_QK, D_V)
    odd_head = (lax.broadcasted_iota(jnp.int32, (1, N_HD, 1), 1) % 2) == 1
    new_n = jnp.where(odd_head, n_new[:, :, D_QK:], n_new[:, :, :D_QK]).reshape(nb, 1, 2, N_HEADS, D_QK)
    new_m = m_new[:, 0, :N_HD].reshape(nb, 1, 2, N_HEADS)
    return (y_prompt, y_sample, new_c, new_n, new_m)
```

```python
import functools

import jax
import jax.numpy as jnp
from jax import lax
from jax.experimental import pallas as pl
from jax.experimental.pallas import tpu as pltpu

F32 = jnp.float32
BF16 = jnp.bfloat16
HIGHEST = lax.Precision.HIGHEST

N_HEADS = 8
D_QK = 64
D_V = 128
GRID_W = 64
TOP_K = 4
SWIGLU_LIMIT = 7.0
SWIGLU_ALPHA = 1.702
EPS = 1e-6
N_MOD = 6

LANES = 128
ROW_SUB = 8
TM = 256
TMX = 512
TM_PROJ = 512
TM_DISPATCH = 2048
TM_COMBINE = 512
ISSUE_UNROLL = 4
N_HD = 2 * N_HEADS
VMEM_LIMIT = 56 * 1024 * 1024

_D = 1024
C_CONV = (0, 3 * _D)
D_QKH = N_HEADS * D_QK
C_Q = (3 * _D, 3 * _D + D_QKH)
C_K = (C_Q[1], C_Q[1] + D_QKH)
C_V = (C_K[1], C_K[1] + _D)
C_O = (C_V[1], C_V[1] + _D)
C_GA = (C_O[1], C_O[1] + _D)
C_GB = (C_GA[1], C_GA[1] + _D)
C_G = (C_GB[1], C_GB[1] + 2 * LANES)
W_COLS = C_G[1]


def _head_lanes(h):
    lane = lax.broadcasted_iota(jnp.int32, (1, LANES), 1)
    return (lane < D_QK) if h % 2 == 0 else (lane >= D_QK)


def _dot(a, b, precision=None):
    return jnp.dot(a, b, preferred_element_type=F32, precision=precision)


def _rms(x):
    return x * lax.rsqrt(jnp.mean(x * x, axis=-1, keepdims=True) + EPS)


def _store_rows(ref, val, r0=0):
    n = val.shape[0]
    for s in range(ROW_SUB):
        ref[pl.ds(r0 * ROW_SUB + s, n, stride=ROW_SUB), :] = val[:, s * LANES:(s + 1) * LANES]


def _load_rows(ref, r0=0, n=None):
    n = ref.shape[0] // ROW_SUB if n is None else n
    return jnp.concatenate([ref[pl.ds(r0 * ROW_SUB + s, n, stride=ROW_SUB), :] for s in range(ROW_SUB)], axis=1)


def _row_tile(ref, r):
    return ref.at[pl.ds(pl.multiple_of(r * ROW_SUB, ROW_SUB), ROW_SUB)]


def _const_spec(shape):
    return pl.BlockSpec(shape, lambda *_: (0,) * len(shape), pipeline_mode=pl.Buffered(1))


def _mod_kernel(c_ref, w_ref, b_ref, o_ref):
    c = c_ref[...]
    o_ref[...] = _dot(c * jax.nn.sigmoid(c), w_ref[...], HIGHEST) + b_ref[...]


def _modulation(cvec, ada_w, ada_b):
    n, d = cvec.shape
    nout = ada_w.shape[1]
    return pl.pallas_call(
        _mod_kernel,
        grid=(nout // d,),
        in_specs=[pl.BlockSpec((n, d), lambda j: (0, 0)),
                  pl.BlockSpec((d, d), lambda j: (0, j)),
                  pl.BlockSpec((1, d), lambda j: (0, j))],
        out_specs=pl.BlockSpec((n, d), lambda j: (0, j)),
        out_shape=jax.ShapeDtypeStruct((n, nout), F32),
        name="adaln_mod",
    )(cvec, ada_w, ada_b.reshape(1, nout))


def _proj_kernel(xc_ref, xl_ref, mod_ref, g1_ref, w_ref, b_ref, cw_ref, cb_ref, wco_ref,
                 ycg_ref, sgb_ref, q_ref, k_ref, v_ref, so_ref, gt_ref, *, n_ctx_tiles, ctx_row):
    i = pl.program_id(0)
    x = jnp.where(i < n_ctx_tiles, xc_ref[...], xl_ref[...])
    h = (_rms(x) * g1_ref[...] * (1.0 + mod_ref[0, 1:2, :]) + mod_ref[0, 0:1, :]).astype(BF16)

    def proj(cols):
        return _dot(h, w_ref[:, cols[0]:cols[1]]) + b_ref[:, cols[0]:cols[1]]

    zc = proj(C_CONV)
    d = x.shape[1]
    u = zc[:, 2 * d:3 * d] * zc[:, 0:d]
    tm = x.shape[0]
    rowlen = jnp.where(i < n_ctx_tiles, ctx_row, GRID_W)
    pos = lax.broadcasted_iota(jnp.int32, (tm, 1), 0) & (rowlen - 1)
    u_prev = jnp.where(pos == 0, 0.0, pltpu.roll(u, 1, 0))
    u_next = jnp.where(pos == rowlen - 1, 0.0, pltpu.roll(u, tm - 1, 0))
    uc = u_prev * cw_ref[0:1, :] + u * cw_ref[1:2, :] + u_next * cw_ref[2:3, :] + cb_ref[...]
    yconv = _dot((zc[:, d:2 * d] * uc).astype(BF16), wco_ref[...])

    ycg_ref[...] = (jax.nn.sigmoid(proj(C_GA)) * yconv).astype(BF16)
    sgb_ref[...] = jax.nn.sigmoid(proj(C_GB)).astype(BF16)
    q_ref[...] = (proj(C_Q) * (D_QK ** -0.5)).astype(BF16)
    k_ref[...] = proj(C_K).astype(BF16)
    v_ref[...] = proj(C_V).astype(BF16)
    so_ref[...] = jax.nn.sigmoid(proj(C_O)).astype(BF16)
    gt_ref[...] = proj(C_G)


def _tri_masks():
    row = lax.broadcasted_iota(jnp.int32, (TM, TM), 0)
    col = lax.broadcasted_iota(jnp.int32, (TM, TM), 1)
    return row >= col, row <= col


def _gate_cumsums(gf):
    lower, upper = _tri_masks()
    lf = jax.nn.log_sigmoid(gf)
    fwd_lane = lax.broadcasted_iota(jnp.int32, (1, LANES), 1) < N_HEADS
    bsum = jnp.where(fwd_lane, _dot(lower.astype(F32), lf, HIGHEST), _dot(upper.astype(F32), lf, HIGHEST))
    return lf, bsum


def _state_kernel(*refs, zero_init, emit_before, emit_after):
    it = iter(refs)
    kf_ref, vf_ref, gf_ref, kb_ref, vb_ref, gb_ref = (next(it) for _ in range(6))
    if not zero_init:
        c0_ref, n0_ref, m0_ref = (next(it) for _ in range(3))
    if emit_before:
        cbf_ref, cbb_ref, nbf_ref, nbb_ref, mbf_ref, mbb_ref = (next(it) for _ in range(6))
    if emit_after:
        ca_ref, na_ref, ma_ref = (next(it) for _ in range(3))
    c_scr, n_scr, m_scr = (next(it) for _ in range(3))
    c = pl.program_id(1)

    @pl.when(c == 0)
    def _():
        if zero_init:
            c_scr[...] = jnp.zeros_like(c_scr)
            n_scr[...] = jnp.zeros_like(n_scr)
            m_scr[...] = jnp.zeros_like(m_scr)
        else:
            c_scr[...] = c0_ref[0]
            n_scr[...] = n0_ref[0]
            m_scr[...] = m0_ref[0]

    if emit_before:
        cbf_ref[0] = c_scr[0:N_HEADS]
        cbb_ref[0] = c_scr[N_HEADS:N_HD]
        nbf_ref[0] = n_scr[0:N_HEADS]
        nbb_ref[0] = n_scr[N_HEADS:N_HD]
        mbf_ref[0] = m_scr[...]
        mbb_ref[0] = m_scr[...]

    fwd_lane = lax.broadcasted_iota(jnp.int32, (1, LANES), 1) < N_HEADS
    gi = jnp.where(fwd_lane, gf_ref[:, 0:LANES], gb_ref[:, 0:LANES])
    gfg = jnp.where(fwd_lane, gf_ref[:, LANES:2 * LANES], gb_ref[:, LANES:2 * LANES])
    lf, bsum = _gate_cumsums(gfg)
    total = jnp.sum(lf, axis=0, keepdims=True)
    g = total - bsum + gi
    m_prev = m_scr[...]
    m_new = jnp.maximum(total + m_prev, jnp.max(g, axis=0, keepdims=True))
    wk = jnp.exp(g - m_new)
    decay = jnp.exp(total + m_prev - m_new)
    for hd in range(N_HD):
        h = hd % N_HEADS
        k_ref, v_ref = (kf_ref, vf_ref) if hd < N_HEADS else (kb_ref, vb_ref)
        k_tile = k_ref[:, (h // 2) * LANES:(h // 2 + 1) * LANES].astype(F32)
        wkk = wk[:, hd:hd + 1] * jnp.where(_head_lanes(h), k_tile, 0.0)
        vh = v_ref[:, h * D_V:(h + 1) * D_V]
        dec = decay[:, hd:hd + 1]
        tn = (((0,), (0,)), ((), ()))
        if emit_before:
            c_scr[hd] = dec * c_scr[hd] + lax.dot_general(vh, wkk.astype(BF16), tn, preferred_element_type=F32)
        else:
            upd = lax.dot_general(wkk.astype(BF16), vh, tn, preferred_element_type=F32)
            c_scr[hd] = dec * c_scr[hd] + upd[(h % 2) * D_QK:(h % 2 + 1) * D_QK, :]
        n_scr[hd:hd + 1, :] = dec * n_scr[hd:hd + 1, :] + jnp.sum(wkk, axis=0, keepdims=True)
    m_scr[...] = m_new

    if emit_after:
        @pl.when(c == pl.num_programs(1) - 1)
        def _():
            ca_ref[0] = c_scr[...]
            na_ref[0] = n_scr[...]
            ma_ref[0] = m_scr[...]


def _state_scan(k, v, gt, tile0, n_seq, n_chunk, init, emit_before, emit_after):
    d = k.shape[1]
    cshape = (D_V, LANES) if emit_before else (D_QK, D_V)
    fwd = lambda s, c: (tile0 + s * n_chunk + c, 0)
    bwd = lambda s, c: (tile0 + s * n_chunk + n_chunk - 1 - c, 0)
    dv = v.shape[1]
    in_specs = [pl.BlockSpec((TM, d), fwd), pl.BlockSpec((TM, dv), fwd), pl.BlockSpec((TM, 2 * LANES), fwd),
                pl.BlockSpec((TM, d), bwd), pl.BlockSpec((TM, dv), bwd), pl.BlockSpec((TM, 2 * LANES), bwd)]
    args = [k, v, gt, k, v, gt]
    if init is not None:
        in_specs += [pl.BlockSpec((1, N_HD) + cshape, lambda s, c: (s, 0, 0, 0)),
                     pl.BlockSpec((1, N_HD, LANES), lambda s, c: (s, 0, 0)),
                     pl.BlockSpec((1, 1, LANES), lambda s, c: (s, 0, 0))]
        args += list(init)
    out_specs, out_shape = [], []
    n_tot = n_seq * n_chunk
    if emit_before:
        cf = lambda s, c: (s * n_chunk + c, 0, 0, 0)
        cb = lambda s, c: (s * n_chunk + n_chunk - 1 - c, 0, 0, 0)
        nf = lambda s, c: (s * n_chunk + c, 0, 0)
        nb = lambda s, c: (s * n_chunk + n_chunk - 1 - c, 0, 0)
        out_specs += [pl.BlockSpec((1, N_HEADS) + cshape, cf), pl.BlockSpec((1, N_HEADS) + cshape, cb),
                      pl.BlockSpec((1, N_HEADS, LANES), nf), pl.BlockSpec((1, N_HEADS, LANES), nb),
                      pl.BlockSpec((1, 1, LANES), nf), pl.BlockSpec((1, 1, LANES), nb)]
        out_shape += [jax.ShapeDtypeStruct((n_tot, N_HEADS) + cshape, F32)] * 2
        out_shape += [jax.ShapeDtypeStruct((n_tot, N_HEADS, LANES), F32)] * 2
        out_shape += [jax.ShapeDtypeStruct((n_tot, 1, LANES), F32)] * 2
    if emit_after:
        out_specs += [pl.BlockSpec((1, N_HD, D_QK, D_V), lambda s, c: (s, 0, 0, 0)),
                      pl.BlockSpec((1, N_HD, LANES), lambda s, c: (s, 0, 0)),
                      pl.BlockSpec((1, 1, LANES), lambda s, c: (s, 0, 0))]
        out_shape += [jax.ShapeDtypeStruct((n_seq, N_HD, D_QK, D_V), F32),
                      jax.ShapeDtypeStruct((n_seq, N_HD, LANES), F32),
                      jax.ShapeDtypeStruct((n_seq, 1, LANES), F32)]
    return pl.pallas_call(
        functools.partial(_state_kernel, zero_init=init is None, emit_before=emit_before, emit_after=emit_after),
        grid=(n_seq, n_chunk),
        in_specs=in_specs,
        out_specs=out_specs,
        out_shape=out_shape,
        scratch_shapes=[pltpu.VMEM((N_HD,) + cshape, F32), pltpu.VMEM((N_HD, LANES), F32),
                        pltpu.VMEM((1, LANES), F32)],
        compiler_params=pltpu.CompilerParams(dimension_semantics=("arbitrary", "arbitrary"),
                                             vmem_limit_bytes=VMEM_LIMIT),
        name="mlstm_state_scan",
    )(*args)


def _mix_kernel(q_ref, k_ref, v_ref, so_ref, sgb_ref, ycg_ref, gt_ref, xc_ref, xl_ref, mod_ref,
                cbf_ref, cbb_ref, nbf_ref, nbb_ref, mbf_ref, mbb_ref,
                gmh_ref, wmo_ref, wo_ref, g2_ref, rw_ref, rb_ref,
                x1_ref, h2_ref, route_ref, routet_ref, cnt_ref, hm_scr, carry_scr, *, n_ctx_tiles, n_experts):
    i = pl.program_id(0)
    is_lat = i >= n_ctx_tiles
    lat_f = is_lat.astype(F32)
    lower, upper = _tri_masks()
    gt_t = gt_ref[...].T
    gi_t = gt_t[0:N_HD, :]
    lf_t = jax.nn.log_sigmoid(gt_t[LANES:LANES + N_HD, :])
    fwd_row = lax.broadcasted_iota(jnp.int32, (N_HD, 1), 0) < N_HEADS
    bsum_t = jnp.where(fwd_row, _dot(lf_t, upper.astype(F32), HIGHEST), _dot(lf_t, lower.astype(F32), HIGHEST))
    a_t = gi_t - bsum_t
    m_row = jnp.where(lax.broadcasted_iota(jnp.int32, (1, LANES), 1) < N_HEADS, mbf_ref[0], mbb_ref[0]) * lat_f
    m_sq = jnp.where(lax.broadcasted_iota(jnp.int32, (LANES, 1), 0) == 0, m_row, 0.0)
    m_prev = m_sq.T[0:N_HD, 0:1]
    a_c = jnp.concatenate([a_t, jnp.zeros((LANES - N_HD, TM), F32)], axis=0).T
    row0 = lax.broadcasted_iota(jnp.int32, (LANES, 1), 0) == 0
    ones_rows = jnp.where(row0, 1.0, 0.0).astype(BF16) * jnp.ones((1, TM), BF16)

    for h in range(N_HEADS):
        hs = slice(h * D_V, (h + 1) * D_V)
        pair = slice((h // 2) * LANES, (h // 2 + 1) * LANES)
        qh = jnp.where(_head_lanes(h), q_ref[:, pair], jnp.zeros((), BF16))
        kh = k_ref[:, pair]
        vext_t = jnp.concatenate([v_ref[:, hs].T, ones_rows], axis=0)
        kq = lax.dot_general(kh, qh, (((1,), (1,)), ((), ())), preferred_element_type=F32)
        hsum = None
        for d in range(2):
            hd = d * N_HEADS + h
            mask = upper if d == 0 else lower
            c_t = (cbf_ref if d == 0 else cbb_ref)[0, h] * lat_f
            n_r = (nbf_ref if d == 0 else nbb_ref)[0, h:h + 1, :] * lat_f
            a_b = jnp.where(mask, a_c[:, hd:hd + 1], -jnp.inf)
            mrow = jnp.maximum(m_prev[hd:hd + 1, :], jnp.max(a_b, axis=0, keepdims=True))
            e = jnp.exp(a_b - mrow)
            nd = _dot(vext_t, (kq * e).astype(BF16))
            cext_t = jnp.concatenate([c_t, jnp.where(row0, n_r, 0.0)], axis=0).astype(BF16)
            qc = lax.dot_general(cext_t, qh, (((1,), (1,)), ((), ())), preferred_element_type=F32)
            wi = jnp.exp(m_prev[hd:hd + 1, :] - mrow)
            num = nd[0:D_V, :] + wi * qc[0:D_V, :]
            den = nd[D_V:D_V + 1, :] + wi * qc[D_V:D_V + 1, :]
            r = 1.0 / jnp.maximum(jnp.abs(den), jnp.exp(-(bsum_t[hd:hd + 1, :] + mrow)))
            hsum = num * r if hsum is None else hsum + num * r
        hn = hsum * lax.rsqrt(jnp.mean(hsum * hsum, axis=0, keepdims=True) + EPS)
        hm_scr[:, hs] = (hn.T * gmh_ref[:, hs] * so_ref[:, hs].astype(F32)).astype(BF16)

    ym = _dot(hm_scr[...], wmo_ref[...])
    mix = (ycg_ref[...].astype(F32) + sgb_ref[...].astype(F32) * ym).astype(BF16)
    x1 = jnp.where(is_lat, xl_ref[...], xc_ref[...]) + mod_ref[0, 2:3, :] * _dot(mix, wo_ref[...])
    x1_ref[...] = x1
    h2 = _rms(x1) * g2_ref[...] * (1.0 + mod_ref[0, 4:5, :]) + mod_ref[0, 3:4, :]
    _store_rows(h2_ref, h2)

    lane = lax.broadcasted_iota(jnp.int32, (TM, LANES), 1)
    rw = rw_ref[...]
    h2_hi, rw_hi = h2.astype(BF16), rw.astype(BF16)
    h2_lo, rw_lo = (h2 - h2_hi.astype(F32)).astype(BF16), (rw - rw_hi.astype(F32)).astype(BF16)
    logits = _dot(h2_hi, rw_hi) + (_dot(h2_hi, rw_lo) + _dot(h2_lo, rw_hi)) + rb_ref[...]
    work = jnp.where(lane < n_experts, logits, -jnp.inf)
    sels, exps, idxs = [], [], []
    top = None
    for _ in range(TOP_K):
        mx = jnp.max(work, axis=-1, keepdims=True)
        ix = jnp.min(jnp.where(work == mx, lane, LANES), axis=-1, keepdims=True)
        sel = lane == ix
        work = jnp.where(sel, -jnp.inf, work)
        top = mx if top is None else top
        sels.append(sel)
        idxs.append(ix.astype(F32))
        exps.append(jnp.exp(mx - top))
    inv = 1.0 / functools.reduce(lambda p, q: p + q, exps)

    @pl.when(i == 0)
    def _():
        carry_scr[...] = jnp.zeros_like(carry_scr)

    onehot = functools.reduce(lambda p, q: p + q, [jnp.where(s, 1.0, 0.0) for s in sels])
    row = lax.broadcasted_iota(jnp.int32, (TM, TM), 0)
    col = lax.broadcasted_iota(jnp.int32, (TM, TM), 1)
    before = _dot((row > col).astype(BF16), onehot.astype(BF16)) + carry_scr[...]
    carry_scr[...] += jnp.sum(onehot, axis=0, keepdims=True)
    cnt_ref[...] = carry_scr[...]
    route = jnp.zeros((TM, LANES), F32)
    for j in range(TOP_K):
        slot = jnp.sum(jnp.where(sels[j], before, 0.0), axis=-1, keepdims=True)
        route = jnp.where(lane == j, idxs[j], route)
        route = jnp.where(lane == TOP_K + j, slot, route)
        route = jnp.where(lane == 2 * TOP_K + j, exps[j] * inv, route)
    route_ref[...] = route
    routet_ref[...] = route.T[0:8, :].astype(jnp.int32)


def _dispatch_kernel(pos_ref, h2_ref, xs_ref, sem):
    def copy(t, j):
        return pltpu.make_async_copy(_row_tile(h2_ref, t), _row_tile(xs_ref, pos_ref[TOP_K * t + j]), sem)

    def start(t, carry):
        for j in range(TOP_K):
            copy(t, j).start(priority=j % 2)
        return carry

    tm = h2_ref.shape[0] // ROW_SUB
    lax.fori_loop(0, tm, start, 0, unroll=ISSUE_UNROLL)
    for j in range(TOP_K):
        pltpu.make_async_copy(h2_ref, xs_ref.at[pl.ds(0, tm * ROW_SUB)], sem).wait()


def _expert_kernel(tile_ref, exp_ref, flag_ref, lo_ref, hi_ref, nxt_ref,
                   xs_ref, wg_hbm, bg_ref, wl_hbm, bl_ref, wd_hbm, bd_ref, ys_ref, w_scr, wf_scr, wsem):
    w = pl.program_id(0)
    flags = flag_ref[w]

    def fetch(e, slot):
        return [pltpu.make_async_copy(hbm.at[e], wf_scr.at[slot, m], wsem.at[slot])
                for m, hbm in enumerate((wg_hbm, wl_hbm, wd_hbm))]

    @pl.when(w == 0)
    def _():
        for cp in fetch(exp_ref[0], 0):
            cp.start()

    @pl.when((flags & 4) != 0)
    def _():
        slot = (flags >> 4) & 1
        for cp in fetch(exp_ref[w], slot):
            cp.wait()
        for m in range(3):
            w_scr[m] = wf_scr[slot, m].astype(BF16)

        @pl.when((flags & 8) != 0)
        def _():
            for cp in fetch(nxt_ref[w], 1 - slot):
                cp.start()

    tmx = ys_ref.shape[0] // ROW_SUB

    def run(r0, m):
        x = _load_rows(xs_ref, r0, m).astype(BF16)
        gt = jnp.minimum(_dot(x, w_scr[0]) + bg_ref[0], SWIGLU_LIMIT)
        lin = jnp.clip(_dot(x, w_scr[1]) + bl_ref[0], -SWIGLU_LIMIT, SWIGLU_LIMIT)
        act = gt * jax.nn.sigmoid(SWIGLU_ALPHA * gt) * (lin + 1.0)
        y = _dot(act.astype(BF16), w_scr[2]) + bd_ref[0]

        @pl.when((flags & 2) != 0)
        def _():
            _store_rows(ys_ref, y, r0)
            if m < tmx:
                _store_rows(ys_ref, jnp.zeros((tmx - m, y.shape[1]), F32), m if r0 == 0 else 0)

        @pl.when((flags & 2) == 0)
        def _():
            rows = r0 + lax.broadcasted_iota(jnp.int32, (m, 1), 0)
            mine = (rows >= lo_ref[w]) & (rows < hi_ref[w])
            _store_rows(ys_ref, jnp.where(mine, y, _load_rows(ys_ref, r0, m)), r0)

    half = (flags >> 5) & 3
    valid = (flags & 1) != 0
    pl.when(valid & (half == 0))(lambda: run(0, tmx))
    pl.when(valid & (half == 1))(lambda: run(0, tmx // 2))
    pl.when(valid & (half == 2))(lambda: run(tmx // 2, tmx // 2))


def _combine_kernel(pos_ref, posn_ref, route_ref, x1_ref, mod_ref, fg_ref, ys_ref, outc_ref, outl_ref,
                    buf, sem, *, n_ctx_tiles):
    i = pl.program_id(0)
    n = pl.num_programs(0)
    tm = x1_ref.shape[0]

    def copy(p_ref, slot, t, j):
        return pltpu.make_async_copy(_row_tile(ys_ref, p_ref[TOP_K * t + j]), _row_tile(buf.at[slot, j], t), sem.at[slot])

    def start_all(p_ref, slot):
        def body(t, carry):
            for j in range(TOP_K):
                copy(p_ref, slot, t, j).start(priority=j % 2)
            return carry
        lax.fori_loop(0, tm, body, 0, unroll=ISSUE_UNROLL)

    @pl.when(i == 0)
    def _():
        start_all(pos_ref, 0)

    @pl.when(i + 1 < n)
    def _():
        start_all(posn_ref, (i + 1) % 2)

    slot = i % 2

    for j in range(TOP_K):
        pltpu.make_async_copy(ys_ref.at[pl.ds(0, tm * ROW_SUB)], buf.at[slot, j], sem.at[slot]).wait()
    acc = None
    for j in range(TOP_K):
        term = route_ref[:, 2 * TOP_K + j:2 * TOP_K + j + 1] * _load_rows(buf.at[slot, j])
        acc = term if acc is None else acc + term
    out = _rms(x1_ref[...] + mod_ref[0, 5:6, :] * acc) * fg_ref[...]

    @pl.when(i < n_ctx_tiles)
    def _():
        outc_ref[...] = out

    @pl.when(i >= n_ctx_tiles)
    def _():
        outl_ref[...] = out


def _work_items(counts, n_experts, n_rows, tmx):
    n_items_max = n_rows // tmx + n_experts - 1
    cnt = counts.astype(jnp.int32)
    offs = jnp.concatenate([jnp.zeros((1,), jnp.int32), jnp.cumsum(cnt)])
    first_tile = offs[:-1] // tmx
    n_it = jnp.where(cnt > 0, (offs[1:] - 1) // tmx - first_tile + 1, 0)
    it_start = jnp.concatenate([jnp.zeros((1,), jnp.int32), jnp.cumsum(n_it)])
    total = it_start[-1]
    w = jnp.arange(n_items_max, dtype=jnp.int32)
    wc = jnp.minimum(w, total - 1)
    e = jnp.sum((it_start[None, 1:] <= wc[:, None]).astype(jnp.int32), axis=1)
    e = jnp.minimum(e, n_experts - 1)
    is_e = e[:, None] == jnp.arange(n_experts, dtype=jnp.int32)
    at_e = lambda a: jnp.sum(jnp.where(is_e, a[None, :], 0), axis=1)
    tile = at_e(first_tile) + wc - at_e(it_start[:-1])
    valid = w < total
    prev = lambda a: jnp.concatenate([jnp.full((1,), -1, jnp.int32), a[:-1]])
    ids = jnp.arange(n_experts, dtype=jnp.int32)
    later = (ids[None, :] > ids[:, None]) & (cnt > 0)[None, :]
    nxt = jnp.min(jnp.where(later, ids[None, :], n_experts), axis=1)
    has_next = nxt < n_experts
    slot = (jnp.cumsum((cnt > 0).astype(jnp.int32)) - 1) & 1
    flags = (valid.astype(jnp.int32) + 2 * (valid & (tile != prev(tile))).astype(jnp.int32)
             + 4 * (valid & (e != prev(e))).astype(jnp.int32)
             + 8 * at_e(has_next.astype(jnp.int32)) + 16 * at_e(slot))
    lo = jnp.clip(at_e(offs[:-1]) - tile * tmx, 0, tmx)
    hi = jnp.clip(at_e(offs[1:]) - tile * tmx, 0, tmx)
    flags = flags + 32 * jnp.where(hi <= tmx // 2, 1, jnp.where(lo >= tmx // 2, 2, 0))
    return offs, tile, e, flags, lo, hi, at_e(jnp.minimum(nxt, n_experts - 1))


def _pack_in_proj(w, b):
    d = w.shape[0]
    o_g = 5 * d + 2 * D_QKH
    o_ga, o_gb = o_g + 4 * N_HEADS, o_g + 4 * N_HEADS + d

    def gates(m):
        gz = jnp.zeros((m.shape[0], LANES - N_HD), m.dtype)
        i_f, f_f, i_b, f_b = (m[:, o_g + j * N_HEADS:o_g + (j + 1) * N_HEADS] for j in range(4))
        return jnp.concatenate([i_f, i_b, gz, f_f, f_b, gz], axis=1)

    def pack(m):
        return jnp.concatenate([m[:, 0:o_g], m[:, o_ga:o_gb], m[:, o_gb:o_gb + d], gates(m)], axis=1)

    return pack(w.astype(BF16)), pack(b.reshape(1, -1))


def kernel(x_prompt, x_sample, c, state_C, state_n, state_m, c_ctx, ada_w, ada_b, norm1_g, norm2_g, w_in, b_in,
           conv_w, conv_b, w_conv_out, mh_norm_g, w_m_out, w_o, router_w, router_b, w_gate, b_gate, w_lin, b_lin,
           w_down, b_down, final_g):
    nb, seq, d = x_prompt.shape
    nd, dseq, _ = x_sample.shape
    n_experts = w_gate.shape[1]
    assert d == _D and w_in.shape[0] == 1 and seq == TM and dseq % TM == 0 and TM % GRID_W == 0
    assert w_gate.shape[-1] == d
    t_ctx, t_lat = nb * seq, nd * dseq
    n_tok = t_ctx + t_lat
    n_ctx_tiles, n_tiles = t_ctx // TM, n_tok // TM
    lat_chunks = dseq // TM
    n_lat_tiles = n_tiles - n_ctx_tiles

    n_c = 1 + nd
    n_cp = -(-n_c // 8) * 8
    cvec = jnp.concatenate([c_ctx[None, :], c, jnp.zeros((n_cp - n_c, d), F32)], axis=0)
    mod = _modulation(cvec, ada_w[0], ada_b[0]).reshape(n_cp, N_MOD, d)
    mod = jnp.pad(mod, ((0, 0), (0, 8 - N_MOD), (0, 0)))

    def mod_row(tile_tokens):
        ctx_t, per_seq = t_ctx // tile_tokens, dseq // tile_tokens
        return lambda i, *_: (jnp.where(i < ctx_t, 0, 1 + (i - ctx_t) // per_seq), 0, 0)

    x_ctx, x_lat = x_prompt.reshape(t_ctx, d), x_sample.reshape(t_lat, d)
    ctx_spec = pl.BlockSpec((TM, d), lambda i: (jnp.minimum(i, n_ctx_tiles - 1), 0))
    lat_spec = pl.BlockSpec((TM, d), lambda i: (jnp.maximum(i - n_ctx_tiles, 0), 0))
    w_all, b_all = _pack_in_proj(w_in[0], b_in[0])
    tile = lambda cols: pl.BlockSpec((TM, cols), lambda i: (i, 0))
    rows = lambda n: pl.BlockSpec((n * ROW_SUB, LANES), lambda i: (i, 0))
    params = pltpu.CompilerParams(dimension_semantics=("arbitrary",), vmem_limit_bytes=VMEM_LIMIT)
    bf = lambda cols: jax.ShapeDtypeStruct((n_tok, cols), BF16)

    tmp = next(t for t in (TM_PROJ, TM) if t_ctx % t == 0 and dseq % t == 0)
    np_ctx = t_ctx // tmp
    ptile = lambda cols: pl.BlockSpec((tmp, cols), lambda i: (i, 0))
    ycg, sgb, q, k, v, so, gt = pl.pallas_call(
        functools.partial(_proj_kernel, n_ctx_tiles=np_ctx, ctx_row=seq),
        grid=(n_tok // tmp,),
        in_specs=[pl.BlockSpec((tmp, d), lambda i: (jnp.minimum(i, np_ctx - 1), 0)),
                  pl.BlockSpec((tmp, d), lambda i: (jnp.maximum(i - np_ctx, 0), 0)),
                  pl.BlockSpec((1, 8, d), mod_row(tmp)), _const_spec((1, d)),
                  _const_spec((d, W_COLS)), _const_spec((1, W_COLS)), _const_spec((3, d)), _const_spec((1, d)),
                  _const_spec((d, d))],
        out_specs=[ptile(d), ptile(d), ptile(D_QKH), ptile(D_QKH), ptile(d), ptile(d), ptile(2 * LANES)],
        out_shape=[bf(d), bf(d), bf(D_QKH), bf(D_QKH), bf(d), bf(d),
                   jax.ShapeDtypeStruct((n_tok, 2 * LANES), F32)],
        compiler_params=params,
        name="in_proj_conv",
    )(x_ctx, x_lat, mod, norm1_g, w_all, b_all, conv_w[0], conv_b, w_conv_out[0].astype(BF16))

    c_new, n_new, m_new = _state_scan(k, v, gt, 0, nb, 1, None, False, True)
    def on_head_lanes(a):
        even = lax.broadcasted_iota(jnp.int32, (N_HD,) + (1,) * (a.ndim - 2), 0) % 2 == 0
        lo = jnp.pad(a, [(0, 0)] * (a.ndim - 1) + [(0, LANES - D_QK)])
        hi = jnp.pad(a, [(0, 0)] * (a.ndim - 1) + [(LANES - D_QK, 0)])
        return jnp.where(even, lo, hi)

    init = (on_head_lanes(jnp.swapaxes(state_C[:, 0].astype(F32).reshape(nd, N_HD, D_QK, D_V), 2, 3)),
            on_head_lanes(state_n[:, 0].astype(F32).reshape(nd, N_HD, D_QK)),
            jnp.pad(state_m[:, 0].astype(F32).reshape(nd, 1, N_HD), ((0, 0), (0, 0), (0, LANES - N_HD))))
    cbf, cbb, nbf, nbb, mbf, mbb = _state_scan(k, v, gt, n_ctx_tiles, nd, lat_chunks, init, True, False)

    lat_idx = lambda i: jnp.maximum(i - n_ctx_tiles, 0)
    st4 = pl.BlockSpec((1, N_HEADS, D_V, LANES), lambda i: (lat_idx(i), 0, 0, 0))
    st3 = pl.BlockSpec((1, N_HEADS, LANES), lambda i: (lat_idx(i), 0, 0))
    st1 = pl.BlockSpec((1, 1, LANES), lambda i: (lat_idx(i), 0, 0))
    rw = jnp.pad(router_w[0], ((0, 0), (0, LANES - n_experts)))
    rb = jnp.pad(router_b[0], (0, LANES - n_experts)).reshape(1, LANES)
    x1, h2, route, route_t, counts = pl.pallas_call(
        functools.partial(_mix_kernel, n_ctx_tiles=n_ctx_tiles, n_experts=n_experts),
        grid=(n_tiles,),
        in_specs=[tile(D_QKH), tile(D_QKH), tile(d), tile(d), tile(d), tile(d), tile(2 * LANES), ctx_spec, lat_spec,
                  pl.BlockSpec((1, 8, d), mod_row(TM)), st4, st4, st3, st3, st1, st1,
                  _const_spec((1, d)), _const_spec((d, d)), _const_spec((d, d)), _const_spec((1, d)),
                  _const_spec((d, LANES)), _const_spec((1, LANES))],
        out_specs=[tile(d), rows(TM), tile(LANES), pl.BlockSpec((8, TM), lambda i: (0, i)),
                   pl.BlockSpec((1, LANES), lambda i: (0, 0))],
        out_shape=[jax.ShapeDtypeStruct((n_tok, d), F32), jax.ShapeDtypeStruct((n_tok * ROW_SUB, LANES), F32),
                   jax.ShapeDtypeStruct((n_tok, LANES), F32), jax.ShapeDtypeStruct((8, n_tok), jnp.int32),
                   jax.ShapeDtypeStruct((1, LANES), F32)],
        scratch_shapes=[pltpu.VMEM((TM, d), BF16), pltpu.VMEM((1, LANES), F32)],
        compiler_params=params,
        name="mlstm_mix_router",
    )(q, k, v, so, sgb, ycg, gt, x_ctx, x_lat, mod, cbf, cbb, nbf, nbb, mbf, mbb,
      mh_norm_g, w_m_out[0].astype(BF16), w_o[0].astype(BF16), norm2_g, rw, rb)

    n_rows = TOP_K * n_tok
    offs, it_tile, it_exp, it_flags, it_lo, it_hi, it_nxt = _work_items(
        counts[0, :n_experts], n_experts, n_rows, TMX)
    is_exp = route_t[0:TOP_K, :, None] == jnp.arange(n_experts, dtype=jnp.int32)
    pos_t = jnp.sum(jnp.where(is_exp, offs[:n_experts], 0), axis=-1) + route_t[TOP_K:2 * TOP_K]
    pos = pos_t.T.reshape(n_rows)
    pos_spec = lambda tm, f: pl.BlockSpec((TOP_K * tm,), f, memory_space=pltpu.SMEM)
    any_spec = pl.BlockSpec(memory_space=pl.ANY)
    tmd = next(t for t in (TM_DISPATCH, TM) if n_tok % t == 0)
    tmc = next(t for t in (TM_COMBINE, TM) if t_ctx % t == 0 and dseq % t == 0)

    xs = pl.pallas_call(
        _dispatch_kernel,
        grid=(n_tok // tmd,),
        in_specs=[pos_spec(tmd, lambda i: (i,)), rows(tmd)],
        out_specs=any_spec,
        out_shape=jax.ShapeDtypeStruct((n_rows * ROW_SUB, LANES), F32),
        scratch_shapes=[pltpu.SemaphoreType.DMA],
        compiler_params=params,
        name="moe_dispatch",
    )(pos, h2)

    dff = w_gate.shape[-1]
    wspec = lambda a, b_: pl.BlockSpec((1, a, b_), lambda w, tl, ex, *_: (ex[w], 0, 0))
    ys = pl.pallas_call(
        _expert_kernel,
        grid_spec=pltpu.PrefetchScalarGridSpec(
            num_scalar_prefetch=6,
            grid=(it_tile.shape[0],),
            in_specs=[pl.BlockSpec((TMX * ROW_SUB, LANES), lambda w, tl, *_: (tl[w], 0)),
                      any_spec, wspec(1, dff), any_spec, wspec(1, dff), any_spec, wspec(1, d)],
            out_specs=pl.BlockSpec((TMX * ROW_SUB, LANES), lambda w, tl, *_: (tl[w], 0)),
            scratch_shapes=[pltpu.VMEM((3, d, dff), BF16), pltpu.VMEM((2, 3, d, dff), F32),
                            pltpu.SemaphoreType.DMA((2,))]),
        out_shape=jax.ShapeDtypeStruct((n_rows * ROW_SUB, LANES), F32),
        compiler_params=params,
        name="moe_experts",
    )(it_tile, it_exp, it_flags, it_lo, it_hi, it_nxt, xs, w_gate[0], b_gate[0].reshape(n_experts, 1, dff), w_lin[0],
      b_lin[0].reshape(n_experts, 1, dff), w_down[0], b_down[0].reshape(n_experts, 1, d))

    nc_ctx, nc_all = t_ctx // tmc, n_tok // tmc
    ctx_i = lambda i: (jnp.minimum(i, nc_ctx - 1), 0)
    lat_i = lambda i: (jnp.maximum(i - nc_ctx, 0), 0)
    ctile = lambda cols: pl.BlockSpec((tmc, cols), lambda i: (i, 0))
    y_prompt, y_sample = pl.pallas_call(
        functools.partial(_combine_kernel, n_ctx_tiles=nc_ctx),
        grid=(nc_all,),
        in_specs=[pos_spec(tmc, lambda i: (i,)), pos_spec(tmc, lambda i: (jnp.minimum(i + 1, nc_all - 1),)),
                  ctile(LANES), ctile(d), pl.BlockSpec((1, 8, d), mod_row(tmc)), _const_spec((1, d)), any_spec],
        out_specs=[pl.BlockSpec((tmc, d), ctx_i), pl.BlockSpec((tmc, d), lat_i)],
        out_shape=[jax.ShapeDtypeStruct((t_ctx, d), F32), jax.ShapeDtypeStruct((t_lat, d), F32)],
        scratch_shapes=[pltpu.VMEM((2, TOP_K, tmc * ROW_SUB, LANES), F32), pltpu.SemaphoreType.DMA((2,))],
        compiler_params=params,
        name="moe_combine",
    )(pos, pos, route, x1, mod, final_g.reshape(1, d), ys)

    y_prompt = y_prompt.reshape(nb, seq, d)
    y_sample = y_sample.reshape(nd, dseq, d)
    new_c = c_new.reshape(nb, 1, 2, N_HEADS, D_QK, D_V)
    odd_head = (lax.broadcasted_iota(jnp.int32, (1, N_HD, 1), 1) % 2) == 1
    new_n = jnp.where(odd_head, n_new[:, :, D_QK:], n_new[:, :, :D_QK]).reshape(nb, 1, 2, N_HEADS, D_QK)
    new_m = m_new[:, 0, :N_HD].reshape(nb, 1, 2, N_HEADS)
    return (y_prompt, y_sample, new_c, new_n, new_m)
```

```python
import functools

import jax
import jax.numpy as jnp
from jax import lax
from jax.experimental import pallas as pl
from jax.experimental.pallas import tpu as pltpu

F32 = jnp.float32
BF16 = jnp.bfloat16
HIGHEST = lax.Precision.HIGHEST

N_HEADS = 8
D_QK = 64
D_V = 128
GRID_W = 64
TOP_K = 4
SWIGLU_LIMIT = 7.0
SWIGLU_ALPHA = 1.702
EPS = 1e-6
N_MOD = 6

LANES = 128
ROW_SUB = 8
TM = 256
TMX = 512
TM_PROJ = 512
TM_DISPATCH = 2048
TM_COMBINE = 512
ISSUE_UNROLL = 8
N_HD = 2 * N_HEADS
VMEM_LIMIT = 56 * 1024 * 1024

_D = 1024
C_CONV = (0, 3 * _D)
D_QKH = N_HEADS * D_QK
C_Q = (3 * _D, 3 * _D + D_QKH)
C_K = (C_Q[1], C_Q[1] + D_QKH)
C_V = (C_K[1], C_K[1] + _D)
C_O = (C_V[1], C_V[1] + _D)
C_GA = (C_O[1], C_O[1] + _D)
C_GB = (C_GA[1], C_GA[1] + _D)
C_G = (C_GB[1], C_GB[1] + 2 * LANES)
W_COLS = C_G[1]


def _head_lanes(h):
    lane = lax.broadcasted_iota(jnp.int32, (1, LANES), 1)
    return (lane < D_QK) if h % 2 == 0 else (lane >= D_QK)


def _dot(a, b, precision=None):
    return jnp.dot(a, b, preferred_element_type=F32, precision=precision)


def _rms(x):
    return x * lax.rsqrt(jnp.mean(x * x, axis=-1, keepdims=True) + EPS)


def _store_rows(ref, val, r0=0):
    n = val.shape[0]
    for s in range(ROW_SUB):
        ref[pl.ds(r0 * ROW_SUB + s, n, stride=ROW_SUB), :] = val[:, s * LANES:(s + 1) * LANES]


def _load_rows(ref, r0=0, n=None):
    n = ref.shape[0] // ROW_SUB if n is None else n
    return jnp.concatenate([ref[pl.ds(r0 * ROW_SUB + s, n, stride=ROW_SUB), :] for s in range(ROW_SUB)], axis=1)


def _row_tile(ref, r):
    return ref.at[pl.ds(pl.multiple_of(r * ROW_SUB, ROW_SUB), ROW_SUB)]


def _const_spec(shape):
    return pl.BlockSpec(shape, lambda *_: (0,) * len(shape), pipeline_mode=pl.Buffered(1))


def _mod_kernel(c_ref, w_ref, b_ref, o_ref):
    c = c_ref[...]
    o_ref[...] = _dot(c * jax.nn.sigmoid(c), w_ref[...], HIGHEST) + b_ref[...]


def _modulation(cvec, ada_w, ada_b):
    n, d = cvec.shape
    nout = ada_w.shape[1]
    return pl.pallas_call(
        _mod_kernel,
        grid=(nout // d,),
        in_specs=[pl.BlockSpec((n, d), lambda j: (0, 0)),
                  pl.BlockSpec((d, d), lambda j: (0, j)),
                  pl.BlockSpec((1, d), lambda j: (0, j))],
        out_specs=pl.BlockSpec((n, d), lambda j: (0, j)),
        out_shape=jax.ShapeDtypeStruct((n, nout), F32),
        name="adaln_mod",
    )(cvec, ada_w, ada_b.reshape(1, nout))


def _proj_kernel(xc_ref, xl_ref, mod_ref, g1_ref, w_ref, b_ref, cw_ref, cb_ref, wco_ref,
                 ycg_ref, sgb_ref, q_ref, k_ref, v_ref, so_ref, gt_ref, *, n_ctx_tiles, ctx_row):
    i = pl.program_id(0)
    x = jnp.where(i < n_ctx_tiles, xc_ref[...], xl_ref[...])
    h = (_rms(x) * g1_ref[...] * (1.0 + mod_ref[0, 1:2, :]) + mod_ref[0, 0:1, :]).astype(BF16)

    def proj(cols):
        return _dot(h, w_ref[:, cols[0]:cols[1]]) + b_ref[:, cols[0]:cols[1]]

    zc = proj(C_CONV)
    d = x.shape[1]
    u = zc[:, 2 * d:3 * d] * zc[:, 0:d]
    tm = x.shape[0]
    rowlen = jnp.where(i < n_ctx_tiles, ctx_row, GRID_W)
    pos = lax.broadcasted_iota(jnp.int32, (tm, 1), 0) & (rowlen - 1)
    u_prev = jnp.where(pos == 0, 0.0, pltpu.roll(u, 1, 0))
    u_next = jnp.where(pos == rowlen - 1, 0.0, pltpu.roll(u, tm - 1, 0))
    uc = u_prev * cw_ref[0:1, :] + u * cw_ref[1:2, :] + u_next * cw_ref[2:3, :] + cb_ref[...]
    yconv = _dot((zc[:, d:2 * d] * uc).astype(BF16), wco_ref[...])

    ycg_ref[...] = (jax.nn.sigmoid(proj(C_GA)) * yconv).astype(BF16)
    sgb_ref[...] = jax.nn.sigmoid(proj(C_GB)).astype(BF16)
    q_ref[...] = (proj(C_Q) * (D_QK ** -0.5)).astype(BF16)
    k_ref[...] = proj(C_K).astype(BF16)
    v_ref[...] = proj(C_V).astype(BF16)
    so_ref[...] = jax.nn.sigmoid(proj(C_O)).astype(BF16)
    gt_ref[...] = proj(C_G)


def _tri_masks():
    row = lax.broadcasted_iota(jnp.int32, (TM, TM), 0)
    col = lax.broadcasted_iota(jnp.int32, (TM, TM), 1)
    return row >= col, row <= col


def _gate_cumsums(gf):
    lower, upper = _tri_masks()
    lf = jax.nn.log_sigmoid(gf)
    fwd_lane = lax.broadcasted_iota(jnp.int32, (1, LANES), 1) < N_HEADS
    bsum = jnp.where(fwd_lane, _dot(lower.astype(F32), lf, HIGHEST), _dot(upper.astype(F32), lf, HIGHEST))
    return lf, bsum


def _state_kernel(*refs, zero_init, emit_before, emit_after):
    it = iter(refs)
    kf_ref, vf_ref, gf_ref, kb_ref, vb_ref, gb_ref = (next(it) for _ in range(6))
    if not zero_init:
        c0_ref, n0_ref, m0_ref = (next(it) for _ in range(3))
    if emit_before:
        cbf_ref, cbb_ref, nbf_ref, nbb_ref, mbf_ref, mbb_ref = (next(it) for _ in range(6))
    if emit_after:
        ca_ref, na_ref, ma_ref = (next(it) for _ in range(3))
    c_scr, n_scr, m_scr = (next(it) for _ in range(3))
    c = pl.program_id(1)

    @pl.when(c == 0)
    def _():
        if zero_init:
            c_scr[...] = jnp.zeros_like(c_scr)
            n_scr[...] = jnp.zeros_like(n_scr)
            m_scr[...] = jnp.zeros_like(m_scr)
        else:
            c_scr[...] = c0_ref[0]
            n_scr[...] = n0_ref[0]
            m_scr[...] = m0_ref[0]

    if emit_before:
        cbf_ref[0] = c_scr[0:N_HEADS]
        cbb_ref[0] = c_scr[N_HEADS:N_HD]
        nbf_ref[0] = n_scr[0:N_HEADS]
        nbb_ref[0] = n_scr[N_HEADS:N_HD]
        mbf_ref[0] = m_scr[...]
        mbb_ref[0] = m_scr[...]

    fwd_lane = lax.broadcasted_iota(jnp.int32, (1, LANES), 1) < N_HEADS
    gi = jnp.where(fwd_lane, gf_ref[:, 0:LANES], gb_ref[:, 0:LANES])
    gfg = jnp.where(fwd_lane, gf_ref[:, LANES:2 * LANES], gb_ref[:, LANES:2 * LANES])
    lf, bsum = _gate_cumsums(gfg)
    total = jnp.sum(lf, axis=0, keepdims=True)
    g = total - bsum + gi
    m_prev = m_scr[...]
    m_new = jnp.maximum(total + m_prev, jnp.max(g, axis=0, keepdims=True))
    wk = jnp.exp(g - m_new)
    decay = jnp.exp(total + m_prev - m_new)
    for hd in range(N_HD):
        h = hd % N_HEADS
        k_ref, v_ref = (kf_ref, vf_ref) if hd < N_HEADS else (kb_ref, vb_ref)
        k_tile = k_ref[:, (h // 2) * LANES:(h // 2 + 1) * LANES].astype(F32)
        wkk = wk[:, hd:hd + 1] * jnp.where(_head_lanes(h), k_tile, 0.0)
        vh = v_ref[:, h * D_V:(h + 1) * D_V]
        dec = decay[:, hd:hd + 1]
        tn = (((0,), (0,)), ((), ()))
        if emit_before:
            c_scr[hd] = dec * c_scr[hd] + lax.dot_general(vh, wkk.astype(BF16), tn, preferred_element_type=F32)
        else:
            upd = lax.dot_general(wkk.astype(BF16), vh, tn, preferred_element_type=F32)
            c_scr[hd] = dec * c_scr[hd] + upd[(h % 2) * D_QK:(h % 2 + 1) * D_QK, :]
        n_scr[hd:hd + 1, :] = dec * n_scr[hd:hd + 1, :] + jnp.sum(wkk, axis=0, keepdims=True)
    m_scr[...] = m_new

    if emit_after:
        @pl.when(c == pl.num_programs(1) - 1)
        def _():
            ca_ref[0] = c_scr[...]
            na_ref[0] = n_scr[...]
            ma_ref[0] = m_scr[...]


def _state_scan(k, v, gt, tile0, n_seq, n_chunk, init, emit_before, emit_after):
    d = k.shape[1]
    cshape = (D_V, LANES) if emit_before else (D_QK, D_V)
    fwd = lambda s, c: (tile0 + s * n_chunk + c, 0)
    bwd = lambda s, c: (tile0 + s * n_chunk + n_chunk - 1 - c, 0)
    dv = v.shape[1]
    in_specs = [pl.BlockSpec((TM, d), fwd), pl.BlockSpec((TM, dv), fwd), pl.BlockSpec((TM, 2 * LANES), fwd),
                pl.BlockSpec((TM, d), bwd), pl.BlockSpec((TM, dv), bwd), pl.BlockSpec((TM, 2 * LANES), bwd)]
    args = [k, v, gt, k, v, gt]
    if init is not None:
        in_specs += [pl.BlockSpec((1, N_HD) + cshape, lambda s, c: (s, 0, 0, 0)),
                     pl.BlockSpec((1, N_HD, LANES), lambda s, c: (s, 0, 0)),
                     pl.BlockSpec((1, 1, LANES), lambda s, c: (s, 0, 0))]
        args += list(init)
    out_specs, out_shape = [], []
    n_tot = n_seq * n_chunk
    if emit_before:
        cf = lambda s, c: (s * n_chunk + c, 0, 0, 0)
        cb = lambda s, c: (s * n_chunk + n_chunk - 1 - c, 0, 0, 0)
        nf = lambda s, c: (s * n_chunk + c, 0, 0)
        nb = lambda s, c: (s * n_chunk + n_chunk - 1 - c, 0, 0)
        out_specs += [pl.BlockSpec((1, N_HEADS) + cshape, cf), pl.BlockSpec((1, N_HEADS) + cshape, cb),
                      pl.BlockSpec((1, N_HEADS, LANES), nf), pl.BlockSpec((1, N_HEADS, LANES), nb),
                      pl.BlockSpec((1, 1, LANES), nf), pl.BlockSpec((1, 1, LANES), nb)]
        out_shape += [jax.ShapeDtypeStruct((n_tot, N_HEADS) + cshape, F32)] * 2
        out_shape += [jax.ShapeDtypeStruct((n_tot, N_HEADS, LANES), F32)] * 2
        out_shape += [jax.ShapeDtypeStruct((n_tot, 1, LANES), F32)] * 2
    if emit_after:
        out_specs += [pl.BlockSpec((1, N_HD, D_QK, D_V), lambda s, c: (s, 0, 0, 0)),
                      pl.BlockSpec((1, N_HD, LANES), lambda s, c: (s, 0, 0)),
                      pl.BlockSpec((1, 1, LANES), lambda s, c: (s, 0, 0))]
        out_shape += [jax.ShapeDtypeStruct((n_seq, N_HD, D_QK, D_V), F32),
                      jax.ShapeDtypeStruct((n_seq, N_HD, LANES), F32),
                      jax.ShapeDtypeStruct((n_seq, 1, LANES), F32)]
    return pl.pallas_call(
        functools.partial(_state_kernel, zero_init=init is None, emit_before=emit_before, emit_after=emit_after),
        grid=(n_seq, n_chunk),
        in_specs=in_specs,
        out_specs=out_specs,
        out_shape=out_shape,
        scratch_shapes=[pltpu.VMEM((N_HD,) + cshape, F32), pltpu.VMEM((N_HD, LANES), F32),
                        pltpu.VMEM((1, LANES), F32)],
        compiler_params=pltpu.CompilerParams(dimension_semantics=("arbitrary", "arbitrary"),
                                             vmem_limit_bytes=VMEM_LIMIT),
        name="mlstm_state_scan",
    )(*args)


def _mix_kernel(q_ref, k_ref, v_ref, so_ref, sgb_ref, ycg_ref, gt_ref, xc_ref, xl_ref, mod_ref,
                cbf_ref, cbb_ref, nbf_ref, nbb_ref, mbf_ref, mbb_ref,
                gmh_ref, wmo_ref, wo_ref, g2_ref, rw_ref, rb_ref,
                x1_ref, h2_ref, route_ref, routet_ref, cnt_ref, hm_scr, carry_scr, *, n_ctx_tiles, n_experts):
    i = pl.program_id(0)
    is_lat = i >= n_ctx_tiles
    lat_f = is_lat.astype(F32)
    lower, upper = _tri_masks()
    gt_t = gt_ref[...].T
    gi_t = gt_t[0:N_HD, :]
    lf_t = jax.nn.log_sigmoid(gt_t[LANES:LANES + N_HD, :])
    fwd_row = lax.broadcasted_iota(jnp.int32, (N_HD, 1), 0) < N_HEADS
    bsum_t = jnp.where(fwd_row, _dot(lf_t, upper.astype(F32), HIGHEST), _dot(lf_t, lower.astype(F32), HIGHEST))
    a_t = gi_t - bsum_t
    m_row = jnp.where(lax.broadcasted_iota(jnp.int32, (1, LANES), 1) < N_HEADS, mbf_ref[0], mbb_ref[0]) * lat_f
    m_sq = jnp.where(lax.broadcasted_iota(jnp.int32, (LANES, 1), 0) == 0, m_row, 0.0)
    m_prev = m_sq.T[0:N_HD, 0:1]
    a_c = jnp.concatenate([a_t, jnp.zeros((LANES - N_HD, TM), F32)], axis=0).T
    row0 = lax.broadcasted_iota(jnp.int32, (LANES, 1), 0) == 0
    ones_rows = jnp.where(row0, 1.0, 0.0).astype(BF16) * jnp.ones((1, TM), BF16)

    for h in range(N_HEADS):
        hs = slice(h * D_V, (h + 1) * D_V)
        pair = slice((h // 2) * LANES, (h // 2 + 1) * LANES)
        qh = jnp.where(_head_lanes(h), q_ref[:, pair], jnp.zeros((), BF16))
        kh = k_ref[:, pair]
        vext_t = jnp.concatenate([v_ref[:, hs].T, ones_rows], axis=0)
        kq = lax.dot_general(kh, qh, (((1,), (1,)), ((), ())), preferred_element_type=F32)
        hsum = None
        for d in range(2):
            hd = d * N_HEADS + h
            mask = upper if d == 0 else lower
            c_t = (cbf_ref if d == 0 else cbb_ref)[0, h] * lat_f
            n_r = (nbf_ref if d == 0 else nbb_ref)[0, h:h + 1, :] * lat_f
            a_b = jnp.where(mask, a_c[:, hd:hd + 1], -jnp.inf)
            mrow = jnp.maximum(m_prev[hd:hd + 1, :], jnp.max(a_b, axis=0, keepdims=True))
            e = jnp.exp(a_b - mrow)
            nd = _dot(vext_t, (kq * e).astype(BF16))
            cext_t = jnp.concatenate([c_t, jnp.where(row0, n_r, 0.0)], axis=0).astype(BF16)
            qc = lax.dot_general(cext_t, qh, (((1,), (1,)), ((), ())), preferred_element_type=F32)
            wi = jnp.exp(m_prev[hd:hd + 1, :] - mrow)
            num = nd[0:D_V, :] + wi * qc[0:D_V, :]
            den = nd[D_V:D_V + 1, :] + wi * qc[D_V:D_V + 1, :]
            r = 1.0 / jnp.maximum(jnp.abs(den), jnp.exp(-(bsum_t[hd:hd + 1, :] + mrow)))
            hsum = num * r if hsum is None else hsum + num * r
        hn = hsum * lax.rsqrt(jnp.mean(hsum * hsum, axis=0, keepdims=True) + EPS)
        hm_scr[:, hs] = (hn.T * gmh_ref[:, hs] * so_ref[:, hs].astype(F32)).astype(BF16)

    ym = _dot(hm_scr[...], wmo_ref[...])
    mix = (ycg_ref[...].astype(F32) + sgb_ref[...].astype(F32) * ym).astype(BF16)
    x1 = jnp.where(is_lat, xl_ref[...], xc_ref[...]) + mod_ref[0, 2:3, :] * _dot(mix, wo_ref[...])
    x1_ref[...] = x1
    h2 = _rms(x1) * g2_ref[...] * (1.0 + mod_ref[0, 4:5, :]) + mod_ref[0, 3:4, :]
    _store_rows(h2_ref, h2)

    lane = lax.broadcasted_iota(jnp.int32, (TM, LANES), 1)
    rw = rw_ref[...]
    h2_hi, rw_hi = h2.astype(BF16), rw.astype(BF16)
    h2_lo, rw_lo = (h2 - h2_hi.astype(F32)).astype(BF16), (rw - rw_hi.astype(F32)).astype(BF16)
    logits = _dot(h2_hi, rw_hi) + (_dot(h2_hi, rw_lo) + _dot(h2_lo, rw_hi)) + rb_ref[...]
    work = jnp.where(lane < n_experts, logits, -jnp.inf)
    sels, exps, idxs = [], [], []
    top = None
    for _ in range(TOP_K):
        mx = jnp.max(work, axis=-1, keepdims=True)
        ix = jnp.min(jnp.where(work == mx, lane, LANES), axis=-1, keepdims=True)
        sel = lane == ix
        work = jnp.where(sel, -jnp.inf, work)
        top = mx if top is None else top
        sels.append(sel)
        idxs.append(ix.astype(F32))
        exps.append(jnp.exp(mx - top))
    inv = 1.0 / functools.reduce(lambda p, q: p + q, exps)

    @pl.when(i == 0)
    def _():
        carry_scr[...] = jnp.zeros_like(carry_scr)

    onehot = functools.reduce(lambda p, q: p + q, [jnp.where(s, 1.0, 0.0) for s in sels])
    row = lax.broadcasted_iota(jnp.int32, (TM, TM), 0)
    col = lax.broadcasted_iota(jnp.int32, (TM, TM), 1)
    before = _dot((row > col).astype(BF16), onehot.astype(BF16)) + carry_scr[...]
    carry_scr[...] += jnp.sum(onehot, axis=0, keepdims=True)
    cnt_ref[...] = carry_scr[...]
    route = jnp.zeros((TM, LANES), F32)
    for j in range(TOP_K):
        slot = jnp.sum(jnp.where(sels[j], before, 0.0), axis=-1, keepdims=True)
        route = jnp.where(lane == j, idxs[j], route)
        route = jnp.where(lane == TOP_K + j, slot, route)
        route = jnp.where(lane == 2 * TOP_K + j, exps[j] * inv, route)
    route_ref[...] = route
    routet_ref[...] = route.T[0:8, :].astype(jnp.int32)


def _dispatch_kernel(pos_ref, h2_ref, xs_ref, sem):
    def copy(t, j):
        return pltpu.make_async_copy(_row_tile(h2_ref, t), _row_tile(xs_ref, pos_ref[TOP_K * t + j]), sem)

    def start(t, carry):
        for j in range(TOP_K):
            copy(t, j).start(priority=j % 2)
        return carry

    tm = h2_ref.shape[0] // ROW_SUB
    lax.fori_loop(0, tm, start, 0, unroll=ISSUE_UNROLL)
    for j in range(TOP_K):
        pltpu.make_async_copy(h2_ref, xs_ref.at[pl.ds(0, tm * ROW_SUB)], sem).wait()


def _expert_kernel(tile_ref, exp_ref, flag_ref, lo_ref, hi_ref, nxt_ref,
                   xs_ref, wg_hbm, bg_ref, wl_hbm, bl_ref, wd_hbm, bd_ref, ys_ref, w_scr, wf_scr, wsem):
    w = pl.program_id(0)
    flags = flag_ref[w]

    def fetch(e, slot):
        return [pltpu.make_async_copy(hbm.at[e], wf_scr.at[slot, m], wsem.at[slot])
                for m, hbm in enumerate((wg_hbm, wl_hbm, wd_hbm))]

    @pl.when(w == 0)
    def _():
        for cp in fetch(exp_ref[0], 0):
            cp.start()

    @pl.when((flags & 4) != 0)
    def _():
        slot = (flags >> 4) & 1
        for cp in fetch(exp_ref[w], slot):
            cp.wait()
        for m in range(3):
            w_scr[m] = wf_scr[slot, m].astype(BF16)

        @pl.when((flags & 8) != 0)
        def _():
            for cp in fetch(nxt_ref[w], 1 - slot):
                cp.start()

    tmx = ys_ref.shape[0] // ROW_SUB

    def run(r0, m):
        x = _load_rows(xs_ref, r0, m).astype(BF16)
        gt = jnp.minimum(_dot(x, w_scr[0]) + bg_ref[0], SWIGLU_LIMIT)
        lin = jnp.clip(_dot(x, w_scr[1]) + bl_ref[0], -SWIGLU_LIMIT, SWIGLU_LIMIT)
        act = gt * jax.nn.sigmoid(SWIGLU_ALPHA * gt) * (lin + 1.0)
        y = _dot(act.astype(BF16), w_scr[2]) + bd_ref[0]

        @pl.when((flags & 2) != 0)
        def _():
            _store_rows(ys_ref, y, r0)
            if m < tmx:
                _store_rows(ys_ref, jnp.zeros((tmx - m, y.shape[1]), F32), m if r0 == 0 else 0)

        @pl.when((flags & 2) == 0)
        def _():
            rows = r0 + lax.broadcasted_iota(jnp.int32, (m, 1), 0)
            mine = (rows >= lo_ref[w]) & (rows < hi_ref[w])
            _store_rows(ys_ref, jnp.where(mine, y, _load_rows(ys_ref, r0, m)), r0)

    half = (flags >> 5) & 3
    valid = (flags & 1) != 0
    pl.when(valid & (half == 0))(lambda: run(0, tmx))
    pl.when(valid & (half == 1))(lambda: run(0, tmx // 2))
    pl.when(valid & (half == 2))(lambda: run(tmx // 2, tmx // 2))


def _combine_kernel(pos_ref, posn_ref, route_ref, x1_ref, mod_ref, fg_ref, ys_ref, outc_ref, outl_ref,
                    buf, sem, *, n_ctx_tiles):
    i = pl.program_id(0)
    n = pl.num_programs(0)
    tm = x1_ref.shape[0]

    def copy(p_ref, slot, t, j):
        return pltpu.make_async_copy(_row_tile(ys_ref, p_ref[TOP_K * t + j]), _row_tile(buf.at[slot, j], t), sem.at[slot])

    def start_all(p_ref, slot):
        def body(t, carry):
            for j in range(TOP_K):
                copy(p_ref, slot, t, j).start(priority=j % 2)
            return carry
        lax.fori_loop(0, tm, body, 0, unroll=ISSUE_UNROLL)

    @pl.when(i == 0)
    def _():
        start_all(pos_ref, 0)

    @pl.when(i + 1 < n)
    def _():
        start_all(posn_ref, (i + 1) % 2)

    slot = i % 2

    for j in range(TOP_K):
        pltpu.make_async_copy(ys_ref.at[pl.ds(0, tm * ROW_SUB)], buf.at[slot, j], sem.at[slot]).wait()
    acc = None
    for j in range(TOP_K):
        term = route_ref[:, 2 * TOP_K + j:2 * TOP_K + j + 1] * _load_rows(buf.at[slot, j])
        acc = term if acc is None else acc + term
    out = _rms(x1_ref[...] + mod_ref[0, 5:6, :] * acc) * fg_ref[...]

    @pl.when(i < n_ctx_tiles)
    def _():
        outc_ref[...] = out

    @pl.when(i >= n_ctx_tiles)
    def _():
        outl_ref[...] = out


def _work_items(counts, n_experts, n_rows, tmx):
    n_items_max = n_rows // tmx + n_experts - 1
    cnt = counts.astype(jnp.int32)
    offs = jnp.concatenate([jnp.zeros((1,), jnp.int32), jnp.cumsum(cnt)])
    first_tile = offs[:-1] // tmx
    n_it = jnp.where(cnt > 0, (offs[1:] - 1) // tmx - first_tile + 1, 0)
    it_start = jnp.concatenate([jnp.zeros((1,), jnp.int32), jnp.cumsum(n_it)])
    total = it_start[-1]
    w = jnp.arange(n_items_max, dtype=jnp.int32)
    wc = jnp.minimum(w, total - 1)
    e = jnp.sum((it_start[None, 1:] <= wc[:, None]).astype(jnp.int32), axis=1)
    e = jnp.minimum(e, n_experts - 1)
    is_e = e[:, None] == jnp.arange(n_experts, dtype=jnp.int32)
    at_e = lambda a: jnp.sum(jnp.where(is_e, a[None, :], 0), axis=1)
    tile = at_e(first_tile) + wc - at_e(it_start[:-1])
    valid = w < total
    prev = lambda a: jnp.concatenate([jnp.full((1,), -1, jnp.int32), a[:-1]])
    ids = jnp.arange(n_experts, dtype=jnp.int32)
    later = (ids[None, :] > ids[:, None]) & (cnt > 0)[None, :]
    nxt = jnp.min(jnp.where(later, ids[None, :], n_experts), axis=1)
    has_next = nxt < n_experts
    slot = (jnp.cumsum((cnt > 0).astype(jnp.int32)) - 1) & 1
    flags = (valid.astype(jnp.int32) + 2 * (valid & (tile != prev(tile))).astype(jnp.int32)
             + 4 * (valid & (e != prev(e))).astype(jnp.int32)
             + 8 * at_e(has_next.astype(jnp.int32)) + 16 * at_e(slot))
    lo = jnp.clip(at_e(offs[:-1]) - tile * tmx, 0, tmx)
    hi = jnp.clip(at_e(offs[1:]) - tile * tmx, 0, tmx)
    flags = flags + 32 * jnp.where(hi <= tmx // 2, 1, jnp.where(lo >= tmx // 2, 2, 0))
    return offs, tile, e, flags, lo, hi, at_e(jnp.minimum(nxt, n_experts - 1))


def _pack_in_proj(w, b):
    d = w.shape[0]
    o_g = 5 * d + 2 * D_QKH
    o_ga, o_gb = o_g + 4 * N_HEADS, o_g + 4 * N_HEADS + d

    def gates(m):
        gz = jnp.zeros((m.shape[0], LANES - N_HD), m.dtype)
        i_f, f_f, i_b, f_b = (m[:, o_g + j * N_HEADS:o_g + (j + 1) * N_HEADS] for j in range(4))
        return jnp.concatenate([i_f, i_b, gz, f_f, f_b, gz], axis=1)

    def pack(m):
        return jnp.concatenate([m[:, 0:o_g], m[:, o_ga:o_gb], m[:, o_gb:o_gb + d], gates(m)], axis=1)

    return pack(w.astype(BF16)), pack(b.reshape(1, -1))


def kernel(x_prompt, x_sample, c, state_C, state_n, state_m, c_ctx, ada_w, ada_b, norm1_g, norm2_g, w_in, b_in,
           conv_w, conv_b, w_conv_out, mh_norm_g, w_m_out, w_o, router_w, router_b, w_gate, b_gate, w_lin, b_lin,
           w_down, b_down, final_g):
    nb, seq, d = x_prompt.shape
    nd, dseq, _ = x_sample.shape
    n_experts = w_gate.shape[1]
    assert d == _D and w_in.shape[0] == 1 and seq == TM and dseq % TM == 0 and TM % GRID_W == 0
    assert w_gate.shape[-1] == d
    t_ctx, t_lat = nb * seq, nd * dseq
    n_tok = t_ctx + t_lat
    n_ctx_tiles, n_tiles = t_ctx // TM, n_tok // TM
    lat_chunks = dseq // TM
    n_lat_tiles = n_tiles - n_ctx_tiles

    n_c = 1 + nd
    n_cp = -(-n_c // 8) * 8
    cvec = jnp.concatenate([c_ctx[None, :], c, jnp.zeros((n_cp - n_c, d), F32)], axis=0)
    mod = _modulation(cvec, ada_w[0], ada_b[0]).reshape(n_cp, N_MOD, d)
    mod = jnp.pad(mod, ((0, 0), (0, 8 - N_MOD), (0, 0)))

    def mod_row(tile_tokens):
        ctx_t, per_seq = t_ctx // tile_tokens, dseq // tile_tokens
        return lambda i, *_: (jnp.where(i < ctx_t, 0, 1 + (i - ctx_t) // per_seq), 0, 0)

    x_ctx, x_lat = x_prompt.reshape(t_ctx, d), x_sample.reshape(t_lat, d)
    ctx_spec = pl.BlockSpec((TM, d), lambda i: (jnp.minimum(i, n_ctx_tiles - 1), 0))
    lat_spec = pl.BlockSpec((TM, d), lambda i: (jnp.maximum(i - n_ctx_tiles, 0), 0))
    w_all, b_all = _pack_in_proj(w_in[0], b_in[0])
    tile = lambda cols: pl.BlockSpec((TM, cols), lambda i: (i, 0))
    rows = lambda n: pl.BlockSpec((n * ROW_SUB, LANES), lambda i: (i, 0))
    params = pltpu.CompilerParams(dimension_semantics=("arbitrary",), vmem_limit_bytes=VMEM_LIMIT)
    bf = lambda cols: jax.ShapeDtypeStruct((n_tok, cols), BF16)

    tmp = next(t for t in (TM_PROJ, TM) if t_ctx % t == 0 and dseq % t == 0)
    np_ctx = t_ctx // tmp
    ptile = lambda cols: pl.BlockSpec((tmp, cols), lambda i: (i, 0))
    ycg, sgb, q, k, v, so, gt = pl.pallas_call(
        functools.partial(_proj_kernel, n_ctx_tiles=np_ctx, ctx_row=seq),
        grid=(n_tok // tmp,),
        in_specs=[pl.BlockSpec((tmp, d), lambda i: (jnp.minimum(i, np_ctx - 1), 0)),
                  pl.BlockSpec((tmp, d), lambda i: (jnp.maximum(i - np_ctx, 0), 0)),
                  pl.BlockSpec((1, 8, d), mod_row(tmp)), _const_spec((1, d)),
                  _const_spec((d, W_COLS)), _const_spec((1, W_COLS)), _const_spec((3, d)), _const_spec((1, d)),
                  _const_spec((d, d))],
        out_specs=[ptile(d), ptile(d), ptile(D_QKH), ptile(D_QKH), ptile(d), ptile(d), ptile(2 * LANES)],
        out_shape=[bf(d), bf(d), bf(D_QKH), bf(D_QKH), bf(d), bf(d),
                   jax.ShapeDtypeStruct((n_tok, 2 * LANES), F32)],
        compiler_params=params,
        name="in_proj_conv",
    )(x_ctx, x_lat, mod, norm1_g, w_all, b_all, conv_w[0], conv_b, w_conv_out[0].astype(BF16))

    c_new, n_new, m_new = _state_scan(k, v, gt, 0, nb, 1, None, False, True)
    def on_head_lanes(a):
        even = lax.broadcasted_iota(jnp.int32, (N_HD,) + (1,) * (a.ndim - 2), 0) % 2 == 0
        lo = jnp.pad(a, [(0, 0)] * (a.ndim - 1) + [(0, LANES - D_QK)])
        hi = jnp.pad(a, [(0, 0)] * (a.ndim - 1) + [(LANES - D_QK, 0)])
        return jnp.where(even, lo, hi)

    init = (on_head_lanes(jnp.swapaxes(state_C[:, 0].astype(F32).reshape(nd, N_HD, D_QK, D_V), 2, 3)),
            on_head_lanes(state_n[:, 0].astype(F32).reshape(nd, N_HD, D_QK)),
            jnp.pad(state_m[:, 0].astype(F32).reshape(nd, 1, N_HD), ((0, 0), (0, 0), (0, LANES - N_HD))))
    cbf, cbb, nbf, nbb, mbf, mbb = _state_scan(k, v, gt, n_ctx_tiles, nd, lat_chunks, init, True, False)

    lat_idx = lambda i: jnp.maximum(i - n_ctx_tiles, 0)
    st4 = pl.BlockSpec((1, N_HEADS, D_V, LANES), lambda i: (lat_idx(i), 0, 0, 0))
    st3 = pl.BlockSpec((1, N_HEADS, LANES), lambda i: (lat_idx(i), 0, 0))
    st1 = pl.BlockSpec((1, 1, LANES), lambda i: (lat_idx(i), 0, 0))
    rw = jnp.pad(router_w[0], ((0, 0), (0, LANES - n_experts)))
    rb = jnp.pad(router_b[0], (0, LANES - n_experts)).reshape(1, LANES)
    x1, h2, route, route_t, counts = pl.pallas_call(
        functools.partial(_mix_kernel, n_ctx_tiles=n_ctx_tiles, n_experts=n_experts),
        grid=(n_tiles,),
        in_specs=[tile(D_QKH), tile(D_QKH), tile(d), tile(d), tile(d), tile(d), tile(2 * LANES), ctx_spec, lat_spec,
                  pl.BlockSpec((1, 8, d), mod_row(TM)), st4, st4, st3, st3, st1, st1,
                  _const_spec((1, d)), _const_spec((d, d)), _const_spec((d, d)), _const_spec((1, d)),
                  _const_spec((d, LANES)), _const_spec((1, LANES))],
        out_specs=[tile(d), rows(TM), tile(LANES), pl.BlockSpec((8, TM), lambda i: (0, i)),
                   pl.BlockSpec((1, LANES), lambda i: (0, 0))],
        out_shape=[jax.ShapeDtypeStruct((n_tok, d), F32), jax.ShapeDtypeStruct((n_tok * ROW_SUB, LANES), F32),
                   jax.ShapeDtypeStruct((n_tok, LANES), F32), jax.ShapeDtypeStruct((8, n_tok), jnp.int32),
                   jax.ShapeDtypeStruct((1, LANES), F32)],
        scratch_shapes=[pltpu.VMEM((TM, d), BF16), pltpu.VMEM((1, LANES), F32)],
        compiler_params=params,
        name="mlstm_mix_router",
    )(q, k, v, so, sgb, ycg, gt, x_ctx, x_lat, mod, cbf, cbb, nbf, nbb, mbf, mbb,
      mh_norm_g, w_m_out[0].astype(BF16), w_o[0].astype(BF16), norm2_g, rw, rb)

    n_rows = TOP_K * n_tok
    offs, it_tile, it_exp, it_flags, it_lo, it_hi, it_nxt = _work_items(
        counts[0, :n_experts], n_experts, n_rows, TMX)
    is_exp = route_t[0:TOP_K, :, None] == jnp.arange(n_experts, dtype=jnp.int32)
    pos_t = jnp.sum(jnp.where(is_exp, offs[:n_experts], 0), axis=-1) + route_t[TOP_K:2 * TOP_K]
    pos = pos_t.T.reshape(n_rows)
    pos_spec = lambda tm, f: pl.BlockSpec((TOP_K * tm,), f, memory_space=pltpu.SMEM)
    any_spec = pl.BlockSpec(memory_space=pl.ANY)
    tmd = next(t for t in (TM_DISPATCH, TM) if n_tok % t == 0)
    tmc = next(t for t in (TM_COMBINE, TM) if t_ctx % t == 0 and dseq % t == 0)

    xs = pl.pallas_call(
        _dispatch_kernel,
        grid=(n_tok // tmd,),
        in_specs=[pos_spec(tmd, lambda i: (i,)), rows(tmd)],
        out_specs=any_spec,
        out_shape=jax.ShapeDtypeStruct((n_rows * ROW_SUB, LANES), F32),
        scratch_shapes=[pltpu.SemaphoreType.DMA],
        compiler_params=params,
        name="moe_dispatch",
    )(pos, h2)

    dff = w_gate.shape[-1]
    wspec = lambda a, b_: pl.BlockSpec((1, a, b_), lambda w, tl, ex, *_: (ex[w], 0, 0))
    ys = pl.pallas_call(
        _expert_kernel,
        grid_spec=pltpu.PrefetchScalarGridSpec(
            num_scalar_prefetch=6,
            grid=(it_tile.shape[0],),
            in_specs=[pl.BlockSpec((TMX * ROW_SUB, LANES), lambda w, tl, *_: (tl[w], 0)),
                      any_spec, wspec(1, dff), any_spec, wspec(1, dff), any_spec, wspec(1, d)],
            out_specs=pl.BlockSpec((TMX * ROW_SUB, LANES), lambda w, tl, *_: (tl[w], 0)),
            scratch_shapes=[pltpu.VMEM((3, d, dff), BF16), pltpu.VMEM((2, 3, d, dff), F32),
                            pltpu.SemaphoreType.DMA((2,))]),
        out_shape=jax.ShapeDtypeStruct((n_rows * ROW_SUB, LANES), F32),
        compiler_params=params,
        name="moe_experts",
    )(it_tile, it_exp, it_flags, it_lo, it_hi, it_nxt, xs, w_gate[0], b_gate[0].reshape(n_experts, 1, dff), w_lin[0],
      b_lin[0].reshape(n_experts, 1, dff), w_down[0], b_down[0].reshape(n_experts, 1, d))

    nc_ctx, nc_all = t_ctx // tmc, n_tok // tmc
    ctx_i = lambda i: (jnp.minimum(i, nc_ctx - 1), 0)
    lat_i = lambda i: (jnp.maximum(i - nc_ctx, 0), 0)
    ctile = lambda cols: pl.BlockSpec((tmc, cols), lambda i: (i, 0))
    y_prompt, y_sample = pl.pallas_call(
        functools.partial(_combine_kernel, n_ctx_tiles=nc_ctx),
        grid=(nc_all,),
        in_specs=[pos_spec(tmc, lambda i: (i,)), pos_spec(tmc, lambda i: (jnp.minimum(i + 1, nc_all - 1),)),
                  ctile(LANES), ctile(d), pl.BlockSpec((1, 8, d), mod_row(tmc)), _const_spec((1, d)), any_spec],
        out_specs=[pl.BlockSpec((tmc, d), ctx_i), pl.BlockSpec((tmc, d), lat_i)],
        out_shape=[jax.ShapeDtypeStruct((t_ctx, d), F32), jax.ShapeDtypeStruct((t_lat, d), F32)],
        scratch_shapes=[pltpu.VMEM((2, TOP_K, tmc * ROW_SUB, LANES), F32), pltpu.SemaphoreType.DMA((2,))],
        compiler_params=params,
        name="moe_combine",
    )(pos, pos, route, x1, mod, final_g.reshape(1, d), ys)

    y_prompt = y_prompt.reshape(nb, seq, d)
    y_sample = y_sample.reshape(nd, dseq, d)
    new_c = c_new.reshape(nb, 1, 2, N_HEADS, D_QK, D_V)
    odd_head = (lax.broadcasted_iota(jnp.int32, (1, N_HD, 1), 1) % 2) == 1
    new_n = jnp.where(odd_head, n_new[:, :, D_QK:], n_new[:, :, :D_QK]).reshape(nb, 1, 2, N_HEADS, D_QK)
    new_m = m_new[:, 0, :N_HD].reshape(nb, 1, 2, N_HEADS)
    return (y_prompt, y_sample, new_c, new_n, new_m)
```
